```python
import math
import jax, jax.numpy as jnp
from jax import lax
import numpy as np

D_MODEL = 2048
BATCH = 8
SEQ = 2048
DEPTH = 2

A_HEADS = 16
A_KV_HEADS = 2
A_HEAD_DIM = 64
A_WINDOW = 128
A_BLOCK = 128
REL_BUCKETS = 32
REL_MAX_DIST = 128
B_HEADS = 8
B_DK = 128
B_DV = 128
B_CHUNK = 64
C_GROUPS = 8
C_GROUP_DIM = 128
C_CHUNK = 128
D_CHANNELS = 1024
D_CONV = 31

A_QW = A_HEADS * A_HEAD_DIM
A_KVW = A_KV_HEADS * A_HEAD_DIM
B_KW = B_HEADS * B_DK
B_VW = B_HEADS * B_DV
C_W = C_GROUPS * C_GROUP_DIM
EVEN_IN = A_QW + 2 * A_KVW + 2 * B_KW + 2 * B_VW
EVEN_MIX = A_QW + B_VW
EVEN_SPLITS = [A_QW, A_QW + A_KVW, A_QW + 2 * A_KVW, A_QW + 2 * A_KVW + B_KW,
               A_QW + 2 * A_KVW + 2 * B_KW, A_QW + 2 * A_KVW + 2 * B_KW + B_VW]
ODD_IN = 2 * C_W + 2 * D_CHANNELS
ODD_MIX = C_W + D_CHANNELS
ODD_SPLITS = [C_W, 2 * C_W, 2 * C_W + D_CHANNELS]

N_GROUPS = 4
EXPERTS_PER_GROUP = 8
N_EXPERTS = N_GROUPS * EXPERTS_PER_GROUP
TOP_K = 2
D_EXPERT = 512
MOE_BLOCK = 256

DN_ALPHA = (2 * DEPTH) ** 0.25
DN_BETA = (8 * DEPTH) ** -0.25
LN_EPS = 1e-5
RMS_EPS = 1e-6
N_EVEN = (DEPTH + 1) // 2
N_ODD = DEPTH // 2

kernel_name = 'hybrid_swa_hgrn2_gmlp_conv_hmoe'


def layer_norm(x, g, b):
    xf = x.astype(jnp.float32)
    mu = jnp.mean(xf, axis=-1, keepdims=True)
    var = jnp.mean(jnp.square(xf - mu), axis=-1, keepdims=True)
    y = (xf - mu) * lax.rsqrt(var + LN_EPS)
    return (y * g.astype(jnp.float32) + b.astype(jnp.float32)).astype(x.dtype)


def t5_causal_bucket(dist):
    max_exact = REL_BUCKETS // 2
    d = jnp.maximum(dist, 1).astype(jnp.float32)
    large = max_exact + (jnp.log(d / max_exact) / math.log(REL_MAX_DIST / max_exact)
                         * (REL_BUCKETS - max_exact)).astype(jnp.int32)
    large = jnp.minimum(large, REL_BUCKETS - 1)
    return jnp.where(dist < max_exact, dist, large)


def sliding_window_sink_attention(q, k, v, sinks, rel_bias):
    bsz, seq = q.shape[0], q.shape[1]
    nb = seq // A_BLOCK
    grp = A_HEADS // A_KV_HEADS
    qb = q.reshape(bsz, nb, A_BLOCK, A_KV_HEADS, grp, A_HEAD_DIM)
    pad = jnp.zeros((bsz, A_BLOCK, A_KV_HEADS, A_HEAD_DIM), k.dtype)

    def band(t):
        tp = jnp.concatenate([pad, t], axis=1)
        prev = tp[:, :seq].reshape(bsz, nb, A_BLOCK, A_KV_HEADS, A_HEAD_DIM)
        cur = t.reshape(bsz, nb, A_BLOCK, A_KV_HEADS, A_HEAD_DIM)
        return jnp.concatenate([prev, cur], axis=2)

    kb, vb = band(k), band(v)
    scores = jnp.einsum('bnqhgd,bnkhd->bnhgqk', qb, kb).astype(jnp.float32) * (A_HEAD_DIM ** -0.5)
    t_loc = jnp.arange(A_BLOCK, dtype=jnp.int32)[:, None]
    s_loc = jnp.arange(2 * A_BLOCK, dtype=jnp.int32)[None, :]
    dist = t_loc + A_BLOCK - s_loc
    bucket = t5_causal_bucket(jnp.maximum(dist, 0))
    bias = rel_bias.astype(jnp.float32)[bucket]
    bias = bias.transpose(2, 0, 1).reshape(A_KV_HEADS, grp, A_BLOCK, 2 * A_BLOCK)
    key_pos = (jnp.arange(nb, dtype=jnp.int32)[:, None, None] - 1) * A_BLOCK + s_loc[None]
    valid = (dist >= 0)[None] & (dist < A_WINDOW)[None] & (key_pos >= 0)
    logits = jnp.where(valid[None, :, None, None], scores + bias, -jnp.inf)
    sink = jnp.broadcast_to(sinks.astype(jnp.float32).reshape(1, 1, A_KV_HEADS, grp, 1, 1),
                            logits.shape[:-1] + (1,))
    probs = jax.nn.softmax(jnp.concatenate([logits, sink], axis=-1), axis=-1)[..., :-1]
    out = jnp.einsum('bnhgqk,bnkhd->bnqhgd', probs.astype(v.dtype), vb)
    return out.reshape(bsz, seq, A_QW)


def hgrn2_recurrence(q, f_logit, i, g, lower_bound, norm_g):
    bsz, seq = q.shape[0], q.shape[1]
    nc = seq // B_CHUNK
    qf = jax.nn.silu(q.astype(jnp.float32))
    lb = lower_bound.astype(jnp.float32)
    f = lb + (1.0 - lb) * jax.nn.sigmoid(f_logit.astype(jnp.float32))
    log_f = jnp.log(f)
    k_in = 1.0 - f

    def to_chunks(t, dh):
        return t.reshape(bsz, nc, B_CHUNK, B_HEADS, dh).transpose(1, 0, 3, 2, 4)

    qc = to_chunks(qf, B_DK)
    kc = to_chunks(k_in, B_DK)
    vc = to_chunks(i.astype(jnp.float32), B_DV)
    bc = jnp.cumsum(to_chunks(log_f, B_DK), axis=3)
    causal = jnp.tril(jnp.ones((B_CHUNK, B_CHUNK), dtype=bool))[:, :, None]

    def chunk_step(state, xs):
        qx, kx, vx, bx = xs
        diff = bx[:, :, :, None, :] - bx[:, :, None, :, :]
        decay = jnp.exp(jnp.where(causal, diff, -jnp.inf))
        scores = jnp.einsum('bhtk,bhsk,bhtsk->bhts', qx, kx, decay)
        o = (jnp.einsum('bhts,bhsv->bhtv', scores, vx)
             + jnp.einsum('bhtk,bhkv->bhtv', qx * jnp.exp(bx), state))
        b_last = bx[:, :, -1, :]
        k_dec = kx * jnp.exp(b_last[:, :, None, :] - bx)
        state = state * jnp.exp(b_last)[..., None] + jnp.einsum('bhsk,bhsv->bhkv', k_dec, vx)
        return state, o

    state0 = jnp.zeros((bsz, B_HEADS, B_DK, B_DV), jnp.float32)
    _, oc = lax.scan(chunk_step, state0, (qc, kc, vc, bc))
    o = oc.transpose(1, 0, 3, 2, 4).reshape(bsz, seq, B_HEADS, B_DV)
    o = o * lax.rsqrt(jnp.mean(jnp.square(o), axis=-1, keepdims=True) + RMS_EPS)
    return o.reshape(bsz, seq, B_VW) * norm_g.astype(jnp.float32) * jax.nn.silu(g.astype(jnp.float32))


def chunked_spatial_gating(u, v, ln_g, ln_b, w_s, b_s):
    bsz, seq = u.shape[0], u.shape[1]
    nc = seq // C_CHUNK
    u = jax.nn.gelu(u)
    v = layer_norm(jax.nn.gelu(v), ln_g, ln_b)
    vc = v.reshape(bsz, nc, C_CHUNK, C_GROUPS, C_GROUP_DIM)
    w = w_s * jnp.tril(jnp.ones((C_CHUNK, C_CHUNK), w_s.dtype))
    mixed = jnp.einsum('gts,bnsgc->bntgc', w, vc) + b_s.T[None, None, :, :, None]
    return u * mixed.reshape(bsz, seq, C_W)


def conformer_conv(a, gate, conv_w, conv_b, ln_g, ln_b):
    h = a * jax.nn.sigmoid(gate)
    hp = jnp.pad(h, ((0, 0), (D_CONV - 1, 0), (0, 0)))
    y = lax.conv_general_dilated(hp, conv_w[:, None, :], window_strides=(1,), padding='VALID',
                                 dimension_numbers=('NWC', 'WIO', 'NWC'),
                                 feature_group_count=D_CHANNELS)
    y = layer_norm(y + conv_b, ln_g, ln_b)
    return jax.nn.silu(y)


def hierarchical_moe(x, w_group, b_group, w_router, b_router, w1, w3, w2):
    bsz, seq, dm = x.shape
    n_tok = bsz * seq
    xf = x.reshape(n_tok, dm)
    g_prob = jax.nn.softmax((xf @ w_group).astype(jnp.float32) + b_group.astype(jnp.float32), axis=-1)
    g_w, g_idx = lax.top_k(g_prob, 1)
    e_logits = ((xf @ w_router).astype(jnp.float32) + b_router.astype(jnp.float32))
    e_logits = e_logits.reshape(n_tok, N_GROUPS, EXPERTS_PER_GROUP)
    e_in = jnp.take_along_axis(e_logits, g_idx[:, :, None], axis=1)[:, 0]
    e_top, e_loc = lax.top_k(e_in, TOP_K)
    e_w = jax.nn.softmax(e_top, axis=-1) * g_w
    e_idx = g_idx * EXPERTS_PER_GROUP + e_loc

    m = n_tok * TOP_K
    flat_e = e_idx.reshape(m).astype(jnp.int32)
    flat_tok = jnp.repeat(jnp.arange(n_tok, dtype=jnp.int32), TOP_K)
    flat_w = e_w.reshape(m)
    order = jnp.argsort(flat_e)
    se, stok, sw = flat_e[order], flat_tok[order], flat_w[order]
    counts = jnp.bincount(flat_e, length=N_EXPERTS).astype(jnp.int32)
    starts = jnp.cumsum(counts) - counts
    pcounts = (counts + MOE_BLOCK - 1) // MOE_BLOCK * MOE_BLOCK
    pends = jnp.cumsum(pcounts)
    pstarts = pends - pcounts
    n_blocks = -(-(m + N_EXPERTS * (MOE_BLOCK - 1)) // MOE_BLOCK)
    n_pad = n_blocks * MOE_BLOCK
    dest = pstarts[se] + (jnp.arange(m, dtype=jnp.int32) - starts[se])
    tok_pad = jnp.zeros((n_pad,), jnp.int32).at[dest].set(stok)
    w_pad = jnp.zeros((n_pad,), jnp.float32).at[dest].set(sw)
    blk_start = jnp.arange(n_blocks, dtype=jnp.int32) * MOE_BLOCK
    blk_e = jnp.minimum(jnp.searchsorted(pends, blk_start, side='right'), N_EXPERTS - 1)
    xb = xf[tok_pad].reshape(n_blocks, MOE_BLOCK, dm)

    def expert_block(args):
        xblk, e = args
        h = jax.nn.silu(xblk @ w1[e]) * (xblk @ w3[e])
        return h @ w2[e]

    yb = lax.map(expert_block, (xb, blk_e))
    y = jnp.zeros((n_tok, dm), jnp.float32).at[tok_pad].add(
        yb.reshape(n_pad, dm).astype(jnp.float32) * w_pad[:, None])
    return y.astype(x.dtype).reshape(bsz, seq, dm)


def setup_inputs(seed: int = 0) -> dict:
    key = jax.random.key(seed)
    ks = jax.random.split(key, 28)
    f32 = jnp.float32

    def nrm(k, shape, scale):
        return jax.random.normal(k, shape, f32) * scale

    return {
        'x': nrm(ks[0], (BATCH, SEQ, D_MODEL), 1.0),
        'w_in_ab': nrm(ks[1], (N_EVEN, D_MODEL, EVEN_IN), D_MODEL ** -0.5),
        'attn_sinks': nrm(ks[2], (N_EVEN, A_HEADS), 0.5),
        'rel_bias': nrm(ks[3], (REL_BUCKETS, A_HEADS), 0.3),
        'hgrn_lb_logits': nrm(ks[4], (DEPTH + 1, B_KW), 0.5),
        'hgrn_norm_g': 1.0 + nrm(ks[5], (N_EVEN, B_VW), 0.1),
        'w_out_ab': nrm(ks[6], (N_EVEN, EVEN_MIX, D_MODEL), EVEN_MIX ** -0.5 * DN_BETA),
        'w_in_cd': nrm(ks[7], (N_ODD, D_MODEL, ODD_IN), D_MODEL ** -0.5),
        'gmlp_ln_g': 1.0 + nrm(ks[8], (N_ODD, C_W), 0.1),
        'gmlp_ln_b': nrm(ks[9], (N_ODD, C_W), 0.02),
        'gmlp_w_s': nrm(ks[10], (N_ODD, C_GROUPS, C_CHUNK, C_CHUNK), C_CHUNK ** -0.5),
        'gmlp_b_s': 1.0 + nrm(ks[11], (N_ODD, C_GROUPS, C_CHUNK), 0.1),
        'conv_w': nrm(ks[12], (N_ODD, D_CONV, D_CHANNELS), D_CONV ** -0.5),
        'conv_b': nrm(ks[13], (N_ODD, D_CHANNELS), 0.02),
        'conv_ln_g': 1.0 + nrm(ks[14], (N_ODD, D_CHANNELS), 0.1),
        'conv_ln_b': nrm(ks[15], (N_ODD, D_CHANNELS), 0.02),
        'w_out_cd': nrm(ks[16], (N_ODD, ODD_MIX, D_MODEL), ODD_MIX ** -0.5 * DN_BETA),
        'ln_mix_g': 1.0 + nrm(ks[17], (DEPTH, D_MODEL), 0.1),
        'ln_mix_b': nrm(ks[18], (DEPTH, D_MODEL), 0.02),
        'ln_ffn_g': 1.0 + nrm(ks[19], (DEPTH, D_MODEL), 0.1),
        'ln_ffn_b': nrm(ks[20], (DEPTH, D_MODEL), 0.02),
        'moe_w_group': nrm(ks[21], (DEPTH, D_MODEL, N_GROUPS), D_MODEL ** -0.5),
        'moe_b_group': nrm(ks[22], (DEPTH, N_GROUPS), 0.01),
        'moe_w_router': nrm(ks[23], (DEPTH, D_MODEL, N_EXPERTS), D_MODEL ** -0.5),
        'moe_b_router': nrm(ks[24], (DEPTH, N_EXPERTS), 0.01),
        'moe_w1': nrm(ks[25], (DEPTH, N_EXPERTS, D_MODEL, D_EXPERT), D_MODEL ** -0.5),
        'moe_w3': nrm(ks[26], (DEPTH, N_EXPERTS, D_MODEL, D_EXPERT), D_MODEL ** -0.5),
        'moe_w2': nrm(ks[27], (DEPTH, N_EXPERTS, D_EXPERT, D_MODEL), D_EXPERT ** -0.5 * DN_BETA),
    }


def reference(x, w_in_ab, attn_sinks, rel_bias, hgrn_lb_logits, hgrn_norm_g, w_out_ab,
              w_in_cd, gmlp_ln_g, gmlp_ln_b, gmlp_w_s, gmlp_b_s, conv_w, conv_b, conv_ln_g, conv_ln_b,
              w_out_cd, ln_mix_g, ln_mix_b, ln_ffn_g, ln_ffn_b,
              moe_w_group, moe_b_group, moe_w_router, moe_b_router, moe_w1, moe_w3, moe_w2):
    bsz, seq = x.shape[0], x.shape[1]
    lb_table = jnp.cumsum(jax.nn.softmax(hgrn_lb_logits.astype(jnp.float32), axis=0), axis=0)
    for layer in range(DEPTH):
        j = layer // 2
        if layer % 2 == 0:
            proj = x @ w_in_ab[j]
            qa, ka, va, qb, fb, ib, gb = jnp.split(proj, EVEN_SPLITS, axis=-1)
            ya = sliding_window_sink_attention(
                qa.reshape(bsz, seq, A_HEADS, A_HEAD_DIM),
                ka.reshape(bsz, seq, A_KV_HEADS, A_HEAD_DIM),
                va.reshape(bsz, seq, A_KV_HEADS, A_HEAD_DIM),
                attn_sinks[j], rel_bias)
            yb = hgrn2_recurrence(qb, fb, ib, gb, lb_table[layer], hgrn_norm_g[j]).astype(x.dtype)
            mix = jnp.concatenate([ya, yb], axis=-1) @ w_out_ab[j]
        else:
            proj = x @ w_in_cd[j]
            uc, vc, ad, gd = jnp.split(proj, ODD_SPLITS, axis=-1)
            yc = chunked_spatial_gating(uc, vc, gmlp_ln_g[j], gmlp_ln_b[j], gmlp_w_s[j], gmlp_b_s[j])
            yd = conformer_conv(ad, gd, conv_w[j], conv_b[j], conv_ln_g[j], conv_ln_b[j])
            mix = jnp.concatenate([yc, yd], axis=-1) @ w_out_cd[j]
        x = layer_norm(DN_ALPHA * x + mix, ln_mix_g[layer], ln_mix_b[layer])
        ffn = hierarchical_moe(x, moe_w_group[layer], moe_b_group[layer], moe_w_router[layer],
                               moe_b_router[layer], moe_w1[layer], moe_w3[layer], moe_w2[layer])
        x = layer_norm(DN_ALPHA * x + ffn, ln_ffn_g[layer], ln_ffn_b[layer])
    return x
```

```python
import functools
import math

import jax
import jax.numpy as jnp
from jax import lax
from jax.experimental import pallas as pl
from jax.experimental.pallas import tpu as pltpu

D_MODEL = 2048
DEPTH = 2
A_HEADS = 16
A_KV_HEADS = 2
A_HEAD_DIM = 64
A_WINDOW = 128
A_BLOCK = 128
REL_BUCKETS = 32
REL_MAX_DIST = 128
B_HEADS = 8
B_DK = 128
B_DV = 128
C_GROUPS = 8
C_GROUP_DIM = 128
C_CHUNK = 128
D_CHANNELS = 1024
D_CONV = 31
A_QW = A_HEADS * A_HEAD_DIM
A_KVW = A_KV_HEADS * A_HEAD_DIM
B_KW = B_HEADS * B_DK
B_VW = B_HEADS * B_DV
C_W = C_GROUPS * C_GROUP_DIM
EVEN_IN = A_QW + 2 * A_KVW + 2 * B_KW + 2 * B_VW
ODD_IN = 2 * C_W + 2 * D_CHANNELS
N_GROUPS = 4
EXPERTS_PER_GROUP = 8
N_EXPERTS = N_GROUPS * EXPERTS_PER_GROUP
TOP_K = 2
D_EXPERT = 512
DN_ALPHA = (2 * DEPTH) ** 0.25
LN_EPS = 1e-5
RMS_EPS = 1e-6

LANES_V7X = 128
MIB = 1024 * 1024
VMEM_LIMIT_V7X = 56 * MIB

HGRN_CHUNK = 128
HGRN_SUB = 16
HGRN_ROWS = 512
MOE_TB = 256
ROUTE_W = LANES_V7X

BF16 = jnp.bfloat16
F32 = jnp.float32
NEG_INF = float("-inf")


def _cparams(n_axes, vmem_bytes=None):
    return pltpu.CompilerParams(dimension_semantics=("arbitrary",) * n_axes, vmem_limit_bytes=vmem_bytes)


def _layer_norm_rows(z, g, b):
    mu = jnp.mean(z, axis=-1, keepdims=True)
    zc = z - mu
    var = jnp.mean(zc * zc, axis=-1, keepdims=True)
    return zc * lax.rsqrt(var + LN_EPS) * g + b


def _dot_nt(a, b):
    return lax.dot_general(a, b, (((1,), (1,)), ((), ())), preferred_element_type=F32)


def _dot_tn(a, b):
    return lax.dot_general(a, b, (((0,), (0,)), ((), ())), preferred_element_type=F32)


def _mm_kernel(a_ref, w_ref, o_ref):
    a = a_ref[...].astype(BF16)
    o_ref[...] = jnp.dot(a, w_ref[...], preferred_element_type=F32).astype(o_ref.dtype)


def _matmul(a, w, tm, tn, out_dtype):
    m, k = a.shape
    n = w.shape[1]
    return pl.pallas_call(
        _mm_kernel,
        grid=(n // tn, m // tm),
        in_specs=[pl.BlockSpec((tm, k), lambda j, i: (i, 0)), pl.BlockSpec((k, tn), lambda j, i: (0, j))],
        out_specs=pl.BlockSpec((tm, tn), lambda j, i: (i, j)),
        out_shape=jax.ShapeDtypeStruct((m, n), out_dtype),
        compiler_params=_cparams(2, VMEM_LIMIT_V7X),
        name="proj_matmul",
    )(a, w)


def _bias_table_kernel(rb_ref, bucket_ref, o_ref):
    bucket = bucket_ref[...]
    for h in range(A_HEADS):
        acc = jnp.zeros(bucket.shape, F32)
        for bk in range(REL_BUCKETS):
            acc = jnp.where(bucket == bk, rb_ref[bk, h], acc)
        o_ref[h] = acc


def _bias_table(rel_bias, bucket):
    return pl.pallas_call(
        _bias_table_kernel,
        in_specs=[pl.BlockSpec(memory_space=pltpu.SMEM), pl.BlockSpec(memory_space=pltpu.VMEM)],
        out_specs=pl.BlockSpec(memory_space=pltpu.VMEM),
        out_shape=jax.ShapeDtypeStruct((A_HEADS,) + bucket.shape, F32),
        name="rel_bias_table",
    )(rel_bias, bucket)


def _t5_bucket_table():
    t_loc = jnp.arange(A_BLOCK, dtype=jnp.int32)[:, None]
    s_loc = jnp.arange(2 * A_BLOCK, dtype=jnp.int32)[None, :]
    dist = jnp.maximum(t_loc + A_BLOCK - s_loc, 0)
    max_exact = REL_BUCKETS // 2
    d = jnp.maximum(dist, 1).astype(F32)
    large = max_exact + (jnp.log(d / max_exact) / math.log(REL_MAX_DIST / max_exact)
                         * (REL_BUCKETS - max_exact)).astype(jnp.int32)
    large = jnp.minimum(large, REL_BUCKETS - 1)
    return jnp.where(dist < max_exact, dist, large)


def _attn_kernel(q_ref, kc_ref, vc_ref, kp_ref, vp_ref, bias_ref, sink_ref, o_ref):
    n = pl.program_id(1)
    blk = A_BLOCK
    k2 = jnp.concatenate([kp_ref[...], kc_ref[...]], axis=0)
    v2 = jnp.concatenate([vp_ref[...], vc_ref[...]], axis=0)
    k2r = pltpu.roll(k2, A_HEAD_DIM, 1)
    v2r = pltpu.roll(v2, A_HEAD_DIM, 1)
    lo = lax.broadcasted_iota(jnp.int32, k2.shape, 1) < A_HEAD_DIM
    zero = jnp.zeros_like(k2)

    def placed(x, xr, g, par):
        src = x if g == par else xr
        return (jnp.where(lo, src, zero) if par == 0 else jnp.where(lo, zero, src)).astype(BF16)

    kk = [[placed(k2, k2r, g, par) for par in range(2)] for g in range(A_KV_HEADS)]
    vv = [[placed(v2, v2r, g, par) for par in range(2)] for g in range(A_KV_HEADS)]

    t_loc = lax.broadcasted_iota(jnp.int32, (blk, 2 * blk), 0)
    s_loc = lax.broadcasted_iota(jnp.int32, (blk, 2 * blk), 1)
    dist = t_loc + blk - s_loc
    valid = (dist >= 0) & (dist < A_WINDOW) & ((s_loc >= blk) | (n > 0))

    heads_per_kv = A_HEADS // A_KV_HEADS
    for p in range(A_HEADS // 2):
        g = (2 * p) // heads_per_kv
        qp = (q_ref[:, p * 128:(p + 1) * 128] * (A_HEAD_DIM ** -0.5)).astype(BF16)
        acc = jnp.zeros((blk, 128), F32)
        for par in range(2):
            h = 2 * p + par
            sink = sink_ref[h]
            logits = jnp.where(valid, _dot_nt(qp, kk[g][par]) + bias_ref[h], NEG_INF)
            m = jnp.maximum(jnp.max(logits, axis=-1, keepdims=True), sink)
            e = jnp.exp(logits - m)
            den = jnp.sum(e, axis=-1, keepdims=True) + jnp.exp(sink - m)
            acc = acc + jnp.dot(e.astype(BF16), vv[g][par], preferred_element_type=F32) * (1.0 / den)
        o_ref[:, p * 128:(p + 1) * 128] = acc.astype(o_ref.dtype)


def _attention(proj, bias, sinks, bsz, seq):
    nb = seq // A_BLOCK
    kcol = A_QW // 128
    vcol = kcol + 1
    row = lambda b, n: b * nb + n
    prow = lambda b, n: b * nb + jnp.maximum(n - 1, 0)
    return pl.pallas_call(
        _attn_kernel,
        grid=(bsz, nb),
        in_specs=[
            pl.BlockSpec((A_BLOCK, A_QW), lambda b, n: (row(b, n), 0)),
            pl.BlockSpec((A_BLOCK, 128), lambda b, n: (row(b, n), kcol)),
            pl.BlockSpec((A_BLOCK, 128), lambda b, n: (row(b, n), vcol)),
            pl.BlockSpec((A_BLOCK, 128), lambda b, n: (prow(b, n), kcol)),
            pl.BlockSpec((A_BLOCK, 128), lambda b, n: (prow(b, n), vcol)),
            pl.BlockSpec((A_HEADS, A_BLOCK, 2 * A_BLOCK), lambda b, n: (0, 0, 0)),
            pl.BlockSpec(memory_space=pltpu.SMEM),
        ],
        out_specs=pl.BlockSpec((A_BLOCK, A_QW), lambda b, n: (row(b, n), 0)),
        out_shape=jax.ShapeDtypeStruct((bsz * seq, A_QW), BF16),
        compiler_params=_cparams(2, 32 * MIB),
        name="swa_attention",
    )(proj, proj, proj, proj, proj, bias, sinks)


def _hgrn_kernel(q_ref, f_ref, i_ref, g_ref, lb_ref, ng_ref, o_ref, st_ref):
    c, sub = HGRN_CHUNK, HGRN_SUB
    nsub = c // sub

    @pl.when(pl.program_id(2) == 0)
    def _():
        st_ref[...] = jnp.zeros_like(st_ref)

    lb = lb_ref[0]
    ng = ng_ref[0]
    rid = lax.broadcasted_iota(jnp.int32, (c, c), 0)
    cid = lax.broadcasted_iota(jnp.int32, (c, c), 1)
    blk_start = (rid // sub) * sub
    m_before = (cid < blk_start).astype(F32)
    m_within = ((cid >= blk_start) & (cid <= rid)).astype(F32)
    row = lax.broadcasted_iota(jnp.int32, (c, B_DK), 0)
    row_in_sub = row % sub
    ones = jnp.ones((B_DK, B_DV), BF16)
    hi = lax.Precision.HIGHEST

    def chunk(ci, carry):
        r0 = pl.multiple_of(ci * c, c)
        q = q_ref[pl.ds(r0, c), :]
        fl = f_ref[pl.ds(r0, c), :]
        v = i_ref[pl.ds(r0, c), :]
        gt = g_ref[pl.ds(r0, c), :]
        qf = q * jax.nn.sigmoid(q)
        f = lb + (1.0 - lb) * jax.nn.sigmoid(fl)
        logf = jnp.log(f)
        kin = 1.0 - f
        rr = jnp.dot(m_before, logf, precision=hi, preferred_element_type=F32)
        bq = jnp.dot(m_within, logf, precision=hi, preferred_element_type=F32)
        b = rr + bq
        qt = qf * jnp.exp(bq)
        st = st_ref[...]
        vb = v.astype(BF16)
        o = _dot_nt((qt * jnp.exp(rr)).astype(BF16), st.astype(BF16))

        parts = [jnp.zeros((sub, c), F32)]
        for i in range(1, nsub):
            r_i = rr[i * sub:i * sub + 1, :]
            kt = kin * jnp.exp(jnp.where(row < i * sub, r_i - b, NEG_INF))
            parts.append(_dot_nt(qt[i * sub:(i + 1) * sub, :].astype(BF16), kt.astype(BF16)))
        s_off = jnp.concatenate(parts, axis=0)
        o = o + jnp.dot(s_off.astype(BF16), vb, preferred_element_type=F32)

        for d in range(sub):
            if d == 0:
                term = qf * kin
                vd = v
            else:
                diff = bq - pltpu.roll(bq, d, 0)
                term = qf * pltpu.roll(kin, d, 0) * jnp.exp(jnp.where(row_in_sub >= d, diff, NEG_INF))
                vd = pltpu.roll(v, d, 0)
            o = o + jnp.dot(term.astype(BF16), ones, preferred_element_type=F32) * vd

        b_last = b[c - 1:c, :]
        k_dec = kin * jnp.exp(b_last - b)
        st_ref[...] = st * jnp.exp(b_last) + _dot_tn(vb, k_dec.astype(BF16))

        o = o * lax.rsqrt(jnp.mean(o * o, axis=-1, keepdims=True) + RMS_EPS)
        o_ref[pl.ds(r0, c), :] = (o * ng * (gt * jax.nn.sigmoid(gt))).astype(o_ref.dtype)
        return carry

    lax.fori_loop(0, HGRN_ROWS // c, chunk, 0)


def _hgrn(proj, lb, norm_g, bsz, seq):
    nr = seq // HGRN_ROWS
    c0 = (A_QW + 2 * A_KVW) // 128
    nh = B_HEADS
    spec = lambda off: pl.BlockSpec((HGRN_ROWS, 128), lambda b, h, r: (b * nr + r, off + h))
    vec = pl.BlockSpec((1, 1, 128), lambda b, h, r: (h, 0, 0))
    return pl.pallas_call(
        _hgrn_kernel,
        grid=(bsz, nh, nr),
        in_specs=[spec(c0), spec(c0 + nh), spec(c0 + 2 * nh), spec(c0 + 3 * nh), vec, vec],
        out_specs=pl.BlockSpec((HGRN_ROWS, 128), lambda b, h, r: (b * nr + r, h)),
        out_shape=jax.ShapeDtypeStruct((bsz * seq, B_VW), BF16),
        scratch_shapes=[pltpu.VMEM((B_DV, B_DK), F32)],
        compiler_params=_cparams(3, 32 * MIB),
        name="hgrn2",
    )(proj, proj, proj, proj, lb.reshape(nh, 1, 128), norm_g.reshape(nh, 1, 128))


def _outproj_kernel(ya_ref, yb_ref, wa_ref, wb_ref, x_ref, g_ref, b_ref, wrh_ref, wrl_ref, rb_ref, xo_ref, lg_ref):
    mix = jnp.dot(ya_ref[...], wa_ref[...], preferred_element_type=F32)
    mix = mix + jnp.dot(yb_ref[...], wb_ref[...], preferred_element_type=F32)
    y = _layer_norm_rows(DN_ALPHA * x_ref[...] + mix, g_ref[...], b_ref[...])
    xo_ref[...] = y
    y_hi = y.astype(BF16)
    y_lo = (y - y_hi.astype(F32)).astype(BF16)
    lg = jnp.dot(y_hi, wrh_ref[...], preferred_element_type=F32)
    lg = lg + jnp.dot(y_lo, wrh_ref[...], preferred_element_type=F32)
    lg = lg + jnp.dot(y_hi, wrl_ref[...], preferred_element_type=F32)
    lg_ref[...] = lg + rb_ref[...]


def _outproj_ln_route(ya, yb, w_out, x, g, b, wr_hi, wr_lo, rbias, tm=256):
    n = x.shape[0]
    ka = ya.shape[1]
    row = lambda width: pl.BlockSpec((tm, width), lambda i: (i, 0))
    const = lambda shape: pl.BlockSpec(shape, lambda i: (0, 0))
    return pl.pallas_call(
        _outproj_kernel,
        grid=(n // tm,),
        in_specs=[row(ka), row(ka),
                  pl.BlockSpec((ka, D_MODEL), lambda i: (0, 0)), pl.BlockSpec((ka, D_MODEL), lambda i: (1, 0)),
                  row(D_MODEL), const((1, D_MODEL)), const((1, D_MODEL)),
                  const((D_MODEL, ROUTE_W)), const((D_MODEL, ROUTE_W)), const((1, ROUTE_W))],
        out_specs=[row(D_MODEL), row(ROUTE_W)],
        out_shape=[jax.ShapeDtypeStruct((n, D_MODEL), F32), jax.ShapeDtypeStruct((n, ROUTE_W), F32)],
        compiler_params=_cparams(1, VMEM_LIMIT_V7X),
        name="outproj_ln_route",
    )(ya, yb, w_out, w_out, x, g.reshape(1, -1), b.reshape(1, -1), wr_hi, wr_lo, rbias)


def _moe_kernel(blk_e_ref, nused_ref, src_ref, nxt_ref, x_hbm, w1_ref, w3_ref, w2_ref, y_ref,
                xbuf, w1b, w3b, w2b, sem):
    s = pl.program_id(0)
    slot = s % 2
    nused = nused_ref[0]

    def row_copy(row, i, sl):
        return pltpu.make_async_copy(x_hbm.at[pl.ds(row, 1)], xbuf.at[sl, pl.ds(i, 1)], sem.at[sl])

    def start_gather(idx_ref, sl):
        def body(i, c):
            row_copy(idx_ref[0, 0, i], i, sl).start()
            return c
        lax.fori_loop(0, MOE_TB, body, 0)

    @pl.when(s == 0)
    def _():
        start_gather(src_ref, 0)

    @pl.when(s + 1 < nused)
    def _():
        start_gather(nxt_ref, 1 - slot)

    @pl.when(s >= nused)
    def _():
        y_ref[...] = jnp.zeros_like(y_ref)

    @pl.when(s < nused)
    def _():
        e = blk_e_ref[s]
        prev = blk_e_ref[jnp.maximum(s - 1, 0)]

        @pl.when((s == 0) | (e != prev))
        def _():
            w1b[...] = w1_ref[0].astype(BF16)
            w3b[...] = w3_ref[0].astype(BF16)
            w2b[...] = w2_ref[0].astype(BF16)

        def wait_body(i, c):
            row_copy(0, i, slot).wait()
            return c
        lax.fori_loop(0, MOE_TB, wait_body, 0)

        xb = xbuf[slot].astype(BF16)
        h1 = jnp.dot(xb, w1b[...], preferred_element_type=F32)
        h3 = jnp.dot(xb, w3b[...], preferred_element_type=F32)
        h = (h1 * jax.nn.sigmoid(h1) * h3).astype(BF16)
        y_ref[...] = jnp.dot(h, w2b[...], preferred_element_type=F32)


def _moe_experts(x, src, blk_e, nused, w1, w3, w2):
    n_blocks = blk_e.shape[0]
    src3 = src.reshape(n_blocks, 1, MOE_TB)
    smem_blk = lambda f: pl.BlockSpec((1, 1, MOE_TB), f, memory_space=pltpu.SMEM)
    wspec = lambda shape: pl.BlockSpec((1,) + shape, lambda s, be, nu: (be[jnp.minimum(s, nu[0] - 1)], 0, 0))
    grid_spec = pltpu.PrefetchScalarGridSpec(
        num_scalar_prefetch=2,
        grid=(n_blocks,),
        in_specs=[
            smem_blk(lambda s, be, nu: (s, 0, 0)),
            smem_blk(lambda s, be, nu: (jnp.minimum(s + 1, n_blocks - 1), 0, 0)),
            pl.BlockSpec(memory_space=pl.ANY),
            wspec((D_MODEL, D_EXPERT)), wspec((D_MODEL, D_EXPERT)), wspec((D_EXPERT, D_MODEL)),
        ],
        out_specs=pl.BlockSpec((MOE_TB, D_MODEL), lambda s, be, nu: (s, 0)),
        scratch_shapes=[
            pltpu.VMEM((2, MOE_TB, D_MODEL), F32),
            pltpu.VMEM((D_MODEL, D_EXPERT), BF16), pltpu.VMEM((D_MODEL, D_EXPERT), BF16),
            pltpu.VMEM((D_EXPERT, D_MODEL), BF16),
            pltpu.SemaphoreType.DMA((2,)),
        ],
    )
    return pl.pallas_call(
        _moe_kernel,
        grid_spec=grid_spec,
        out_shape=jax.ShapeDtypeStruct((n_blocks * MOE_TB, D_MODEL), F32),
        compiler_params=_cparams(1, VMEM_LIMIT_V7X),
        name="moe_experts",
    )(blk_e, nused, src3, src3, x, w1, w3, w2)


def _combine_kernel(pos_ref, nxt_ref, y_hbm, x_ref, ew_ref, g_ref, b_ref, o_ref, ybuf, sem, *, tm):
    s = pl.program_id(0)
    nsteps = pl.num_programs(0)
    slot = s % 2

    def row_copy(row, j, sl):
        return pltpu.make_async_copy(y_hbm.at[pl.ds(row, 1)], ybuf.at[sl, pl.ds(j, 1)], sem.at[sl])

    def start_gather(idx_ref, sl):
        def body(j, c):
            row_copy(idx_ref[0, 0, j], j, sl).start()
            return c
        lax.fori_loop(0, TOP_K * tm, body, 0)

    @pl.when(s == 0)
    def _():
        start_gather(pos_ref, 0)

    @pl.when(s + 1 < nsteps)
    def _():
        start_gather(nxt_ref, 1 - slot)

    def wait_body(j, c):
        row_copy(0, j, slot).wait()
        return c
    lax.fori_loop(0, TOP_K * tm, wait_body, 0)

    ew = ew_ref[...]
    ffn = ybuf[slot, 0:tm, :] * ew[:, 0:1] + ybuf[slot, tm:2 * tm, :] * ew[:, 1:2]
    o_ref[...] = _layer_norm_rows(DN_ALPHA * x_ref[...] + ffn, g_ref[...], b_ref[...])


def _moe_combine(y_sorted, pos, ew, x, g, b, tm=256):
    n = x.shape[0]
    nsteps = n // tm
    pos3 = pos.reshape(nsteps, tm, TOP_K).transpose(0, 2, 1).reshape(nsteps, 1, TOP_K * tm)
    smem_blk = lambda f: pl.BlockSpec((1, 1, TOP_K * tm), f, memory_space=pltpu.SMEM)
    row = lambda width: pl.BlockSpec((tm, width), lambda i: (i, 0))
    const = lambda shape: pl.BlockSpec(shape, lambda i: (0, 0))
    return pl.pallas_call(
        functools.partial(_combine_kernel, tm=tm),
        grid=(nsteps,),
        in_specs=[smem_blk(lambda i: (i, 0, 0)), smem_blk(lambda i: (jnp.minimum(i + 1, nsteps - 1), 0, 0)),
                  pl.BlockSpec(memory_space=pl.ANY), row(D_MODEL), row(ROUTE_W),
                  const((1, D_MODEL)), const((1, D_MODEL))],
        out_specs=row(D_MODEL),
        out_shape=jax.ShapeDtypeStruct((n, D_MODEL), F32),
        scratch_shapes=[pltpu.VMEM((2, TOP_K * tm, D_MODEL), F32), pltpu.SemaphoreType.DMA((2,))],
        compiler_params=_cparams(1, 40 * MIB),
        name="moe_combine",
    )(pos3, pos3, y_sorted, x, ew, g.reshape(1, -1), b.reshape(1, -1))


def _route(logits):
    n_tok = logits.shape[0]
    g_prob = jax.nn.softmax(logits[:, :N_GROUPS], axis=-1)
    g_w, g_idx = lax.top_k(g_prob, 1)
    e_logits = logits[:, N_GROUPS:N_GROUPS + N_EXPERTS].reshape(n_tok, N_GROUPS, EXPERTS_PER_GROUP)
    e_in = jnp.take_along_axis(e_logits, g_idx[:, :, None], axis=1)[:, 0]
    e_top, e_loc = lax.top_k(e_in, TOP_K)
    e_w = jax.nn.softmax(e_top, axis=-1) * g_w
    e_idx = g_idx * EXPERTS_PER_GROUP + e_loc

    m = n_tok * TOP_K
    flat_e = e_idx.reshape(m).astype(jnp.int32)
    flat_tok = jnp.repeat(jnp.arange(n_tok, dtype=jnp.int32), TOP_K)
    order = jnp.argsort(flat_e)
    se, stok = flat_e[order], flat_tok[order]
    counts = jnp.bincount(flat_e, length=N_EXPERTS).astype(jnp.int32)
    starts = jnp.cumsum(counts) - counts
    pcounts = (counts + MOE_TB - 1) // MOE_TB * MOE_TB
    pends = jnp.cumsum(pcounts)
    pstarts = pends - pcounts
    n_blocks = -(-(m + N_EXPERTS * (MOE_TB - 1)) // MOE_TB)
    n_pad = n_blocks * MOE_TB
    dest = pstarts[se] + (jnp.arange(m, dtype=jnp.int32) - starts[se])
    src = jnp.zeros((n_pad,), jnp.int32).at[dest].set(stok)
    pos = jnp.zeros((m,), jnp.int32).at[order].set(dest).reshape(n_tok, TOP_K)
    blk_start = jnp.arange(n_blocks, dtype=jnp.int32) * MOE_TB
    blk_e = jnp.minimum(jnp.searchsorted(pends, blk_start, side="right"), N_EXPERTS - 1).astype(jnp.int32)
    nused = (pends[-1:] // MOE_TB).astype(jnp.int32)
    ew = jnp.zeros((n_tok, ROUTE_W), F32).at[:, :TOP_K].set(e_w)
    return src, pos, ew, blk_e, nused


def _router_weights(w_group, b_group, w_router, b_router):
    w = jnp.zeros((D_MODEL, ROUTE_W), F32)
    w = w.at[:, :N_GROUPS].set(w_group).at[:, N_GROUPS:N_GROUPS + N_EXPERTS].set(w_router)
    w_hi = w.astype(BF16)
    w_lo = (w - w_hi.astype(F32)).astype(BF16)
    rb = jnp.zeros((1, ROUTE_W), F32)
    rb = rb.at[0, :N_GROUPS].set(b_group).at[0, N_GROUPS:N_GROUPS + N_EXPERTS].set(b_router)
    return w_hi, w_lo, rb


def _gmlp_kernel(u_ref, v_ref, g_ref, b_ref, w_ref, bs_ref, o_ref, *, chunks):
    for ci in range(chunks):
        rows = slice(ci * C_CHUNK, (ci + 1) * C_CHUNK)
        u = jax.nn.gelu(u_ref[rows, :])
        v = _layer_norm_rows(jax.nn.gelu(v_ref[rows, :]), g_ref[...], b_ref[...]).astype(BF16)
        for gi in range(C_GROUPS):
            cols = slice(gi * C_GROUP_DIM, (gi + 1) * C_GROUP_DIM)
            mixed = jnp.dot(w_ref[gi], v[:, cols], preferred_element_type=F32) + bs_ref[:, cols]
            o_ref[rows, cols] = (u[:, cols] * mixed).astype(o_ref.dtype)


def _gmlp(proj, ln_g, ln_b, w_s, b_s, chunks=2):
    n = proj.shape[0]
    tm = chunks * C_CHUNK
    w = (w_s * jnp.tril(jnp.ones((C_CHUNK, C_CHUNK), w_s.dtype))).astype(BF16)
    bs_full = jnp.repeat(b_s.T, C_GROUP_DIM, axis=1)
    const2 = lambda shape: pl.BlockSpec(shape, lambda i: (0, 0))
    return pl.pallas_call(
        functools.partial(_gmlp_kernel, chunks=chunks),
        grid=(n // tm,),
        in_specs=[pl.BlockSpec((tm, C_W), lambda i: (i, 0)), pl.BlockSpec((tm, C_W), lambda i: (i, 1)),
                  const2((1, C_W)), const2((1, C_W)),
                  pl.BlockSpec((C_GROUPS, C_CHUNK, C_CHUNK), lambda i: (0, 0, 0)), const2((C_CHUNK, C_W))],
        out_specs=pl.BlockSpec((tm, C_W), lambda i: (i, 0)),
        out_shape=jax.ShapeDtypeStruct((n, C_W), BF16),
        compiler_params=_cparams(1, 32 * MIB),
        name="gmlp_gating",
    )(proj, proj, ln_g.reshape(1, -1), ln_b.reshape(1, -1), w, bs_full)


CONV_HIST = 32


def _conv_kernel(a_ref, gt_ref, ap_ref, gp_ref, w_ref, cb_ref, g_ref, b_ref, o_ref, hbuf, *, ts):
    i = pl.program_id(1)
    hist = ap_ref[...] * jax.nn.sigmoid(gp_ref[...])
    hbuf[0:CONV_HIST, :] = jnp.where(i > 0, hist, jnp.zeros_like(hist))
    hbuf[CONV_HIST:CONV_HIST + ts, :] = a_ref[...] * jax.nn.sigmoid(gt_ref[...])
    off = CONV_HIST - (D_CONV - 1)
    acc = jnp.zeros((ts, D_CHANNELS), F32) + cb_ref[...]
    for j in range(D_CONV):
        acc = acc + w_ref[j:j + 1, :] * hbuf[off + j:off + j + ts, :]
    y = _layer_norm_rows(acc, g_ref[...], b_ref[...])
    o_ref[...] = (y * jax.nn.sigmoid(y)).astype(o_ref.dtype)


def _conformer_conv(proj, conv_w, conv_b, ln_g, ln_b, bsz, seq, ts=256):
    nt = seq // ts
    acol = 2 * C_W // D_CHANNELS
    gcol = acol + 1
    hb = ts // CONV_HIST
    cur = lambda col: pl.BlockSpec((ts, D_CHANNELS), lambda b, i: (b * nt + i, col))
    prev = lambda col: pl.BlockSpec((CONV_HIST, D_CHANNELS),
                                    lambda b, i: (jnp.maximum((b * nt + i) * hb - 1, 0), col))
    const2 = lambda shape: pl.BlockSpec(shape, lambda b, i: (0, 0))
    return pl.pallas_call(
        functools.partial(_conv_kernel, ts=ts),
        grid=(bsz, nt),
        in_specs=[cur(acol), cur(gcol), prev(acol), prev(gcol),
                  const2((D_CONV, D_CHANNELS)), const2((1, D_CHANNELS)), const2((1, D_CHANNELS)),
                  const2((1, D_CHANNELS))],
        out_specs=pl.BlockSpec((ts, D_CHANNELS), lambda b, i: (b * nt + i, 0)),
        out_shape=jax.ShapeDtypeStruct((bsz * seq, D_CHANNELS), BF16),
        scratch_shapes=[pltpu.VMEM((CONV_HIST + ts, D_CHANNELS), F32)],
        compiler_params=_cparams(2, 32 * MIB),
        name="conformer_conv",
    )(proj, proj, proj, proj, conv_w, conv_b.reshape(1, -1), ln_g.reshape(1, -1), ln_b.reshape(1, -1))


def _moe_layer(x1, logits, w1, w3, w2, g, b):
    src, pos, ew, blk_e, nused = _route(logits)
    y_sorted = _moe_experts(x1, src, blk_e, nused, w1, w3, w2)
    return _moe_combine(y_sorted, pos, ew, x1, g, b)


def kernel(x, w_in_ab, attn_sinks, rel_bias, hgrn_lb_logits, hgrn_norm_g, w_out_ab, w_in_cd, gmlp_ln_g, gmlp_ln_b, gmlp_w_s, gmlp_b_s, conv_w, conv_b, conv_ln_g, conv_ln_b, w_out_cd, ln_mix_g, ln_mix_b, ln_ffn_g, ln_ffn_b, moe_w_group, moe_b_group, moe_w_router, moe_b_router, moe_w1, moe_w3, moe_w2):
    bsz, seq = x.shape[0], x.shape[1]
    n_tok = bsz * seq
    xf = x.reshape(n_tok, D_MODEL)
    lb_table = jnp.cumsum(jax.nn.softmax(hgrn_lb_logits.astype(F32), axis=0), axis=0)
    bias = _bias_table(rel_bias.astype(F32), _t5_bucket_table())

    for layer in range(DEPTH):
        j = layer // 2
        if layer % 2 == 0:
            proj = _matmul(xf, w_in_ab[j].astype(BF16), 512, EVEN_IN // 3, F32)
            ya = _attention(proj, bias, attn_sinks[j].astype(F32), bsz, seq)
            yb = _hgrn(proj, lb_table[layer], hgrn_norm_g[j].astype(F32), bsz, seq)
            w_out = w_out_ab[j]
        else:
            proj = _matmul(xf, w_in_cd[j].astype(BF16), 512, ODD_IN // 2, F32)
            ya = _gmlp(proj, gmlp_ln_g[j], gmlp_ln_b[j], gmlp_w_s[j], gmlp_b_s[j])
            yb = _conformer_conv(proj, conv_w[j], conv_b[j], conv_ln_g[j], conv_ln_b[j], bsz, seq)
            w_out = w_out_cd[j]
        wr_hi, wr_lo, rbias = _router_weights(moe_w_group[layer], moe_b_group[layer],
                                              moe_w_router[layer], moe_b_router[layer])
        x1, logits = _outproj_ln_route(ya, yb, w_out.astype(BF16), xf, ln_mix_g[layer], ln_mix_b[layer],
                                       wr_hi, wr_lo, rbias)
        xf = _moe_layer(x1, logits, moe_w1[layer], moe_w3[layer], moe_w2[layer],
                        ln_ffn_g[layer], ln_ffn_b[layer])
    return xf.reshape(bsz, seq, D_MODEL)
```

```python
import functools
import math

import jax
import jax.numpy as jnp
from jax import lax
from jax.experimental import pallas as pl
from jax.experimental.pallas import tpu as pltpu

D_MODEL = 2048
DEPTH = 2
A_HEADS = 16
A_KV_HEADS = 2
A_HEAD_DIM = 64
A_WINDOW = 128
A_BLOCK = 128
REL_BUCKETS = 32
REL_MAX_DIST = 128
B_HEADS = 8
B_DK = 128
B_DV = 128
C_GROUPS = 8
C_GROUP_DIM = 128
C_CHUNK = 128
D_CHANNELS = 1024
D_CONV = 31
A_QW = A_HEADS * A_HEAD_DIM
A_KVW = A_KV_HEADS * A_HEAD_DIM
B_KW = B_HEADS * B_DK
B_VW = B_HEADS * B_DV
C_W = C_GROUPS * C_GROUP_DIM
EVEN_IN = A_QW + 2 * A_KVW + 2 * B_KW + 2 * B_VW
ODD_IN = 2 * C_W + 2 * D_CHANNELS
N_GROUPS = 4
EXPERTS_PER_GROUP = 8
N_EXPERTS = N_GROUPS * EXPERTS_PER_GROUP
TOP_K = 2
D_EXPERT = 512
DN_ALPHA = (2 * DEPTH) ** 0.25
LN_EPS = 1e-5
RMS_EPS = 1e-6

LANES_V7X = 128
SUBLANES_V7X = 8
MIB = 1024 * 1024
VMEM_LIMIT_V7X = 56 * MIB

HGRN_CHUNK = 128
HGRN_SUB = 16
HGRN_ROWS = 512
MOE_TB = 256
ROUTE_W = LANES_V7X
HALF = D_MODEL // 2
TOK_TILE = HALF // LANES_V7X
assert TOK_TILE == SUBLANES_V7X

BF16 = jnp.bfloat16
F32 = jnp.float32
U32 = jnp.uint32
NEG_INF = float("-inf")
RT_W0, RT_W1, RT_E0, RT_E1, RT_R0, RT_R1 = range(6)


def _cparams(n_axes, vmem_bytes=None):
    return pltpu.CompilerParams(dimension_semantics=("arbitrary",) * n_axes, vmem_limit_bytes=vmem_bytes)


def _layer_norm_rows(z, g, b):
    mu = jnp.mean(z, axis=-1, keepdims=True)
    zc = z - mu
    var = jnp.mean(zc * zc, axis=-1, keepdims=True)
    return zc * lax.rsqrt(var + LN_EPS) * g + b


def _dot_nt(a, b):
    return lax.dot_general(a, b, (((1,), (1,)), ((), ())), preferred_element_type=F32)


def _dot_tn(a, b):
    return lax.dot_general(a, b, (((0,), (0,)), ((), ())), preferred_element_type=F32)


def _pack_rows(y):
    lo = lax.bitcast_convert_type(y[:, :HALF].astype(BF16).astype(F32), U32) >> 16
    hi = lax.bitcast_convert_type(y[:, HALF:].astype(BF16).astype(F32), U32) & jnp.uint32(0xFFFF0000)
    return lo | hi


def _unpack_rows(p):
    lo = lax.bitcast_convert_type(p << 16, F32)
    hi = lax.bitcast_convert_type(p & jnp.uint32(0xFFFF0000), F32)
    return jnp.concatenate([lo, hi], axis=1)


def _store_token_tiles(ref, base, packed):
    rows = packed.shape[0]
    for c in range(TOK_TILE):
        ref[pl.ds(base + c, rows, stride=TOK_TILE), :] = packed[:, c * LANES_V7X:(c + 1) * LANES_V7X]


def _load_token_tiles(ref, base, rows, lead=None):
    parts = []
    for c in range(TOK_TILE):
        idx = (pl.ds(base + c, rows, stride=TOK_TILE), slice(None))
        parts.append(ref[idx] if lead is None else ref[(lead,) + idx])
    return jnp.concatenate(parts, axis=1)


def _mm_kernel(a_ref, w_ref, o_ref):
    a = a_ref[...].astype(BF16)
    o_ref[...] = jnp.dot(a, w_ref[...], preferred_element_type=F32).astype(o_ref.dtype)


def _matmul(a, w, tm, tn, out_dtype):
    m, k = a.shape
    n = w.shape[1]
    return pl.pallas_call(
        _mm_kernel,
        grid=(n // tn, m // tm),
        in_specs=[pl.BlockSpec((tm, k), lambda j, i: (i, 0)), pl.BlockSpec((k, tn), lambda j, i: (0, j))],
        out_specs=pl.BlockSpec((tm, tn), lambda j, i: (i, j)),
        out_shape=jax.ShapeDtypeStruct((m, n), out_dtype),
        compiler_params=_cparams(2, VMEM_LIMIT_V7X),
        name="proj_matmul",
    )(a, w)


def _bias_table_kernel(rb_ref, bucket_ref, o_ref):
    bucket = bucket_ref[...]
    for h in range(A_HEADS):
        acc = jnp.zeros(bucket.shape, F32)
        for bk in range(REL_BUCKETS):
            acc = jnp.where(bucket == bk, rb_ref[bk, h], acc)
        o_ref[h] = acc


def _bias_table(rel_bias, bucket):
    return pl.pallas_call(
        _bias_table_kernel,
        in_specs=[pl.BlockSpec(memory_space=pltpu.SMEM), pl.BlockSpec(memory_space=pltpu.VMEM)],
        out_specs=pl.BlockSpec(memory_space=pltpu.VMEM),
        out_shape=jax.ShapeDtypeStruct((A_HEADS,) + bucket.shape, F32),
        name="rel_bias_table",
    )(rel_bias, bucket)


def _t5_bucket_table():
    t_loc = jnp.arange(A_BLOCK, dtype=jnp.int32)[:, None]
    s_loc = jnp.arange(2 * A_BLOCK, dtype=jnp.int32)[None, :]
    dist = jnp.maximum(t_loc + A_BLOCK - s_loc, 0)
    max_exact = REL_BUCKETS // 2
    d = jnp.maximum(dist, 1).astype(F32)
    large = max_exact + (jnp.log(d / max_exact) / math.log(REL_MAX_DIST / max_exact)
                         * (REL_BUCKETS - max_exact)).astype(jnp.int32)
    large = jnp.minimum(large, REL_BUCKETS - 1)
    return jnp.where(dist < max_exact, dist, large)


def _attn_kernel(q_ref, kc_ref, vc_ref, kp_ref, vp_ref, bias_ref, sink_ref, o_ref):
    n = pl.program_id(1)
    blk = A_BLOCK
    k2 = jnp.concatenate([kp_ref[...], kc_ref[...]], axis=0)
    v2 = jnp.concatenate([vp_ref[...], vc_ref[...]], axis=0)
    k2r = pltpu.roll(k2, A_HEAD_DIM, 1)
    v2r = pltpu.roll(v2, A_HEAD_DIM, 1)
    lo = lax.broadcasted_iota(jnp.int32, k2.shape, 1) < A_HEAD_DIM
    zero = jnp.zeros_like(k2)

    def placed(x, xr, g, par):
        src = x if g == par else xr
        return (jnp.where(lo, src, zero) if par == 0 else jnp.where(lo, zero, src)).astype(BF16)

    kk = [[placed(k2, k2r, g, par) for par in range(2)] for g in range(A_KV_HEADS)]
    vv = [[placed(v2, v2r, g, par) for par in range(2)] for g in range(A_KV_HEADS)]

    t_loc = lax.broadcasted_iota(jnp.int32, (blk, 2 * blk), 0)
    s_loc = lax.broadcasted_iota(jnp.int32, (blk, 2 * blk), 1)
    dist = t_loc + blk - s_loc
    valid = (dist >= 0) & (dist < A_WINDOW) & ((s_loc >= blk) | (n > 0))

    heads_per_kv = A_HEADS // A_KV_HEADS
    for p in range(A_HEADS // 2):
        g = (2 * p) // heads_per_kv
        qp = (q_ref[:, p * 128:(p + 1) * 128] * (A_HEAD_DIM ** -0.5)).astype(BF16)
        acc = jnp.zeros((blk, 128), F32)
        for par in range(2):
            h = 2 * p + par
            sink = sink_ref[h]
            logits = jnp.where(valid, _dot_nt(qp, kk[g][par]) + bias_ref[h], NEG_INF)
            m = jnp.maximum(jnp.max(logits, axis=-1, keepdims=True), sink)
            e = jnp.exp(logits - m)
            den = jnp.sum(e, axis=-1, keepdims=True) + jnp.exp(sink - m)
            acc = acc + jnp.dot(e.astype(BF16), vv[g][par], preferred_element_type=F32) * (1.0 / den)
        o_ref[:, p * 128:(p + 1) * 128] = acc.astype(o_ref.dtype)


def _attention(proj, bias, sinks, bsz, seq):
    nb = seq // A_BLOCK
    kcol = A_QW // 128
    vcol = kcol + 1
    row = lambda b, n: b * nb + n
    prow = lambda b, n: b * nb + jnp.maximum(n - 1, 0)
    return pl.pallas_call(
        _attn_kernel,
        grid=(bsz, nb),
        in_specs=[
            pl.BlockSpec((A_BLOCK, A_QW), lambda b, n: (row(b, n), 0)),
            pl.BlockSpec((A_BLOCK, 128), lambda b, n: (row(b, n), kcol)),
            pl.BlockSpec((A_BLOCK, 128), lambda b, n: (row(b, n), vcol)),
            pl.BlockSpec((A_BLOCK, 128), lambda b, n: (prow(b, n), kcol)),
            pl.BlockSpec((A_BLOCK, 128), lambda b, n: (prow(b, n), vcol)),
            pl.BlockSpec((A_HEADS, A_BLOCK, 2 * A_BLOCK), lambda b, n: (0, 0, 0)),
            pl.BlockSpec(memory_space=pltpu.SMEM),
        ],
        out_specs=pl.BlockSpec((A_BLOCK, A_QW), lambda b, n: (row(b, n), 0)),
        out_shape=jax.ShapeDtypeStruct((bsz * seq, A_QW), BF16),
        compiler_params=_cparams(2, 32 * MIB),
        name="swa_attention",
    )(proj, proj, proj, proj, proj, bias, sinks)


def _hgrn_kernel(q_ref, f_ref, i_ref, g_ref, lb_ref, ng_ref, o_ref, st_ref):
    c, sub = HGRN_CHUNK, HGRN_SUB
    nsub = c // sub

    @pl.when(pl.program_id(2) == 0)
    def _():
        st_ref[...] = jnp.zeros_like(st_ref)

    lb = lb_ref[0]
    ng = ng_ref[0]
    rid = lax.broadcasted_iota(jnp.int32, (c, c), 0)
    cid = lax.broadcasted_iota(jnp.int32, (c, c), 1)
    blk_start = (rid // sub) * sub
    m_before = (cid < blk_start).astype(F32)
    m_within = ((cid >= blk_start) & (cid <= rid)).astype(F32)
    row = lax.broadcasted_iota(jnp.int32, (c, B_DK), 0)
    row_in_sub = row % sub
    ones = jnp.ones((B_DK, B_DV), BF16)
    hi = lax.Precision.HIGHEST

    def chunk(ci, carry):
        r0 = pl.multiple_of(ci * c, c)
        q = q_ref[pl.ds(r0, c), :]
        fl = f_ref[pl.ds(r0, c), :]
        v = i_ref[pl.ds(r0, c), :]
        gt = g_ref[pl.ds(r0, c), :]
        qf = q * jax.nn.sigmoid(q)
        f = lb + (1.0 - lb) * jax.nn.sigmoid(fl)
        logf = jnp.log(f)
        kin = 1.0 - f
        rr = jnp.dot(m_before, logf, precision=hi, preferred_element_type=F32)
        bq = jnp.dot(m_within, logf, precision=hi, preferred_element_type=F32)
        b = rr + bq
        qt = qf * jnp.exp(bq)
        st = st_ref[...]
        vb = v.astype(BF16)
        o = _dot_nt((qt * jnp.exp(rr)).astype(BF16), st.astype(BF16))

        parts = [jnp.zeros((sub, c), F32)]
        for i in range(1, nsub):
            r_i = rr[i * sub:i * sub + 1, :]
            kt = kin * jnp.exp(jnp.where(row < i * sub, r_i - b, NEG_INF))
            parts.append(_dot_nt(qt[i * sub:(i + 1) * sub, :].astype(BF16), kt.astype(BF16)))
        s_off = jnp.concatenate(parts, axis=0)
        o = o + jnp.dot(s_off.astype(BF16), vb, preferred_element_type=F32)

        for d in range(sub):
            if d == 0:
                term = qf * kin
                vd = v
            else:
                diff = bq - pltpu.roll(bq, d, 0)
                term = qf * pltpu.roll(kin, d, 0) * jnp.exp(jnp.where(row_in_sub >= d, diff, NEG_INF))
                vd = pltpu.roll(v, d, 0)
            o = o + jnp.dot(term.astype(BF16), ones, preferred_element_type=F32) * vd

        b_last = b[c - 1:c, :]
        k_dec = kin * jnp.exp(b_last - b)
        st_ref[...] = st * jnp.exp(b_last) + _dot_tn(vb, k_dec.astype(BF16))

        o = o * lax.rsqrt(jnp.mean(o * o, axis=-1, keepdims=True) + RMS_EPS)
        o_ref[pl.ds(r0, c), :] = (o * ng * (gt * jax.nn.sigmoid(gt))).astype(o_ref.dtype)
        return carry

    lax.fori_loop(0, HGRN_ROWS // c, chunk, 0)


def _hgrn(proj, lb, norm_g, bsz, seq):
    nr = seq // HGRN_ROWS
    c0 = (A_QW + 2 * A_KVW) // 128
    nh = B_HEADS
    spec = lambda off: pl.BlockSpec((HGRN_ROWS, 128), lambda b, h, r: (b * nr + r, off + h))
    vec = pl.BlockSpec((1, 1, 128), lambda b, h, r: (h, 0, 0))
    return pl.pallas_call(
        _hgrn_kernel,
        grid=(bsz, nh, nr),
        in_specs=[spec(c0), spec(c0 + nh), spec(c0 + 2 * nh), spec(c0 + 3 * nh), vec, vec],
        out_specs=pl.BlockSpec((HGRN_ROWS, 128), lambda b, h, r: (b * nr + r, h)),
        out_shape=jax.ShapeDtypeStruct((bsz * seq, B_VW), BF16),
        scratch_shapes=[pltpu.VMEM((B_DV, B_DK), F32)],
        compiler_params=_cparams(3, 32 * MIB),
        name="hgrn2",
    )(proj, proj, proj, proj, lb.reshape(nh, 1, 128), norm_g.reshape(nh, 1, 128))


def _outproj_kernel(ya_ref, yb_ref, wa_ref, wb_ref, x_ref, g_ref, b_ref, wrh_ref, wrl_ref, rb_ref,
                    xo_ref, xp_ref, lg_ref):
    mix = jnp.dot(ya_ref[...], wa_ref[...], preferred_element_type=F32)
    mix = mix + jnp.dot(yb_ref[...], wb_ref[...], preferred_element_type=F32)
    y = _layer_norm_rows(DN_ALPHA * x_ref[...] + mix, g_ref[...], b_ref[...])
    xo_ref[...] = y
    _store_token_tiles(xp_ref, 0, _pack_rows(y))
    y_hi = y.astype(BF16)
    y_lo = (y - y_hi.astype(F32)).astype(BF16)
    lg = jnp.dot(y_hi, wrh_ref[...], preferred_element_type=F32)
    lg = lg + jnp.dot(y_lo, wrh_ref[...], preferred_element_type=F32)
    lg = lg + jnp.dot(y_hi, wrl_ref[...], preferred_element_type=F32)
    lg_ref[...] = lg + rb_ref[...]


def _outproj_ln_route(ya, yb, w_out, x, g, b, wr_hi, wr_lo, rbias, tm=256):
    n = x.shape[0]
    ka = ya.shape[1]
    row = lambda width: pl.BlockSpec((tm, width), lambda i: (i, 0))
    const = lambda shape: pl.BlockSpec(shape, lambda i: (0, 0))
    return pl.pallas_call(
        _outproj_kernel,
        grid=(n // tm,),
        in_specs=[row(ka), row(ka),
                  pl.BlockSpec((ka, D_MODEL), lambda i: (0, 0)), pl.BlockSpec((ka, D_MODEL), lambda i: (1, 0)),
                  row(D_MODEL), const((1, D_MODEL)), const((1, D_MODEL)),
                  const((D_MODEL, ROUTE_W)), const((D_MODEL, ROUTE_W)), const((1, ROUTE_W))],
        out_specs=[row(D_MODEL), pl.BlockSpec((tm * TOK_TILE, LANES_V7X), lambda i: (i, 0)), row(ROUTE_W)],
        out_shape=[jax.ShapeDtypeStruct((n, D_MODEL), F32),
                   jax.ShapeDtypeStruct((n * TOK_TILE, LANES_V7X), U32),
                   jax.ShapeDtypeStruct((n, ROUTE_W), F32)],
        compiler_params=_cparams(1, VMEM_LIMIT_V7X),
        name="outproj_ln_route",
    )(ya, yb, w_out, w_out, x, g.reshape(1, -1), b.reshape(1, -1), wr_hi, wr_lo, rbias)


def _router_weights(w_group, b_group, w_router, b_router):
    w = jnp.zeros((D_MODEL, ROUTE_W), F32)
    w = w.at[:, :N_GROUPS].set(w_group).at[:, N_GROUPS:N_GROUPS + N_EXPERTS].set(w_router)
    w_hi = w.astype(BF16)
    w_lo = (w - w_hi.astype(F32)).astype(BF16)
    rb = jnp.zeros((1, ROUTE_W), F32)
    rb = rb.at[0, :N_GROUPS].set(b_group).at[0, N_GROUPS:N_GROUPS + N_EXPERTS].set(b_router)
    return w_hi, w_lo, rb


def _route_kernel(lg_ref, rt_ref, cnt_ref, run_ref):
    @pl.when(pl.program_id(0) == 0)
    def _():
        run_ref[...] = jnp.zeros_like(run_ref)

    lg = lg_ref[...]
    tm = lg.shape[0]
    lane = lax.broadcasted_iota(jnp.int32, lg.shape, 1)
    sentinel = jnp.int32(ROUTE_W)
    rowmax = lambda mask: jnp.max(jnp.where(mask, lg, NEG_INF), axis=-1, keepdims=True)
    first = lambda mask: jnp.min(jnp.where(mask, lane, sentinel), axis=-1, keepdims=True)

    is_g = lane < N_GROUPS
    gmax = rowmax(is_g)
    g_idx = first(is_g & (lg == gmax))
    g_w = 1.0 / jnp.sum(jnp.where(is_g, jnp.exp(lg - gmax), 0.0), axis=-1, keepdims=True)

    e_lane = lane - N_GROUPS
    sel = (e_lane >= 0) & (e_lane < N_EXPERTS) & ((e_lane >> 3) == g_idx)
    m1 = rowmax(sel)
    i1 = first(sel & (lg == m1))
    sel2 = sel & (lane != i1)
    m2 = rowmax(sel2)
    i2 = first(sel2 & (lg == m2))
    ex = jnp.exp(m2 - m1)
    w0 = g_w / (1.0 + ex)
    w1 = g_w * ex / (1.0 + ex)

    oh0 = lane == i1
    oh1 = lane == i2
    both = (oh0 | oh1).astype(F32)
    rid = lax.broadcasted_iota(jnp.int32, (tm, tm), 0)
    cid = lax.broadcasted_iota(jnp.int32, (tm, tm), 1)
    before = jnp.dot((cid < rid).astype(BF16), both.astype(BF16), preferred_element_type=F32) + run_ref[...]
    rank0 = jnp.sum(jnp.where(oh0, before, 0.0), axis=-1, keepdims=True)
    rank1 = jnp.sum(jnp.where(oh1, before, 0.0), axis=-1, keepdims=True)
    run = run_ref[...] + jnp.sum(both, axis=0, keepdims=True)
    run_ref[...] = run
    cnt_ref[...] = run

    slab = jnp.zeros(lg.shape, F32)
    for ln, val in ((RT_W0, w0), (RT_W1, w1), (RT_E0, (i1 - N_GROUPS).astype(F32)),
                    (RT_E1, (i2 - N_GROUPS).astype(F32)), (RT_R0, rank0), (RT_R1, rank1)):
        slab = jnp.where(lane == ln, val, slab)
    rt_ref[...] = slab


assert EXPERTS_PER_GROUP == 8


def _route(logits, tm=256):
    n = logits.shape[0]
    return pl.pallas_call(
        _route_kernel,
        grid=(n // tm,),
        in_specs=[pl.BlockSpec((tm, ROUTE_W), lambda i: (i, 0))],
        out_specs=[pl.BlockSpec((tm, ROUTE_W), lambda i: (i, 0)), pl.BlockSpec((1, ROUTE_W), lambda i: (0, 0))],
        out_shape=[jax.ShapeDtypeStruct((n, ROUTE_W), F32), jax.ShapeDtypeStruct((1, ROUTE_W), F32)],
        scratch_shapes=[pltpu.VMEM((1, ROUTE_W), F32)],
        compiler_params=_cparams(1),
        name="moe_router",
    )(logits)


def _block_layout(rt, cnt, n_tok):
    m = n_tok * TOP_K
    counts = cnt[0, N_GROUPS:N_GROUPS + N_EXPERTS].astype(jnp.int32)
    pcounts = (counts + MOE_TB - 1) // MOE_TB * MOE_TB
    pends = jnp.cumsum(pcounts)
    pstarts = pends - pcounts
    n_blocks = -(-(m + N_EXPERTS * (MOE_TB - 1)) // MOE_TB)
    e_idx = rt[:, RT_E0:RT_E1 + 1].astype(jnp.int32)
    rank = rt[:, RT_R0:RT_R1 + 1].astype(jnp.int32)
    onehot = e_idx[:, :, None] == jnp.arange(N_EXPERTS, dtype=jnp.int32)
    pos = jnp.sum(jnp.where(onehot, pstarts, 0), axis=-1) + rank
    blk_start = jnp.arange(n_blocks, dtype=jnp.int32) * MOE_TB
    blk_e = jnp.minimum(jnp.sum(blk_start[:, None] >= pends[None, :], axis=-1), N_EXPERTS - 1).astype(jnp.int32)
    nused = (pends[-1:] // MOE_TB).astype(jnp.int32)
    return pos, blk_e, nused, n_blocks


def _step_indices(pos, tm):
    nsteps = pos.shape[0] // tm
    return pos.reshape(nsteps, tm, TOP_K).transpose(0, 2, 1).reshape(nsteps, 1, TOP_K * tm)


def _tile_rows(i):
    return pl.ds(pl.multiple_of(i * TOK_TILE, TOK_TILE), TOK_TILE)


def _dispatch_kernel(pos_ref, xp_hbm, init_hbm, xs_hbm, sem, *, tm):
    del init_hbm
    s = pl.program_id(0)
    nsteps = pl.num_programs(0)
    slot = s % 2

    def copy(tok, dst, sl):
        return pltpu.make_async_copy(xp_hbm.at[_tile_rows(tok)], xs_hbm.at[_tile_rows(dst)], sem.at[sl])

    def start(j, c):
        for k in range(TOP_K):
            copy(s * tm + j, pos_ref[0, 0, k * tm + j], slot).start()
        return c
    lax.fori_loop(0, tm, start, 0, unroll=8)

    def drain(sl):
        def body(j, c):
            copy(0, 0, sl).wait()
            return c
        lax.fori_loop(0, TOP_K * tm, body, 0, unroll=8)

    @pl.when(s > 0)
    def _():
        drain(1 - slot)

    @pl.when(s == nsteps - 1)
    def _():
        drain(slot)


def _moe_dispatch(xp, pos, n_pad, tm=256):
    n = pos.shape[0]
    pos3 = _step_indices(pos, tm)
    init = jnp.zeros((n_pad * TOK_TILE, LANES_V7X), U32)
    return pl.pallas_call(
        functools.partial(_dispatch_kernel, tm=tm),
        grid=(n // tm,),
        in_specs=[pl.BlockSpec((1, 1, TOP_K * tm), lambda i: (i, 0, 0), memory_space=pltpu.SMEM),
                  pl.BlockSpec(memory_space=pl.ANY), pl.BlockSpec(memory_space=pl.ANY)],
        out_specs=pl.BlockSpec(memory_space=pl.ANY),
        out_shape=jax.ShapeDtypeStruct(init.shape, U32),
        scratch_shapes=[pltpu.SemaphoreType.DMA((2,))],
        input_output_aliases={2: 0},
        compiler_params=_cparams(1),
        name="moe_dispatch",
    )(pos3, xp, init)


def _moe_kernel(blk_e_ref, nused_ref, xs_ref, w1_ref, w3_ref, w2_ref, ys_ref, w1b, w3b, w2b):
    s = pl.program_id(0)
    nused = nused_ref[0]

    @pl.when(s >= nused)
    def _():
        ys_ref[...] = jnp.zeros_like(ys_ref)

    @pl.when(s < nused)
    def _():
        e = blk_e_ref[s]
        prev = blk_e_ref[jnp.maximum(s - 1, 0)]

        @pl.when((s == 0) | (e != prev))
        def _():
            w1b[...] = w1_ref[0, 0].astype(BF16)
            w3b[...] = w3_ref[0, 0].astype(BF16)
            w2b[...] = w2_ref[0, 0].astype(BF16)

        xb = _unpack_rows(_load_token_tiles(xs_ref, 0, MOE_TB)).astype(BF16)
        h1 = jnp.dot(xb, w1b[...], preferred_element_type=F32)
        h3 = jnp.dot(xb, w3b[...], preferred_element_type=F32)
        h = (h1 * jax.nn.sigmoid(h1) * h3).astype(BF16)
        y = jnp.dot(h, w2b[...], preferred_element_type=F32)
        _store_token_tiles(ys_ref, 0, _pack_rows(y))


def _moe_experts(xs, blk_e, nused, w1, w3, w2, layer):
    n_blocks = blk_e.shape[0]
    wspec = lambda shape: pl.BlockSpec((1, 1) + shape,
                                       lambda s, be, nu: (layer, be[jnp.minimum(s, nu[0] - 1)], 0, 0))
    tiles = pl.BlockSpec((MOE_TB * TOK_TILE, LANES_V7X), lambda s, be, nu: (s, 0))
    grid_spec = pltpu.PrefetchScalarGridSpec(
        num_scalar_prefetch=2,
        grid=(n_blocks,),
        in_specs=[tiles, wspec((D_MODEL, D_EXPERT)), wspec((D_MODEL, D_EXPERT)), wspec((D_EXPERT, D_MODEL))],
        out_specs=tiles,
        scratch_shapes=[pltpu.VMEM((D_MODEL, D_EXPERT), BF16), pltpu.VMEM((D_MODEL, D_EXPERT), BF16),
                        pltpu.VMEM((D_EXPERT, D_MODEL), BF16)],
    )
    return pl.pallas_call(
        _moe_kernel,
        grid_spec=grid_spec,
        out_shape=jax.ShapeDtypeStruct(xs.shape, U32),
        compiler_params=_cparams(1, VMEM_LIMIT_V7X),
        name="moe_experts",
    )(blk_e, nused, xs, w1, w3, w2)


def _combine_kernel(pos_ref, nxt_ref, ys_hbm, x_ref, rt_ref, g_ref, b_ref, o_ref, ybuf, sem, *, tm):
    s = pl.program_id(0)
    nsteps = pl.num_programs(0)
    slot = s % 2

    def copy(src, j, sl):
        return pltpu.make_async_copy(ys_hbm.at[_tile_rows(src)], ybuf.at[sl, _tile_rows(j)], sem.at[sl])

    def start_gather(idx_ref, sl):
        def body(j, c):
            copy(idx_ref[0, 0, j], j, sl).start()
            return c
        lax.fori_loop(0, TOP_K * tm, body, 0, unroll=8)

    @pl.when(s == 0)
    def _():
        start_gather(pos_ref, 0)

    @pl.when(s + 1 < nsteps)
    def _():
        start_gather(nxt_ref, 1 - slot)

    def wait_body(j, c):
        copy(0, j, slot).wait()
        return c
    lax.fori_loop(0, TOP_K * tm, wait_body, 0, unroll=8)

    rt = rt_ref[...]
    ffn = _unpack_rows(_load_token_tiles(ybuf, 0, tm, lead=slot)) * rt[:, RT_W0:RT_W0 + 1]
    ffn = ffn + _unpack_rows(_load_token_tiles(ybuf, tm * TOK_TILE, tm, lead=slot)) * rt[:, RT_W1:RT_W1 + 1]
    o_ref[...] = _layer_norm_rows(DN_ALPHA * x_ref[...] + ffn, g_ref[...], b_ref[...])


def _moe_combine(ys, pos, rt, x, g, b, tm=256):
    n = x.shape[0]
    nsteps = n // tm
    pos3 = _step_indices(pos, tm)
    smem_blk = lambda f: pl.BlockSpec((1, 1, TOP_K * tm), f, memory_space=pltpu.SMEM)
    row = lambda width: pl.BlockSpec((tm, width), lambda i: (i, 0))
    const = lambda shape: pl.BlockSpec(shape, lambda i: (0, 0))
    return pl.pallas_call(
        functools.partial(_combine_kernel, tm=tm),
        grid=(nsteps,),
        in_specs=[smem_blk(lambda i: (i, 0, 0)), smem_blk(lambda i: (jnp.minimum(i + 1, nsteps - 1), 0, 0)),
                  pl.BlockSpec(memory_space=pl.ANY), row(D_MODEL), row(ROUTE_W),
                  const((1, D_MODEL)), const((1, D_MODEL))],
        out_specs=row(D_MODEL),
        out_shape=jax.ShapeDtypeStruct((n, D_MODEL), F32),
        scratch_shapes=[pltpu.VMEM((2, TOP_K * tm * TOK_TILE, LANES_V7X), U32), pltpu.SemaphoreType.DMA((2,))],
        compiler_params=_cparams(1, 40 * MIB),
        name="moe_combine",
    )(pos3, pos3, ys, x, rt, g.reshape(1, -1), b.reshape(1, -1))


def _moe_layer(x1, xp, logits, w1, w3, w2, layer, g, b):
    n_tok = x1.shape[0]
    rt, cnt = _route(logits)
    pos, blk_e, nused, n_blocks = _block_layout(rt, cnt, n_tok)
    xs = _moe_dispatch(xp, pos, n_blocks * MOE_TB)
    ys = _moe_experts(xs, blk_e, nused, w1, w3, w2, layer)
    return _moe_combine(ys, pos, rt, x1, g, b)


def _gmlp_kernel(u_ref, v_ref, g_ref, b_ref, w_ref, bs_ref, o_ref, *, chunks):
    for ci in range(chunks):
        rows = slice(ci * C_CHUNK, (ci + 1) * C_CHUNK)
        u = jax.nn.gelu(u_ref[rows, :])
        v = _layer_norm_rows(jax.nn.gelu(v_ref[rows, :]), g_ref[...], b_ref[...]).astype(BF16)
        for gi in range(C_GROUPS):
            cols = slice(gi * C_GROUP_DIM, (gi + 1) * C_GROUP_DIM)
            mixed = jnp.dot(w_ref[gi], v[:, cols], preferred_element_type=F32) + bs_ref[:, cols]
            o_ref[rows, cols] = (u[:, cols] * mixed).astype(o_ref.dtype)


def _gmlp(proj, ln_g, ln_b, w_s, b_s, chunks=2):
    n = proj.shape[0]
    tm = chunks * C_CHUNK
    w = (w_s * jnp.tril(jnp.ones((C_CHUNK, C_CHUNK), w_s.dtype))).astype(BF16)
    bs_full = jnp.repeat(b_s.T, C_GROUP_DIM, axis=1)
    const2 = lambda shape: pl.BlockSpec(shape, lambda i: (0, 0))
    return pl.pallas_call(
        functools.partial(_gmlp_kernel, chunks=chunks),
        grid=(n // tm,),
        in_specs=[pl.BlockSpec((tm, C_W), lambda i: (i, 0)), pl.BlockSpec((tm, C_W), lambda i: (i, 1)),
                  const2((1, C_W)), const2((1, C_W)),
                  pl.BlockSpec((C_GROUPS, C_CHUNK, C_CHUNK), lambda i: (0, 0, 0)), const2((C_CHUNK, C_W))],
        out_specs=pl.BlockSpec((tm, C_W), lambda i: (i, 0)),
        out_shape=jax.ShapeDtypeStruct((n, C_W), BF16),
        compiler_params=_cparams(1, 32 * MIB),
        name="gmlp_gating",
    )(proj, proj, ln_g.reshape(1, -1), ln_b.reshape(1, -1), w, bs_full)


CONV_HIST = 32


def _conv_kernel(a_ref, gt_ref, ap_ref, gp_ref, w_ref, cb_ref, g_ref, b_ref, o_ref, hbuf, *, ts):
    i = pl.program_id(1)
    hist = ap_ref[...] * jax.nn.sigmoid(gp_ref[...])
    hbuf[0:CONV_HIST, :] = jnp.where(i > 0, hist, jnp.zeros_like(hist))
    hbuf[CONV_HIST:CONV_HIST + ts, :] = a_ref[...] * jax.nn.sigmoid(gt_ref[...])
    off = CONV_HIST - (D_CONV - 1)
    acc = jnp.zeros((ts, D_CHANNELS), F32) + cb_ref[...]
    for j in range(D_CONV):
        acc = acc + w_ref[j:j + 1, :] * hbuf[off + j:off + j + ts, :]
    y = _layer_norm_rows(acc, g_ref[...], b_ref[...])
    o_ref[...] = (y * jax.nn.sigmoid(y)).astype(o_ref.dtype)


def _conformer_conv(proj, conv_w, conv_b, ln_g, ln_b, bsz, seq, ts=256):
    nt = seq // ts
    acol = 2 * C_W // D_CHANNELS
    gcol = acol + 1
    hb = ts // CONV_HIST
    cur = lambda col: pl.BlockSpec((ts, D_CHANNELS), lambda b, i: (b * nt + i, col))
    prev = lambda col: pl.BlockSpec((CONV_HIST, D_CHANNELS),
                                    lambda b, i: (jnp.maximum((b * nt + i) * hb - 1, 0), col))
    const2 = lambda shape: pl.BlockSpec(shape, lambda b, i: (0, 0))
    return pl.pallas_call(
        functools.partial(_conv_kernel, ts=ts),
        grid=(bsz, nt),
        in_specs=[cur(acol), cur(gcol), prev(acol), prev(gcol),
                  const2((D_CONV, D_CHANNELS)), const2((1, D_CHANNELS)), const2((1, D_CHANNELS)),
                  const2((1, D_CHANNELS))],
        out_specs=pl.BlockSpec((ts, D_CHANNELS), lambda b, i: (b * nt + i, 0)),
        out_shape=jax.ShapeDtypeStruct((bsz * seq, D_CHANNELS), BF16),
        scratch_shapes=[pltpu.VMEM((CONV_HIST + ts, D_CHANNELS), F32)],
        compiler_params=_cparams(2, 32 * MIB),
        name="conformer_conv",
    )(proj, proj, proj, proj, conv_w, conv_b.reshape(1, -1), ln_g.reshape(1, -1), ln_b.reshape(1, -1))


def kernel(x, w_in_ab, attn_sinks, rel_bias, hgrn_lb_logits, hgrn_norm_g, w_out_ab, w_in_cd, gmlp_ln_g, gmlp_ln_b, gmlp_w_s, gmlp_b_s, conv_w, conv_b, conv_ln_g, conv_ln_b, w_out_cd, ln_mix_g, ln_mix_b, ln_ffn_g, ln_ffn_b, moe_w_group, moe_b_group, moe_w_router, moe_b_router, moe_w1, moe_w3, moe_w2):
    bsz, seq = x.shape[0], x.shape[1]
    n_tok = bsz * seq
    xf = x.reshape(n_tok, D_MODEL)
    lb_table = jnp.cumsum(jax.nn.softmax(hgrn_lb_logits.astype(F32), axis=0), axis=0)
    bias = _bias_table(rel_bias.astype(F32), _t5_bucket_table())

    for layer in range(DEPTH):
        j = layer // 2
        if layer % 2 == 0:
            proj = _matmul(xf, w_in_ab[j].astype(BF16), 512, EVEN_IN // 3, F32)
            ya = _attention(proj, bias, attn_sinks[j].astype(F32), bsz, seq)
            yb = _hgrn(proj, lb_table[layer], hgrn_norm_g[j].astype(F32), bsz, seq)
            w_out = w_out_ab[j]
        else:
            proj = _matmul(xf, w_in_cd[j].astype(BF16), 512, ODD_IN // 2, F32)
            ya = _gmlp(proj, gmlp_ln_g[j], gmlp_ln_b[j], gmlp_w_s[j], gmlp_b_s[j])
            yb = _conformer_conv(proj, conv_w[j], conv_b[j], conv_ln_g[j], conv_ln_b[j], bsz, seq)
            w_out = w_out_cd[j]
        wr_hi, wr_lo, rbias = _router_weights(moe_w_group[layer], moe_b_group[layer],
                                              moe_w_router[layer], moe_b_router[layer])
        x1, xp, logits = _outproj_ln_route(ya, yb, w_out.astype(BF16), xf, ln_mix_g[layer], ln_mix_b[layer],
                                           wr_hi, wr_lo, rbias)
        xf = _moe_layer(x1, xp, logits, moe_w1, moe_w3, moe_w2, layer, ln_ffn_g[layer], ln_ffn_b[layer])
    return xf.reshape(bsz, seq, D_MODEL)
```

```python
import functools
import math

import jax
import jax.numpy as jnp
from jax import lax
from jax.experimental import pallas as pl
from jax.experimental.pallas import tpu as pltpu

D_MODEL = 2048
DEPTH = 2
A_HEADS = 16
A_KV_HEADS = 2
A_HEAD_DIM = 64
A_WINDOW = 128
A_BLOCK = 128
REL_BUCKETS = 32
REL_MAX_DIST = 128
B_HEADS = 8
B_DK = 128
B_DV = 128
C_GROUPS = 8
C_GROUP_DIM = 128
C_CHUNK = 128
D_CHANNELS = 1024
D_CONV = 31
A_QW = A_HEADS * A_HEAD_DIM
A_KVW = A_KV_HEADS * A_HEAD_DIM
B_KW = B_HEADS * B_DK
B_VW = B_HEADS * B_DV
C_W = C_GROUPS * C_GROUP_DIM
EVEN_IN = A_QW + 2 * A_KVW + 2 * B_KW + 2 * B_VW
ODD_IN = 2 * C_W + 2 * D_CHANNELS
N_GROUPS = 4
EXPERTS_PER_GROUP = 8
N_EXPERTS = N_GROUPS * EXPERTS_PER_GROUP
TOP_K = 2
D_EXPERT = 512
DN_ALPHA = (2 * DEPTH) ** 0.25
LN_EPS = 1e-5
RMS_EPS = 1e-6

LANES_V7X = 128
SUBLANES_V7X = 8
MIB = 1024 * 1024
VMEM_LIMIT_V7X = 56 * MIB

HGRN_CHUNK = 128
HGRN_SUB = 16
HGRN_ROWS = 512
MOE_TB = 256
ROUTE_W = LANES_V7X
HALF = D_MODEL // 2
TOK_TILE = HALF // LANES_V7X
assert TOK_TILE == SUBLANES_V7X

BF16 = jnp.bfloat16
F32 = jnp.float32
U32 = jnp.uint32
NEG_INF = float("-inf")
RT_W0, RT_W1, RT_E0, RT_E1, RT_R0, RT_R1 = range(6)


def _cparams(n_axes, vmem_bytes=None):
    return pltpu.CompilerParams(dimension_semantics=("arbitrary",) * n_axes, vmem_limit_bytes=vmem_bytes)


def _layer_norm_rows(z, g, b):
    mu = jnp.mean(z, axis=-1, keepdims=True)
    zc = z - mu
    var = jnp.mean(zc * zc, axis=-1, keepdims=True)
    return zc * lax.rsqrt(var + LN_EPS) * g + b


def _dot_nt(a, b):
    return lax.dot_general(a, b, (((1,), (1,)), ((), ())), preferred_element_type=F32)


def _dot_tn(a, b):
    return lax.dot_general(a, b, (((0,), (0,)), ((), ())), preferred_element_type=F32)


def _pack_rows(y):
    lo = lax.bitcast_convert_type(y[:, :HALF].astype(BF16).astype(F32), U32) >> 16
    hi = lax.bitcast_convert_type(y[:, HALF:].astype(BF16).astype(F32), U32) & jnp.uint32(0xFFFF0000)
    return lo | hi


def _unpack_rows(p):
    lo = lax.bitcast_convert_type(p << 16, F32)
    hi = lax.bitcast_convert_type(p & jnp.uint32(0xFFFF0000), F32)
    return jnp.concatenate([lo, hi], axis=1)


def _store_token_tiles(ref, base, packed):
    rows = packed.shape[0]
    for c in range(TOK_TILE):
        ref[pl.ds(base + c, rows, stride=TOK_TILE), :] = packed[:, c * LANES_V7X:(c + 1) * LANES_V7X]


def _load_token_tiles(ref, base, rows, lead=None):
    parts = []
    for c in range(TOK_TILE):
        idx = (pl.ds(base + c, rows, stride=TOK_TILE), slice(None))
        parts.append(ref[idx] if lead is None else ref[(lead,) + idx])
    return jnp.concatenate(parts, axis=1)


def _mm_kernel(a_ref, w_ref, o_ref):
    a = a_ref[...].astype(BF16)
    o_ref[...] = jnp.dot(a, w_ref[...], preferred_element_type=F32).astype(o_ref.dtype)


def _matmul(a, w, tm, tn, out_dtype):
    m, k = a.shape
    n = w.shape[1]
    return pl.pallas_call(
        _mm_kernel,
        grid=(n // tn, m // tm),
        in_specs=[pl.BlockSpec((tm, k), lambda j, i: (i, 0)), pl.BlockSpec((k, tn), lambda j, i: (0, j))],
        out_specs=pl.BlockSpec((tm, tn), lambda j, i: (i, j)),
        out_shape=jax.ShapeDtypeStruct((m, n), out_dtype),
        compiler_params=_cparams(2, VMEM_LIMIT_V7X),
        name="proj_matmul",
    )(a, w)


def _bias_table_kernel(rb_ref, bucket_ref, o_ref):
    bucket = bucket_ref[...]
    for h in range(A_HEADS):
        acc = jnp.zeros(bucket.shape, F32)
        for bk in range(REL_BUCKETS):
            acc = jnp.where(bucket == bk, rb_ref[bk, h], acc)
        o_ref[h] = acc


def _bias_table(rel_bias, bucket):
    return pl.pallas_call(
        _bias_table_kernel,
        in_specs=[pl.BlockSpec(memory_space=pltpu.SMEM), pl.BlockSpec(memory_space=pltpu.VMEM)],
        out_specs=pl.BlockSpec(memory_space=pltpu.VMEM),
        out_shape=jax.ShapeDtypeStruct((A_HEADS,) + bucket.shape, F32),
        name="rel_bias_table",
    )(rel_bias, bucket)


def _t5_bucket_table():
    t_loc = jnp.arange(A_BLOCK, dtype=jnp.int32)[:, None]
    s_loc = jnp.arange(2 * A_BLOCK, dtype=jnp.int32)[None, :]
    dist = jnp.maximum(t_loc + A_BLOCK - s_loc, 0)
    max_exact = REL_BUCKETS // 2
    d = jnp.maximum(dist, 1).astype(F32)
    large = max_exact + (jnp.log(d / max_exact) / math.log(REL_MAX_DIST / max_exact)
                         * (REL_BUCKETS - max_exact)).astype(jnp.int32)
    large = jnp.minimum(large, REL_BUCKETS - 1)
    return jnp.where(dist < max_exact, dist, large)


def _attn_kernel(q_ref, kc_ref, vc_ref, kp_ref, vp_ref, bias_ref, sink_ref, o_ref):
    n = pl.program_id(1)
    blk = A_BLOCK
    k2 = jnp.concatenate([kp_ref[...], kc_ref[...]], axis=0)
    v2 = jnp.concatenate([vp_ref[...], vc_ref[...]], axis=0)
    k2r = pltpu.roll(k2, A_HEAD_DIM, 1)
    v2r = pltpu.roll(v2, A_HEAD_DIM, 1)
    lo = lax.broadcasted_iota(jnp.int32, k2.shape, 1) < A_HEAD_DIM
    zero = jnp.zeros_like(k2)

    def placed(x, xr, g, par):
        src = x if g == par else xr
        return (jnp.where(lo, src, zero) if par == 0 else jnp.where(lo, zero, src)).astype(BF16)

    kk = [[placed(k2, k2r, g, par) for par in range(2)] for g in range(A_KV_HEADS)]
    vv = [[placed(v2, v2r, g, par) for par in range(2)] for g in range(A_KV_HEADS)]

    t_loc = lax.broadcasted_iota(jnp.int32, (blk, 2 * blk), 0)
    s_loc = lax.broadcasted_iota(jnp.int32, (blk, 2 * blk), 1)
    dist = t_loc + blk - s_loc
    valid = (dist >= 0) & (dist < A_WINDOW) & ((s_loc >= blk) | (n > 0))

    heads_per_kv = A_HEADS // A_KV_HEADS
    for p in range(A_HEADS // 2):
        g = (2 * p) // heads_per_kv
        qp = (q_ref[:, p * 128:(p + 1) * 128] * (A_HEAD_DIM ** -0.5)).astype(BF16)
        acc = jnp.zeros((blk, 128), F32)
        for par in range(2):
            h = 2 * p + par
            sink = sink_ref[h]
            logits = jnp.where(valid, _dot_nt(qp, kk[g][par]) + bias_ref[h], NEG_INF)
            m = jnp.maximum(jnp.max(logits, axis=-1, keepdims=True), sink)
            e = jnp.exp(logits - m)
            den = jnp.sum(e, axis=-1, keepdims=True) + jnp.exp(sink - m)
            acc = acc + jnp.dot(e.astype(BF16), vv[g][par], preferred_element_type=F32) * (1.0 / den)
        o_ref[:, p * 128:(p + 1) * 128] = acc.astype(o_ref.dtype)


def _attention(proj, bias, sinks, bsz, seq):
    nb = seq // A_BLOCK
    kcol = A_QW // 128
    vcol = kcol + 1
    row = lambda b, n: b * nb + n
    prow = lambda b, n: b * nb + jnp.maximum(n - 1, 0)
    return pl.pallas_call(
        _attn_kernel,
        grid=(bsz, nb),
        in_specs=[
            pl.BlockSpec((A_BLOCK, A_QW), lambda b, n: (row(b, n), 0)),
            pl.BlockSpec((A_BLOCK, 128), lambda b, n: (row(b, n), kcol)),
            pl.BlockSpec((A_BLOCK, 128), lambda b, n: (row(b, n), vcol)),
            pl.BlockSpec((A_BLOCK, 128), lambda b, n: (prow(b, n), kcol)),
            pl.BlockSpec((A_BLOCK, 128), lambda b, n: (prow(b, n), vcol)),
            pl.BlockSpec((A_HEADS, A_BLOCK, 2 * A_BLOCK), lambda b, n: (0, 0, 0)),
            pl.BlockSpec(memory_space=pltpu.SMEM),
        ],
        out_specs=pl.BlockSpec((A_BLOCK, A_QW), lambda b, n: (row(b, n), 0)),
        out_shape=jax.ShapeDtypeStruct((bsz * seq, A_QW), BF16),
        compiler_params=_cparams(2, 32 * MIB),
        name="swa_attention",
    )(proj, proj, proj, proj, proj, bias, sinks)


HGRN_HEADS_PER_STEP = 2


def _hgrn_kernel(q_ref, f_ref, i_ref, g_ref, lb_ref, ng_ref, o_ref, st_ref):
    c, sub = HGRN_CHUNK, HGRN_SUB
    nsub = c // sub

    @pl.when(pl.program_id(2) == 0)
    def _():
        st_ref[...] = jnp.zeros_like(st_ref)

    rid = lax.broadcasted_iota(jnp.int32, (c, c), 0)
    cid = lax.broadcasted_iota(jnp.int32, (c, c), 1)
    blk_start = (rid // sub) * sub
    m_before = (cid < blk_start).astype(F32)
    m_within = ((cid >= blk_start) & (cid <= rid)).astype(F32)
    band = jnp.where(cid >= blk_start, rid - cid, -1)
    ones = jnp.ones((B_DK, B_DV), BF16)
    hi = lax.Precision.HIGHEST

    def one_head(hh, r0):
        cols = slice(hh * 128, (hh + 1) * 128)
        lb = lb_ref[0, :, cols]
        ng = ng_ref[0, :, cols]
        q = q_ref[pl.ds(r0, c), cols]
        fl = f_ref[pl.ds(r0, c), cols]
        v = i_ref[pl.ds(r0, c), cols]
        gt = g_ref[pl.ds(r0, c), cols]
        qf = q * jax.nn.sigmoid(q)
        f = lb + (1.0 - lb) * jax.nn.sigmoid(fl)
        logf = jnp.log(f)
        kin = 1.0 - f
        lk = jnp.log(kin)
        rr = jnp.dot(m_before, logf, precision=hi, preferred_element_type=F32)
        bq = jnp.dot(m_within, logf, precision=hi, preferred_element_type=F32)
        wq = lk - bq
        wb = wq - rr
        qt = qf * jnp.exp(bq)
        st = st_ref[hh]
        vb = v.astype(BF16)
        o = _dot_nt((qt * jnp.exp(rr)).astype(BF16), st.astype(BF16))

        parts = [jnp.zeros((sub, c), F32)]
        for i in range(1, nsub):
            n = i * sub
            kt = jnp.exp(rr[n:n + 1, :] + wb[:n, :]).astype(BF16)
            kt = jnp.concatenate([kt, jnp.zeros((c - n, B_DK), BF16)], axis=0)
            parts.append(_dot_nt(qt[n:n + sub, :].astype(BF16), kt))
        s_all = jnp.concatenate(parts, axis=0)

        for d in range(sub):
            e = kin if d == 0 else jnp.exp(jnp.minimum(bq + pltpu.roll(wq, d, 0), 0.0))
            rd = jnp.dot((qf * e).astype(BF16), ones, preferred_element_type=F32)
            s_all = jnp.where(band == d, rd, s_all)
        o = o + jnp.dot(s_all.astype(BF16), vb, preferred_element_type=F32)

        b_last = rr[c - 1:c, :] + bq[c - 1:c, :]
        st_ref[hh] = st * jnp.exp(b_last) + _dot_tn(vb, jnp.exp(b_last + wb).astype(BF16))

        o = o * lax.rsqrt(jnp.mean(o * o, axis=-1, keepdims=True) + RMS_EPS)
        o_ref[pl.ds(r0, c), cols] = (o * ng * (gt * jax.nn.sigmoid(gt))).astype(o_ref.dtype)

    def chunk(ci, carry):
        r0 = pl.multiple_of(ci * c, c)
        for hh in range(HGRN_HEADS_PER_STEP):
            one_head(hh, r0)
        return carry

    lax.fori_loop(0, HGRN_ROWS // c, chunk, 0)


def _hgrn(proj, lb, norm_g, bsz, seq):
    nr = seq // HGRN_ROWS
    hp = HGRN_HEADS_PER_STEP
    width = hp * 128
    c0 = (A_QW + 2 * A_KVW) // width
    nhp = B_HEADS // hp
    spec = lambda off: pl.BlockSpec((HGRN_ROWS, width), lambda b, h, r: (b * nr + r, off + h))
    vec = pl.BlockSpec((1, 1, width), lambda b, h, r: (h, 0, 0))
    return pl.pallas_call(
        _hgrn_kernel,
        grid=(bsz, nhp, nr),
        in_specs=[spec(c0), spec(c0 + nhp), spec(c0 + 2 * nhp), spec(c0 + 3 * nhp), vec, vec],
        out_specs=pl.BlockSpec((HGRN_ROWS, width), lambda b, h, r: (b * nr + r, h)),
        out_shape=jax.ShapeDtypeStruct((bsz * seq, B_VW), BF16),
        scratch_shapes=[pltpu.VMEM((hp, B_DV, B_DK), F32)],
        compiler_params=_cparams(3, 32 * MIB),
        name="hgrn2",
    )(proj, proj, proj, proj, lb.reshape(nhp, 1, width), norm_g.reshape(nhp, 1, width))


def _outproj_kernel(ya_ref, yb_ref, wa_ref, wb_ref, x_ref, g_ref, b_ref, wrh_ref, wrl_ref, rb_ref,
                    xo_ref, xp_ref, lg_ref):
    mix = jnp.dot(ya_ref[...], wa_ref[...], preferred_element_type=F32)
    mix = mix + jnp.dot(yb_ref[...], wb_ref[...], preferred_element_type=F32)
    y = _layer_norm_rows(DN_ALPHA * x_ref[...] + mix, g_ref[...], b_ref[...])
    xo_ref[...] = y
    _store_token_tiles(xp_ref, 0, _pack_rows(y))
    y_hi = y.astype(BF16)
    y_lo = (y - y_hi.astype(F32)).astype(BF16)
    lg = jnp.dot(y_hi, wrh_ref[...], preferred_element_type=F32)
    lg = lg + jnp.dot(y_lo, wrh_ref[...], preferred_element_type=F32)
    lg = lg + jnp.dot(y_hi, wrl_ref[...], preferred_element_type=F32)
    lg_ref[...] = lg + rb_ref[...]


def _outproj_ln_route(ya, yb, w_out, x, g, b, wr_hi, wr_lo, rbias, tm=256):
    n = x.shape[0]
    ka = ya.shape[1]
    row = lambda width: pl.BlockSpec((tm, width), lambda i: (i, 0))
    const = lambda shape: pl.BlockSpec(shape, lambda i: (0, 0))
    return pl.pallas_call(
        _outproj_kernel,
        grid=(n // tm,),
        in_specs=[row(ka), row(ka),
                  pl.BlockSpec((ka, D_MODEL), lambda i: (0, 0)), pl.BlockSpec((ka, D_MODEL), lambda i: (1, 0)),
                  row(D_MODEL), const((1, D_MODEL)), const((1, D_MODEL)),
                  const((D_MODEL, ROUTE_W)), const((D_MODEL, ROUTE_W)), const((1, ROUTE_W))],
        out_specs=[row(D_MODEL), pl.BlockSpec((tm * TOK_TILE, LANES_V7X), lambda i: (i, 0)), row(ROUTE_W)],
        out_shape=[jax.ShapeDtypeStruct((n, D_MODEL), F32),
                   jax.ShapeDtypeStruct((n * TOK_TILE, LANES_V7X), U32),
                   jax.ShapeDtypeStruct((n, ROUTE_W), F32)],
        compiler_params=_cparams(1, VMEM_LIMIT_V7X),
        name="outproj_ln_route",
    )(ya, yb, w_out, w_out, x, g.reshape(1, -1), b.reshape(1, -1), wr_hi, wr_lo, rbias)


def _router_weights(w_group, b_group, w_router, b_router):
    w = jnp.zeros((D_MODEL, ROUTE_W), F32)
    w = w.at[:, :N_GROUPS].set(w_group).at[:, N_GROUPS:N_GROUPS + N_EXPERTS].set(w_router)
    w_hi = w.astype(BF16)
    w_lo = (w - w_hi.astype(F32)).astype(BF16)
    rb = jnp.zeros((1, ROUTE_W), F32)
    rb = rb.at[0, :N_GROUPS].set(b_group).at[0, N_GROUPS:N_GROUPS + N_EXPERTS].set(b_router)
    return w_hi, w_lo, rb


def _route_kernel(lg_ref, rt_ref, cnt_ref, run_ref):
    @pl.when(pl.program_id(0) == 0)
    def _():
        run_ref[...] = jnp.zeros_like(run_ref)

    lg = lg_ref[...]
    tm = lg.shape[0]
    lane = lax.broadcasted_iota(jnp.int32, lg.shape, 1)
    sentinel = jnp.int32(ROUTE_W)
    rowmax = lambda mask: jnp.max(jnp.where(mask, lg, NEG_INF), axis=-1, keepdims=True)
    first = lambda mask: jnp.min(jnp.where(mask, lane, sentinel), axis=-1, keepdims=True)

    is_g = lane < N_GROUPS
    gmax = rowmax(is_g)
    g_idx = first(is_g & (lg == gmax))
    g_w = 1.0 / jnp.sum(jnp.where(is_g, jnp.exp(lg - gmax), 0.0), axis=-1, keepdims=True)

    e_lane = lane - N_GROUPS
    sel = (e_lane >= 0) & (e_lane < N_EXPERTS) & ((e_lane >> 3) == g_idx)
    m1 = rowmax(sel)
    i1 = first(sel & (lg == m1))
    sel2 = sel & (lane != i1)
    m2 = rowmax(sel2)
    i2 = first(sel2 & (lg == m2))
    ex = jnp.exp(m2 - m1)
    w0 = g_w / (1.0 + ex)
    w1 = g_w * ex / (1.0 + ex)

    oh0 = lane == i1
    oh1 = lane == i2
    both = (oh0 | oh1).astype(F32)
    rid = lax.broadcasted_iota(jnp.int32, (tm, tm), 0)
    cid = lax.broadcasted_iota(jnp.int32, (tm, tm), 1)
    before = jnp.dot((cid < rid).astype(BF16), both.astype(BF16), preferred_element_type=F32) + run_ref[...]
    rank0 = jnp.sum(jnp.where(oh0, before, 0.0), axis=-1, keepdims=True)
    rank1 = jnp.sum(jnp.where(oh1, before, 0.0), axis=-1, keepdims=True)
    run = run_ref[...] + jnp.sum(both, axis=0, keepdims=True)
    run_ref[...] = run
    cnt_ref[...] = run

    slab = jnp.zeros(lg.shape, F32)
    for ln, val in ((RT_W0, w0), (RT_W1, w1), (RT_E0, (i1 - N_GROUPS).astype(F32)),
                    (RT_E1, (i2 - N_GROUPS).astype(F32)), (RT_R0, rank0), (RT_R1, rank1)):
        slab = jnp.where(lane == ln, val, slab)
    rt_ref[...] = slab


assert EXPERTS_PER_GROUP == 8


def _route(logits, tm=256):
    n = logits.shape[0]
    return pl.pallas_call(
        _route_kernel,
        grid=(n // tm,),
        in_specs=[pl.BlockSpec((tm, ROUTE_W), lambda i: (i, 0))],
        out_specs=[pl.BlockSpec((tm, ROUTE_W), lambda i: (i, 0)), pl.BlockSpec((1, ROUTE_W), lambda i: (0, 0))],
        out_shape=[jax.ShapeDtypeStruct((n, ROUTE_W), F32), jax.ShapeDtypeStruct((1, ROUTE_W), F32)],
        scratch_shapes=[pltpu.VMEM((1, ROUTE_W), F32)],
        compiler_params=_cparams(1),
        name="moe_router",
    )(logits)


def _block_layout(rt, cnt, n_tok):
    m = n_tok * TOP_K
    counts = cnt[0, N_GROUPS:N_GROUPS + N_EXPERTS].astype(jnp.int32)
    pcounts = (counts + MOE_TB - 1) // MOE_TB * MOE_TB
    pends = jnp.cumsum(pcounts)
    pstarts = pends - pcounts
    n_blocks = -(-(m + N_EXPERTS * (MOE_TB - 1)) // MOE_TB)
    e_idx = rt[:, RT_E0:RT_E1 + 1].astype(jnp.int32)
    rank = rt[:, RT_R0:RT_R1 + 1].astype(jnp.int32)
    onehot = e_idx[:, :, None] == jnp.arange(N_EXPERTS, dtype=jnp.int32)
    pos = jnp.sum(jnp.where(onehot, pstarts, 0), axis=-1) + rank
    blk_start = jnp.arange(n_blocks, dtype=jnp.int32) * MOE_TB
    blk_e = jnp.minimum(jnp.sum(blk_start[:, None] >= pends[None, :], axis=-1), N_EXPERTS - 1).astype(jnp.int32)
    nused = (pends[-1:] // MOE_TB).astype(jnp.int32)
    return pos, blk_e, nused, n_blocks


def _step_indices(pos, tm):
    nsteps = pos.shape[0] // tm
    return pos.reshape(nsteps, tm, TOP_K).transpose(0, 2, 1).reshape(nsteps, 1, TOP_K * tm)


def _tile_rows(i):
    return pl.ds(pl.multiple_of(i * TOK_TILE, TOK_TILE), TOK_TILE)


def _dispatch_kernel(pos_ref, xp_ref, init_hbm, xs_hbm, xbuf, sem, *, tm):
    del init_hbm
    s = pl.program_id(0)
    nsteps = pl.num_programs(0)
    slot = s % 2

    def copy(j, dst, sl):
        return pltpu.make_async_copy(xbuf.at[sl, _tile_rows(j)], xs_hbm.at[_tile_rows(dst)], sem.at[sl])

    def drain(sl):
        def body(j, c):
            copy(0, 0, sl).wait()
            return c
        lax.fori_loop(0, TOP_K * tm, body, 0, unroll=8)

    @pl.when(s >= 2)
    def _():
        drain(slot)

    xbuf[slot] = xp_ref[...]

    def start(j, c):
        for k in range(TOP_K):
            copy(j, pos_ref[0, 0, k * tm + j], slot).start()
        return c
    lax.fori_loop(0, tm, start, 0, unroll=8)

    @pl.when(s == nsteps - 1)
    def _():
        drain(1 - slot)
        drain(slot)


def _moe_dispatch(xp, pos, n_pad, tm=256):
    n = pos.shape[0]
    assert n // tm >= 2
    pos3 = _step_indices(pos, tm)
    init = jnp.zeros((n_pad * TOK_TILE, LANES_V7X), U32)
    return pl.pallas_call(
        functools.partial(_dispatch_kernel, tm=tm),
        grid=(n // tm,),
        in_specs=[pl.BlockSpec((1, 1, TOP_K * tm), lambda i: (i, 0, 0), memory_space=pltpu.SMEM),
                  pl.BlockSpec((tm * TOK_TILE, LANES_V7X), lambda i: (i, 0)), pl.BlockSpec(memory_space=pl.ANY)],
        out_specs=pl.BlockSpec(memory_space=pl.ANY),
        out_shape=jax.ShapeDtypeStruct(init.shape, U32),
        scratch_shapes=[pltpu.VMEM((2, tm * TOK_TILE, LANES_V7X), U32), pltpu.SemaphoreType.DMA((2,))],
        input_output_aliases={2: 0},
        compiler_params=_cparams(1),
        name="moe_dispatch",
    )(pos3, xp, init)


def _moe_kernel(blk_e_ref, nused_ref, xs_ref, w1_ref, w3_ref, w2_ref, ys_ref, w1b, w3b, w2b):
    s = pl.program_id(0)
    nused = nused_ref[0]

    @pl.when(s >= nused)
    def _():
        ys_ref[...] = jnp.zeros_like(ys_ref)

    @pl.when(s < nused)
    def _():
        e = blk_e_ref[s]
        prev = blk_e_ref[jnp.maximum(s - 1, 0)]

        @pl.when((s == 0) | (e != prev))
        def _():
            w1b[...] = w1_ref[0, 0].astype(BF16)
            w3b[...] = w3_ref[0, 0].astype(BF16)
            w2b[...] = w2_ref[0, 0].astype(BF16)

        xb = _unpack_rows(_load_token_tiles(xs_ref, 0, MOE_TB)).astype(BF16)
        h1 = jnp.dot(xb, w1b[...], preferred_element_type=F32)
        h3 = jnp.dot(xb, w3b[...], preferred_element_type=F32)
        h = (h1 * jax.nn.sigmoid(h1) * h3).astype(BF16)
        y = jnp.dot(h, w2b[...], preferred_element_type=F32)
        _store_token_tiles(ys_ref, 0, _pack_rows(y))


def _moe_experts(xs, blk_e, nused, w1, w3, w2, layer):
    n_blocks = blk_e.shape[0]
    wspec = lambda shape: pl.BlockSpec((1, 1) + shape,
                                       lambda s, be, nu: (layer, be[jnp.minimum(s, nu[0] - 1)], 0, 0))
    tiles = pl.BlockSpec((MOE_TB * TOK_TILE, LANES_V7X), lambda s, be, nu: (s, 0))
    grid_spec = pltpu.PrefetchScalarGridSpec(
        num_scalar_prefetch=2,
        grid=(n_blocks,),
        in_specs=[tiles, wspec((D_MODEL, D_EXPERT)), wspec((D_MODEL, D_EXPERT)), wspec((D_EXPERT, D_MODEL))],
        out_specs=tiles,
        scratch_shapes=[pltpu.VMEM((D_MODEL, D_EXPERT), BF16), pltpu.VMEM((D_MODEL, D_EXPERT), BF16),
                        pltpu.VMEM((D_EXPERT, D_MODEL), BF16)],
    )
    return pl.pallas_call(
        _moe_kernel,
        grid_spec=grid_spec,
        out_shape=jax.ShapeDtypeStruct(xs.shape, U32),
        compiler_params=_cparams(1, VMEM_LIMIT_V7X),
        name="moe_experts",
    )(blk_e, nused, xs, w1, w3, w2)


def _combine_kernel(pos_ref, nxt_ref, ys_hbm, x_ref, rt_ref, g_ref, b_ref, o_ref, ybuf, sem, *, tm):
    s = pl.program_id(0)
    nsteps = pl.num_programs(0)
    slot = s % 2

    def copy(src, j, sl):
        return pltpu.make_async_copy(ys_hbm.at[_tile_rows(src)], ybuf.at[sl, _tile_rows(j)], sem.at[sl])

    def start_gather(idx_ref, sl):
        def body(j, c):
            copy(idx_ref[0, 0, j], j, sl).start()
            return c
        lax.fori_loop(0, TOP_K * tm, body, 0, unroll=8)

    @pl.when(s == 0)
    def _():
        start_gather(pos_ref, 0)

    @pl.when(s + 1 < nsteps)
    def _():
        start_gather(nxt_ref, 1 - slot)

    def wait_body(j, c):
        copy(0, j, slot).wait()
        return c
    lax.fori_loop(0, TOP_K * tm, wait_body, 0, unroll=8)

    rt = rt_ref[...]
    ffn = _unpack_rows(_load_token_tiles(ybuf, 0, tm, lead=slot)) * rt[:, RT_W0:RT_W0 + 1]
    ffn = ffn + _unpack_rows(_load_token_tiles(ybuf, tm * TOK_TILE, tm, lead=slot)) * rt[:, RT_W1:RT_W1 + 1]
    o_ref[...] = _layer_norm_rows(DN_ALPHA * x_ref[...] + ffn, g_ref[...], b_ref[...])


def _moe_combine(ys, pos, rt, x, g, b, tm=256):
    n = x.shape[0]
    nsteps = n // tm
    pos3 = _step_indices(pos, tm)
    smem_blk = lambda f: pl.BlockSpec((1, 1, TOP_K * tm), f, memory_space=pltpu.SMEM)
    row = lambda width: pl.BlockSpec((tm, width), lambda i: (i, 0))
    const = lambda shape: pl.BlockSpec(shape, lambda i: (0, 0))
    return pl.pallas_call(
        functools.partial(_combine_kernel, tm=tm),
        grid=(nsteps,),
        in_specs=[smem_blk(lambda i: (i, 0, 0)), smem_blk(lambda i: (jnp.minimum(i + 1, nsteps - 1), 0, 0)),
                  pl.BlockSpec(memory_space=pl.ANY), row(D_MODEL), row(ROUTE_W),
                  const((1, D_MODEL)), const((1, D_MODEL))],
        out_specs=row(D_MODEL),
        out_shape=jax.ShapeDtypeStruct((n, D_MODEL), F32),
        scratch_shapes=[pltpu.VMEM((2, TOP_K * tm * TOK_TILE, LANES_V7X), U32), pltpu.SemaphoreType.DMA((2,))],
        compiler_params=_cparams(1, 40 * MIB),
        name="moe_combine",
    )(pos3, pos3, ys, x, rt, g.reshape(1, -1), b.reshape(1, -1))


def _moe_layer(x1, xp, logits, w1, w3, w2, layer, g, b):
    n_tok = x1.shape[0]
    rt, cnt = _route(logits)
    pos, blk_e, nused, n_blocks = _block_layout(rt, cnt, n_tok)
    xs = _moe_dispatch(xp, pos, n_blocks * MOE_TB)
    ys = _moe_experts(xs, blk_e, nused, w1, w3, w2, layer)
    return _moe_combine(ys, pos, rt, x1, g, b)


def _gmlp_kernel(u_ref, v_ref, g_ref, b_ref, w_ref, bs_ref, o_ref, *, chunks):
    for ci in range(chunks):
        rows = slice(ci * C_CHUNK, (ci + 1) * C_CHUNK)
        u = jax.nn.gelu(u_ref[rows, :])
        v = _layer_norm_rows(jax.nn.gelu(v_ref[rows, :]), g_ref[...], b_ref[...]).astype(BF16)
        for gi in range(C_GROUPS):
            cols = slice(gi * C_GROUP_DIM, (gi + 1) * C_GROUP_DIM)
            mixed = jnp.dot(w_ref[gi], v[:, cols], preferred_element_type=F32) + bs_ref[:, cols]
            o_ref[rows, cols] = (u[:, cols] * mixed).astype(o_ref.dtype)


def _gmlp(proj, ln_g, ln_b, w_s, b_s, chunks=2):
    n = proj.shape[0]
    tm = chunks * C_CHUNK
    w = (w_s * jnp.tril(jnp.ones((C_CHUNK, C_CHUNK), w_s.dtype))).astype(BF16)
    bs_full = jnp.repeat(b_s.T, C_GROUP_DIM, axis=1)
    const2 = lambda shape: pl.BlockSpec(shape, lambda i: (0, 0))
    return pl.pallas_call(
        functools.partial(_gmlp_kernel, chunks=chunks),
        grid=(n // tm,),
        in_specs=[pl.BlockSpec((tm, C_W), lambda i: (i, 0)), pl.BlockSpec((tm, C_W), lambda i: (i, 1)),
                  const2((1, C_W)), const2((1, C_W)),
                  pl.BlockSpec((C_GROUPS, C_CHUNK, C_CHUNK), lambda i: (0, 0, 0)), const2((C_CHUNK, C_W))],
        out_specs=pl.BlockSpec((tm, C_W), lambda i: (i, 0)),
        out_shape=jax.ShapeDtypeStruct((n, C_W), BF16),
        compiler_params=_cparams(1, 32 * MIB),
        name="gmlp_gating",
    )(proj, proj, ln_g.reshape(1, -1), ln_b.reshape(1, -1), w, bs_full)


CONV_HIST = 32


def _conv_kernel(a_ref, gt_ref, ap_ref, gp_ref, w_ref, cb_ref, g_ref, b_ref, o_ref, hbuf, hshift, *, ts):
    i = pl.program_id(1)
    hist = ap_ref[...] * jax.nn.sigmoid(gp_ref[...])
    hbuf[0:CONV_HIST, :] = jnp.where(i > 0, hist, jnp.zeros_like(hist))
    hbuf[CONV_HIST:CONV_HIST + ts, :] = a_ref[...] * jax.nn.sigmoid(gt_ref[...])
    off = CONV_HIST - (D_CONV - 1)
    acc = jnp.zeros((ts, D_CHANNELS), F32) + cb_ref[...]
    for r in range(SUBLANES_V7X):
        taps = [j for j in range(D_CONV) if (off + j) % SUBLANES_V7X == r]
        if not taps:
            continue
        src = hbuf
        if r:
            span = max(taps) + off - r + ts
            hshift[0:span, :] = hbuf[r:r + span, :]
            src = hshift
        for j in taps:
            base = off + j - r
            acc = acc + w_ref[j:j + 1, :] * src[base:base + ts, :]
    y = _layer_norm_rows(acc, g_ref[...], b_ref[...])
    o_ref[...] = (y * jax.nn.sigmoid(y)).astype(o_ref.dtype)


def _conformer_conv(proj, conv_w, conv_b, ln_g, ln_b, bsz, seq, ts=256):
    nt = seq // ts
    acol = 2 * C_W // D_CHANNELS
    gcol = acol + 1
    hb = ts // CONV_HIST
    cur = lambda col: pl.BlockSpec((ts, D_CHANNELS), lambda b, i: (b * nt + i, col))
    prev = lambda col: pl.BlockSpec((CONV_HIST, D_CHANNELS),
                                    lambda b, i: (jnp.maximum((b * nt + i) * hb - 1, 0), col))
    const2 = lambda shape: pl.BlockSpec(shape, lambda b, i: (0, 0))
    return pl.pallas_call(
        functools.partial(_conv_kernel, ts=ts),
        grid=(bsz, nt),
        in_specs=[cur(acol), cur(gcol), prev(acol), prev(gcol),
                  const2((D_CONV, D_CHANNELS)), const2((1, D_CHANNELS)), const2((1, D_CHANNELS)),
                  const2((1, D_CHANNELS))],
        out_specs=pl.BlockSpec((ts, D_CHANNELS), lambda b, i: (b * nt + i, 0)),
        out_shape=jax.ShapeDtypeStruct((bsz * seq, D_CHANNELS), BF16),
        scratch_shapes=[pltpu.VMEM((CONV_HIST + ts, D_CHANNELS), F32), pltpu.VMEM((CONV_HIST + ts, D_CHANNELS), F32)],
        compiler_params=_cparams(2, 32 * MIB),
        name="conformer_conv",
    )(proj, proj, proj, proj, conv_w, conv_b.reshape(1, -1), ln_g.reshape(1, -1), ln_b.reshape(1, -1))


def kernel(x, w_in_ab, attn_sinks, rel_bias, hgrn_lb_logits, hgrn_norm_g, w_out_ab, w_in_cd, gmlp_ln_g, gmlp_ln_b, gmlp_w_s, gmlp_b_s, conv_w, conv_b, conv_ln_g, conv_ln_b, w_out_cd, ln_mix_g, ln_mix_b, ln_ffn_g, ln_ffn_b, moe_w_group, moe_b_group, moe_w_router, moe_b_router, moe_w1, moe_w3, moe_w2):
    bsz, seq = x.shape[0], x.shape[1]
    n_tok = bsz * seq
    xf = x.reshape(n_tok, D_MODEL)
    lb_table = jnp.cumsum(jax.nn.softmax(hgrn_lb_logits.astype(F32), axis=0), axis=0)
    bias = _bias_table(rel_bias.astype(F32), _t5_bucket_table())

    for layer in range(DEPTH):
        j = layer // 2
        if layer % 2 == 0:
            proj = _matmul(xf, w_in_ab[j].astype(BF16), 512, EVEN_IN // 3, F32)
            ya = _attention(proj, bias, attn_sinks[j].astype(F32), bsz, seq)
            yb = _hgrn(proj, lb_table[layer], hgrn_norm_g[j].astype(F32), bsz, seq)
            w_out = w_out_ab[j]
        else:
            proj = _matmul(xf, w_in_cd[j].astype(BF16), 512, ODD_IN // 2, F32)
            ya = _gmlp(proj, gmlp_ln_g[j], gmlp_ln_b[j], gmlp_w_s[j], gmlp_b_s[j])
            yb = _conformer_conv(proj, conv_w[j], conv_b[j], conv_ln_g[j], conv_ln_b[j], bsz, seq)
            w_out = w_out_cd[j]
        wr_hi, wr_lo, rbias = _router_weights(moe_w_group[layer], moe_b_group[layer],
                                              moe_w_router[layer], moe_b_router[layer])
        x1, xp, logits = _outproj_ln_route(ya, yb, w_out.astype(BF16), xf, ln_mix_g[layer], ln_mix_b[layer],
                                           wr_hi, wr_lo, rbias)
        xf = _moe_layer(x1, xp, logits, moe_w1, moe_w3, moe_w2, layer, ln_ffn_g[layer], ln_ffn_b[layer])
    return xf.reshape(bsz, seq, D_MODEL)
```

```python
import functools
import math

import jax
import jax.numpy as jnp
from jax import lax
from jax.experimental import pallas as pl
from jax.experimental.pallas import tpu as pltpu

D_MODEL = 2048
DEPTH = 2
A_HEADS = 16
A_KV_HEADS = 2
A_HEAD_DIM = 64
A_WINDOW = 128
A_BLOCK = 128
REL_BUCKETS = 32
REL_MAX_DIST = 128
B_HEADS = 8
B_DK = 128
B_DV = 128
C_GROUPS = 8
C_GROUP_DIM = 128
C_CHUNK = 128
D_CHANNELS = 1024
D_CONV = 31
A_QW = A_HEADS * A_HEAD_DIM
A_KVW = A_KV_HEADS * A_HEAD_DIM
B_KW = B_HEADS * B_DK
B_VW = B_HEADS * B_DV
C_W = C_GROUPS * C_GROUP_DIM
EVEN_IN = A_QW + 2 * A_KVW + 2 * B_KW + 2 * B_VW
ODD_IN = 2 * C_W + 2 * D_CHANNELS
N_GROUPS = 4
EXPERTS_PER_GROUP = 8
N_EXPERTS = N_GROUPS * EXPERTS_PER_GROUP
TOP_K = 2
D_EXPERT = 512
DN_ALPHA = (2 * DEPTH) ** 0.25
LN_EPS = 1e-5
RMS_EPS = 1e-6

LANES_V7X = 128
SUBLANES_V7X = 8
MIB = 1024 * 1024
VMEM_LIMIT_V7X = 56 * MIB

HGRN_CHUNK = 128
HGRN_SUB = 16
HGRN_ROWS = 512
MOE_TB = 256
ROUTE_W = LANES_V7X
HALF = D_MODEL // 2
TOK_TILE = HALF // LANES_V7X
assert TOK_TILE == SUBLANES_V7X

BF16 = jnp.bfloat16
F32 = jnp.float32
U32 = jnp.uint32
NEG_INF = float("-inf")
LOG2E = math.log2(math.e)
RT_W0, RT_W1, RT_E0, RT_E1, RT_R0, RT_R1 = range(6)


def _cparams(n_axes, vmem_bytes=None):
    return pltpu.CompilerParams(dimension_semantics=("arbitrary",) * n_axes, vmem_limit_bytes=vmem_bytes)


def _layer_norm_rows(z, g, b):
    mu = jnp.mean(z, axis=-1, keepdims=True)
    zc = z - mu
    var = jnp.mean(zc * zc, axis=-1, keepdims=True)
    return zc * lax.rsqrt(var + LN_EPS) * g + b


def _dot_nt(a, b):
    return lax.dot_general(a, b, (((1,), (1,)), ((), ())), preferred_element_type=F32)


def _dot_tn(a, b):
    return lax.dot_general(a, b, (((0,), (0,)), ((), ())), preferred_element_type=F32)


def _pack_rows(y):
    lo = lax.bitcast_convert_type(y[:, :HALF].astype(BF16).astype(F32), U32) >> 16
    hi = lax.bitcast_convert_type(y[:, HALF:].astype(BF16).astype(F32), U32) & jnp.uint32(0xFFFF0000)
    return lo | hi


def _unpack_rows(p):
    lo = lax.bitcast_convert_type(p << 16, F32)
    hi = lax.bitcast_convert_type(p & jnp.uint32(0xFFFF0000), F32)
    return jnp.concatenate([lo, hi], axis=1)


def _store_token_tiles(ref, base, packed):
    rows = packed.shape[0]
    for c in range(TOK_TILE):
        ref[pl.ds(base + c, rows, stride=TOK_TILE), :] = packed[:, c * LANES_V7X:(c + 1) * LANES_V7X]


def _load_token_tiles(ref, base, rows, lead=None):
    parts = []
    for c in range(TOK_TILE):
        idx = (pl.ds(base + c, rows, stride=TOK_TILE), slice(None))
        parts.append(ref[idx] if lead is None else ref[(lead,) + idx])
    return jnp.concatenate(parts, axis=1)


def _mm_kernel(a_ref, w_ref, o_ref):
    a = a_ref[...].astype(BF16)
    o_ref[...] = jnp.dot(a, w_ref[...], preferred_element_type=F32).astype(o_ref.dtype)


def _matmul(a, w, tm, tn, out_dtype):
    m, k = a.shape
    n = w.shape[1]
    return pl.pallas_call(
        _mm_kernel,
        grid=(n // tn, m // tm),
        in_specs=[pl.BlockSpec((tm, k), lambda j, i: (i, 0)), pl.BlockSpec((k, tn), lambda j, i: (0, j))],
        out_specs=pl.BlockSpec((tm, tn), lambda j, i: (i, j)),
        out_shape=jax.ShapeDtypeStruct((m, n), out_dtype),
        compiler_params=_cparams(2, VMEM_LIMIT_V7X),
        name="proj_matmul",
    )(a, w)


def _bias_table_kernel(rb_ref, bucket_ref, o_ref):
    bucket = bucket_ref[...]
    for h in range(A_HEADS):
        acc = jnp.zeros(bucket.shape, F32)
        for bk in range(REL_BUCKETS):
            acc = jnp.where(bucket == bk, rb_ref[bk, h], acc)
        o_ref[h] = acc


def _bias_table(rel_bias, bucket):
    return pl.pallas_call(
        _bias_table_kernel,
        in_specs=[pl.BlockSpec(memory_space=pltpu.SMEM), pl.BlockSpec(memory_space=pltpu.VMEM)],
        out_specs=pl.BlockSpec(memory_space=pltpu.VMEM),
        out_shape=jax.ShapeDtypeStruct((A_HEADS,) + bucket.shape, F32),
        name="rel_bias_table",
    )(rel_bias, bucket)


def _t5_bucket_table():
    t_loc = jnp.arange(A_BLOCK, dtype=jnp.int32)[:, None]
    s_loc = jnp.arange(2 * A_BLOCK, dtype=jnp.int32)[None, :]
    dist = jnp.maximum(t_loc + A_BLOCK - s_loc, 0)
    max_exact = REL_BUCKETS // 2
    d = jnp.maximum(dist, 1).astype(F32)
    large = max_exact + (jnp.log(d / max_exact) / math.log(REL_MAX_DIST / max_exact)
                         * (REL_BUCKETS - max_exact)).astype(jnp.int32)
    large = jnp.minimum(large, REL_BUCKETS - 1)
    return jnp.where(dist < max_exact, dist, large)


def _attn_kernel(q_ref, kc_ref, vc_ref, kp_ref, vp_ref, bias_ref, sink_ref, o_ref):
    n = pl.program_id(1)
    blk = A_BLOCK
    k2 = jnp.concatenate([kp_ref[...], kc_ref[...]], axis=0)
    v2 = jnp.concatenate([vp_ref[...], vc_ref[...]], axis=0)
    k2r = pltpu.roll(k2, A_HEAD_DIM, 1)
    v2r = pltpu.roll(v2, A_HEAD_DIM, 1)
    lo = lax.broadcasted_iota(jnp.int32, k2.shape, 1) < A_HEAD_DIM
    zero = jnp.zeros_like(k2)

    def placed(x, xr, g, par):
        src = x if g == par else xr
        return (jnp.where(lo, src, zero) if par == 0 else jnp.where(lo, zero, src)).astype(BF16)

    kk = [[placed(k2, k2r, g, par) for par in range(2)] for g in range(A_KV_HEADS)]
    vv = [[placed(v2, v2r, g, par) for par in range(2)] for g in range(A_KV_HEADS)]

    t_loc = lax.broadcasted_iota(jnp.int32, (blk, 2 * blk), 0)
    s_loc = lax.broadcasted_iota(jnp.int32, (blk, 2 * blk), 1)
    dist = t_loc + blk - s_loc
    valid = (dist >= 0) & (dist < A_WINDOW) & ((s_loc >= blk) | (n > 0))

    heads_per_kv = A_HEADS // A_KV_HEADS
    for p in range(A_HEADS // 2):
        g = (2 * p) // heads_per_kv
        qp = (q_ref[:, p * 128:(p + 1) * 128] * (A_HEAD_DIM ** -0.5)).astype(BF16)
        acc = jnp.zeros((blk, 128), F32)
        for par in range(2):
            h = 2 * p + par
            sink = sink_ref[h]
            logits = jnp.where(valid, _dot_nt(qp, kk[g][par]) + bias_ref[h], NEG_INF)
            m = jnp.maximum(jnp.max(logits, axis=-1, keepdims=True), sink)
            e = jnp.exp(logits - m)
            den = jnp.sum(e, axis=-1, keepdims=True) + jnp.exp(sink - m)
            acc = acc + jnp.dot(e.astype(BF16), vv[g][par], preferred_element_type=F32) * (1.0 / den)
        o_ref[:, p * 128:(p + 1) * 128] = acc.astype(o_ref.dtype)


def _attention(proj, bias, sinks, bsz, seq):
    nb = seq // A_BLOCK
    kcol = A_QW // 128
    vcol = kcol + 1
    row = lambda b, n: b * nb + n
    prow = lambda b, n: b * nb + jnp.maximum(n - 1, 0)
    return pl.pallas_call(
        _attn_kernel,
        grid=(bsz, nb),
        in_specs=[
            pl.BlockSpec((A_BLOCK, A_QW), lambda b, n: (row(b, n), 0)),
            pl.BlockSpec((A_BLOCK, 128), lambda b, n: (row(b, n), kcol)),
            pl.BlockSpec((A_BLOCK, 128), lambda b, n: (row(b, n), vcol)),
            pl.BlockSpec((A_BLOCK, 128), lambda b, n: (prow(b, n), kcol)),
            pl.BlockSpec((A_BLOCK, 128), lambda b, n: (prow(b, n), vcol)),
            pl.BlockSpec((A_HEADS, A_BLOCK, 2 * A_BLOCK), lambda b, n: (0, 0, 0)),
            pl.BlockSpec(memory_space=pltpu.SMEM),
        ],
        out_specs=pl.BlockSpec((A_BLOCK, A_QW), lambda b, n: (row(b, n), 0)),
        out_shape=jax.ShapeDtypeStruct((bsz * seq, A_QW), BF16),
        compiler_params=_cparams(2, 32 * MIB),
        name="swa_attention",
    )(proj, proj, proj, proj, proj, bias, sinks)


HGRN_HEADS_PER_STEP = 2


def _hgrn_kernel(q_ref, f_ref, i_ref, g_ref, lb_ref, ng_ref, o_ref, st_ref):
    c, sub = HGRN_CHUNK, HGRN_SUB
    nsub = c // sub

    @pl.when(pl.program_id(2) == 0)
    def _():
        st_ref[...] = jnp.zeros_like(st_ref)

    rid = lax.broadcasted_iota(jnp.int32, (c, c), 0)
    cid = lax.broadcasted_iota(jnp.int32, (c, c), 1)
    blk_start = (rid // sub) * sub
    m_cum = jnp.concatenate([(cid < blk_start).astype(F32),
                             ((cid >= blk_start) & (cid <= rid)).astype(F32)], axis=0).astype(BF16)
    band = jnp.where(cid >= blk_start, rid - cid, -1)
    heads = range(HGRN_HEADS_PER_STEP)

    def scores(hh, r0):
        cols = slice(hh * 128, (hh + 1) * 128)
        lb = lb_ref[0, :, cols]
        q = q_ref[pl.ds(r0, c), cols]
        fl = f_ref[pl.ds(r0, c), cols]
        qf = q * jax.nn.sigmoid(q)
        f = lb + (1.0 - lb) * jax.nn.sigmoid(fl)
        kin = 1.0 - f
        logf = jnp.log(f)
        l1 = logf.astype(BF16)
        res = logf - l1.astype(F32)
        l2 = res.astype(BF16)
        l3 = (res - l2.astype(F32)).astype(BF16)
        cum = jnp.dot(m_cum, jnp.concatenate([l1, l2, l3], axis=1), preferred_element_type=F32)
        cum = cum[:, :B_DK] + cum[:, B_DK:2 * B_DK] + cum[:, 2 * B_DK:]
        rr, bq = cum[:c], cum[c:]
        wq = jnp.log(kin) - bq
        wb = wq - rr
        qt = qf * jnp.exp(bq)
        o = _dot_nt((qt * jnp.exp(rr)).astype(BF16), st_ref[hh].astype(BF16))

        kts = []
        for i in range(1, nsub):
            n = i * sub
            kts += [jnp.exp(rr[n:n + 1, :] + wb[:n, :]).astype(BF16), jnp.zeros((c - n, B_DK), BF16)]
        g_off = _dot_nt(qt.astype(BF16), jnp.concatenate(kts, axis=0))
        s = jnp.concatenate([jnp.zeros((sub, c), F32)] +
                            [g_off[i * sub:(i + 1) * sub, (i - 1) * c:i * c] for i in range(1, nsub)], axis=0)

        bq2, wq2 = bq * LOG2E, wq * LOG2E
        kds = [kin.astype(BF16)] + [jnp.exp2(pltpu.roll(bq2, c - d, 0) + wq2).astype(BF16) for d in range(1, sub)]
        g_diag = _dot_nt(qf.astype(BF16), jnp.concatenate(kds, axis=0))
        return dict(o=o, s=s, g_diag=g_diag, wb=wb, b_last=rr[c - 1:c, :] + bq[c - 1:c, :])

    def finish(hh, r0, h):
        cols = slice(hh * 128, (hh + 1) * 128)
        vb = i_ref[pl.ds(r0, c), cols].astype(BF16)
        gt = g_ref[pl.ds(r0, c), cols]
        o = h["o"] + jnp.dot(h["s"].astype(BF16), vb, preferred_element_type=F32)
        st_ref[hh] = (st_ref[hh] * jnp.exp(h["b_last"])
                      + _dot_tn(vb, jnp.exp(h["b_last"] + h["wb"]).astype(BF16)))
        o = o * lax.rsqrt(jnp.mean(o * o, axis=-1, keepdims=True) + RMS_EPS)
        o_ref[pl.ds(r0, c), cols] = (o * ng_ref[0, :, cols] * (gt * jax.nn.sigmoid(gt))).astype(o_ref.dtype)

    def chunk(ci, carry):
        r0 = pl.multiple_of(ci * c, c)
        hs = [scores(hh, r0) for hh in heads]
        for d in range(sub):
            on_diag = band == d
            for h in hs:
                h["s"] = jnp.where(on_diag, h["g_diag"][:, d * c:(d + 1) * c], h["s"])
        for hh in heads:
            finish(hh, r0, hs[hh])
        return carry

    lax.fori_loop(0, HGRN_ROWS // c, chunk, 0, unroll=True)


def _hgrn(proj, lb, norm_g, bsz, seq):
    nr = seq // HGRN_ROWS
    hp = HGRN_HEADS_PER_STEP
    width = hp * 128
    c0 = (A_QW + 2 * A_KVW) // width
    nhp = B_HEADS // hp
    spec = lambda off: pl.BlockSpec((HGRN_ROWS, width), lambda b, h, r: (b * nr + r, off + h))
    vec = pl.BlockSpec((1, 1, width), lambda b, h, r: (h, 0, 0))
    return pl.pallas_call(
        _hgrn_kernel,
        grid=(bsz, nhp, nr),
        in_specs=[spec(c0), spec(c0 + nhp), spec(c0 + 2 * nhp), spec(c0 + 3 * nhp), vec, vec],
        out_specs=pl.BlockSpec((HGRN_ROWS, width), lambda b, h, r: (b * nr + r, h)),
        out_shape=jax.ShapeDtypeStruct((bsz * seq, B_VW), BF16),
        scratch_shapes=[pltpu.VMEM((hp, B_DV, B_DK), F32)],
        compiler_params=_cparams(3, 32 * MIB),
        name="hgrn2",
    )(proj, proj, proj, proj, lb.reshape(nhp, 1, width), norm_g.reshape(nhp, 1, width))


def _outproj_kernel(ya_ref, yb_ref, wa_ref, wb_ref, x_ref, g_ref, b_ref, wrh_ref, wrl_ref, rb_ref,
                    xo_ref, xp_ref, lg_ref):
    mix = jnp.dot(ya_ref[...], wa_ref[...], preferred_element_type=F32)
    mix = mix + jnp.dot(yb_ref[...], wb_ref[...], preferred_element_type=F32)
    y = _layer_norm_rows(DN_ALPHA * x_ref[...] + mix, g_ref[...], b_ref[...])
    xo_ref[...] = y
    _store_token_tiles(xp_ref, 0, _pack_rows(y))
    y_hi = y.astype(BF16)
    y_lo = (y - y_hi.astype(F32)).astype(BF16)
    lg = jnp.dot(y_hi, wrh_ref[...], preferred_element_type=F32)
    lg = lg + jnp.dot(y_lo, wrh_ref[...], preferred_element_type=F32)
    lg = lg + jnp.dot(y_hi, wrl_ref[...], preferred_element_type=F32)
    lg_ref[...] = lg + rb_ref[...]


def _outproj_ln_route(ya, yb, w_out, x, g, b, wr_hi, wr_lo, rbias, tm=256):
    n = x.shape[0]
    ka = ya.shape[1]
    row = lambda width: pl.BlockSpec((tm, width), lambda i: (i, 0))
    const = lambda shape: pl.BlockSpec(shape, lambda i: (0, 0))
    return pl.pallas_call(
        _outproj_kernel,
        grid=(n // tm,),
        in_specs=[row(ka), row(ka),
                  pl.BlockSpec((ka, D_MODEL), lambda i: (0, 0)), pl.BlockSpec((ka, D_MODEL), lambda i: (1, 0)),
                  row(D_MODEL), const((1, D_MODEL)), const((1, D_MODEL)),
                  const((D_MODEL, ROUTE_W)), const((D_MODEL, ROUTE_W)), const((1, ROUTE_W))],
        out_specs=[row(D_MODEL), pl.BlockSpec((tm * TOK_TILE, LANES_V7X), lambda i: (i, 0)), row(ROUTE_W)],
        out_shape=[jax.ShapeDtypeStruct((n, D_MODEL), F32),
                   jax.ShapeDtypeStruct((n * TOK_TILE, LANES_V7X), U32),
                   jax.ShapeDtypeStruct((n, ROUTE_W), F32)],
        compiler_params=_cparams(1, VMEM_LIMIT_V7X),
        name="outproj_ln_route",
    )(ya, yb, w_out, w_out, x, g.reshape(1, -1), b.reshape(1, -1), wr_hi, wr_lo, rbias)


def _router_weights(w_group, b_group, w_router, b_router):
    w = jnp.zeros((D_MODEL, ROUTE_W), F32)
    w = w.at[:, :N_GROUPS].set(w_group).at[:, N_GROUPS:N_GROUPS + N_EXPERTS].set(w_router)
    w_hi = w.astype(BF16)
    w_lo = (w - w_hi.astype(F32)).astype(BF16)
    rb = jnp.zeros((1, ROUTE_W), F32)
    rb = rb.at[0, :N_GROUPS].set(b_group).at[0, N_GROUPS:N_GROUPS + N_EXPERTS].set(b_router)
    return w_hi, w_lo, rb


def _route_kernel(lg_ref, rt_ref, cnt_ref, run_ref):
    @pl.when(pl.program_id(0) == 0)
    def _():
        run_ref[...] = jnp.zeros_like(run_ref)

    lg = lg_ref[...]
    tm = lg.shape[0]
    lane = lax.broadcasted_iota(jnp.int32, lg.shape, 1)
    sentinel = jnp.int32(ROUTE_W)
    rowmax = lambda mask: jnp.max(jnp.where(mask, lg, NEG_INF), axis=-1, keepdims=True)
    first = lambda mask: jnp.min(jnp.where(mask, lane, sentinel), axis=-1, keepdims=True)

    is_g = lane < N_GROUPS
    gmax = rowmax(is_g)
    g_idx = first(is_g & (lg == gmax))
    g_w = 1.0 / jnp.sum(jnp.where(is_g, jnp.exp(lg - gmax), 0.0), axis=-1, keepdims=True)

    e_lane = lane - N_GROUPS
    sel = (e_lane >= 0) & (e_lane < N_EXPERTS) & ((e_lane >> 3) == g_idx)
    m1 = rowmax(sel)
    i1 = first(sel & (lg == m1))
    sel2 = sel & (lane != i1)
    m2 = rowmax(sel2)
    i2 = first(sel2 & (lg == m2))
    ex = jnp.exp(m2 - m1)
    w0 = g_w / (1.0 + ex)
    w1 = g_w * ex / (1.0 + ex)

    oh0 = lane == i1
    oh1 = lane == i2
    both = (oh0 | oh1).astype(F32)
    rid = lax.broadcasted_iota(jnp.int32, (tm, tm), 0)
    cid = lax.broadcasted_iota(jnp.int32, (tm, tm), 1)
    before = jnp.dot((cid < rid).astype(BF16), both.astype(BF16), preferred_element_type=F32) + run_ref[...]
    rank0 = jnp.sum(jnp.where(oh0, before, 0.0), axis=-1, keepdims=True)
    rank1 = jnp.sum(jnp.where(oh1, before, 0.0), axis=-1, keepdims=True)
    run = run_ref[...] + jnp.sum(both, axis=0, keepdims=True)
    run_ref[...] = run
    cnt_ref[...] = run

    slab = jnp.zeros(lg.shape, F32)
    for ln, val in ((RT_W0, w0), (RT_W1, w1), (RT_E0, (i1 - N_GROUPS).astype(F32)),
                    (RT_E1, (i2 - N_GROUPS).astype(F32)), (RT_R0, rank0), (RT_R1, rank1)):
        slab = jnp.where(lane == ln, val, slab)
    rt_ref[...] = slab


assert EXPERTS_PER_GROUP == 8


def _route(logits, tm=256):
    n = logits.shape[0]
    return pl.pallas_call(
        _route_kernel,
        grid=(n // tm,),
        in_specs=[pl.BlockSpec((tm, ROUTE_W), lambda i: (i, 0))],
        out_specs=[pl.BlockSpec((tm, ROUTE_W), lambda i: (i, 0)), pl.BlockSpec((1, ROUTE_W), lambda i: (0, 0))],
        out_shape=[jax.ShapeDtypeStruct((n, ROUTE_W), F32), jax.ShapeDtypeStruct((1, ROUTE_W), F32)],
        scratch_shapes=[pltpu.VMEM((1, ROUTE_W), F32)],
        compiler_params=_cparams(1),
        name="moe_router",
    )(logits)


def _block_layout(rt, cnt, n_tok):
    m = n_tok * TOP_K
    counts = cnt[0, N_GROUPS:N_GROUPS + N_EXPERTS].astype(jnp.int32)
    pcounts = (counts + MOE_TB - 1) // MOE_TB * MOE_TB
    pends = jnp.cumsum(pcounts)
    pstarts = pends - pcounts
    n_blocks = -(-(m + N_EXPERTS * (MOE_TB - 1)) // MOE_TB)
    e_idx = rt[:, RT_E0:RT_E1 + 1].astype(jnp.int32)
    rank = rt[:, RT_R0:RT_R1 + 1].astype(jnp.int32)
    onehot = e_idx[:, :, None] == jnp.arange(N_EXPERTS, dtype=jnp.int32)
    pos = jnp.sum(jnp.where(onehot, pstarts, 0), axis=-1) + rank
    blk_start = jnp.arange(n_blocks, dtype=jnp.int32) * MOE_TB
    blk_e = jnp.minimum(jnp.sum(blk_start[:, None] >= pends[None, :], axis=-1), N_EXPERTS - 1).astype(jnp.int32)
    nused = (pends[-1:] // MOE_TB).astype(jnp.int32)
    ids = jnp.arange(N_EXPERTS, dtype=jnp.int32)
    later_used = (ids[None, :] > ids[:, None]) & (counts[None, :] > 0)
    next_used = jnp.min(jnp.where(later_used, ids[None, :], N_EXPERTS), axis=-1)
    next_used = jnp.where(next_used == N_EXPERTS, -1, next_used).astype(jnp.int32)
    nxt_e = jnp.sum(jnp.where(blk_e[:, None] == ids[None, :], next_used[None, :], 0), axis=-1).astype(jnp.int32)
    return pos, blk_e, nxt_e, nused, n_blocks


def _step_indices(pos, tm):
    nsteps = pos.shape[0] // tm
    return pos.reshape(nsteps, tm, TOP_K).transpose(0, 2, 1).reshape(nsteps, 1, TOP_K * tm)


def _tile_rows(i):
    return pl.ds(pl.multiple_of(i * TOK_TILE, TOK_TILE), TOK_TILE)


def _dispatch_kernel(pos_ref, xp_ref, init_hbm, xs_hbm, xbuf, sem, *, tm):
    del init_hbm
    s = pl.program_id(0)
    nsteps = pl.num_programs(0)
    slot = s % 2

    def copy(j, dst, sl):
        return pltpu.make_async_copy(xbuf.at[sl, _tile_rows(j)], xs_hbm.at[_tile_rows(dst)], sem.at[sl])

    def drain(sl):
        def body(j, c):
            copy(0, 0, sl).wait()
            return c
        lax.fori_loop(0, TOP_K * tm, body, 0, unroll=8)

    @pl.when(s >= 2)
    def _():
        drain(slot)

    xbuf[slot] = xp_ref[...]

    for j in range(tm):
        for k in range(TOP_K):
            copy(j, pos_ref[0, 0, k * tm + j], slot).start()

    @pl.when(s == nsteps - 1)
    def _():
        drain(1 - slot)
        drain(slot)


def _moe_dispatch(xp, pos, n_pad, tm=256):
    n = pos.shape[0]
    assert n // tm >= 2
    pos3 = _step_indices(pos, tm)
    init = jnp.zeros((n_pad * TOK_TILE, LANES_V7X), U32)
    return pl.pallas_call(
        functools.partial(_dispatch_kernel, tm=tm),
        grid=(n // tm,),
        in_specs=[pl.BlockSpec((1, 1, TOP_K * tm), lambda i: (i, 0, 0), memory_space=pltpu.SMEM),
                  pl.BlockSpec((tm * TOK_TILE, LANES_V7X), lambda i: (i, 0)), pl.BlockSpec(memory_space=pl.ANY)],
        out_specs=pl.BlockSpec(memory_space=pl.ANY),
        out_shape=jax.ShapeDtypeStruct(init.shape, U32),
        scratch_shapes=[pltpu.VMEM((2, tm * TOK_TILE, LANES_V7X), U32), pltpu.SemaphoreType.DMA((2,))],
        input_output_aliases={2: 0},
        compiler_params=_cparams(1),
        name="moe_dispatch",
    )(pos3, xp, init)


def _moe_kernel(blk_e_ref, nxt_e_ref, nused_ref, xs_ref, w1_hbm, w3_hbm, w2_hbm, ys_ref,
                wf1, wf3, wf2, w1b, w3b, w2b, slot_ref, sem, *, layer):
    s = pl.program_id(0)
    nused = nused_ref[0]

    def fetch(e, sl):
        return [pltpu.make_async_copy(w_hbm.at[layer, e], wf.at[sl], sem.at[sl, k])
                for k, (w_hbm, wf) in enumerate(((w1_hbm, wf1), (w3_hbm, wf3), (w2_hbm, wf2)))]

    @pl.when(s >= nused)
    def _():
        ys_ref[...] = jnp.zeros_like(ys_ref)

    @pl.when(s < nused)
    def _():
        e = blk_e_ref[s]
        prev = blk_e_ref[jnp.maximum(s - 1, 0)]

        @pl.when(s == 0)
        def _():
            slot_ref[0] = 0
            for cp in fetch(e, 0):
                cp.start()

        @pl.when((s > 0) & (e != prev))
        def _():
            slot_ref[0] = 1 - slot_ref[0]

        @pl.when((s == 0) | (e != prev))
        def _():
            sl = slot_ref[0]
            for cp in fetch(e, sl):
                cp.wait()
            w1b[...] = wf1[sl].astype(BF16)
            w3b[...] = wf3[sl].astype(BF16)
            w2b[...] = wf2[sl].astype(BF16)
            nxt = nxt_e_ref[s]

            @pl.when(nxt >= 0)
            def _():
                for cp in fetch(nxt, 1 - sl):
                    cp.start()

        xb = _unpack_rows(_load_token_tiles(xs_ref, 0, MOE_TB)).astype(BF16)
        h1 = jnp.dot(xb, w1b[...], preferred_element_type=F32)
        h3 = jnp.dot(xb, w3b[...], preferred_element_type=F32)
        h = (h1 * jax.nn.sigmoid(h1) * h3).astype(BF16)
        y = jnp.dot(h, w2b[...], preferred_element_type=F32)
        _store_token_tiles(ys_ref, 0, _pack_rows(y))


def _moe_experts(xs, blk_e, nxt_e, nused, w1, w3, w2, layer):
    n_blocks = blk_e.shape[0]
    tiles = pl.BlockSpec((MOE_TB * TOK_TILE, LANES_V7X), lambda s, be, ne, nu: (s, 0))
    hbm = pl.BlockSpec(memory_space=pl.ANY)
    up, down = (D_MODEL, D_EXPERT), (D_EXPERT, D_MODEL)
    grid_spec = pltpu.PrefetchScalarGridSpec(
        num_scalar_prefetch=3,
        grid=(n_blocks,),
        in_specs=[tiles, hbm, hbm, hbm],
        out_specs=tiles,
        scratch_shapes=[pltpu.VMEM((2,) + up, F32), pltpu.VMEM((2,) + up, F32), pltpu.VMEM((2,) + down, F32),
                        pltpu.VMEM(up, BF16), pltpu.VMEM(up, BF16), pltpu.VMEM(down, BF16),
                        pltpu.SMEM((1,), jnp.int32), pltpu.SemaphoreType.DMA((2, 3))],
    )
    return pl.pallas_call(
        functools.partial(_moe_kernel, layer=layer),
        grid_spec=grid_spec,
        out_shape=jax.ShapeDtypeStruct(xs.shape, U32),
        compiler_params=_cparams(1, VMEM_LIMIT_V7X),
        name="moe_experts",
    )(blk_e, nxt_e, nused, xs, w1, w3, w2)


def _combine_kernel(pos_ref, nxt_ref, ys_hbm, x_ref, rt_ref, g_ref, b_ref, o_ref, ybuf, sem, *, tm):
    s = pl.program_id(0)
    nsteps = pl.num_programs(0)
    slot = s % 2

    def copy(src, j, sl):
        return pltpu.make_async_copy(ys_hbm.at[_tile_rows(src)], ybuf.at[sl, _tile_rows(j)], sem.at[sl])

    def start_gather(idx_ref, sl):
        def body(j, c):
            copy(idx_ref[0, 0, j], j, sl).start()
            return c
        lax.fori_loop(0, TOP_K * tm, body, 0, unroll=8)

    @pl.when(s == 0)
    def _():
        start_gather(pos_ref, 0)

    @pl.when(s + 1 < nsteps)
    def _():
        for j in range(TOP_K * tm):
            copy(nxt_ref[0, 0, j], j, 1 - slot).start()

    def wait_body(j, c):
        copy(0, j, slot).wait()
        return c
    lax.fori_loop(0, TOP_K * tm, wait_body, 0, unroll=8)

    rt = rt_ref[...]
    ffn = _unpack_rows(_load_token_tiles(ybuf, 0, tm, lead=slot)) * rt[:, RT_W0:RT_W0 + 1]
    ffn = ffn + _unpack_rows(_load_token_tiles(ybuf, tm * TOK_TILE, tm, lead=slot)) * rt[:, RT_W1:RT_W1 + 1]
    o_ref[...] = _layer_norm_rows(DN_ALPHA * x_ref[...] + ffn, g_ref[...], b_ref[...])


def _moe_combine(ys, pos, rt, x, g, b, tm=256):
    n = x.shape[0]
    nsteps = n // tm
    pos3 = _step_indices(pos, tm)
    smem_blk = lambda f: pl.BlockSpec((1, 1, TOP_K * tm), f, memory_space=pltpu.SMEM)
    row = lambda width: pl.BlockSpec((tm, width), lambda i: (i, 0))
    const = lambda shape: pl.BlockSpec(shape, lambda i: (0, 0))
    return pl.pallas_call(
        functools.partial(_combine_kernel, tm=tm),
        grid=(nsteps,),
        in_specs=[smem_blk(lambda i: (i, 0, 0)), smem_blk(lambda i: (jnp.minimum(i + 1, nsteps - 1), 0, 0)),
                  pl.BlockSpec(memory_space=pl.ANY), row(D_MODEL), row(ROUTE_W),
                  const((1, D_MODEL)), const((1, D_MODEL))],
        out_specs=row(D_MODEL),
        out_shape=jax.ShapeDtypeStruct((n, D_MODEL), F32),
        scratch_shapes=[pltpu.VMEM((2, TOP_K * tm * TOK_TILE, LANES_V7X), U32), pltpu.SemaphoreType.DMA((2,))],
        compiler_params=_cparams(1, 40 * MIB),
        name="moe_combine",
    )(pos3, pos3, ys, x, rt, g.reshape(1, -1), b.reshape(1, -1))


def _moe_layer(x1, xp, logits, w1, w3, w2, layer, g, b):
    n_tok = x1.shape[0]
    rt, cnt = _route(logits)
    pos, blk_e, nxt_e, nused, n_blocks = _block_layout(rt, cnt, n_tok)
    xs = _moe_dispatch(xp, pos, n_blocks * MOE_TB)
    ys = _moe_experts(xs, blk_e, nxt_e, nused, w1, w3, w2, layer)
    return _moe_combine(ys, pos, rt, x1, g, b)


def _gmlp_kernel(u_ref, v_ref, g_ref, b_ref, w_ref, bs_ref, o_ref, *, chunks):
    for ci in range(chunks):
        rows = slice(ci * C_CHUNK, (ci + 1) * C_CHUNK)
        u = jax.nn.gelu(u_ref[rows, :])
        v = _layer_norm_rows(jax.nn.gelu(v_ref[rows, :]), g_ref[...], b_ref[...]).astype(BF16)
        for gi in range(C_GROUPS):
            cols = slice(gi * C_GROUP_DIM, (gi + 1) * C_GROUP_DIM)
            mixed = jnp.dot(w_ref[gi], v[:, cols], preferred_element_type=F32) + bs_ref[:, cols]
            o_ref[rows, cols] = (u[:, cols] * mixed).astype(o_ref.dtype)


def _gmlp(proj, ln_g, ln_b, w_s, b_s, chunks=2):
    n = proj.shape[0]
    tm = chunks * C_CHUNK
    w = (w_s * jnp.tril(jnp.ones((C_CHUNK, C_CHUNK), w_s.dtype))).astype(BF16)
    bs_full = jnp.repeat(b_s.T, C_GROUP_DIM, axis=1)
    const2 = lambda shape: pl.BlockSpec(shape, lambda i: (0, 0))
    return pl.pallas_call(
        functools.partial(_gmlp_kernel, chunks=chunks),
        grid=(n // tm,),
        in_specs=[pl.BlockSpec((tm, C_W), lambda i: (i, 0)), pl.BlockSpec((tm, C_W), lambda i: (i, 1)),
                  const2((1, C_W)), const2((1, C_W)),
                  pl.BlockSpec((C_GROUPS, C_CHUNK, C_CHUNK), lambda i: (0, 0, 0)), const2((C_CHUNK, C_W))],
        out_specs=pl.BlockSpec((tm, C_W), lambda i: (i, 0)),
        out_shape=jax.ShapeDtypeStruct((n, C_W), BF16),
        compiler_params=_cparams(1, 32 * MIB),
        name="gmlp_gating",
    )(proj, proj, ln_g.reshape(1, -1), ln_b.reshape(1, -1), w, bs_full)


CONV_HIST = 32


def _conv_kernel(a_ref, gt_ref, ap_ref, gp_ref, w_ref, cb_ref, g_ref, b_ref, o_ref, hbuf, hshift, *, ts):
    i = pl.program_id(1)
    hist = ap_ref[...] * jax.nn.sigmoid(gp_ref[...])
    hbuf[0:CONV_HIST, :] = jnp.where(i > 0, hist, jnp.zeros_like(hist))
    hbuf[CONV_HIST:CONV_HIST + ts, :] = a_ref[...] * jax.nn.sigmoid(gt_ref[...])
    off = CONV_HIST - (D_CONV - 1)
    acc = jnp.zeros((ts, D_CHANNELS), F32) + cb_ref[...]
    for r in range(SUBLANES_V7X):
        taps = [j for j in range(D_CONV) if (off + j) % SUBLANES_V7X == r]
        if not taps:
            continue
        src = hbuf
        if r:
            span = max(taps) + off - r + ts
            hshift[0:span, :] = hbuf[r:r + span, :]
            src = hshift
        for j in taps:
            base = off + j - r
            acc = acc + w_ref[j:j + 1, :] * src[base:base + ts, :]
    y = _layer_norm_rows(acc, g_ref[...], b_ref[...])
    o_ref[...] = (y * jax.nn.sigmoid(y)).astype(o_ref.dtype)


def _conformer_conv(proj, conv_w, conv_b, ln_g, ln_b, bsz, seq, ts=256):
    nt = seq // ts
    acol = 2 * C_W // D_CHANNELS
    gcol = acol + 1
    hb = ts // CONV_HIST
    cur = lambda col: pl.BlockSpec((ts, D_CHANNELS), lambda b, i: (b * nt + i, col))
    prev = lambda col: pl.BlockSpec((CONV_HIST, D_CHANNELS),
                                    lambda b, i: (jnp.maximum((b * nt + i) * hb - 1, 0), col))
    const2 = lambda shape: pl.BlockSpec(shape, lambda b, i: (0, 0))
    return pl.pallas_call(
        functools.partial(_conv_kernel, ts=ts),
        grid=(bsz, nt),
        in_specs=[cur(acol), cur(gcol), prev(acol), prev(gcol),
                  const2((D_CONV, D_CHANNELS)), const2((1, D_CHANNELS)), const2((1, D_CHANNELS)),
                  const2((1, D_CHANNELS))],
        out_specs=pl.BlockSpec((ts, D_CHANNELS), lambda b, i: (b * nt + i, 0)),
        out_shape=jax.ShapeDtypeStruct((bsz * seq, D_CHANNELS), BF16),
        scratch_shapes=[pltpu.VMEM((CONV_HIST + ts, D_CHANNELS), F32), pltpu.VMEM((CONV_HIST + ts, D_CHANNELS), F32)],
        compiler_params=_cparams(2, 32 * MIB),
        name="conformer_conv",
    )(proj, proj, proj, proj, conv_w, conv_b.reshape(1, -1), ln_g.reshape(1, -1), ln_b.reshape(1, -1))


def kernel(x, w_in_ab, attn_sinks, rel_bias, hgrn_lb_logits, hgrn_norm_g, w_out_ab, w_in_cd, gmlp_ln_g, gmlp_ln_b, gmlp_w_s, gmlp_b_s, conv_w, conv_b, conv_ln_g, conv_ln_b, w_out_cd, ln_mix_g, ln_mix_b, ln_ffn_g, ln_ffn_b, moe_w_group, moe_b_group, moe_w_router, moe_b_router, moe_w1, moe_w3, moe_w2):
    bsz, seq = x.shape[0], x.shape[1]
    n_tok = bsz * seq
    xf = x.reshape(n_tok, D_MODEL)
    lb_table = jnp.cumsum(jax.nn.softmax(hgrn_lb_logits.astype(F32), axis=0), axis=0)
    bias = _bias_table(rel_bias.astype(F32), _t5_bucket_table())

    for layer in range(DEPTH):
        j = layer // 2
        if layer % 2 == 0:
            proj = _matmul(xf, w_in_ab[j].astype(BF16), 512, EVEN_IN // 3, F32)
            ya = _attention(proj, bias, attn_sinks[j].astype(F32), bsz, seq)
            yb = _hgrn(proj, lb_table[layer], hgrn_norm_g[j].astype(F32), bsz, seq)
            w_out = w_out_ab[j]
        else:
            proj = _matmul(xf, w_in_cd[j].astype(BF16), 512, ODD_IN // 2, F32)
            ya = _gmlp(proj, gmlp_ln_g[j], gmlp_ln_b[j], gmlp_w_s[j], gmlp_b_s[j])
            yb = _conformer_conv(proj, conv_w[j], conv_b[j], conv_ln_g[j], conv_ln_b[j], bsz, seq)
            w_out = w_out_cd[j]
        wr_hi, wr_lo, rbias = _router_weights(moe_w_group[layer], moe_b_group[layer],
                                              moe_w_router[layer], moe_b_router[layer])
        x1, xp, logits = _outproj_ln_route(ya, yb, w_out.astype(BF16), xf, ln_mix_g[layer], ln_mix_b[layer],
                                           wr_hi, wr_lo, rbias)
        xf = _moe_layer(x1, xp, logits, moe_w1, moe_w3, moe_w2, layer, ln_ffn_g[layer], ln_ffn_b[layer])
    return xf.reshape(bsz, seq, D_MODEL)
```

```python
import functools
import math

import jax
import jax.numpy as jnp
from jax import lax
from jax.experimental import pallas as pl
from jax.experimental.pallas import tpu as pltpu

D_MODEL = 2048
DEPTH = 2
A_HEADS = 16
A_KV_HEADS = 2
A_HEAD_DIM = 64
A_WINDOW = 128
A_BLOCK = 128
REL_BUCKETS = 32
REL_MAX_DIST = 128
B_HEADS = 8
B_DK = 128
B_DV = 128
C_GROUPS = 8
C_GROUP_DIM = 128
C_CHUNK = 128
D_CHANNELS = 1024
D_CONV = 31
A_QW = A_HEADS * A_HEAD_DIM
A_KVW = A_KV_HEADS * A_HEAD_DIM
B_KW = B_HEADS * B_DK
B_VW = B_HEADS * B_DV
C_W = C_GROUPS * C_GROUP_DIM
EVEN_IN = A_QW + 2 * A_KVW + 2 * B_KW + 2 * B_VW
ODD_IN = 2 * C_W + 2 * D_CHANNELS
N_GROUPS = 4
EXPERTS_PER_GROUP = 8
N_EXPERTS = N_GROUPS * EXPERTS_PER_GROUP
TOP_K = 2
D_EXPERT = 512
DN_ALPHA = (2 * DEPTH) ** 0.25
LN_EPS = 1e-5
RMS_EPS = 1e-6

LANES_V7X = 128
SUBLANES_V7X = 8
MIB = 1024 * 1024
VMEM_LIMIT_V7X = 56 * MIB

HGRN_CHUNK = 128
HGRN_SUB = 16
HGRN_ROWS = 512
MOE_TB = 256
ROUTE_W = LANES_V7X
HALF = D_MODEL // 2
TOK_TILE = HALF // LANES_V7X
assert TOK_TILE == SUBLANES_V7X

BF16 = jnp.bfloat16
F32 = jnp.float32
U32 = jnp.uint32
NEG_INF = float("-inf")
LOG2E = math.log2(math.e)
RT_W0, RT_W1, RT_E0, RT_E1, RT_R0, RT_R1 = range(6)


def _cparams(n_axes, vmem_bytes=None):
    return pltpu.CompilerParams(dimension_semantics=("arbitrary",) * n_axes, vmem_limit_bytes=vmem_bytes)


def _layer_norm_rows(z, g, b):
    mu = jnp.mean(z, axis=-1, keepdims=True)
    zc = z - mu
    var = jnp.mean(zc * zc, axis=-1, keepdims=True)
    return zc * lax.rsqrt(var + LN_EPS) * g + b


def _dot_nt(a, b):
    return lax.dot_general(a, b, (((1,), (1,)), ((), ())), preferred_element_type=F32)


def _dot_tn(a, b):
    return lax.dot_general(a, b, (((0,), (0,)), ((), ())), preferred_element_type=F32)


def _pack_rows(y):
    lo = lax.bitcast_convert_type(y[:, :HALF].astype(BF16).astype(F32), U32) >> 16
    hi = lax.bitcast_convert_type(y[:, HALF:].astype(BF16).astype(F32), U32) & jnp.uint32(0xFFFF0000)
    return lo | hi


def _unpack_rows(p):
    lo = lax.bitcast_convert_type(p << 16, F32)
    hi = lax.bitcast_convert_type(p & jnp.uint32(0xFFFF0000), F32)
    return jnp.concatenate([lo, hi], axis=1)


def _store_token_tiles(ref, base, packed):
    rows = packed.shape[0]
    for c in range(TOK_TILE):
        ref[pl.ds(base + c, rows, stride=TOK_TILE), :] = packed[:, c * LANES_V7X:(c + 1) * LANES_V7X]


def _load_token_tiles(ref, base, rows, lead=None):
    parts = []
    for c in range(TOK_TILE):
        idx = (pl.ds(base + c, rows, stride=TOK_TILE), slice(None))
        parts.append(ref[idx] if lead is None else ref[(lead,) + idx])
    return jnp.concatenate(parts, axis=1)


def _mm_kernel(a_ref, w_ref, o_ref):
    a = a_ref[...].astype(BF16)
    o_ref[...] = jnp.dot(a, w_ref[...], preferred_element_type=F32).astype(o_ref.dtype)


def _matmul(a, w, tm, tn, out_dtype):
    m, k = a.shape
    n = w.shape[1]
    return pl.pallas_call(
        _mm_kernel,
        grid=(n // tn, m // tm),
        in_specs=[pl.BlockSpec((tm, k), lambda j, i: (i, 0)), pl.BlockSpec((k, tn), lambda j, i: (0, j))],
        out_specs=pl.BlockSpec((tm, tn), lambda j, i: (i, j)),
        out_shape=jax.ShapeDtypeStruct((m, n), out_dtype),
        compiler_params=_cparams(2, VMEM_LIMIT_V7X),
        name="proj_matmul",
    )(a, w)


def _bias_table_kernel(rb_ref, bucket_ref, o_ref):
    bucket = bucket_ref[...]
    for h in range(A_HEADS):
        acc = jnp.zeros(bucket.shape, F32)
        for bk in range(REL_BUCKETS):
            acc = jnp.where(bucket == bk, rb_ref[bk, h], acc)
        o_ref[h] = acc


def _bias_table(rel_bias, bucket):
    return pl.pallas_call(
        _bias_table_kernel,
        in_specs=[pl.BlockSpec(memory_space=pltpu.SMEM), pl.BlockSpec(memory_space=pltpu.VMEM)],
        out_specs=pl.BlockSpec(memory_space=pltpu.VMEM),
        out_shape=jax.ShapeDtypeStruct((A_HEADS,) + bucket.shape, F32),
        name="rel_bias_table",
    )(rel_bias, bucket)


def _t5_bucket_table():
    t_loc = jnp.arange(A_BLOCK, dtype=jnp.int32)[:, None]
    s_loc = jnp.arange(2 * A_BLOCK, dtype=jnp.int32)[None, :]
    dist = jnp.maximum(t_loc + A_BLOCK - s_loc, 0)
    max_exact = REL_BUCKETS // 2
    d = jnp.maximum(dist, 1).astype(F32)
    large = max_exact + (jnp.log(d / max_exact) / math.log(REL_MAX_DIST / max_exact)
                         * (REL_BUCKETS - max_exact)).astype(jnp.int32)
    large = jnp.minimum(large, REL_BUCKETS - 1)
    return jnp.where(dist < max_exact, dist, large)


def _attn_kernel(q_ref, kc_ref, vc_ref, kp_ref, vp_ref, bias_ref, sink_ref, o_ref):
    n = pl.program_id(1)
    blk = A_BLOCK
    k2 = jnp.concatenate([kp_ref[...], kc_ref[...]], axis=0)
    v2 = jnp.concatenate([vp_ref[...], vc_ref[...]], axis=0)
    k2r = pltpu.roll(k2, A_HEAD_DIM, 1)
    v2r = pltpu.roll(v2, A_HEAD_DIM, 1)
    lo = lax.broadcasted_iota(jnp.int32, k2.shape, 1) < A_HEAD_DIM
    zero = jnp.zeros_like(k2)

    def placed(x, xr, g, par):
        src = x if g == par else xr
        return (jnp.where(lo, src, zero) if par == 0 else jnp.where(lo, zero, src)).astype(BF16)

    kk = [[placed(k2, k2r, g, par) for par in range(2)] for g in range(A_KV_HEADS)]
    vv = [[placed(v2, v2r, g, par) for par in range(2)] for g in range(A_KV_HEADS)]

    t_loc = lax.broadcasted_iota(jnp.int32, (blk, 2 * blk), 0)
    s_loc = lax.broadcasted_iota(jnp.int32, (blk, 2 * blk), 1)
    dist = t_loc + blk - s_loc
    valid = (dist >= 0) & (dist < A_WINDOW) & ((s_loc >= blk) | (n > 0))

    heads_per_kv = A_HEADS // A_KV_HEADS
    for p in range(A_HEADS // 2):
        g = (2 * p) // heads_per_kv
        qp = (q_ref[:, p * 128:(p + 1) * 128] * (A_HEAD_DIM ** -0.5)).astype(BF16)
        acc = jnp.zeros((blk, 128), F32)
        for par in range(2):
            h = 2 * p + par
            sink = sink_ref[h]
            logits = jnp.where(valid, _dot_nt(qp, kk[g][par]) + bias_ref[h], NEG_INF)
            m = jnp.maximum(jnp.max(logits, axis=-1, keepdims=True), sink)
            e = jnp.exp(logits - m)
            den = jnp.sum(e, axis=-1, keepdims=True) + jnp.exp(sink - m)
            acc = acc + jnp.dot(e.astype(BF16), vv[g][par], preferred_element_type=F32) * (1.0 / den)
        o_ref[:, p * 128:(p + 1) * 128] = acc.astype(o_ref.dtype)


def _attention(proj, bias, sinks, bsz, seq):
    nb = seq // A_BLOCK
    kcol = A_QW // 128
    vcol = kcol + 1
    row = lambda b, n: b * nb + n
    prow = lambda b, n: b * nb + jnp.maximum(n - 1, 0)
    return pl.pallas_call(
        _attn_kernel,
        grid=(bsz, nb),
        in_specs=[
            pl.BlockSpec((A_BLOCK, A_QW), lambda b, n: (row(b, n), 0)),
            pl.BlockSpec((A_BLOCK, 128), lambda b, n: (row(b, n), kcol)),
            pl.BlockSpec((A_BLOCK, 128), lambda b, n: (row(b, n), vcol)),
            pl.BlockSpec((A_BLOCK, 128), lambda b, n: (prow(b, n), kcol)),
            pl.BlockSpec((A_BLOCK, 128), lambda b, n: (prow(b, n), vcol)),
            pl.BlockSpec((A_HEADS, A_BLOCK, 2 * A_BLOCK), lambda b, n: (0, 0, 0)),
            pl.BlockSpec(memory_space=pltpu.SMEM),
        ],
        out_specs=pl.BlockSpec((A_BLOCK, A_QW), lambda b, n: (row(b, n), 0)),
        out_shape=jax.ShapeDtypeStruct((bsz * seq, A_QW), BF16),
        compiler_params=_cparams(2, 32 * MIB),
        name="swa_attention",
    )(proj, proj, proj, proj, proj, bias, sinks)


HGRN_HEADS_PER_STEP = 2


def _hgrn_kernel(q_ref, f_ref, i_ref, g_ref, lb_ref, ng_ref, o_ref, st_ref):
    c, sub = HGRN_CHUNK, HGRN_SUB
    nsub = c // sub

    @pl.when(pl.program_id(2) == 0)
    def _():
        st_ref[...] = jnp.zeros_like(st_ref)

    rid = lax.broadcasted_iota(jnp.int32, (c, c), 0)
    cid = lax.broadcasted_iota(jnp.int32, (c, c), 1)
    blk_start = (rid // sub) * sub
    m_cum = jnp.concatenate([(cid < blk_start).astype(F32),
                             ((cid >= blk_start) & (cid <= rid)).astype(F32)], axis=0).astype(BF16)
    band = jnp.where(cid >= blk_start, rid - cid, -1)
    heads = range(HGRN_HEADS_PER_STEP)

    def scores(hh, r0):
        cols = slice(hh * 128, (hh + 1) * 128)
        lb = lb_ref[0, :, cols]
        q = q_ref[pl.ds(r0, c), cols]
        fl = f_ref[pl.ds(r0, c), cols]
        qf = q * jax.nn.sigmoid(q)
        f = lb + (1.0 - lb) * jax.nn.sigmoid(fl)
        kin = 1.0 - f
        logf = jnp.log(f)
        l1 = logf.astype(BF16)
        res = logf - l1.astype(F32)
        l2 = res.astype(BF16)
        l3 = (res - l2.astype(F32)).astype(BF16)
        cum = jnp.dot(m_cum, jnp.concatenate([l1, l2, l3], axis=1), preferred_element_type=F32)
        cum = cum[:, :B_DK] + cum[:, B_DK:2 * B_DK] + cum[:, 2 * B_DK:]
        rr, bq = cum[:c], cum[c:]
        wq = jnp.log(kin) - bq
        wb = wq - rr
        qt = qf * jnp.exp(bq)
        o = _dot_nt((qt * jnp.exp(rr)).astype(BF16), st_ref[hh].astype(BF16))

        kts = []
        for i in range(1, nsub):
            n = i * sub
            kts += [jnp.exp(rr[n:n + 1, :] + wb[:n, :]).astype(BF16), jnp.zeros((c - n, B_DK), BF16)]
        g_off = _dot_nt(qt.astype(BF16), jnp.concatenate(kts, axis=0))
        s = jnp.concatenate([jnp.zeros((sub, c), F32)] +
                            [g_off[i * sub:(i + 1) * sub, (i - 1) * c:i * c] for i in range(1, nsub)], axis=0)

        bq2, wq2 = bq * LOG2E, wq * LOG2E
        kds = [kin.astype(BF16)] + [jnp.exp2(pltpu.roll(bq2, c - d, 0) + wq2).astype(BF16) for d in range(1, sub)]
        g_diag = _dot_nt(qf.astype(BF16), jnp.concatenate(kds, axis=0))
        return dict(o=o, s=s, g_diag=g_diag, wb=wb, b_last=rr[c - 1:c, :] + bq[c - 1:c, :])

    def finish(hh, r0, h):
        cols = slice(hh * 128, (hh + 1) * 128)
        vb = i_ref[pl.ds(r0, c), cols].astype(BF16)
        gt = g_ref[pl.ds(r0, c), cols]
        o = h["o"] + jnp.dot(h["s"].astype(BF16), vb, preferred_element_type=F32)
        st_ref[hh] = (st_ref[hh] * jnp.exp(h["b_last"])
                      + _dot_tn(vb, jnp.exp(h["b_last"] + h["wb"]).astype(BF16)))
        o = o * lax.rsqrt(jnp.mean(o * o, axis=-1, keepdims=True) + RMS_EPS)
        o_ref[pl.ds(r0, c), cols] = (o * ng_ref[0, :, cols] * (gt * jax.nn.sigmoid(gt))).astype(o_ref.dtype)

    def chunk(ci, carry):
        r0 = pl.multiple_of(ci * c, c)
        hs = [scores(hh, r0) for hh in heads]
        for d in range(sub):
            on_diag = band == d
            for h in hs:
                h["s"] = jnp.where(on_diag, h["g_diag"][:, d * c:(d + 1) * c], h["s"])
        for hh in heads:
            finish(hh, r0, hs[hh])
        return carry

    lax.fori_loop(0, HGRN_ROWS // c, chunk, 0, unroll=True)


def _hgrn(proj, lb, norm_g, bsz, seq):
    nr = seq // HGRN_ROWS
    hp = HGRN_HEADS_PER_STEP
    width = hp * 128
    c0 = (A_QW + 2 * A_KVW) // width
    nhp = B_HEADS // hp
    spec = lambda off: pl.BlockSpec((HGRN_ROWS, width), lambda b, h, r: (b * nr + r, off + h))
    vec = pl.BlockSpec((1, 1, width), lambda b, h, r: (h, 0, 0))
    return pl.pallas_call(
        _hgrn_kernel,
        grid=(bsz, nhp, nr),
        in_specs=[spec(c0), spec(c0 + nhp), spec(c0 + 2 * nhp), spec(c0 + 3 * nhp), vec, vec],
        out_specs=pl.BlockSpec((HGRN_ROWS, width), lambda b, h, r: (b * nr + r, h)),
        out_shape=jax.ShapeDtypeStruct((bsz * seq, B_VW), BF16),
        scratch_shapes=[pltpu.VMEM((hp, B_DV, B_DK), F32)],
        compiler_params=_cparams(3, 32 * MIB),
        name="hgrn2",
    )(proj, proj, proj, proj, lb.reshape(nhp, 1, width), norm_g.reshape(nhp, 1, width))


def _outproj_kernel(ya_ref, yb_ref, wa_ref, wb_ref, x_ref, g_ref, b_ref, wrh_ref, wrl_ref, rb_ref,
                    xo_ref, xp_ref, rt_ref, cnt_ref, run_ref):
    mix = jnp.dot(ya_ref[...], wa_ref[...], preferred_element_type=F32)
    mix = mix + jnp.dot(yb_ref[...], wb_ref[...], preferred_element_type=F32)
    y = _layer_norm_rows(DN_ALPHA * x_ref[...] + mix, g_ref[...], b_ref[...])
    xo_ref[...] = y
    _store_token_tiles(xp_ref, 0, _pack_rows(y))
    y_hi = y.astype(BF16)
    y_lo = (y - y_hi.astype(F32)).astype(BF16)
    lg = jnp.dot(y_hi, wrh_ref[...], preferred_element_type=F32)
    lg = lg + jnp.dot(y_lo, wrh_ref[...], preferred_element_type=F32)
    lg = lg + jnp.dot(y_hi, wrl_ref[...], preferred_element_type=F32)
    _route_rows(lg + rb_ref[...], rt_ref, cnt_ref, run_ref)


def _outproj_ln_route(ya, yb, w_out, x, g, b, wr_hi, wr_lo, rbias, tm=256):
    n = x.shape[0]
    ka = ya.shape[1]
    row = lambda width: pl.BlockSpec((tm, width), lambda i: (i, 0))
    const = lambda shape: pl.BlockSpec(shape, lambda i: (0, 0))
    return pl.pallas_call(
        _outproj_kernel,
        grid=(n // tm,),
        in_specs=[row(ka), row(ka),
                  pl.BlockSpec((ka, D_MODEL), lambda i: (0, 0)), pl.BlockSpec((ka, D_MODEL), lambda i: (1, 0)),
                  row(D_MODEL), const((1, D_MODEL)), const((1, D_MODEL)),
                  const((D_MODEL, ROUTE_W)), const((D_MODEL, ROUTE_W)), const((1, ROUTE_W))],
        out_specs=[row(D_MODEL), pl.BlockSpec((tm * TOK_TILE, LANES_V7X), lambda i: (i, 0)), row(ROUTE_W),
                   const((1, ROUTE_W))],
        out_shape=[jax.ShapeDtypeStruct((n, D_MODEL), F32),
                   jax.ShapeDtypeStruct((n * TOK_TILE, LANES_V7X), U32),
                   jax.ShapeDtypeStruct((n, ROUTE_W), F32), jax.ShapeDtypeStruct((1, ROUTE_W), F32)],
        scratch_shapes=[pltpu.VMEM((1, ROUTE_W), F32)],
        compiler_params=_cparams(1, VMEM_LIMIT_V7X),
        name="outproj_ln_route",
    )(ya, yb, w_out, w_out, x, g.reshape(1, -1), b.reshape(1, -1), wr_hi, wr_lo, rbias)


def _router_weights(w_group, b_group, w_router, b_router):
    w = jnp.zeros((D_MODEL, ROUTE_W), F32)
    w = w.at[:, :N_GROUPS].set(w_group).at[:, N_GROUPS:N_GROUPS + N_EXPERTS].set(w_router)
    w_hi = w.astype(BF16)
    w_lo = (w - w_hi.astype(F32)).astype(BF16)
    rb = jnp.zeros((1, ROUTE_W), F32)
    rb = rb.at[0, :N_GROUPS].set(b_group).at[0, N_GROUPS:N_GROUPS + N_EXPERTS].set(b_router)
    return w_hi, w_lo, rb


def _route_rows(lg, rt_ref, cnt_ref, run_ref):
    @pl.when(pl.program_id(0) == 0)
    def _():
        run_ref[...] = jnp.zeros_like(run_ref)

    tm = lg.shape[0]
    lane = lax.broadcasted_iota(jnp.int32, lg.shape, 1)
    sentinel = jnp.int32(ROUTE_W)
    rowmax = lambda mask: jnp.max(jnp.where(mask, lg, NEG_INF), axis=-1, keepdims=True)
    first = lambda mask: jnp.min(jnp.where(mask, lane, sentinel), axis=-1, keepdims=True)

    is_g = lane < N_GROUPS
    gmax = rowmax(is_g)
    g_idx = first(is_g & (lg == gmax))
    g_w = 1.0 / jnp.sum(jnp.where(is_g, jnp.exp(lg - gmax), 0.0), axis=-1, keepdims=True)

    e_lane = lane - N_GROUPS
    sel = (e_lane >= 0) & (e_lane < N_EXPERTS) & ((e_lane >> 3) == g_idx)
    m1 = rowmax(sel)
    i1 = first(sel & (lg == m1))
    sel2 = sel & (lane != i1)
    m2 = rowmax(sel2)
    i2 = first(sel2 & (lg == m2))
    ex = jnp.exp(m2 - m1)
    w0 = g_w / (1.0 + ex)
    w1 = g_w * ex / (1.0 + ex)

    oh0 = lane == i1
    oh1 = lane == i2
    both = (oh0 | oh1).astype(F32)
    rid = lax.broadcasted_iota(jnp.int32, (tm, tm), 0)
    cid = lax.broadcasted_iota(jnp.int32, (tm, tm), 1)
    before = jnp.dot((cid < rid).astype(BF16), both.astype(BF16), preferred_element_type=F32) + run_ref[...]
    rank0 = jnp.sum(jnp.where(oh0, before, 0.0), axis=-1, keepdims=True)
    rank1 = jnp.sum(jnp.where(oh1, before, 0.0), axis=-1, keepdims=True)
    run = run_ref[...] + jnp.sum(both, axis=0, keepdims=True)
    run_ref[...] = run
    cnt_ref[...] = run

    slab = jnp.zeros(lg.shape, F32)
    for ln, val in ((RT_W0, w0), (RT_W1, w1), (RT_E0, (i1 - N_GROUPS).astype(F32)),
                    (RT_E1, (i2 - N_GROUPS).astype(F32)), (RT_R0, rank0), (RT_R1, rank1)):
        slab = jnp.where(lane == ln, val, slab)
    rt_ref[...] = slab


assert EXPERTS_PER_GROUP == 8


def _block_layout(rt, cnt, n_tok):
    m = n_tok * TOP_K
    counts = cnt[0, N_GROUPS:N_GROUPS + N_EXPERTS].astype(jnp.int32)
    pcounts = (counts + MOE_TB - 1) // MOE_TB * MOE_TB
    pends = jnp.cumsum(pcounts)
    pstarts = pends - pcounts
    n_blocks = -(-(m + N_EXPERTS * (MOE_TB - 1)) // MOE_TB)
    e_idx = rt[:, RT_E0:RT_E1 + 1].astype(jnp.int32)
    rank = rt[:, RT_R0:RT_R1 + 1].astype(jnp.int32)
    onehot = e_idx[:, :, None] == jnp.arange(N_EXPERTS, dtype=jnp.int32)
    pos = jnp.sum(jnp.where(onehot, pstarts, 0), axis=-1) + rank
    blk_start = jnp.arange(n_blocks, dtype=jnp.int32) * MOE_TB
    blk_e = jnp.minimum(jnp.sum(blk_start[:, None] >= pends[None, :], axis=-1), N_EXPERTS - 1).astype(jnp.int32)
    nused = (pends[-1:] // MOE_TB).astype(jnp.int32)
    ids = jnp.arange(N_EXPERTS, dtype=jnp.int32)
    later_used = (ids[None, :] > ids[:, None]) & (counts[None, :] > 0)
    next_used = jnp.min(jnp.where(later_used, ids[None, :], N_EXPERTS), axis=-1)
    next_used = jnp.where(next_used == N_EXPERTS, -1, next_used).astype(jnp.int32)
    nxt_e = jnp.sum(jnp.where(blk_e[:, None] == ids[None, :], next_used[None, :], 0), axis=-1).astype(jnp.int32)
    return pos, blk_e, nxt_e, nused, n_blocks


def _step_indices(pos, tm):
    nsteps = pos.shape[0] // tm
    return pos.reshape(nsteps, tm, TOP_K).transpose(0, 2, 1).reshape(nsteps, 1, TOP_K * tm)


def _tile_rows(i):
    return pl.ds(pl.multiple_of(i * TOK_TILE, TOK_TILE), TOK_TILE)


def _dispatch_kernel(pos_ref, xp_ref, init_hbm, xs_hbm, xbuf, sem, *, tm):
    del init_hbm
    s = pl.program_id(0)
    nsteps = pl.num_programs(0)
    slot = s % 2

    def copy(j, dst, sl):
        return pltpu.make_async_copy(xbuf.at[sl, _tile_rows(j)], xs_hbm.at[_tile_rows(dst)], sem.at[sl])

    def drain(sl):
        def body(j, c):
            copy(0, 0, sl).wait()
            return c
        lax.fori_loop(0, TOP_K * tm, body, 0, unroll=8)

    @pl.when(s >= 2)
    def _():
        drain(slot)

    xbuf[slot] = xp_ref[...]

    for j in range(tm):
        for k in range(TOP_K):
            copy(j, pos_ref[0, 0, k * tm + j], slot).start(priority=k)

    @pl.when(s == nsteps - 1)
    def _():
        drain(1 - slot)
        drain(slot)


def _moe_dispatch(xp, pos, n_pad, tm=512):
    n = pos.shape[0]
    assert n // tm >= 2
    pos3 = _step_indices(pos, tm)
    init = jnp.zeros((n_pad * TOK_TILE, LANES_V7X), U32)
    return pl.pallas_call(
        functools.partial(_dispatch_kernel, tm=tm),
        grid=(n // tm,),
        in_specs=[pl.BlockSpec((1, 1, TOP_K * tm), lambda i: (i, 0, 0), memory_space=pltpu.SMEM),
                  pl.BlockSpec((tm * TOK_TILE, LANES_V7X), lambda i: (i, 0)), pl.BlockSpec(memory_space=pl.ANY)],
        out_specs=pl.BlockSpec(memory_space=pl.ANY),
        out_shape=jax.ShapeDtypeStruct(init.shape, U32),
        scratch_shapes=[pltpu.VMEM((2, tm * TOK_TILE, LANES_V7X), U32), pltpu.SemaphoreType.DMA((2,))],
        input_output_aliases={2: 0},
        compiler_params=_cparams(1),
        name="moe_dispatch",
    )(pos3, xp, init)


def _moe_kernel(blk_e_ref, nxt_e_ref, nused_ref, xs_ref, w1_hbm, w3_hbm, w2_hbm, ys_ref,
                wf1, wf3, wf2, w1b, w3b, w2b, slot_ref, sem, *, layer):
    s = pl.program_id(0)
    nused = nused_ref[0]

    def fetch(e, sl):
        return [pltpu.make_async_copy(w_hbm.at[layer, e], wf.at[sl], sem.at[sl, k])
                for k, (w_hbm, wf) in enumerate(((w1_hbm, wf1), (w3_hbm, wf3), (w2_hbm, wf2)))]

    @pl.when(s >= nused)
    def _():
        ys_ref[...] = jnp.zeros_like(ys_ref)

    @pl.when(s < nused)
    def _():
        e = blk_e_ref[s]
        prev = blk_e_ref[jnp.maximum(s - 1, 0)]

        @pl.when(s == 0)
        def _():
            slot_ref[0] = 0
            for cp in fetch(e, 0):
                cp.start()

        @pl.when((s > 0) & (e != prev))
        def _():
            slot_ref[0] = 1 - slot_ref[0]

        @pl.when((s == 0) | (e != prev))
        def _():
            sl = slot_ref[0]
            for cp in fetch(e, sl):
                cp.wait()
            w1b[...] = wf1[sl].astype(BF16)
            w3b[...] = wf3[sl].astype(BF16)
            w2b[...] = wf2[sl].astype(BF16)
            nxt = nxt_e_ref[s]

            @pl.when(nxt >= 0)
            def _():
                for cp in fetch(nxt, 1 - sl):
                    cp.start()

        xb = _unpack_rows(_load_token_tiles(xs_ref, 0, MOE_TB)).astype(BF16)
        h1 = jnp.dot(xb, w1b[...], preferred_element_type=F32)
        h3 = jnp.dot(xb, w3b[...], preferred_element_type=F32)
        h = (h1 * jax.nn.sigmoid(h1) * h3).astype(BF16)
        y = jnp.dot(h, w2b[...], preferred_element_type=F32)
        _store_token_tiles(ys_ref, 0, _pack_rows(y))


def _moe_experts(xs, blk_e, nxt_e, nused, w1, w3, w2, layer):
    n_blocks = blk_e.shape[0]
    tiles = pl.BlockSpec((MOE_TB * TOK_TILE, LANES_V7X), lambda s, be, ne, nu: (s, 0))
    hbm = pl.BlockSpec(memory_space=pl.ANY)
    up, down = (D_MODEL, D_EXPERT), (D_EXPERT, D_MODEL)
    grid_spec = pltpu.PrefetchScalarGridSpec(
        num_scalar_prefetch=3,
        grid=(n_blocks,),
        in_specs=[tiles, hbm, hbm, hbm],
        out_specs=tiles,
        scratch_shapes=[pltpu.VMEM((2,) + up, F32), pltpu.VMEM((2,) + up, F32), pltpu.VMEM((2,) + down, F32),
                        pltpu.VMEM(up, BF16), pltpu.VMEM(up, BF16), pltpu.VMEM(down, BF16),
                        pltpu.SMEM((1,), jnp.int32), pltpu.SemaphoreType.DMA((2, 3))],
    )
    return pl.pallas_call(
        functools.partial(_moe_kernel, layer=layer),
        grid_spec=grid_spec,
        out_shape=jax.ShapeDtypeStruct(xs.shape, U32),
        compiler_params=_cparams(1, VMEM_LIMIT_V7X),
        name="moe_experts",
    )(blk_e, nxt_e, nused, xs, w1, w3, w2)


def _combine_kernel(pos_ref, nxt_ref, ys_hbm, x_ref, rt_ref, g_ref, b_ref, o_ref, ybuf, sem, *, tm):
    s = pl.program_id(0)
    nsteps = pl.num_programs(0)
    slot = s % 2

    def copy(src, j, sl):
        return pltpu.make_async_copy(ys_hbm.at[_tile_rows(src)], ybuf.at[sl, _tile_rows(j)], sem.at[sl])

    def start_gather(idx_ref, sl):
        def body(j, c):
            copy(idx_ref[0, 0, j], j, sl).start()
            return c
        lax.fori_loop(0, TOP_K * tm, body, 0, unroll=8)

    @pl.when(s == 0)
    def _():
        start_gather(pos_ref, 0)

    @pl.when(s + 1 < nsteps)
    def _():
        for j in range(TOP_K * tm):
            copy(nxt_ref[0, 0, j], j, 1 - slot).start(priority=j % 2)

    def wait_body(j, c):
        copy(0, j, slot).wait()
        return c
    lax.fori_loop(0, TOP_K * tm, wait_body, 0, unroll=8)

    rt = rt_ref[...]
    ffn = _unpack_rows(_load_token_tiles(ybuf, 0, tm, lead=slot)) * rt[:, RT_W0:RT_W0 + 1]
    ffn = ffn + _unpack_rows(_load_token_tiles(ybuf, tm * TOK_TILE, tm, lead=slot)) * rt[:, RT_W1:RT_W1 + 1]
    o_ref[...] = _layer_norm_rows(DN_ALPHA * x_ref[...] + ffn, g_ref[...], b_ref[...])


def _moe_combine(ys, pos, rt, x, g, b, tm=512):
    n = x.shape[0]
    nsteps = n // tm
    pos3 = _step_indices(pos, tm)
    smem_blk = lambda f: pl.BlockSpec((1, 1, TOP_K * tm), f, memory_space=pltpu.SMEM)
    row = lambda width: pl.BlockSpec((tm, width), lambda i: (i, 0))
    const = lambda shape: pl.BlockSpec(shape, lambda i: (0, 0))
    return pl.pallas_call(
        functools.partial(_combine_kernel, tm=tm),
        grid=(nsteps,),
        in_specs=[smem_blk(lambda i: (i, 0, 0)), smem_blk(lambda i: (jnp.minimum(i + 1, nsteps - 1), 0, 0)),
                  pl.BlockSpec(memory_space=pl.ANY), row(D_MODEL), row(ROUTE_W),
                  const((1, D_MODEL)), const((1, D_MODEL))],
        out_specs=row(D_MODEL),
        out_shape=jax.ShapeDtypeStruct((n, D_MODEL), F32),
        scratch_shapes=[pltpu.VMEM((2, TOP_K * tm * TOK_TILE, LANES_V7X), U32), pltpu.SemaphoreType.DMA((2,))],
        compiler_params=_cparams(1, 40 * MIB),
        name="moe_combine",
    )(pos3, pos3, ys, x, rt, g.reshape(1, -1), b.reshape(1, -1))


def _moe_layer(x1, xp, rt, cnt, w1, w3, w2, layer, g, b):
    n_tok = x1.shape[0]
    pos, blk_e, nxt_e, nused, n_blocks = _block_layout(rt, cnt, n_tok)
    xs = _moe_dispatch(xp, pos, n_blocks * MOE_TB)
    ys = _moe_experts(xs, blk_e, nxt_e, nused, w1, w3, w2, layer)
    return _moe_combine(ys, pos, rt, x1, g, b)


def _gmlp_kernel(u_ref, v_ref, g_ref, b_ref, w_ref, bs_ref, o_ref, *, chunks):
    for ci in range(chunks):
        rows = slice(ci * C_CHUNK, (ci + 1) * C_CHUNK)
        u = jax.nn.gelu(u_ref[rows, :])
        v = _layer_norm_rows(jax.nn.gelu(v_ref[rows, :]), g_ref[...], b_ref[...]).astype(BF16)
        for gi in range(C_GROUPS):
            cols = slice(gi * C_GROUP_DIM, (gi + 1) * C_GROUP_DIM)
            mixed = jnp.dot(w_ref[gi], v[:, cols], preferred_element_type=F32) + bs_ref[:, cols]
            o_ref[rows, cols] = (u[:, cols] * mixed).astype(o_ref.dtype)


def _gmlp(proj, ln_g, ln_b, w_s, b_s, chunks=4):
    n = proj.shape[0]
    tm = chunks * C_CHUNK
    w = (w_s * jnp.tril(jnp.ones((C_CHUNK, C_CHUNK), w_s.dtype))).astype(BF16)
    bs_full = jnp.repeat(b_s.T, C_GROUP_DIM, axis=1)
    const2 = lambda shape: pl.BlockSpec(shape, lambda i: (0, 0))
    return pl.pallas_call(
        functools.partial(_gmlp_kernel, chunks=chunks),
        grid=(n // tm,),
        in_specs=[pl.BlockSpec((tm, C_W), lambda i: (i, 0)), pl.BlockSpec((tm, C_W), lambda i: (i, 1)),
                  const2((1, C_W)), const2((1, C_W)),
                  pl.BlockSpec((C_GROUPS, C_CHUNK, C_CHUNK), lambda i: (0, 0, 0)), const2((C_CHUNK, C_W))],
        out_specs=pl.BlockSpec((tm, C_W), lambda i: (i, 0)),
        out_shape=jax.ShapeDtypeStruct((n, C_W), BF16),
        compiler_params=_cparams(1, 32 * MIB),
        name="gmlp_gating",
    )(proj, proj, ln_g.reshape(1, -1), ln_b.reshape(1, -1), w, bs_full)


CONV_HIST = 32


def _conv_kernel(a_ref, gt_ref, ap_ref, gp_ref, w_ref, cb_ref, g_ref, b_ref, o_ref, hbuf, hshift, *, ts):
    i = pl.program_id(1)
    hist = ap_ref[...] * jax.nn.sigmoid(gp_ref[...])
    hbuf[0:CONV_HIST, :] = jnp.where(i > 0, hist, jnp.zeros_like(hist))
    hbuf[CONV_HIST:CONV_HIST + ts, :] = a_ref[...] * jax.nn.sigmoid(gt_ref[...])
    off = CONV_HIST - (D_CONV - 1)
    acc = jnp.zeros((ts, D_CHANNELS), F32) + cb_ref[...]
    for r in range(SUBLANES_V7X):
        taps = [j for j in range(D_CONV) if (off + j) % SUBLANES_V7X == r]
        if not taps:
            continue
        src = hbuf
        if r:
            span = max(taps) + off - r + ts
            hshift[0:span, :] = hbuf[r:r + span, :]
            src = hshift
        for j in taps:
            base = off + j - r
            acc = acc + w_ref[j:j + 1, :] * src[base:base + ts, :]
    y = _layer_norm_rows(acc, g_ref[...], b_ref[...])
    o_ref[...] = (y * jax.nn.sigmoid(y)).astype(o_ref.dtype)


def _conformer_conv(proj, conv_w, conv_b, ln_g, ln_b, bsz, seq, ts=512):
    nt = seq // ts
    acol = 2 * C_W // D_CHANNELS
    gcol = acol + 1
    hb = ts // CONV_HIST
    cur = lambda col: pl.BlockSpec((ts, D_CHANNELS), lambda b, i: (b * nt + i, col))
    prev = lambda col: pl.BlockSpec((CONV_HIST, D_CHANNELS),
                                    lambda b, i: (jnp.maximum((b * nt + i) * hb - 1, 0), col))
    const2 = lambda shape: pl.BlockSpec(shape, lambda b, i: (0, 0))
    return pl.pallas_call(
        functools.partial(_conv_kernel, ts=ts),
        grid=(bsz, nt),
        in_specs=[cur(acol), cur(gcol), prev(acol), prev(gcol),
                  const2((D_CONV, D_CHANNELS)), const2((1, D_CHANNELS)), const2((1, D_CHANNELS)),
                  const2((1, D_CHANNELS))],
        out_specs=pl.BlockSpec((ts, D_CHANNELS), lambda b, i: (b * nt + i, 0)),
        out_shape=jax.ShapeDtypeStruct((bsz * seq, D_CHANNELS), BF16),
        scratch_shapes=[pltpu.VMEM((CONV_HIST + ts, D_CHANNELS), F32), pltpu.VMEM((CONV_HIST + ts, D_CHANNELS), F32)],
        compiler_params=_cparams(2, 32 * MIB),
        name="conformer_conv",
    )(proj, proj, proj, proj, conv_w, conv_b.reshape(1, -1), ln_g.reshape(1, -1), ln_b.reshape(1, -1))


def kernel(x, w_in_ab, attn_sinks, rel_bias, hgrn_lb_logits, hgrn_norm_g, w_out_ab, w_in_cd, gmlp_ln_g, gmlp_ln_b, gmlp_w_s, gmlp_b_s, conv_w, conv_b, conv_ln_g, conv_ln_b, w_out_cd, ln_mix_g, ln_mix_b, ln_ffn_g, ln_ffn_b, moe_w_group, moe_b_group, moe_w_router, moe_b_router, moe_w1, moe_w3, moe_w2):
    bsz, seq = x.shape[0], x.shape[1]
    n_tok = bsz * seq
    xf = x.reshape(n_tok, D_MODEL)
    lb_table = jnp.cumsum(jax.nn.softmax(hgrn_lb_logits.astype(F32), axis=0), axis=0)
    bias = _bias_table(rel_bias.astype(F32), _t5_bucket_table())

    for layer in range(DEPTH):
        j = layer // 2
        if layer % 2 == 0:
            proj = _matmul(xf, w_in_ab[j].astype(BF16), 512, EVEN_IN // 3, F32)
            ya = _attention(proj, bias, attn_sinks[j].astype(F32), bsz, seq)
            yb = _hgrn(proj, lb_table[layer], hgrn_norm_g[j].astype(F32), bsz, seq)
            w_out = w_out_ab[j]
        else:
            proj = _matmul(xf, w_in_cd[j].astype(BF16), 512, ODD_IN // 2, F32)
            ya = _gmlp(proj, gmlp_ln_g[j], gmlp_ln_b[j], gmlp_w_s[j], gmlp_b_s[j])
            yb = _conformer_conv(proj, conv_w[j], conv_b[j], conv_ln_g[j], conv_ln_b[j], bsz, seq)
            w_out = w_out_cd[j]
        wr_hi, wr_lo, rbias = _router_weights(moe_w_group[layer], moe_b_group[layer],
                                              moe_w_router[layer], moe_b_router[layer])
        x1, xp, rt, cnt = _outproj_ln_route(ya, yb, w_out.astype(BF16), xf, ln_mix_g[layer], ln_mix_b[layer],
                                            wr_hi, wr_lo, rbias)
        xf = _moe_layer(x1, xp, rt, cnt, moe_w1, moe_w3, moe_w2, layer, ln_ffn_g[layer], ln_ffn_b[layer])
    return xf.reshape(bsz, seq, D_MODEL)
```

```python
import functools
import math

import jax
import jax.numpy as jnp
from jax import lax
from jax.experimental import pallas as pl
from jax.experimental.pallas import tpu as pltpu

D_MODEL = 2048
DEPTH = 2
A_HEADS = 16
A_KV_HEADS = 2
A_HEAD_DIM = 64
A_WINDOW = 128
A_BLOCK = 128
REL_BUCKETS = 32
REL_MAX_DIST = 128
B_HEADS = 8
B_DK = 128
B_DV = 128
C_GROUPS = 8
C_GROUP_DIM = 128
C_CHUNK = 128
D_CHANNELS = 1024
D_CONV = 31
A_QW = A_HEADS * A_HEAD_DIM
A_KVW = A_KV_HEADS * A_HEAD_DIM
B_KW = B_HEADS * B_DK
B_VW = B_HEADS * B_DV
C_W = C_GROUPS * C_GROUP_DIM
EVEN_IN = A_QW + 2 * A_KVW + 2 * B_KW + 2 * B_VW
ODD_IN = 2 * C_W + 2 * D_CHANNELS
N_GROUPS = 4
EXPERTS_PER_GROUP = 8
N_EXPERTS = N_GROUPS * EXPERTS_PER_GROUP
TOP_K = 2
D_EXPERT = 512
DN_ALPHA = (2 * DEPTH) ** 0.25
LN_EPS = 1e-5
RMS_EPS = 1e-6

LANES_V7X = 128
SUBLANES_V7X = 8
MIB = 1024 * 1024
VMEM_LIMIT_V7X = 56 * MIB

HGRN_CHUNK = 128
HGRN_SUB = 16
HGRN_ROWS = 512
MOE_TB = 256
ROUTE_W = LANES_V7X
HALF = D_MODEL // 2
TOK_TILE = HALF // LANES_V7X
assert TOK_TILE == SUBLANES_V7X

BF16 = jnp.bfloat16
F32 = jnp.float32
U32 = jnp.uint32
NEG_INF = float("-inf")
LOG2E = math.log2(math.e)
RT_W0, RT_W1, RT_E0, RT_E1, RT_R0, RT_R1 = range(6)


def _cparams(n_axes, vmem_bytes=None):
    return pltpu.CompilerParams(dimension_semantics=("arbitrary",) * n_axes, vmem_limit_bytes=vmem_bytes)


def _layer_norm_rows(z, g, b):
    mu = jnp.mean(z, axis=-1, keepdims=True)
    zc = z - mu
    var = jnp.mean(zc * zc, axis=-1, keepdims=True)
    return zc * lax.rsqrt(var + LN_EPS) * g + b


def _dot_nt(a, b):
    return lax.dot_general(a, b, (((1,), (1,)), ((), ())), preferred_element_type=F32)


def _dot_tn(a, b):
    return lax.dot_general(a, b, (((0,), (0,)), ((), ())), preferred_element_type=F32)


def _pack_rows(y):
    lo = lax.bitcast_convert_type(y[:, :HALF].astype(BF16).astype(F32), U32) >> 16
    hi = lax.bitcast_convert_type(y[:, HALF:].astype(BF16).astype(F32), U32) & jnp.uint32(0xFFFF0000)
    return lo | hi


def _unpack_rows(p):
    lo = lax.bitcast_convert_type(p << 16, F32)
    hi = lax.bitcast_convert_type(p & jnp.uint32(0xFFFF0000), F32)
    return jnp.concatenate([lo, hi], axis=1)


def _store_token_tiles(ref, base, packed):
    rows = packed.shape[0]
    for c in range(TOK_TILE):
        ref[pl.ds(base + c, rows, stride=TOK_TILE), :] = packed[:, c * LANES_V7X:(c + 1) * LANES_V7X]


def _load_token_tiles(ref, base, rows, lead=None):
    parts = []
    for c in range(TOK_TILE):
        idx = (pl.ds(base + c, rows, stride=TOK_TILE), slice(None))
        parts.append(ref[idx] if lead is None else ref[(lead,) + idx])
    return jnp.concatenate(parts, axis=1)


def _mm_kernel(a_ref, w_ref, o_ref):
    a = a_ref[...].astype(BF16)
    o_ref[...] = jnp.dot(a, w_ref[...], preferred_element_type=F32).astype(o_ref.dtype)


def _matmul(a, w, tm, tn, out_dtype):
    m, k = a.shape
    n = w.shape[1]
    return pl.pallas_call(
        _mm_kernel,
        grid=(n // tn, m // tm),
        in_specs=[pl.BlockSpec((tm, k), lambda j, i: (i, 0)), pl.BlockSpec((k, tn), lambda j, i: (0, j))],
        out_specs=pl.BlockSpec((tm, tn), lambda j, i: (i, j)),
        out_shape=jax.ShapeDtypeStruct((m, n), out_dtype),
        compiler_params=_cparams(2, VMEM_LIMIT_V7X),
        name="proj_matmul",
    )(a, w)


def _bias_table_kernel(rb_ref, bucket_ref, o_ref):
    bucket = bucket_ref[...]
    for h in range(A_HEADS):
        acc = jnp.zeros(bucket.shape, F32)
        for bk in range(REL_BUCKETS):
            acc = jnp.where(bucket == bk, rb_ref[bk, h], acc)
        o_ref[h] = acc


def _bias_table(rel_bias, bucket):
    return pl.pallas_call(
        _bias_table_kernel,
        in_specs=[pl.BlockSpec(memory_space=pltpu.SMEM), pl.BlockSpec(memory_space=pltpu.VMEM)],
        out_specs=pl.BlockSpec(memory_space=pltpu.VMEM),
        out_shape=jax.ShapeDtypeStruct((A_HEADS,) + bucket.shape, F32),
        name="rel_bias_table",
    )(rel_bias, bucket)


def _t5_bucket_table():
    t_loc = jnp.arange(A_BLOCK, dtype=jnp.int32)[:, None]
    j_loc = jnp.arange(A_BLOCK, dtype=jnp.int32)[None, :]
    dist = jnp.where(j_loc <= t_loc, t_loc - j_loc, t_loc + A_BLOCK - j_loc)
    max_exact = REL_BUCKETS // 2
    d = jnp.maximum(dist, 1).astype(F32)
    large = max_exact + (jnp.log(d / max_exact) / math.log(REL_MAX_DIST / max_exact)
                         * (REL_BUCKETS - max_exact)).astype(jnp.int32)
    large = jnp.minimum(large, REL_BUCKETS - 1)
    return jnp.where(dist < max_exact, dist, large)


assert A_WINDOW == A_BLOCK


def _attn_kernel(q_ref, kc_ref, vc_ref, kp_ref, vp_ref, bias_ref, sink_ref, o_ref):
    n = pl.program_id(1)
    blk = A_BLOCK
    k2 = jnp.concatenate([kp_ref[...], kc_ref[...]], axis=0)
    v2 = jnp.concatenate([vp_ref[...], vc_ref[...]], axis=0)
    k2r = pltpu.roll(k2, A_HEAD_DIM, 1)
    v2r = pltpu.roll(v2, A_HEAD_DIM, 1)
    lo = lax.broadcasted_iota(jnp.int32, k2.shape, 1) < A_HEAD_DIM
    zero = jnp.zeros_like(k2)

    def placed(x, xr, g, par):
        src = x if g == par else xr
        return (jnp.where(lo, src, zero) if par == 0 else jnp.where(lo, zero, src)).astype(BF16)

    kk = [[placed(k2, k2r, g, par) for par in range(2)] for g in range(A_KV_HEADS)]
    vv = [[placed(v2, v2r, g, par) for par in range(2)] for g in range(A_KV_HEADS)]

    own = (lax.broadcasted_iota(jnp.int32, (blk, blk), 1) <= lax.broadcasted_iota(jnp.int32, (blk, blk), 0))
    no_prev = jnp.where(n > 0, 0.0, NEG_INF)
    zero_p = jnp.zeros((blk, blk), F32)

    heads_per_kv = A_HEADS // A_KV_HEADS
    for p in range(A_HEADS // 2):
        g = (2 * p) // heads_per_kv
        qp = (q_ref[:, p * 128:(p + 1) * 128] * (A_HEAD_DIM ** -0.5)).astype(BF16)
        acc = jnp.zeros((blk, 128), F32)
        for par in range(2):
            h = 2 * p + par
            sink = sink_ref[h]
            s2 = _dot_nt(qp, kk[g][par])
            logits = jnp.where(own, s2[:, blk:], s2[:, :blk] + no_prev) + bias_ref[h]
            m = jnp.maximum(jnp.max(logits, axis=-1, keepdims=True), sink)
            e = jnp.exp(logits - m)
            den = jnp.sum(e, axis=-1, keepdims=True) + jnp.exp(sink - m)
            e2 = jnp.concatenate([jnp.where(own, zero_p, e), jnp.where(own, e, zero_p)], axis=1).astype(BF16)
            acc = acc + jnp.dot(e2, vv[g][par], preferred_element_type=F32) * (1.0 / den)
        o_ref[:, p * 128:(p + 1) * 128] = acc.astype(o_ref.dtype)


def _attention(proj, bias, sinks, bsz, seq):
    nb = seq // A_BLOCK
    kcol = A_QW // 128
    vcol = kcol + 1
    row = lambda b, n: b * nb + n
    prow = lambda b, n: b * nb + jnp.maximum(n - 1, 0)
    return pl.pallas_call(
        _attn_kernel,
        grid=(bsz, nb),
        in_specs=[
            pl.BlockSpec((A_BLOCK, A_QW), lambda b, n: (row(b, n), 0)),
            pl.BlockSpec((A_BLOCK, 128), lambda b, n: (row(b, n), kcol)),
            pl.BlockSpec((A_BLOCK, 128), lambda b, n: (row(b, n), vcol)),
            pl.BlockSpec((A_BLOCK, 128), lambda b, n: (prow(b, n), kcol)),
            pl.BlockSpec((A_BLOCK, 128), lambda b, n: (prow(b, n), vcol)),
            pl.BlockSpec((A_HEADS, A_BLOCK, A_BLOCK), lambda b, n: (0, 0, 0)),
            pl.BlockSpec(memory_space=pltpu.SMEM),
        ],
        out_specs=pl.BlockSpec((A_BLOCK, A_QW), lambda b, n: (row(b, n), 0)),
        out_shape=jax.ShapeDtypeStruct((bsz * seq, A_QW), BF16),
        compiler_params=_cparams(2, 32 * MIB),
        name="swa_attention",
    )(proj, proj, proj, proj, proj, bias, sinks)


HGRN_HEADS_PER_STEP = 2


def _hgrn_kernel(q_ref, f_ref, i_ref, g_ref, lb_ref, ng_ref, o_ref, st_ref):
    c, sub = HGRN_CHUNK, HGRN_SUB
    nsub = c // sub

    @pl.when(pl.program_id(2) == 0)
    def _():
        st_ref[...] = jnp.zeros_like(st_ref)

    rid = lax.broadcasted_iota(jnp.int32, (c, c), 0)
    cid = lax.broadcasted_iota(jnp.int32, (c, c), 1)
    blk_start = (rid // sub) * sub
    m_cum = jnp.concatenate([(cid < blk_start).astype(F32),
                             ((cid >= blk_start) & (cid <= rid)).astype(F32)], axis=0).astype(BF16)
    band = jnp.where(cid >= blk_start, rid - cid, -1)
    heads = range(HGRN_HEADS_PER_STEP)

    def scores(hh, r0):
        cols = slice(hh * 128, (hh + 1) * 128)
        lb = lb_ref[0, :, cols]
        q = q_ref[pl.ds(r0, c), cols]
        fl = f_ref[pl.ds(r0, c), cols]
        qf = q * jax.nn.sigmoid(q)
        f = lb + (1.0 - lb) * jax.nn.sigmoid(fl)
        kin = 1.0 - f
        logf = jnp.log(f)
        l1 = logf.astype(BF16)
        res = logf - l1.astype(F32)
        l2 = res.astype(BF16)
        l3 = (res - l2.astype(F32)).astype(BF16)
        cum = jnp.dot(m_cum, jnp.concatenate([l1, l2, l3], axis=1), preferred_element_type=F32)
        cum = cum[:, :B_DK] + cum[:, B_DK:2 * B_DK] + cum[:, 2 * B_DK:]
        rr, bq = cum[:c], cum[c:]
        wq = jnp.log(kin) - bq
        wb = wq - rr
        qt = qf * jnp.exp(bq)
        o = _dot_nt((qt * jnp.exp(rr)).astype(BF16), st_ref[hh].astype(BF16))

        kts = []
        for i in range(1, nsub):
            n = i * sub
            kts += [jnp.exp(rr[n:n + 1, :] + wb[:n, :]).astype(BF16), jnp.zeros((c - n, B_DK), BF16)]
        g_off = _dot_nt(qt.astype(BF16), jnp.concatenate(kts, axis=0))
        s = jnp.concatenate([jnp.zeros((sub, c), F32)] +
                            [g_off[i * sub:(i + 1) * sub, (i - 1) * c:i * c] for i in range(1, nsub)], axis=0)

        bq2, wq2 = bq * LOG2E, wq * LOG2E
        kds = [kin.astype(BF16)] + [jnp.exp2(pltpu.roll(bq2, c - d, 0) + wq2).astype(BF16) for d in range(1, sub)]
        g_diag = _dot_nt(qf.astype(BF16), jnp.concatenate(kds, axis=0))
        return dict(o=o, s=s, g_diag=g_diag, wb=wb, b_last=rr[c - 1:c, :] + bq[c - 1:c, :])

    def finish(hh, r0, h):
        cols = slice(hh * 128, (hh + 1) * 128)
        vb = i_ref[pl.ds(r0, c), cols].astype(BF16)
        gt = g_ref[pl.ds(r0, c), cols]
        o = h["o"] + jnp.dot(h["s"].astype(BF16), vb, preferred_element_type=F32)
        st_ref[hh] = (st_ref[hh] * jnp.exp(h["b_last"])
                      + _dot_tn(vb, jnp.exp(h["b_last"] + h["wb"]).astype(BF16)))
        o = o * lax.rsqrt(jnp.mean(o * o, axis=-1, keepdims=True) + RMS_EPS)
        o_ref[pl.ds(r0, c), cols] = (o * ng_ref[0, :, cols] * (gt * jax.nn.sigmoid(gt))).astype(o_ref.dtype)

    def chunk(ci, carry):
        r0 = pl.multiple_of(ci * c, c)
        hs = [scores(hh, r0) for hh in heads]
        for d in range(sub):
            on_diag = band == d
            for h in hs:
                h["s"] = jnp.where(on_diag, h["g_diag"][:, d * c:(d + 1) * c], h["s"])
        for hh in heads:
            finish(hh, r0, hs[hh])
        return carry

    lax.fori_loop(0, HGRN_ROWS // c, chunk, 0, unroll=True)


def _hgrn(proj, lb, norm_g, bsz, seq):
    nr = seq // HGRN_ROWS
    hp = HGRN_HEADS_PER_STEP
    width = hp * 128
    c0 = (A_QW + 2 * A_KVW) // width
    nhp = B_HEADS // hp
    spec = lambda off: pl.BlockSpec((HGRN_ROWS, width), lambda b, h, r: (b * nr + r, off + h))
    vec = pl.BlockSpec((1, 1, width), lambda b, h, r: (h, 0, 0))
    return pl.pallas_call(
        _hgrn_kernel,
        grid=(bsz, nhp, nr),
        in_specs=[spec(c0), spec(c0 + nhp), spec(c0 + 2 * nhp), spec(c0 + 3 * nhp), vec, vec],
        out_specs=pl.BlockSpec((HGRN_ROWS, width), lambda b, h, r: (b * nr + r, h)),
        out_shape=jax.ShapeDtypeStruct((bsz * seq, B_VW), BF16),
        scratch_shapes=[pltpu.VMEM((hp, B_DV, B_DK), F32)],
        compiler_params=_cparams(3, 32 * MIB),
        name="hgrn2",
    )(proj, proj, proj, proj, lb.reshape(nhp, 1, width), norm_g.reshape(nhp, 1, width))


def _outproj_kernel(ya_ref, yb_ref, wa_ref, wb_ref, x_ref, g_ref, b_ref, wrh_ref, wrl_ref, rb_ref,
                    xo_ref, xp_ref, rt_ref, cnt_ref, run_ref, lg_ref):
    step = pl.program_id(0)

    @pl.when(step == 0)
    def _():
        run_ref[...] = jnp.zeros_like(run_ref)
        lg_ref[...] = jnp.zeros_like(lg_ref)

    lg_prev = lg_ref[...]
    mix = jnp.dot(ya_ref[...], wa_ref[...], preferred_element_type=F32)
    mix = mix + jnp.dot(yb_ref[...], wb_ref[...], preferred_element_type=F32)
    y = _layer_norm_rows(DN_ALPHA * x_ref[...] + mix, g_ref[...], b_ref[...])
    xo_ref[...] = y
    _store_token_tiles(xp_ref, 0, _pack_rows(y))
    y_hi = y.astype(BF16)
    y_lo = (y - y_hi.astype(F32)).astype(BF16)
    lg = jnp.dot(y_hi, wrh_ref[...], preferred_element_type=F32)
    lg = lg + jnp.dot(y_lo, wrh_ref[...], preferred_element_type=F32)
    lg = lg + jnp.dot(y_hi, wrl_ref[...], preferred_element_type=F32)
    lg_ref[...] = lg + rb_ref[...]
    _route_rows(lg_prev, (step > 0).astype(F32), rt_ref, cnt_ref, run_ref)


def _outproj_ln_route(ya, yb, w_out, x, g, b, wr_hi, wr_lo, rbias, tm=256):
    n = x.shape[0]
    ka = ya.shape[1]
    nsteps = n // tm
    cur = lambda i: jnp.minimum(i, nsteps - 1)
    row = lambda width: pl.BlockSpec((tm, width), lambda i: (cur(i), 0))
    const = lambda shape: pl.BlockSpec(shape, lambda i: (0, 0))
    return pl.pallas_call(
        _outproj_kernel,
        grid=(nsteps + 1,),
        in_specs=[row(ka), row(ka),
                  pl.BlockSpec((ka, D_MODEL), lambda i: (0, 0)), pl.BlockSpec((ka, D_MODEL), lambda i: (1, 0)),
                  row(D_MODEL), const((1, D_MODEL)), const((1, D_MODEL)),
                  const((D_MODEL, ROUTE_W)), const((D_MODEL, ROUTE_W)), const((1, ROUTE_W))],
        out_specs=[row(D_MODEL), pl.BlockSpec((tm * TOK_TILE, LANES_V7X), lambda i: (cur(i), 0)),
                   pl.BlockSpec((tm, ROUTE_W), lambda i: (jnp.maximum(i - 1, 0), 0)), const((1, ROUTE_W))],
        out_shape=[jax.ShapeDtypeStruct((n, D_MODEL), F32),
                   jax.ShapeDtypeStruct((n * TOK_TILE, LANES_V7X), U32),
                   jax.ShapeDtypeStruct((n, ROUTE_W), F32), jax.ShapeDtypeStruct((1, ROUTE_W), F32)],
        scratch_shapes=[pltpu.VMEM((1, ROUTE_W), F32), pltpu.VMEM((tm, ROUTE_W), F32)],
        compiler_params=_cparams(1, VMEM_LIMIT_V7X),
        name="outproj_ln_route",
    )(ya, yb, w_out, w_out, x, g.reshape(1, -1), b.reshape(1, -1), wr_hi, wr_lo, rbias)


def _router_weights(w_group, b_group, w_router, b_router):
    w = jnp.zeros((D_MODEL, ROUTE_W), F32)
    w = w.at[:, :N_GROUPS].set(w_group).at[:, N_GROUPS:N_GROUPS + N_EXPERTS].set(w_router)
    w_hi = w.astype(BF16)
    w_lo = (w - w_hi.astype(F32)).astype(BF16)
    rb = jnp.zeros((1, ROUTE_W), F32)
    rb = rb.at[0, :N_GROUPS].set(b_group).at[0, N_GROUPS:N_GROUPS + N_EXPERTS].set(b_router)
    return w_hi, w_lo, rb


def _route_rows(lg, live, rt_ref, cnt_ref, run_ref):
    tm = lg.shape[0]
    lane = lax.broadcasted_iota(jnp.int32, lg.shape, 1)
    sentinel = jnp.int32(ROUTE_W)
    rowmax = lambda mask: jnp.max(jnp.where(mask, lg, NEG_INF), axis=-1, keepdims=True)
    first = lambda mask: jnp.min(jnp.where(mask, lane, sentinel), axis=-1, keepdims=True)

    is_g = lane < N_GROUPS
    gmax = rowmax(is_g)
    g_idx = first(is_g & (lg == gmax))
    g_w = 1.0 / jnp.sum(jnp.where(is_g, jnp.exp(lg - gmax), 0.0), axis=-1, keepdims=True)

    e_lane = lane - N_GROUPS
    sel = (e_lane >= 0) & (e_lane < N_EXPERTS) & ((e_lane >> 3) == g_idx)
    m1 = rowmax(sel)
    i1 = first(sel & (lg == m1))
    sel2 = sel & (lane != i1)
    m2 = rowmax(sel2)
    i2 = first(sel2 & (lg == m2))
    ex = jnp.exp(m2 - m1)
    w0 = g_w / (1.0 + ex)
    w1 = g_w * ex / (1.0 + ex)

    oh0 = lane == i1
    oh1 = lane == i2
    both = (oh0 | oh1).astype(F32) * live
    rid = lax.broadcasted_iota(jnp.int32, (tm, tm), 0)
    cid = lax.broadcasted_iota(jnp.int32, (tm, tm), 1)
    before = jnp.dot((cid < rid).astype(BF16), both.astype(BF16), preferred_element_type=F32) + run_ref[...]
    rank0 = jnp.sum(jnp.where(oh0, before, 0.0), axis=-1, keepdims=True)
    rank1 = jnp.sum(jnp.where(oh1, before, 0.0), axis=-1, keepdims=True)
    run = run_ref[...] + jnp.sum(both, axis=0, keepdims=True)
    run_ref[...] = run
    cnt_ref[...] = run

    slab = jnp.zeros(lg.shape, F32)
    for ln, val in ((RT_W0, w0), (RT_W1, w1), (RT_E0, (i1 - N_GROUPS).astype(F32)),
                    (RT_E1, (i2 - N_GROUPS).astype(F32)), (RT_R0, rank0), (RT_R1, rank1)):
        slab = jnp.where(lane == ln, val, slab)
    rt_ref[...] = slab


assert EXPERTS_PER_GROUP == 8


def _block_layout(rt, cnt, n_tok):
    m = n_tok * TOP_K
    counts = cnt[0, N_GROUPS:N_GROUPS + N_EXPERTS].astype(jnp.int32)
    pcounts = (counts + MOE_TB - 1) // MOE_TB * MOE_TB
    pends = jnp.cumsum(pcounts)
    pstarts = pends - pcounts
    n_blocks = -(-(m + N_EXPERTS * (MOE_TB - 1)) // MOE_TB)
    e_idx = rt[:, RT_E0:RT_E1 + 1].astype(jnp.int32)
    rank = rt[:, RT_R0:RT_R1 + 1].astype(jnp.int32)
    onehot = e_idx[:, :, None] == jnp.arange(N_EXPERTS, dtype=jnp.int32)
    pos = jnp.sum(jnp.where(onehot, pstarts, 0), axis=-1) + rank
    blk_start = jnp.arange(n_blocks, dtype=jnp.int32) * MOE_TB
    blk_e = jnp.minimum(jnp.sum(blk_start[:, None] >= pends[None, :], axis=-1), N_EXPERTS - 1).astype(jnp.int32)
    nused = (pends[-1:] // MOE_TB).astype(jnp.int32)
    ids = jnp.arange(N_EXPERTS, dtype=jnp.int32)
    later_used = (ids[None, :] > ids[:, None]) & (counts[None, :] > 0)
    next_used = jnp.min(jnp.where(later_used, ids[None, :], N_EXPERTS), axis=-1)
    next_used = jnp.where(next_used == N_EXPERTS, -1, next_used).astype(jnp.int32)
    nxt_e = jnp.sum(jnp.where(blk_e[:, None] == ids[None, :], next_used[None, :], 0), axis=-1).astype(jnp.int32)
    return pos, blk_e, nxt_e, nused, n_blocks


def _step_indices(pos, tm):
    nsteps = pos.shape[0] // tm
    return pos.reshape(nsteps, tm, TOP_K).transpose(0, 2, 1).reshape(nsteps, 1, TOP_K * tm)


def _tile_rows(i):
    return pl.ds(pl.multiple_of(i * TOK_TILE, TOK_TILE), TOK_TILE)


def _dispatch_kernel(pos_ref, xp_ref, init_hbm, xs_hbm, xbuf, sem, *, tm):
    del init_hbm
    s = pl.program_id(0)
    nsteps = pl.num_programs(0)
    slot = s % 2

    def copy(j, dst, sl):
        return pltpu.make_async_copy(xbuf.at[sl, _tile_rows(j)], xs_hbm.at[_tile_rows(dst)], sem.at[sl])

    def drain(sl):
        def body(j, c):
            copy(0, 0, sl).wait()
            return c
        lax.fori_loop(0, TOP_K * tm, body, 0, unroll=8)

    @pl.when(s >= 2)
    def _():
        drain(slot)

    xbuf[slot] = xp_ref[...]

    for j in range(tm):
        for k in range(TOP_K):
            copy(j, pos_ref[0, 0, k * tm + j], slot).start(priority=k)

    @pl.when(s == nsteps - 1)
    def _():
        drain(1 - slot)
        drain(slot)


def _moe_dispatch(xp, pos, n_pad, tm=512):
    n = pos.shape[0]
    assert n // tm >= 2
    pos3 = _step_indices(pos, tm)
    init = jnp.zeros((n_pad * TOK_TILE, LANES_V7X), U32)
    return pl.pallas_call(
        functools.partial(_dispatch_kernel, tm=tm),
        grid=(n // tm,),
        in_specs=[pl.BlockSpec((1, 1, TOP_K * tm), lambda i: (i, 0, 0), memory_space=pltpu.SMEM),
                  pl.BlockSpec((tm * TOK_TILE, LANES_V7X), lambda i: (i, 0)), pl.BlockSpec(memory_space=pl.ANY)],
        out_specs=pl.BlockSpec(memory_space=pl.ANY),
        out_shape=jax.ShapeDtypeStruct(init.shape, U32),
        scratch_shapes=[pltpu.VMEM((2, tm * TOK_TILE, LANES_V7X), U32), pltpu.SemaphoreType.DMA((2,))],
        input_output_aliases={2: 0},
        compiler_params=_cparams(1),
        name="moe_dispatch",
    )(pos3, xp, init)


def _moe_kernel(blk_e_ref, nxt_e_ref, nused_ref, xs_ref, w1_hbm, w3_hbm, w2_hbm, ys_ref,
                wf1, wf3, wf2, w1b, w3b, w2b, slot_ref, sem, *, layer):
    s = pl.program_id(0)
    nused = nused_ref[0]

    def fetch(e, sl):
        return [pltpu.make_async_copy(w_hbm.at[layer, e], wf.at[sl], sem.at[sl, k])
                for k, (w_hbm, wf) in enumerate(((w1_hbm, wf1), (w3_hbm, wf3), (w2_hbm, wf2)))]

    @pl.when(s >= nused)
    def _():
        ys_ref[...] = jnp.zeros_like(ys_ref)

    @pl.when(s < nused)
    def _():
        e = blk_e_ref[s]
        prev = blk_e_ref[jnp.maximum(s - 1, 0)]

        @pl.when(s == 0)
        def _():
            slot_ref[0] = 0
            for cp in fetch(e, 0):
                cp.start()

        @pl.when((s > 0) & (e != prev))
        def _():
            slot_ref[0] = 1 - slot_ref[0]

        @pl.when((s == 0) | (e != prev))
        def _():
            sl = slot_ref[0]
            for cp in fetch(e, sl):
                cp.wait()
            w1b[...] = wf1[sl].astype(BF16)
            w3b[...] = wf3[sl].astype(BF16)
            w2b[...] = wf2[sl].astype(BF16)
            nxt = nxt_e_ref[s]

            @pl.when(nxt >= 0)
            def _():
                for cp in fetch(nxt, 1 - sl):
                    cp.start()

        xb = _unpack_rows(_load_token_tiles(xs_ref, 0, MOE_TB)).astype(BF16)
        h1 = jnp.dot(xb, w1b[...], preferred_element_type=F32)
        h3 = jnp.dot(xb, w3b[...], preferred_element_type=F32)
        h = (h1 * jax.nn.sigmoid(h1) * h3).astype(BF16)
        y = jnp.dot(h, w2b[...], preferred_element_type=F32)
        _store_token_tiles(ys_ref, 0, _pack_rows(y))


def _moe_experts(xs, blk_e, nxt_e, nused, w1, w3, w2, layer):
    n_blocks = blk_e.shape[0]
    tiles = pl.BlockSpec((MOE_TB * TOK_TILE, LANES_V7X), lambda s, be, ne, nu: (s, 0))
    hbm = pl.BlockSpec(memory_space=pl.ANY)
    up, down = (D_MODEL, D_EXPERT), (D_EXPERT, D_MODEL)
    grid_spec = pltpu.PrefetchScalarGridSpec(
        num_scalar_prefetch=3,
        grid=(n_blocks,),
        in_specs=[tiles, hbm, hbm, hbm],
        out_specs=tiles,
        scratch_shapes=[pltpu.VMEM((2,) + up, F32), pltpu.VMEM((2,) + up, F32), pltpu.VMEM((2,) + down, F32),
                        pltpu.VMEM(up, BF16), pltpu.VMEM(up, BF16), pltpu.VMEM(down, BF16),
                        pltpu.SMEM((1,), jnp.int32), pltpu.SemaphoreType.DMA((2, 3))],
    )
    return pl.pallas_call(
        functools.partial(_moe_kernel, layer=layer),
        grid_spec=grid_spec,
        out_shape=jax.ShapeDtypeStruct(xs.shape, U32),
        compiler_params=_cparams(1, VMEM_LIMIT_V7X),
        name="moe_experts",
    )(blk_e, nxt_e, nused, xs, w1, w3, w2)


def _combine_kernel(pos_ref, nxt_ref, ys_hbm, x_ref, rt_ref, g_ref, b_ref, o_ref, ybuf, sem, *, tm):
    s = pl.program_id(0)
    nsteps = pl.num_programs(0)
    slot = s % 2

    def copy(src, j, sl):
        return pltpu.make_async_copy(ys_hbm.at[_tile_rows(src)], ybuf.at[sl, _tile_rows(j)], sem.at[sl])

    def start_gather(idx_ref, sl):
        def body(j, c):
            copy(idx_ref[0, 0, j], j, sl).start()
            return c
        lax.fori_loop(0, TOP_K * tm, body, 0, unroll=8)

    @pl.when(s == 0)
    def _():
        start_gather(pos_ref, 0)

    @pl.when(s + 1 < nsteps)
    def _():
        for j in range(TOP_K * tm):
            copy(nxt_ref[0, 0, j], j, 1 - slot).start()

    def wait_body(j, c):
        copy(0, j, slot).wait()
        return c
    lax.fori_loop(0, TOP_K * tm, wait_body, 0, unroll=8)

    rt = rt_ref[...]
    ffn = _unpack_rows(_load_token_tiles(ybuf, 0, tm, lead=slot)) * rt[:, RT_W0:RT_W0 + 1]
    ffn = ffn + _unpack_rows(_load_token_tiles(ybuf, tm * TOK_TILE, tm, lead=slot)) * rt[:, RT_W1:RT_W1 + 1]
    o_ref[...] = _layer_norm_rows(DN_ALPHA * x_ref[...] + ffn, g_ref[...], b_ref[...])


def _moe_combine(ys, pos, rt, x, g, b, tm=256):
    n = x.shape[0]
    nsteps = n // tm
    pos3 = _step_indices(pos, tm)
    smem_blk = lambda f: pl.BlockSpec((1, 1, TOP_K * tm), f, memory_space=pltpu.SMEM)
    row = lambda width: pl.BlockSpec((tm, width), lambda i: (i, 0))
    const = lambda shape: pl.BlockSpec(shape, lambda i: (0, 0))
    return pl.pallas_call(
        functools.partial(_combine_kernel, tm=tm),
        grid=(nsteps,),
        in_specs=[smem_blk(lambda i: (i, 0, 0)), smem_blk(lambda i: (jnp.minimum(i + 1, nsteps - 1), 0, 0)),
                  pl.BlockSpec(memory_space=pl.ANY), row(D_MODEL), row(ROUTE_W),
                  const((1, D_MODEL)), const((1, D_MODEL))],
        out_specs=row(D_MODEL),
        out_shape=jax.ShapeDtypeStruct((n, D_MODEL), F32),
        scratch_shapes=[pltpu.VMEM((2, TOP_K * tm * TOK_TILE, LANES_V7X), U32), pltpu.SemaphoreType.DMA((2,))],
        compiler_params=_cparams(1, 40 * MIB),
        name="moe_combine",
    )(pos3, pos3, ys, x, rt, g.reshape(1, -1), b.reshape(1, -1))


def _moe_layer(x1, xp, rt, cnt, w1, w3, w2, layer, g, b):
    n_tok = x1.shape[0]
    pos, blk_e, nxt_e, nused, n_blocks = _block_layout(rt, cnt, n_tok)
    xs = _moe_dispatch(xp, pos, n_blocks * MOE_TB)
    ys = _moe_experts(xs, blk_e, nxt_e, nused, w1, w3, w2, layer)
    return _moe_combine(ys, pos, rt, x1, g, b)


def _gmlp_kernel(u_ref, v_ref, g_ref, b_ref, w_ref, bs_ref, o_ref, *, chunks):
    for ci in range(chunks):
        rows = slice(ci * C_CHUNK, (ci + 1) * C_CHUNK)
        u = jax.nn.gelu(u_ref[rows, :])
        v = _layer_norm_rows(jax.nn.gelu(v_ref[rows, :]), g_ref[...], b_ref[...]).astype(BF16)
        for gi in range(C_GROUPS):
            cols = slice(gi * C_GROUP_DIM, (gi + 1) * C_GROUP_DIM)
            mixed = jnp.dot(w_ref[gi], v[:, cols], preferred_element_type=F32) + bs_ref[:, cols]
            o_ref[rows, cols] = (u[:, cols] * mixed).astype(o_ref.dtype)


def _gmlp(proj, ln_g, ln_b, w_s, b_s, chunks=4):
    n = proj.shape[0]
    tm = chunks * C_CHUNK
    w = (w_s * jnp.tril(jnp.ones((C_CHUNK, C_CHUNK), w_s.dtype))).astype(BF16)
    bs_full = jnp.repeat(b_s.T, C_GROUP_DIM, axis=1)
    const2 = lambda shape: pl.BlockSpec(shape, lambda i: (0, 0))
    return pl.pallas_call(
        functools.partial(_gmlp_kernel, chunks=chunks),
        grid=(n // tm,),
        in_specs=[pl.BlockSpec((tm, C_W), lambda i: (i, 0)), pl.BlockSpec((tm, C_W), lambda i: (i, 1)),
                  const2((1, C_W)), const2((1, C_W)),
                  pl.BlockSpec((C_GROUPS, C_CHUNK, C_CHUNK), lambda i: (0, 0, 0)), const2((C_CHUNK, C_W))],
        out_specs=pl.BlockSpec((tm, C_W), lambda i: (i, 0)),
        out_shape=jax.ShapeDtypeStruct((n, C_W), BF16),
        compiler_params=_cparams(1, 32 * MIB),
        name="gmlp_gating",
    )(proj, proj, ln_g.reshape(1, -1), ln_b.reshape(1, -1), w, bs_full)


CONV_HIST = 32


def _conv_kernel(a_ref, gt_ref, ap_ref, gp_ref, w_ref, cb_ref, g_ref, b_ref, o_ref, hbuf, hshift, *, ts):
    i = pl.program_id(1)
    hist = ap_ref[...] * jax.nn.sigmoid(gp_ref[...])
    hbuf[0:CONV_HIST, :] = jnp.where(i > 0, hist, jnp.zeros_like(hist))
    hbuf[CONV_HIST:CONV_HIST + ts, :] = a_ref[...] * jax.nn.sigmoid(gt_ref[...])
    off = CONV_HIST - (D_CONV - 1)
    acc = jnp.zeros((ts, D_CHANNELS), F32) + cb_ref[...]
    for r in range(SUBLANES_V7X):
        taps = [j for j in range(D_CONV) if (off + j) % SUBLANES_V7X == r]
        if not taps:
            continue
        src = hbuf
        if r:
            span = max(taps) + off - r + ts
            hshift[0:span, :] = hbuf[r:r + span, :]
            src = hshift
        for j in taps:
            base = off + j - r
            acc = acc + w_ref[j:j + 1, :] * src[base:base + ts, :]
    y = _layer_norm_rows(acc, g_ref[...], b_ref[...])
    o_ref[...] = (y * jax.nn.sigmoid(y)).astype(o_ref.dtype)


def _conformer_conv(proj, conv_w, conv_b, ln_g, ln_b, bsz, seq, ts=512):
    nt = seq // ts
    acol = 2 * C_W // D_CHANNELS
    gcol = acol + 1
    hb = ts // CONV_HIST
    cur = lambda col: pl.BlockSpec((ts, D_CHANNELS), lambda b, i: (b * nt + i, col))
    prev = lambda col: pl.BlockSpec((CONV_HIST, D_CHANNELS),
                                    lambda b, i: (jnp.maximum((b * nt + i) * hb - 1, 0), col))
    const2 = lambda shape: pl.BlockSpec(shape, lambda b, i: (0, 0))
    return pl.pallas_call(
        functools.partial(_conv_kernel, ts=ts),
        grid=(bsz, nt),
        in_specs=[cur(acol), cur(gcol), prev(acol), prev(gcol),
                  const2((D_CONV, D_CHANNELS)), const2((1, D_CHANNELS)), const2((1, D_CHANNELS)),
                  const2((1, D_CHANNELS))],
        out_specs=pl.BlockSpec((ts, D_CHANNELS), lambda b, i: (b * nt + i, 0)),
        out_shape=jax.ShapeDtypeStruct((bsz * seq, D_CHANNELS), BF16),
        scratch_shapes=[pltpu.VMEM((CONV_HIST + ts, D_CHANNELS), F32), pltpu.VMEM((CONV_HIST + ts, D_CHANNELS), F32)],
        compiler_params=_cparams(2, 32 * MIB),
        name="conformer_conv",
    )(proj, proj, proj, proj, conv_w, conv_b.reshape(1, -1), ln_g.reshape(1, -1), ln_b.reshape(1, -1))


def kernel(x, w_in_ab, attn_sinks, rel_bias, hgrn_lb_logits, hgrn_norm_g, w_out_ab, w_in_cd, gmlp_ln_g, gmlp_ln_b, gmlp_w_s, gmlp_b_s, conv_w, conv_b, conv_ln_g, conv_ln_b, w_out_cd, ln_mix_g, ln_mix_b, ln_ffn_g, ln_ffn_b, moe_w_group, moe_b_group, moe_w_router, moe_b_router, moe_w1, moe_w3, moe_w2):
    bsz, seq = x.shape[0], x.shape[1]
    n_tok = bsz * seq
    xf = x.reshape(n_tok, D_MODEL)
    lb_table = jnp.cumsum(jax.nn.softmax(hgrn_lb_logits.astype(F32), axis=0), axis=0)
    bias = _bias_table(rel_bias.astype(F32), _t5_bucket_table())

    for layer in range(DEPTH):
        j = layer // 2
        if layer % 2 == 0:
            proj = _matmul(xf, w_in_ab[j].astype(BF16), 512, EVEN_IN // 3, F32)
            ya = _attention(proj, bias, attn_sinks[j].astype(F32), bsz, seq)
            yb = _hgrn(proj, lb_table[layer], hgrn_norm_g[j].astype(F32), bsz, seq)
            w_out = w_out_ab[j]
        else:
            proj = _matmul(xf, w_in_cd[j].astype(BF16), 512, ODD_IN // 2, F32)
            ya = _gmlp(proj, gmlp_ln_g[j], gmlp_ln_b[j], gmlp_w_s[j], gmlp_b_s[j])
            yb = _conformer_conv(proj, conv_w[j], conv_b[j], conv_ln_g[j], conv_ln_b[j], bsz, seq)
            w_out = w_out_cd[j]
        wr_hi, wr_lo, rbias = _router_weights(moe_w_group[layer], moe_b_group[layer],
                                              moe_w_router[layer], moe_b_router[layer])
        x1, xp, rt, cnt = _outproj_ln_route(ya, yb, w_out.astype(BF16), xf, ln_mix_g[layer], ln_mix_b[layer],
                                            wr_hi, wr_lo, rbias)
        xf = _moe_layer(x1, xp, rt, cnt, moe_w1, moe_w3, moe_w2, layer, ln_ffn_g[layer], ln_ffn_b[layer])
    return xf.reshape(bsz, seq, D_MODEL)
```

```python
import functools
import math

import jax
import jax.numpy as jnp
from jax import lax
from jax.experimental import pallas as pl
from jax.experimental.pallas import tpu as pltpu

D_MODEL = 2048
DEPTH = 2
A_HEADS = 16
A_KV_HEADS = 2
A_HEAD_DIM = 64
A_WINDOW = 128
A_BLOCK = 128
REL_BUCKETS = 32
REL_MAX_DIST = 128
B_HEADS = 8
B_DK = 128
B_DV = 128
C_GROUPS = 8
C_GROUP_DIM = 128
C_CHUNK = 128
D_CHANNELS = 1024
D_CONV = 31
A_QW = A_HEADS * A_HEAD_DIM
A_KVW = A_KV_HEADS * A_HEAD_DIM
B_KW = B_HEADS * B_DK
B_VW = B_HEADS * B_DV
C_W = C_GROUPS * C_GROUP_DIM
EVEN_IN = A_QW + 2 * A_KVW + 2 * B_KW + 2 * B_VW
ODD_IN = 2 * C_W + 2 * D_CHANNELS
N_GROUPS = 4
EXPERTS_PER_GROUP = 8
N_EXPERTS = N_GROUPS * EXPERTS_PER_GROUP
TOP_K = 2
D_EXPERT = 512
DN_ALPHA = (2 * DEPTH) ** 0.25
LN_EPS = 1e-5
RMS_EPS = 1e-6

LANES_V7X = 128
SUBLANES_V7X = 8
MIB = 1024 * 1024
VMEM_LIMIT_V7X = 56 * MIB

HGRN_CHUNK = 128
HGRN_SUB = 8
HGRN_GROUP = 32
HGRN_ROWS = 512
MOE_TB = 256
ROUTE_W = LANES_V7X
HALF = D_MODEL // 2
TOK_TILE = HALF // LANES_V7X
assert TOK_TILE == SUBLANES_V7X

BF16 = jnp.bfloat16
F32 = jnp.float32
U32 = jnp.uint32
NEG_INF = float("-inf")
LOG2E = math.log2(math.e)
RT_W0, RT_W1, RT_E0, RT_E1, RT_R0, RT_R1 = range(6)


def _cparams(n_axes, vmem_bytes=None):
    return pltpu.CompilerParams(dimension_semantics=("arbitrary",) * n_axes, vmem_limit_bytes=vmem_bytes)


def _layer_norm_rows(z, g, b):
    mu = jnp.mean(z, axis=-1, keepdims=True)
    zc = z - mu
    var = jnp.mean(zc * zc, axis=-1, keepdims=True)
    return zc * lax.rsqrt(var + LN_EPS) * g + b


def _dot_nt(a, b):
    return lax.dot_general(a, b, (((1,), (1,)), ((), ())), preferred_element_type=F32)


def _dot_tn(a, b):
    return lax.dot_general(a, b, (((0,), (0,)), ((), ())), preferred_element_type=F32)


def _pack_rows(y):
    lo = lax.bitcast_convert_type(y[:, :HALF].astype(BF16).astype(F32), U32) >> 16
    hi = lax.bitcast_convert_type(y[:, HALF:].astype(BF16).astype(F32), U32) & jnp.uint32(0xFFFF0000)
    return lo | hi


def _unpack_rows(p):
    lo = lax.bitcast_convert_type(p << 16, F32)
    hi = lax.bitcast_convert_type(p & jnp.uint32(0xFFFF0000), F32)
    return jnp.concatenate([lo, hi], axis=1)


def _store_token_tiles(ref, base, packed):
    rows = packed.shape[0]
    for c in range(TOK_TILE):
        ref[pl.ds(base + c, rows, stride=TOK_TILE), :] = packed[:, c * LANES_V7X:(c + 1) * LANES_V7X]


def _load_token_tiles(ref, base, rows, lead=None):
    parts = []
    for c in range(TOK_TILE):
        idx = (pl.ds(base + c, rows, stride=TOK_TILE), slice(None))
        parts.append(ref[idx] if lead is None else ref[(lead,) + idx])
    return jnp.concatenate(parts, axis=1)


def _mm_kernel(a_ref, w_ref, o_ref):
    a = a_ref[...].astype(BF16)
    o_ref[...] = jnp.dot(a, w_ref[...], preferred_element_type=F32).astype(o_ref.dtype)


def _matmul(a, w, tm, tn, out_dtype):
    m, k = a.shape
    n = w.shape[1]
    return pl.pallas_call(
        _mm_kernel,
        grid=(n // tn, m // tm),
        in_specs=[pl.BlockSpec((tm, k), lambda j, i: (i, 0)), pl.BlockSpec((k, tn), lambda j, i: (0, j))],
        out_specs=pl.BlockSpec((tm, tn), lambda j, i: (i, j)),
        out_shape=jax.ShapeDtypeStruct((m, n), out_dtype),
        compiler_params=_cparams(2, VMEM_LIMIT_V7X),
        name="proj_matmul",
    )(a, w)


def _bias_table_kernel(rb_ref, bucket_ref, o_ref):
    bucket = bucket_ref[...]
    for h in range(A_HEADS):
        acc = jnp.zeros(bucket.shape, F32)
        for bk in range(REL_BUCKETS):
            acc = jnp.where(bucket == bk, rb_ref[bk, h], acc)
        o_ref[h] = acc


def _bias_table(rel_bias, bucket):
    return pl.pallas_call(
        _bias_table_kernel,
        in_specs=[pl.BlockSpec(memory_space=pltpu.SMEM), pl.BlockSpec(memory_space=pltpu.VMEM)],
        out_specs=pl.BlockSpec(memory_space=pltpu.VMEM),
        out_shape=jax.ShapeDtypeStruct((A_HEADS,) + bucket.shape, F32),
        name="rel_bias_table",
    )(rel_bias, bucket)


def _t5_bucket_table():
    t_loc = jnp.arange(A_BLOCK, dtype=jnp.int32)[:, None]
    j_loc = jnp.arange(A_BLOCK, dtype=jnp.int32)[None, :]
    dist = jnp.where(j_loc <= t_loc, t_loc - j_loc, t_loc + A_BLOCK - j_loc)
    max_exact = REL_BUCKETS // 2
    d = jnp.maximum(dist, 1).astype(F32)
    large = max_exact + (jnp.log(d / max_exact) / math.log(REL_MAX_DIST / max_exact)
                         * (REL_BUCKETS - max_exact)).astype(jnp.int32)
    large = jnp.minimum(large, REL_BUCKETS - 1)
    return jnp.where(dist < max_exact, dist, large)


assert A_WINDOW == A_BLOCK


def _attn_kernel(q_ref, kc_ref, vc_ref, kp_ref, vp_ref, bias_ref, sink_ref, o_ref):
    n = pl.program_id(1)
    blk = A_BLOCK
    k2 = jnp.concatenate([kp_ref[...], kc_ref[...]], axis=0)
    v2 = jnp.concatenate([vp_ref[...], vc_ref[...]], axis=0)
    k2r = pltpu.roll(k2, A_HEAD_DIM, 1)
    v2r = pltpu.roll(v2, A_HEAD_DIM, 1)
    lo = lax.broadcasted_iota(jnp.int32, k2.shape, 1) < A_HEAD_DIM
    zero = jnp.zeros_like(k2)

    def placed(x, xr, g, par):
        src = x if g == par else xr
        return (jnp.where(lo, src, zero) if par == 0 else jnp.where(lo, zero, src)).astype(BF16)

    kk = [[placed(k2, k2r, g, par) for par in range(2)] for g in range(A_KV_HEADS)]
    vv = [[placed(v2, v2r, g, par) for par in range(2)] for g in range(A_KV_HEADS)]

    own = (lax.broadcasted_iota(jnp.int32, (blk, blk), 1) <= lax.broadcasted_iota(jnp.int32, (blk, blk), 0))
    no_prev = jnp.where(n > 0, 0.0, NEG_INF)
    zero_p = jnp.zeros((blk, blk), F32)

    heads_per_kv = A_HEADS // A_KV_HEADS
    for p in range(A_HEADS // 2):
        g = (2 * p) // heads_per_kv
        qp = (q_ref[:, p * 128:(p + 1) * 128] * (A_HEAD_DIM ** -0.5)).astype(BF16)
        acc = jnp.zeros((blk, 128), F32)
        for par in range(2):
            h = 2 * p + par
            sink = sink_ref[h]
            s2 = _dot_nt(qp, kk[g][par])
            logits = jnp.where(own, s2[:, blk:], s2[:, :blk] + no_prev) + bias_ref[h]
            m = jnp.maximum(jnp.max(logits, axis=-1, keepdims=True), sink)
            e = jnp.exp(logits - m)
            den = jnp.sum(e, axis=-1, keepdims=True) + jnp.exp(sink - m)
            e2 = jnp.concatenate([jnp.where(own, zero_p, e), jnp.where(own, e, zero_p)], axis=1).astype(BF16)
            acc = acc + jnp.dot(e2, vv[g][par], preferred_element_type=F32) * (1.0 / den)
        o_ref[:, p * 128:(p + 1) * 128] = acc.astype(o_ref.dtype)


def _attention(proj, bias, sinks, bsz, seq):
    nb = seq // A_BLOCK
    kcol = A_QW // 128
    vcol = kcol + 1
    row = lambda b, n: b * nb + n
    prow = lambda b, n: b * nb + jnp.maximum(n - 1, 0)
    return pl.pallas_call(
        _attn_kernel,
        grid=(bsz, nb),
        in_specs=[
            pl.BlockSpec((A_BLOCK, A_QW), lambda b, n: (row(b, n), 0)),
            pl.BlockSpec((A_BLOCK, 128), lambda b, n: (row(b, n), kcol)),
            pl.BlockSpec((A_BLOCK, 128), lambda b, n: (row(b, n), vcol)),
            pl.BlockSpec((A_BLOCK, 128), lambda b, n: (prow(b, n), kcol)),
            pl.BlockSpec((A_BLOCK, 128), lambda b, n: (prow(b, n), vcol)),
            pl.BlockSpec((A_HEADS, A_BLOCK, A_BLOCK), lambda b, n: (0, 0, 0)),
            pl.BlockSpec(memory_space=pltpu.SMEM),
        ],
        out_specs=pl.BlockSpec((A_BLOCK, A_QW), lambda b, n: (row(b, n), 0)),
        out_shape=jax.ShapeDtypeStruct((bsz * seq, A_QW), BF16),
        compiler_params=_cparams(2, 32 * MIB),
        name="swa_attention",
    )(proj, proj, proj, proj, proj, bias, sinks)


HGRN_HEADS_PER_STEP = 2


def _hgrn_kernel(q_ref, f_ref, i_ref, g_ref, lb_ref, ng_ref, o_ref, st_ref):
    c, sub, grp = HGRN_CHUNK, HGRN_SUB, HGRN_GROUP
    ngrp = c // grp

    @pl.when(pl.program_id(2) == 0)
    def _():
        st_ref[...] = jnp.zeros_like(st_ref)

    rid = lax.broadcasted_iota(jnp.int32, (c, c), 0)
    cid = lax.broadcasted_iota(jnp.int32, (c, c), 1)
    grp_start = (rid // grp) * grp
    sub_start = (rid // sub) * sub
    m_cum = jnp.concatenate([(cid < grp_start).astype(F32),
                             ((cid >= grp_start) & (cid < sub_start)).astype(F32),
                             ((cid >= sub_start) & (cid <= rid)).astype(F32)], axis=0).astype(BF16)
    band = jnp.where(cid >= sub_start, rid - cid, -1)
    same_grp = (cid // grp) == (rid // grp)
    heads = range(HGRN_HEADS_PER_STEP)
    zeros_bf = lambda rows: jnp.zeros((rows, B_DK), BF16)

    def scores(hh, r0):
        cols = slice(hh * 128, (hh + 1) * 128)
        lb = lb_ref[0, :, cols]
        q = q_ref[pl.ds(r0, c), cols]
        fl = f_ref[pl.ds(r0, c), cols]
        qf = q * jax.nn.sigmoid(q)
        f = lb + (1.0 - lb) * jax.nn.sigmoid(fl)
        kin = 1.0 - f
        logf = jnp.log(f)
        l1 = logf.astype(BF16)
        res = logf - l1.astype(F32)
        l2 = res.astype(BF16)
        l3 = (res - l2.astype(F32)).astype(BF16)
        cum = jnp.dot(m_cum, jnp.concatenate([l1, l2, l3], axis=1), preferred_element_type=F32)
        cum = cum[:, :B_DK] + cum[:, B_DK:2 * B_DK] + cum[:, 2 * B_DK:]
        rg, rs, bq = cum[:c], cum[c:2 * c], cum[2 * c:]
        lk = jnp.log(kin)
        wq = lk - bq
        wg = wq - rs
        wb = wg - rg
        qt = qf * jnp.exp(bq)
        qg = qt * jnp.exp(rs)
        o = _dot_nt((qg * jnp.exp(rg)).astype(BF16), st_ref[hh].astype(BF16))

        kts = []
        for i in range(1, ngrp):
            n = i * grp
            kts += [jnp.exp(rg[n:n + 1, :] + wb[:n, :]).astype(BF16), zeros_bf(c - n)]
        g_far = _dot_nt(qg.astype(BF16), jnp.concatenate(kts, axis=0))
        s_far = jnp.concatenate([jnp.zeros((grp, c), F32)] +
                                [g_far[i * grp:(i + 1) * grp, (i - 1) * c:i * c] for i in range(1, ngrp)], axis=0)

        kts = []
        for j in range(1, grp // sub):
            for gi in range(ngrp):
                a, n = gi * grp, j * sub
                kts += [jnp.exp(rs[a + n:a + n + 1, :] + wg[a:a + n, :]).astype(BF16), zeros_bf(grp - n)]
        g_near = _dot_nt(qt.astype(BF16), jnp.concatenate(kts, axis=0))
        pieces = []
        for i in range(c // sub):
            j = i % (grp // sub)
            pieces.append(jnp.zeros((sub, c), F32) if j == 0 else g_near[i * sub:(i + 1) * sub, (j - 1) * c:j * c])
        s = jnp.where(same_grp, jnp.concatenate(pieces, axis=0), s_far)

        bq2, wq2 = bq * LOG2E, wq * LOG2E
        kds = [kin.astype(BF16)] + [jnp.exp2(pltpu.roll(bq2, c - d, 0) + wq2).astype(BF16) for d in range(1, sub)]
        g_diag = _dot_nt(qf.astype(BF16), jnp.concatenate(kds, axis=0))
        last = slice(c - 1, c)
        return dict(o=o, s=s, g_diag=g_diag, wb=wb, b_last=rg[last, :] + rs[last, :] + bq[last, :])

    def finish(hh, r0, h):
        cols = slice(hh * 128, (hh + 1) * 128)
        vb = i_ref[pl.ds(r0, c), cols].astype(BF16)
        gt = g_ref[pl.ds(r0, c), cols]
        o = h["o"] + jnp.dot(h["s"].astype(BF16), vb, preferred_element_type=F32)
        st_ref[hh] = (st_ref[hh] * jnp.exp(h["b_last"])
                      + _dot_tn(vb, jnp.exp(h["b_last"] + h["wb"]).astype(BF16)))
        o = o * lax.rsqrt(jnp.mean(o * o, axis=-1, keepdims=True) + RMS_EPS)
        o_ref[pl.ds(r0, c), cols] = (o * ng_ref[0, :, cols] * (gt * jax.nn.sigmoid(gt))).astype(o_ref.dtype)

    def chunk(ci, carry):
        r0 = pl.multiple_of(ci * c, c)
        hs = [scores(hh, r0) for hh in heads]
        for d in range(sub):
            on_diag = band == d
            for h in hs:
                h["s"] = jnp.where(on_diag, h["g_diag"][:, d * c:(d + 1) * c], h["s"])
        for hh in heads:
            finish(hh, r0, hs[hh])
        return carry

    lax.fori_loop(0, HGRN_ROWS // c, chunk, 0, unroll=True)


def _hgrn(proj, lb, norm_g, bsz, seq):
    nr = seq // HGRN_ROWS
    hp = HGRN_HEADS_PER_STEP
    width = hp * 128
    c0 = (A_QW + 2 * A_KVW) // width
    nhp = B_HEADS // hp
    spec = lambda off: pl.BlockSpec((HGRN_ROWS, width), lambda b, h, r: (b * nr + r, off + h))
    vec = pl.BlockSpec((1, 1, width), lambda b, h, r: (h, 0, 0))
    return pl.pallas_call(
        _hgrn_kernel,
        grid=(bsz, nhp, nr),
        in_specs=[spec(c0), spec(c0 + nhp), spec(c0 + 2 * nhp), spec(c0 + 3 * nhp), vec, vec],
        out_specs=pl.BlockSpec((HGRN_ROWS, width), lambda b, h, r: (b * nr + r, h)),
        out_shape=jax.ShapeDtypeStruct((bsz * seq, B_VW), BF16),
        scratch_shapes=[pltpu.VMEM((hp, B_DV, B_DK), F32)],
        compiler_params=_cparams(3, 32 * MIB),
        name="hgrn2",
    )(proj, proj, proj, proj, lb.reshape(nhp, 1, width), norm_g.reshape(nhp, 1, width))


OUTPROJ_LAG = 2


def _outproj_kernel(ya_ref, yb_ref, wa_ref, wb_ref, x_ref, g_ref, b_ref, wr_ref, rb_ref,
                    xo_ref, xp_ref, rt_ref, cnt_ref, run_ref, lg_ref, mix_ref):
    step = pl.program_id(0)

    @pl.when(step == 0)
    def _():
        run_ref[...] = jnp.zeros_like(run_ref)
        lg_ref[...] = jnp.zeros_like(lg_ref)
        mix_ref[...] = jnp.zeros_like(mix_ref)

    lg_prev = lg_ref[...]
    mix_prev = mix_ref[...]
    mix = jnp.dot(ya_ref[...], wa_ref[...], preferred_element_type=F32)
    mix_ref[...] = mix + jnp.dot(yb_ref[...], wb_ref[...], preferred_element_type=F32)

    y = _layer_norm_rows(DN_ALPHA * x_ref[...] + mix_prev, g_ref[...], b_ref[...])
    xo_ref[...] = y
    _store_token_tiles(xp_ref, 0, _pack_rows(y))
    tm = y.shape[0]
    y_hi = y.astype(BF16)
    y_lo = (y - y_hi.astype(F32)).astype(BF16)
    prod = jnp.dot(jnp.concatenate([y_hi, y_lo], axis=0), wr_ref[...], preferred_element_type=F32)
    lg = (prod[:tm, :ROUTE_W] + prod[tm:, :ROUTE_W]) + (prod[:tm, ROUTE_W:] + prod[tm:, ROUTE_W:])
    lg_ref[...] = lg + rb_ref[...]
    _route_rows(lg_prev, (step >= OUTPROJ_LAG).astype(F32), rt_ref, cnt_ref, run_ref)


def _outproj_ln_route(ya, yb, w_out, x, g, b, wr, rbias, tm=256):
    n = x.shape[0]
    ka = ya.shape[1]
    nsteps = n // tm
    lagged = lambda lag: (lambda i: (jnp.clip(i - lag, 0, nsteps - 1), 0))
    row = lambda width, lag: pl.BlockSpec((tm, width), lagged(lag))
    const = lambda shape: pl.BlockSpec(shape, lambda i: (0, 0))
    return pl.pallas_call(
        _outproj_kernel,
        grid=(nsteps + OUTPROJ_LAG,),
        in_specs=[row(ka, 0), row(ka, 0),
                  pl.BlockSpec((ka, D_MODEL), lambda i: (0, 0)), pl.BlockSpec((ka, D_MODEL), lambda i: (1, 0)),
                  row(D_MODEL, 1), const((1, D_MODEL)), const((1, D_MODEL)),
                  const((D_MODEL, 2 * ROUTE_W)), const((1, ROUTE_W))],
        out_specs=[row(D_MODEL, 1), pl.BlockSpec((tm * TOK_TILE, LANES_V7X), lagged(1)),
                   row(ROUTE_W, OUTPROJ_LAG), const((1, ROUTE_W))],
        out_shape=[jax.ShapeDtypeStruct((n, D_MODEL), F32),
                   jax.ShapeDtypeStruct((n * TOK_TILE, LANES_V7X), U32),
                   jax.ShapeDtypeStruct((n, ROUTE_W), F32), jax.ShapeDtypeStruct((1, ROUTE_W), F32)],
        scratch_shapes=[pltpu.VMEM((1, ROUTE_W), F32), pltpu.VMEM((tm, ROUTE_W), F32),
                        pltpu.VMEM((tm, D_MODEL), F32)],
        compiler_params=_cparams(1, VMEM_LIMIT_V7X),
        name="outproj_ln_route",
    )(ya, yb, w_out, w_out, x, g.reshape(1, -1), b.reshape(1, -1), wr, rbias)


def _router_weights(w_group, b_group, w_router, b_router):
    w = jnp.zeros((D_MODEL, ROUTE_W), F32)
    w = w.at[:, :N_GROUPS].set(w_group).at[:, N_GROUPS:N_GROUPS + N_EXPERTS].set(w_router)
    w_hi = w.astype(BF16)
    w_lo = (w - w_hi.astype(F32)).astype(BF16)
    rb = jnp.zeros((1, ROUTE_W), F32)
    rb = rb.at[0, :N_GROUPS].set(b_group).at[0, N_GROUPS:N_GROUPS + N_EXPERTS].set(b_router)
    return jnp.concatenate([w_hi, w_lo], axis=1), rb


def _route_rows(lg, live, rt_ref, cnt_ref, run_ref):
    tm = lg.shape[0]
    lane = lax.broadcasted_iota(jnp.int32, lg.shape, 1)
    sentinel = jnp.int32(ROUTE_W)
    rowmax = lambda mask: jnp.max(jnp.where(mask, lg, NEG_INF), axis=-1, keepdims=True)
    first = lambda mask: jnp.min(jnp.where(mask, lane, sentinel), axis=-1, keepdims=True)

    is_g = lane < N_GROUPS
    gmax = rowmax(is_g)
    g_idx = first(is_g & (lg == gmax))
    g_w = 1.0 / jnp.sum(jnp.where(is_g, jnp.exp(lg - gmax), 0.0), axis=-1, keepdims=True)

    e_lane = lane - N_GROUPS
    sel = (e_lane >= 0) & (e_lane < N_EXPERTS) & ((e_lane >> 3) == g_idx)
    m1 = rowmax(sel)
    i1 = first(sel & (lg == m1))
    sel2 = sel & (lane != i1)
    m2 = rowmax(sel2)
    i2 = first(sel2 & (lg == m2))
    ex = jnp.exp(m2 - m1)
    w0 = g_w / (1.0 + ex)
    w1 = g_w * ex / (1.0 + ex)

    oh0 = lane == i1
    oh1 = lane == i2
    both = (oh0 | oh1).astype(F32) * live
    rid = lax.broadcasted_iota(jnp.int32, (tm, tm), 0)
    cid = lax.broadcasted_iota(jnp.int32, (tm, tm), 1)
    before = jnp.dot((cid < rid).astype(BF16), both.astype(BF16), preferred_element_type=F32) + run_ref[...]
    rank0 = jnp.sum(jnp.where(oh0, before, 0.0), axis=-1, keepdims=True)
    rank1 = jnp.sum(jnp.where(oh1, before, 0.0), axis=-1, keepdims=True)
    run = run_ref[...] + jnp.sum(both, axis=0, keepdims=True)
    run_ref[...] = run
    cnt_ref[...] = run

    slab = jnp.zeros(lg.shape, F32)
    for ln, val in ((RT_W0, w0), (RT_W1, w1), (RT_E0, (i1 - N_GROUPS).astype(F32)),
                    (RT_E1, (i2 - N_GROUPS).astype(F32)), (RT_R0, rank0), (RT_R1, rank1)):
        slab = jnp.where(lane == ln, val, slab)
    rt_ref[...] = slab


assert EXPERTS_PER_GROUP == 8


def _block_layout(rt, cnt, n_tok):
    m = n_tok * TOP_K
    counts = cnt[0, N_GROUPS:N_GROUPS + N_EXPERTS].astype(jnp.int32)
    pcounts = (counts + MOE_TB - 1) // MOE_TB * MOE_TB
    pends = jnp.cumsum(pcounts)
    pstarts = pends - pcounts
    n_blocks = -(-(m + N_EXPERTS * (MOE_TB - 1)) // MOE_TB)
    e_idx = rt[:, RT_E0:RT_E1 + 1].astype(jnp.int32)
    rank = rt[:, RT_R0:RT_R1 + 1].astype(jnp.int32)
    onehot = e_idx[:, :, None] == jnp.arange(N_EXPERTS, dtype=jnp.int32)
    pos = jnp.sum(jnp.where(onehot, pstarts, 0), axis=-1) + rank
    blk_start = jnp.arange(n_blocks, dtype=jnp.int32) * MOE_TB
    blk_e = jnp.minimum(jnp.sum(blk_start[:, None] >= pends[None, :], axis=-1), N_EXPERTS - 1).astype(jnp.int32)
    nused = (pends[-1:] // MOE_TB).astype(jnp.int32)
    ids = jnp.arange(N_EXPERTS, dtype=jnp.int32)
    later_used = (ids[None, :] > ids[:, None]) & (counts[None, :] > 0)
    next_used = jnp.min(jnp.where(later_used, ids[None, :], N_EXPERTS), axis=-1)
    next_used = jnp.where(next_used == N_EXPERTS, -1, next_used).astype(jnp.int32)
    nxt_e = jnp.sum(jnp.where(blk_e[:, None] == ids[None, :], next_used[None, :], 0), axis=-1).astype(jnp.int32)
    return pos, blk_e, nxt_e, nused, n_blocks


def _step_indices(pos, tm):
    nsteps = pos.shape[0] // tm
    return pos.reshape(nsteps, tm, TOP_K).transpose(0, 2, 1).reshape(nsteps, 1, TOP_K * tm)


def _tile_rows(i):
    return pl.ds(pl.multiple_of(i * TOK_TILE, TOK_TILE), TOK_TILE)


def _dispatch_kernel(pos_ref, xp_ref, init_hbm, xs_hbm, xbuf, sem, *, tm):
    del init_hbm
    s = pl.program_id(0)
    nsteps = pl.num_programs(0)
    slot = s % 2

    def copy(j, dst, sl):
        return pltpu.make_async_copy(xbuf.at[sl, _tile_rows(j)], xs_hbm.at[_tile_rows(dst)], sem.at[sl])

    def drain(sl):
        def body(j, c):
            copy(0, 0, sl).wait()
            return c
        lax.fori_loop(0, TOP_K * tm, body, 0, unroll=8)

    @pl.when(s >= 2)
    def _():
        drain(slot)

    xbuf[slot] = xp_ref[...]

    for j in range(tm):
        for k in range(TOP_K):
            copy(j, pos_ref[0, 0, k * tm + j], slot).start(priority=k)

    @pl.when(s == nsteps - 1)
    def _():
        drain(1 - slot)
        drain(slot)


def _moe_dispatch(xp, pos, n_pad, tm=512):
    n = pos.shape[0]
    assert n // tm >= 2
    pos3 = _step_indices(pos, tm)
    init = jnp.zeros((n_pad * TOK_TILE, LANES_V7X), U32)
    return pl.pallas_call(
        functools.partial(_dispatch_kernel, tm=tm),
        grid=(n // tm,),
        in_specs=[pl.BlockSpec((1, 1, TOP_K * tm), lambda i: (i, 0, 0), memory_space=pltpu.SMEM),
                  pl.BlockSpec((tm * TOK_TILE, LANES_V7X), lambda i: (i, 0)), pl.BlockSpec(memory_space=pl.ANY)],
        out_specs=pl.BlockSpec(memory_space=pl.ANY),
        out_shape=jax.ShapeDtypeStruct(init.shape, U32),
        scratch_shapes=[pltpu.VMEM((2, tm * TOK_TILE, LANES_V7X), U32), pltpu.SemaphoreType.DMA((2,))],
        input_output_aliases={2: 0},
        compiler_params=_cparams(1),
        name="moe_dispatch",
    )(pos3, xp, init)


def _moe_kernel(blk_e_ref, nxt_e_ref, nused_ref, xs_ref, w1_hbm, w3_hbm, w2_hbm, ys_ref,
                wf1, wf3, wf2, w1b, w3b, w2b, slot_ref, sem, *, layer):
    s = pl.program_id(0)
    nused = nused_ref[0]

    def fetch(e, sl):
        return [pltpu.make_async_copy(w_hbm.at[layer, e], wf.at[sl], sem.at[sl, k])
                for k, (w_hbm, wf) in enumerate(((w1_hbm, wf1), (w3_hbm, wf3), (w2_hbm, wf2)))]

    @pl.when(s >= nused)
    def _():
        ys_ref[...] = jnp.zeros_like(ys_ref)

    @pl.when(s < nused)
    def _():
        e = blk_e_ref[s]
        prev = blk_e_ref[jnp.maximum(s - 1, 0)]

        @pl.when(s == 0)
        def _():
            slot_ref[0] = 0
            for cp in fetch(e, 0):
                cp.start()

        @pl.when((s > 0) & (e != prev))
        def _():
            slot_ref[0] = 1 - slot_ref[0]

        @pl.when((s == 0) | (e != prev))
        def _():
            sl = slot_ref[0]
            for cp in fetch(e, sl):
                cp.wait()
            w1b[...] = wf1[sl].astype(BF16)
            w3b[...] = wf3[sl].astype(BF16)
            w2b[...] = wf2[sl].astype(BF16)
            nxt = nxt_e_ref[s]

            @pl.when(nxt >= 0)
            def _():
                for cp in fetch(nxt, 1 - sl):
                    cp.start()

        xb = _unpack_rows(_load_token_tiles(xs_ref, 0, MOE_TB)).astype(BF16)
        h1 = jnp.dot(xb, w1b[...], preferred_element_type=F32)
        h3 = jnp.dot(xb, w3b[...], preferred_element_type=F32)
        h = (h1 * jax.nn.sigmoid(h1) * h3).astype(BF16)
        y = jnp.dot(h, w2b[...], preferred_element_type=F32)
        _store_token_tiles(ys_ref, 0, _pack_rows(y))


def _moe_experts(xs, blk_e, nxt_e, nused, w1, w3, w2, layer):
    n_blocks = blk_e.shape[0]
    tiles = pl.BlockSpec((MOE_TB * TOK_TILE, LANES_V7X), lambda s, be, ne, nu: (s, 0))
    hbm = pl.BlockSpec(memory_space=pl.ANY)
    up, down = (D_MODEL, D_EXPERT), (D_EXPERT, D_MODEL)
    grid_spec = pltpu.PrefetchScalarGridSpec(
        num_scalar_prefetch=3,
        grid=(n_blocks,),
        in_specs=[tiles, hbm, hbm, hbm],
        out_specs=tiles,
        scratch_shapes=[pltpu.VMEM((2,) + up, F32), pltpu.VMEM((2,) + up, F32), pltpu.VMEM((2,) + down, F32),
                        pltpu.VMEM(up, BF16), pltpu.VMEM(up, BF16), pltpu.VMEM(down, BF16),
                        pltpu.SMEM((1,), jnp.int32), pltpu.SemaphoreType.DMA((2, 3))],
    )
    return pl.pallas_call(
        functools.partial(_moe_kernel, layer=layer),
        grid_spec=grid_spec,
        out_shape=jax.ShapeDtypeStruct(xs.shape, U32),
        compiler_params=_cparams(1, VMEM_LIMIT_V7X),
        name="moe_experts",
    )(blk_e, nxt_e, nused, xs, w1, w3, w2)


def _combine_kernel(pos_ref, nxt_ref, ys_hbm, x_ref, rt_ref, g_ref, b_ref, o_ref, ybuf, sem, *, tm):
    s = pl.program_id(0)
    nsteps = pl.num_programs(0)
    slot = s % 2

    def copy(src, j, sl):
        return pltpu.make_async_copy(ys_hbm.at[_tile_rows(src)], ybuf.at[sl, _tile_rows(j)], sem.at[sl])

    def start_gather(idx_ref, sl):
        def body(j, c):
            copy(idx_ref[0, 0, j], j, sl).start()
            return c
        lax.fori_loop(0, TOP_K * tm, body, 0, unroll=8)

    @pl.when(s == 0)
    def _():
        start_gather(pos_ref, 0)

    @pl.when(s + 1 < nsteps)
    def _():
        for j in range(TOP_K * tm):
            copy(nxt_ref[0, 0, j], j, 1 - slot).start()

    def wait_body(j, c):
        copy(0, j, slot).wait()
        return c
    lax.fori_loop(0, TOP_K * tm, wait_body, 0, unroll=8)

    rt = rt_ref[...]
    ffn = _unpack_rows(_load_token_tiles(ybuf, 0, tm, lead=slot)) * rt[:, RT_W0:RT_W0 + 1]
    ffn = ffn + _unpack_rows(_load_token_tiles(ybuf, tm * TOK_TILE, tm, lead=slot)) * rt[:, RT_W1:RT_W1 + 1]
    o_ref[...] = _layer_norm_rows(DN_ALPHA * x_ref[...] + ffn, g_ref[...], b_ref[...])


def _moe_combine(ys, pos, rt, x, g, b, tm=256):
    n = x.shape[0]
    nsteps = n // tm
    pos3 = _step_indices(pos, tm)
    smem_blk = lambda f: pl.BlockSpec((1, 1, TOP_K * tm), f, memory_space=pltpu.SMEM)
    row = lambda width: pl.BlockSpec((tm, width), lambda i: (i, 0))
    const = lambda shape: pl.BlockSpec(shape, lambda i: (0, 0))
    return pl.pallas_call(
        functools.partial(_combine_kernel, tm=tm),
        grid=(nsteps,),
        in_specs=[smem_blk(lambda i: (i, 0, 0)), smem_blk(lambda i: (jnp.minimum(i + 1, nsteps - 1), 0, 0)),
                  pl.BlockSpec(memory_space=pl.ANY), row(D_MODEL), row(ROUTE_W),
                  const((1, D_MODEL)), const((1, D_MODEL))],
        out_specs=row(D_MODEL),
        out_shape=jax.ShapeDtypeStruct((n, D_MODEL), F32),
        scratch_shapes=[pltpu.VMEM((2, TOP_K * tm * TOK_TILE, LANES_V7X), U32), pltpu.SemaphoreType.DMA((2,))],
        compiler_params=_cparams(1, 40 * MIB),
        name="moe_combine",
    )(pos3, pos3, ys, x, rt, g.reshape(1, -1), b.reshape(1, -1))


def _moe_layer(x1, xp, rt, cnt, w1, w3, w2, layer, g, b):
    n_tok = x1.shape[0]
    pos, blk_e, nxt_e, nused, n_blocks = _block_layout(rt, cnt, n_tok)
    xs = _moe_dispatch(xp, pos, n_blocks * MOE_TB)
    ys = _moe_experts(xs, blk_e, nxt_e, nused, w1, w3, w2, layer)
    return _moe_combine(ys, pos, rt, x1, g, b)


def _gmlp_kernel(u_ref, v_ref, g_ref, b_ref, w_ref, bs_ref, o_ref, *, chunks):
    for ci in range(chunks):
        rows = slice(ci * C_CHUNK, (ci + 1) * C_CHUNK)
        u = jax.nn.gelu(u_ref[rows, :])
        v = _layer_norm_rows(jax.nn.gelu(v_ref[rows, :]), g_ref[...], b_ref[...]).astype(BF16)
        for gi in range(C_GROUPS):
            cols = slice(gi * C_GROUP_DIM, (gi + 1) * C_GROUP_DIM)
            mixed = jnp.dot(w_ref[gi], v[:, cols], preferred_element_type=F32) + bs_ref[:, cols]
            o_ref[rows, cols] = (u[:, cols] * mixed).astype(o_ref.dtype)


def _gmlp(proj, ln_g, ln_b, w_s, b_s, chunks=4):
    n = proj.shape[0]
    tm = chunks * C_CHUNK
    w = (w_s * jnp.tril(jnp.ones((C_CHUNK, C_CHUNK), w_s.dtype))).astype(BF16)
    bs_full = jnp.repeat(b_s.T, C_GROUP_DIM, axis=1)
    const2 = lambda shape: pl.BlockSpec(shape, lambda i: (0, 0))
    return pl.pallas_call(
        functools.partial(_gmlp_kernel, chunks=chunks),
        grid=(n // tm,),
        in_specs=[pl.BlockSpec((tm, C_W), lambda i: (i, 0)), pl.BlockSpec((tm, C_W), lambda i: (i, 1)),
                  const2((1, C_W)), const2((1, C_W)),
                  pl.BlockSpec((C_GROUPS, C_CHUNK, C_CHUNK), lambda i: (0, 0, 0)), const2((C_CHUNK, C_W))],
        out_specs=pl.BlockSpec((tm, C_W), lambda i: (i, 0)),
        out_shape=jax.ShapeDtypeStruct((n, C_W), BF16),
        compiler_params=_cparams(1, 32 * MIB),
        name="gmlp_gating",
    )(proj, proj, ln_g.reshape(1, -1), ln_b.reshape(1, -1), w, bs_full)


CONV_HIST = 32


def _conv_kernel(a_ref, gt_ref, ap_ref, gp_ref, w_ref, cb_ref, g_ref, b_ref, o_ref, hbuf, hshift, *, ts):
    i = pl.program_id(1)
    hist = ap_ref[...] * jax.nn.sigmoid(gp_ref[...])
    hbuf[0:CONV_HIST, :] = jnp.where(i > 0, hist, jnp.zeros_like(hist))
    hbuf[CONV_HIST:CONV_HIST + ts, :] = a_ref[...] * jax.nn.sigmoid(gt_ref[...])
    off = CONV_HIST - (D_CONV - 1)
    acc = jnp.zeros((ts, D_CHANNELS), F32) + cb_ref[...]
    for r in range(SUBLANES_V7X):
        taps = [j for j in range(D_CONV) if (off + j) % SUBLANES_V7X == r]
        if not taps:
            continue
        src = hbuf
        if r:
            span = max(taps) + off - r + ts
            hshift[0:span, :] = hbuf[r:r + span, :]
            src = hshift
        for j in taps:
            base = off + j - r
            acc = acc + w_ref[j:j + 1, :] * src[base:base + ts, :]
    y = _layer_norm_rows(acc, g_ref[...], b_ref[...])
    o_ref[...] = (y * jax.nn.sigmoid(y)).astype(o_ref.dtype)


def _conformer_conv(proj, conv_w, conv_b, ln_g, ln_b, bsz, seq, ts=512):
    nt = seq // ts
    acol = 2 * C_W // D_CHANNELS
    gcol = acol + 1
    hb = ts // CONV_HIST
    cur = lambda col: pl.BlockSpec((ts, D_CHANNELS), lambda b, i: (b * nt + i, col))
    prev = lambda col: pl.BlockSpec((CONV_HIST, D_CHANNELS),
                                    lambda b, i: (jnp.maximum((b * nt + i) * hb - 1, 0), col))
    const2 = lambda shape: pl.BlockSpec(shape, lambda b, i: (0, 0))
    return pl.pallas_call(
        functools.partial(_conv_kernel, ts=ts),
        grid=(bsz, nt),
        in_specs=[cur(acol), cur(gcol), prev(acol), prev(gcol),
                  const2((D_CONV, D_CHANNELS)), const2((1, D_CHANNELS)), const2((1, D_CHANNELS)),
                  const2((1, D_CHANNELS))],
        out_specs=pl.BlockSpec((ts, D_CHANNELS), lambda b, i: (b * nt + i, 0)),
        out_shape=jax.ShapeDtypeStruct((bsz * seq, D_CHANNELS), BF16),
        scratch_shapes=[pltpu.VMEM((CONV_HIST + ts, D_CHANNELS), F32), pltpu.VMEM((CONV_HIST + ts, D_CHANNELS), F32)],
        compiler_params=_cparams(2, 32 * MIB),
        name="conformer_conv",
    )(proj, proj, proj, proj, conv_w, conv_b.reshape(1, -1), ln_g.reshape(1, -1), ln_b.reshape(1, -1))


def kernel(x, w_in_ab, attn_sinks, rel_bias, hgrn_lb_logits, hgrn_norm_g, w_out_ab, w_in_cd, gmlp_ln_g, gmlp_ln_b, gmlp_w_s, gmlp_b_s, conv_w, conv_b, conv_ln_g, conv_ln_b, w_out_cd, ln_mix_g, ln_mix_b, ln_ffn_g, ln_ffn_b, moe_w_group, moe_b_group, moe_w_router, moe_b_router, moe_w1, moe_w3, moe_w2):
    bsz, seq = x.shape[0], x.shape[1]
    n_tok = bsz * seq
    xf = x.reshape(n_tok, D_MODEL)
    lb_table = jnp.cumsum(jax.nn.softmax(hgrn_lb_logits.astype(F32), axis=0), axis=0)
    bias = _bias_table(rel_bias.astype(F32), _t5_bucket_table())

    for layer in range(DEPTH):
        j = layer // 2
        if layer % 2 == 0:
            proj = _matmul(xf, w_in_ab[j].astype(BF16), 512, EVEN_IN // 3, F32)
            ya = _attention(proj, bias, attn_sinks[j].astype(F32), bsz, seq)
            yb = _hgrn(proj, lb_table[layer], hgrn_norm_g[j].astype(F32), bsz, seq)
            w_out = w_out_ab[j]
        else:
            proj = _matmul(xf, w_in_cd[j].astype(BF16), 512, ODD_IN // 2, F32)
            ya = _gmlp(proj, gmlp_ln_g[j], gmlp_ln_b[j], gmlp_w_s[j], gmlp_b_s[j])
            yb = _conformer_conv(proj, conv_w[j], conv_b[j], conv_ln_g[j], conv_ln_b[j], bsz, seq)
            w_out = w_out_cd[j]
        wr, rbias = _router_weights(moe_w_group[layer], moe_b_group[layer],
                                    moe_w_router[layer], moe_b_router[layer])
        x1, xp, rt, cnt = _outproj_ln_route(ya, yb, w_out.astype(BF16), xf, ln_mix_g[layer], ln_mix_b[layer],
                                            wr, rbias)
        xf = _moe_layer(x1, xp, rt, cnt, moe_w1, moe_w3, moe_w2, layer, ln_ffn_g[layer], ln_ffn_b[layer])
    return xf.reshape(bsz, seq, D_MODEL)
```

```python
import functools
import math

import jax
import jax.numpy as jnp
from jax import lax
from jax.experimental import pallas as pl
from jax.experimental.pallas import tpu as pltpu

D_MODEL = 2048
DEPTH = 2
A_HEADS = 16
A_KV_HEADS = 2
A_HEAD_DIM = 64
A_WINDOW = 128
A_BLOCK = 128
REL_BUCKETS = 32
REL_MAX_DIST = 128
B_HEADS = 8
B_DK = 128
B_DV = 128
C_GROUPS = 8
C_GROUP_DIM = 128
C_CHUNK = 128
D_CHANNELS = 1024
D_CONV = 31
A_QW = A_HEADS * A_HEAD_DIM
A_KVW = A_KV_HEADS * A_HEAD_DIM
B_KW = B_HEADS * B_DK
B_VW = B_HEADS * B_DV
C_W = C_GROUPS * C_GROUP_DIM
EVEN_IN = A_QW + 2 * A_KVW + 2 * B_KW + 2 * B_VW
ODD_IN = 2 * C_W + 2 * D_CHANNELS
N_GROUPS = 4
EXPERTS_PER_GROUP = 8
N_EXPERTS = N_GROUPS * EXPERTS_PER_GROUP
TOP_K = 2
D_EXPERT = 512
DN_ALPHA = (2 * DEPTH) ** 0.25
LN_EPS = 1e-5
RMS_EPS = 1e-6

LANES_V7X = 128
SUBLANES_V7X = 8
MIB = 1024 * 1024
VMEM_LIMIT_V7X = 56 * MIB

HGRN_CHUNK = 128
HGRN_SUB = 8
HGRN_GROUP = 32
HGRN_ROWS = 512
MOE_TB = 256
ROUTE_W = LANES_V7X
HALF = D_MODEL // 2
TOK_TILE = HALF // LANES_V7X
assert TOK_TILE == SUBLANES_V7X

BF16 = jnp.bfloat16
F32 = jnp.float32
U32 = jnp.uint32
NEG_INF = float("-inf")
LOG2E = math.log2(math.e)
RT_W0, RT_W1, RT_E0, RT_E1, RT_R0, RT_R1 = range(6)


def _cparams(n_axes, vmem_bytes=None):
    return pltpu.CompilerParams(dimension_semantics=("arbitrary",) * n_axes, vmem_limit_bytes=vmem_bytes)


def _layer_norm_rows(z, g, b):
    mu = jnp.mean(z, axis=-1, keepdims=True)
    zc = z - mu
    var = jnp.mean(zc * zc, axis=-1, keepdims=True)
    return zc * lax.rsqrt(var + LN_EPS) * g + b


def _dot_nt(a, b):
    return lax.dot_general(a, b, (((1,), (1,)), ((), ())), preferred_element_type=F32)


def _dot_tn(a, b):
    return lax.dot_general(a, b, (((0,), (0,)), ((), ())), preferred_element_type=F32)


def _pack_rows(y):
    lo = lax.bitcast_convert_type(y[:, :HALF].astype(BF16).astype(F32), U32) >> 16
    hi = lax.bitcast_convert_type(y[:, HALF:].astype(BF16).astype(F32), U32) & jnp.uint32(0xFFFF0000)
    return lo | hi


def _unpack_rows(p):
    lo = lax.bitcast_convert_type(p << 16, F32)
    hi = lax.bitcast_convert_type(p & jnp.uint32(0xFFFF0000), F32)
    return jnp.concatenate([lo, hi], axis=1)


def _store_token_tiles(ref, base, packed):
    rows = packed.shape[0]
    for c in range(TOK_TILE):
        ref[pl.ds(base + c, rows, stride=TOK_TILE), :] = packed[:, c * LANES_V7X:(c + 1) * LANES_V7X]


def _load_token_tiles(ref, base, rows, lead=None):
    parts = []
    for c in range(TOK_TILE):
        idx = (pl.ds(base + c, rows, stride=TOK_TILE), slice(None))
        parts.append(ref[idx] if lead is None else ref[(lead,) + idx])
    return jnp.concatenate(parts, axis=1)


def _mm_kernel(a_ref, w_ref, o_ref):
    a = a_ref[...].astype(BF16)
    o_ref[...] = jnp.dot(a, w_ref[...], preferred_element_type=F32).astype(o_ref.dtype)


def _matmul(a, w, tm, tn, out_dtype):
    m, k = a.shape
    n = w.shape[1]
    return pl.pallas_call(
        _mm_kernel,
        grid=(n // tn, m // tm),
        in_specs=[pl.BlockSpec((tm, k), lambda j, i: (i, 0)), pl.BlockSpec((k, tn), lambda j, i: (0, j))],
        out_specs=pl.BlockSpec((tm, tn), lambda j, i: (i, j)),
        out_shape=jax.ShapeDtypeStruct((m, n), out_dtype),
        compiler_params=_cparams(2, VMEM_LIMIT_V7X),
        name="proj_matmul",
    )(a, w)


def _bias_table_kernel(rb_ref, bucket_ref, o_ref):
    bucket = bucket_ref[...]
    for h in range(A_HEADS):
        acc = jnp.zeros(bucket.shape, F32)
        for bk in range(REL_BUCKETS):
            acc = jnp.where(bucket == bk, rb_ref[bk, h], acc)
        o_ref[h] = acc


def _bias_table(rel_bias, bucket):
    return pl.pallas_call(
        _bias_table_kernel,
        in_specs=[pl.BlockSpec(memory_space=pltpu.SMEM), pl.BlockSpec(memory_space=pltpu.VMEM)],
        out_specs=pl.BlockSpec(memory_space=pltpu.VMEM),
        out_shape=jax.ShapeDtypeStruct((A_HEADS,) + bucket.shape, F32),
        name="rel_bias_table",
    )(rel_bias, bucket)


def _t5_bucket_table():
    t_loc = jnp.arange(A_BLOCK, dtype=jnp.int32)[:, None]
    j_loc = jnp.arange(A_BLOCK, dtype=jnp.int32)[None, :]
    dist = jnp.where(j_loc <= t_loc, t_loc - j_loc, t_loc + A_BLOCK - j_loc)
    max_exact = REL_BUCKETS // 2
    d = jnp.maximum(dist, 1).astype(F32)
    large = max_exact + (jnp.log(d / max_exact) / math.log(REL_MAX_DIST / max_exact)
                         * (REL_BUCKETS - max_exact)).astype(jnp.int32)
    large = jnp.minimum(large, REL_BUCKETS - 1)
    return jnp.where(dist < max_exact, dist, large)


assert A_WINDOW == A_BLOCK


def _attn_kernel(q_ref, kc_ref, vc_ref, kp_ref, vp_ref, bias_ref, sink_ref, o_ref):
    n = pl.program_id(1)
    blk = A_BLOCK
    k2 = jnp.concatenate([kp_ref[...], kc_ref[...]], axis=0)
    v2 = jnp.concatenate([vp_ref[...], vc_ref[...]], axis=0)
    k2r = pltpu.roll(k2, A_HEAD_DIM, 1)
    v2r = pltpu.roll(v2, A_HEAD_DIM, 1)
    lo = lax.broadcasted_iota(jnp.int32, k2.shape, 1) < A_HEAD_DIM
    zero = jnp.zeros_like(k2)

    def placed(x, xr, g, par):
        src = x if g == par else xr
        return (jnp.where(lo, src, zero) if par == 0 else jnp.where(lo, zero, src)).astype(BF16)

    kk = [[placed(k2, k2r, g, par) for par in range(2)] for g in range(A_KV_HEADS)]
    vv = [[placed(v2, v2r, g, par) for par in range(2)] for g in range(A_KV_HEADS)]

    own = (lax.broadcasted_iota(jnp.int32, (blk, blk), 1) <= lax.broadcasted_iota(jnp.int32, (blk, blk), 0))
    no_prev = jnp.where(n > 0, 0.0, NEG_INF)
    zero_p = jnp.zeros((blk, blk), F32)

    heads_per_kv = A_HEADS // A_KV_HEADS
    for p in range(A_HEADS // 2):
        g = (2 * p) // heads_per_kv
        qp = (q_ref[:, p * 128:(p + 1) * 128] * (A_HEAD_DIM ** -0.5)).astype(BF16)
        acc = jnp.zeros((blk, 128), F32)
        for par in range(2):
            h = 2 * p + par
            sink = sink_ref[h]
            s2 = _dot_nt(qp, kk[g][par])
            logits = jnp.where(own, s2[:, blk:], s2[:, :blk] + no_prev) + bias_ref[h]
            m = jnp.maximum(jnp.max(logits, axis=-1, keepdims=True), sink)
            e = jnp.exp(logits - m)
            den = jnp.sum(e, axis=-1, keepdims=True) + jnp.exp(sink - m)
            e2 = jnp.concatenate([jnp.where(own, zero_p, e), jnp.where(own, e, zero_p)], axis=1).astype(BF16)
            acc = acc + jnp.dot(e2, vv[g][par], preferred_element_type=F32) * (1.0 / den)
        o_ref[:, p * 128:(p + 1) * 128] = acc.astype(o_ref.dtype)


def _attention(proj, bias, sinks, bsz, seq):
    nb = seq // A_BLOCK
    kcol = A_QW // 128
    vcol = kcol + 1
    row = lambda b, n: b * nb + n
    prow = lambda b, n: b * nb + jnp.maximum(n - 1, 0)
    return pl.pallas_call(
        _attn_kernel,
        grid=(bsz, nb),
        in_specs=[
            pl.BlockSpec((A_BLOCK, A_QW), lambda b, n: (row(b, n), 0)),
            pl.BlockSpec((A_BLOCK, 128), lambda b, n: (row(b, n), kcol)),
            pl.BlockSpec((A_BLOCK, 128), lambda b, n: (row(b, n), vcol)),
            pl.BlockSpec((A_BLOCK, 128), lambda b, n: (prow(b, n), kcol)),
            pl.BlockSpec((A_BLOCK, 128), lambda b, n: (prow(b, n), vcol)),
            pl.BlockSpec((A_HEADS, A_BLOCK, A_BLOCK), lambda b, n: (0, 0, 0)),
            pl.BlockSpec(memory_space=pltpu.SMEM),
        ],
        out_specs=pl.BlockSpec((A_BLOCK, A_QW), lambda b, n: (row(b, n), 0)),
        out_shape=jax.ShapeDtypeStruct((bsz * seq, A_QW), BF16),
        compiler_params=_cparams(2, 32 * MIB),
        name="swa_attention",
    )(proj, proj, proj, proj, proj, bias, sinks)


HGRN_HEADS_PER_STEP = 2


def _hgrn_kernel(q_ref, f_ref, i_ref, g_ref, lb_ref, ng_ref, o_ref, st_ref):
    c, sub, grp = HGRN_CHUNK, HGRN_SUB, HGRN_GROUP
    ngrp = c // grp

    @pl.when(pl.program_id(2) == 0)
    def _():
        st_ref[...] = jnp.zeros_like(st_ref)

    rid = lax.broadcasted_iota(jnp.int32, (c, c), 0)
    cid = lax.broadcasted_iota(jnp.int32, (c, c), 1)
    grp_start = (rid // grp) * grp
    sub_start = (rid // sub) * sub
    m_cum = jnp.concatenate([(cid < grp_start).astype(F32),
                             ((cid >= grp_start) & (cid < sub_start)).astype(F32),
                             ((cid >= sub_start) & (cid <= rid)).astype(F32)], axis=0).astype(BF16)
    band = jnp.where(cid >= sub_start, rid - cid, -1)
    same_grp = (cid // grp) == (rid // grp)
    heads = range(HGRN_HEADS_PER_STEP)
    zeros_bf = lambda rows: jnp.zeros((rows, B_DK), BF16)

    def scores(hh, r0):
        cols = slice(hh * 128, (hh + 1) * 128)
        lb = lb_ref[0, :, cols]
        q = q_ref[pl.ds(r0, c), cols]
        fl = f_ref[pl.ds(r0, c), cols]
        qf = q * jax.nn.sigmoid(q)
        f = lb + (1.0 - lb) * jax.nn.sigmoid(fl)
        kin = 1.0 - f
        logf = jnp.log(f)
        l1 = logf.astype(BF16)
        res = logf - l1.astype(F32)
        l2 = res.astype(BF16)
        l3 = (res - l2.astype(F32)).astype(BF16)
        cum = jnp.dot(m_cum, jnp.concatenate([l1, l2, l3], axis=1), preferred_element_type=F32)
        cum = cum[:, :B_DK] + cum[:, B_DK:2 * B_DK] + cum[:, 2 * B_DK:]
        rg, rs, bq = cum[:c], cum[c:2 * c], cum[2 * c:]
        lk = jnp.log(kin)
        wq = lk - bq
        wg = wq - rs
        wb = wg - rg
        qt = qf * jnp.exp(bq)
        qg = qt * jnp.exp(rs)
        o = _dot_nt((qg * jnp.exp(rg)).astype(BF16), st_ref[hh].astype(BF16))

        kts = []
        for i in range(1, ngrp):
            n = i * grp
            kts += [jnp.exp(rg[n:n + 1, :] + wb[:n, :]).astype(BF16), zeros_bf(c - n)]
        g_far = _dot_nt(qg.astype(BF16), jnp.concatenate(kts, axis=0))
        s_far = jnp.concatenate([jnp.zeros((grp, c), F32)] +
                                [g_far[i * grp:(i + 1) * grp, (i - 1) * c:i * c] for i in range(1, ngrp)], axis=0)

        kts = []
        for j in range(1, grp // sub):
            for gi in range(ngrp):
                a, n = gi * grp, j * sub
                kts += [jnp.exp(rs[a + n:a + n + 1, :] + wg[a:a + n, :]).astype(BF16), zeros_bf(grp - n)]
        g_near = _dot_nt(qt.astype(BF16), jnp.concatenate(kts, axis=0))
        pieces = []
        for i in range(c // sub):
            j = i % (grp // sub)
            pieces.append(jnp.zeros((sub, c), F32) if j == 0 else g_near[i * sub:(i + 1) * sub, (j - 1) * c:j * c])
        s = jnp.where(same_grp, jnp.concatenate(pieces, axis=0), s_far)

        bq2, wq2 = bq * LOG2E, wq * LOG2E
        kds = [kin.astype(BF16)] + [jnp.exp2(pltpu.roll(bq2, c - d, 0) + wq2).astype(BF16) for d in range(1, sub)]
        g_diag = _dot_nt(qf.astype(BF16), jnp.concatenate(kds, axis=0))
        last = slice(c - 1, c)
        return dict(o=o, s=s, g_diag=g_diag, wb=wb, b_last=rg[last, :] + rs[last, :] + bq[last, :])

    def finish(hh, r0, h):
        cols = slice(hh * 128, (hh + 1) * 128)
        vb = i_ref[pl.ds(r0, c), cols].astype(BF16)
        gt = g_ref[pl.ds(r0, c), cols]
        o = h["o"] + jnp.dot(h["s"].astype(BF16), vb, preferred_element_type=F32)
        st_ref[hh] = (st_ref[hh] * jnp.exp(h["b_last"])
                      + _dot_tn(vb, jnp.exp(h["b_last"] + h["wb"]).astype(BF16)))
        o = o * lax.rsqrt(jnp.mean(o * o, axis=-1, keepdims=True) + RMS_EPS)
        o_ref[pl.ds(r0, c), cols] = (o * ng_ref[0, :, cols] * (gt * jax.nn.sigmoid(gt))).astype(o_ref.dtype)

    def chunk(ci, carry):
        r0 = pl.multiple_of(ci * c, c)
        hs = [scores(hh, r0) for hh in heads]
        for d in range(sub):
            on_diag = band == d
            for h in hs:
                h["s"] = jnp.where(on_diag, h["g_diag"][:, d * c:(d + 1) * c], h["s"])
        for hh in heads:
            finish(hh, r0, hs[hh])
        return carry

    lax.fori_loop(0, HGRN_ROWS // c, chunk, 0, unroll=True)


def _hgrn(proj, lb, norm_g, bsz, seq):
    nr = seq // HGRN_ROWS
    hp = HGRN_HEADS_PER_STEP
    width = hp * 128
    c0 = (A_QW + 2 * A_KVW) // width
    nhp = B_HEADS // hp
    spec = lambda off: pl.BlockSpec((HGRN_ROWS, width), lambda b, h, r: (b * nr + r, off + h))
    vec = pl.BlockSpec((1, 1, width), lambda b, h, r: (h, 0, 0))
    return pl.pallas_call(
        _hgrn_kernel,
        grid=(bsz, nhp, nr),
        in_specs=[spec(c0), spec(c0 + nhp), spec(c0 + 2 * nhp), spec(c0 + 3 * nhp), vec, vec],
        out_specs=pl.BlockSpec((HGRN_ROWS, width), lambda b, h, r: (b * nr + r, h)),
        out_shape=jax.ShapeDtypeStruct((bsz * seq, B_VW), BF16),
        scratch_shapes=[pltpu.VMEM((hp, B_DV, B_DK), F32)],
        compiler_params=_cparams(3, 32 * MIB),
        name="hgrn2",
    )(proj, proj, proj, proj, lb.reshape(nhp, 1, width), norm_g.reshape(nhp, 1, width))


OUTPROJ_LAG = 2


def _outproj_kernel(ya_ref, yb_ref, wa_ref, wb_ref, x_ref, g_ref, b_ref, wr_ref, rb_ref,
                    xo_ref, xp_ref, rt_ref, cnt_ref, run_ref, lg_ref, mix_ref):
    step = pl.program_id(0)

    @pl.when(step == 0)
    def _():
        run_ref[...] = jnp.zeros_like(run_ref)
        lg_ref[...] = jnp.zeros_like(lg_ref)
        mix_ref[...] = jnp.zeros_like(mix_ref)

    lg_prev = lg_ref[...]
    mix_prev = mix_ref[...]
    mix = jnp.dot(ya_ref[...], wa_ref[...], preferred_element_type=F32)
    mix_ref[...] = mix + jnp.dot(yb_ref[...], wb_ref[...], preferred_element_type=F32)

    y = _layer_norm_rows(DN_ALPHA * x_ref[...] + mix_prev, g_ref[...], b_ref[...])
    xo_ref[...] = y
    _store_token_tiles(xp_ref, 0, _pack_rows(y))
    tm = y.shape[0]
    y_hi = y.astype(BF16)
    y_lo = (y - y_hi.astype(F32)).astype(BF16)
    prod = jnp.dot(jnp.concatenate([y_hi, y_lo], axis=0), wr_ref[...], preferred_element_type=F32)
    lg = (prod[:tm, :ROUTE_W] + prod[tm:, :ROUTE_W]) + (prod[:tm, ROUTE_W:] + prod[tm:, ROUTE_W:])
    lg_ref[...] = lg + rb_ref[...]
    _route_rows(lg_prev, (step >= OUTPROJ_LAG).astype(F32), rt_ref, cnt_ref, run_ref)


def _outproj_ln_route(ya, yb, w_out, x, g, b, wr, rbias, tm=512):
    n = x.shape[0]
    ka = ya.shape[1]
    nsteps = n // tm
    lagged = lambda lag: (lambda i: (jnp.clip(i - lag, 0, nsteps - 1), 0))
    row = lambda width, lag: pl.BlockSpec((tm, width), lagged(lag))
    once = pl.Buffered(1)
    const = lambda shape: pl.BlockSpec(shape, lambda i: (0, 0), pipeline_mode=once)
    return pl.pallas_call(
        _outproj_kernel,
        grid=(nsteps + OUTPROJ_LAG,),
        in_specs=[row(ka, 0), row(ka, 0),
                  pl.BlockSpec((ka, D_MODEL), lambda i: (0, 0), pipeline_mode=once),
                  pl.BlockSpec((ka, D_MODEL), lambda i: (1, 0), pipeline_mode=once),
                  row(D_MODEL, 1), const((1, D_MODEL)), const((1, D_MODEL)),
                  const((D_MODEL, 2 * ROUTE_W)), const((1, ROUTE_W))],
        out_specs=[row(D_MODEL, 1), pl.BlockSpec((tm * TOK_TILE, LANES_V7X), lagged(1)),
                   row(ROUTE_W, OUTPROJ_LAG), pl.BlockSpec((1, ROUTE_W), lambda i: (0, 0))],
        out_shape=[jax.ShapeDtypeStruct((n, D_MODEL), F32),
                   jax.ShapeDtypeStruct((n * TOK_TILE, LANES_V7X), U32),
                   jax.ShapeDtypeStruct((n, ROUTE_W), F32), jax.ShapeDtypeStruct((1, ROUTE_W), F32)],
        scratch_shapes=[pltpu.VMEM((1, ROUTE_W), F32), pltpu.VMEM((tm, ROUTE_W), F32),
                        pltpu.VMEM((tm, D_MODEL), F32)],
        compiler_params=_cparams(1, VMEM_LIMIT_V7X),
        name="outproj_ln_route",
    )(ya, yb, w_out, w_out, x, g.reshape(1, -1), b.reshape(1, -1), wr, rbias)


def _router_weights(w_group, b_group, w_router, b_router):
    w = jnp.zeros((D_MODEL, ROUTE_W), F32)
    w = w.at[:, :N_GROUPS].set(w_group).at[:, N_GROUPS:N_GROUPS + N_EXPERTS].set(w_router)
    w_hi = w.astype(BF16)
    w_lo = (w - w_hi.astype(F32)).astype(BF16)
    rb = jnp.zeros((1, ROUTE_W), F32)
    rb = rb.at[0, :N_GROUPS].set(b_group).at[0, N_GROUPS:N_GROUPS + N_EXPERTS].set(b_router)
    return jnp.concatenate([w_hi, w_lo], axis=1), rb


def _route_rows(lg, live, rt_ref, cnt_ref, run_ref):
    tm = lg.shape[0]
    lane = lax.broadcasted_iota(jnp.int32, lg.shape, 1)
    sentinel = jnp.int32(ROUTE_W)
    rowmax = lambda mask: jnp.max(jnp.where(mask, lg, NEG_INF), axis=-1, keepdims=True)
    first = lambda mask: jnp.min(jnp.where(mask, lane, sentinel), axis=-1, keepdims=True)

    is_g = lane < N_GROUPS
    gmax = rowmax(is_g)
    g_idx = first(is_g & (lg == gmax))
    g_w = 1.0 / jnp.sum(jnp.where(is_g, jnp.exp(lg - gmax), 0.0), axis=-1, keepdims=True)

    e_lane = lane - N_GROUPS
    sel = (e_lane >= 0) & (e_lane < N_EXPERTS) & ((e_lane >> 3) == g_idx)
    m1 = rowmax(sel)
    i1 = first(sel & (lg == m1))
    sel2 = sel & (lane != i1)
    m2 = rowmax(sel2)
    i2 = first(sel2 & (lg == m2))
    ex = jnp.exp(m2 - m1)
    w0 = g_w / (1.0 + ex)
    w1 = g_w * ex / (1.0 + ex)

    oh0 = lane == i1
    oh1 = lane == i2
    both = (oh0 | oh1).astype(F32) * live
    rid = lax.broadcasted_iota(jnp.int32, (tm, tm), 0)
    cid = lax.broadcasted_iota(jnp.int32, (tm, tm), 1)
    before = jnp.dot((cid < rid).astype(BF16), both.astype(BF16), preferred_element_type=F32) + run_ref[...]
    rank0 = jnp.sum(jnp.where(oh0, before, 0.0), axis=-1, keepdims=True)
    rank1 = jnp.sum(jnp.where(oh1, before, 0.0), axis=-1, keepdims=True)
    run = run_ref[...] + jnp.sum(both, axis=0, keepdims=True)
    run_ref[...] = run
    cnt_ref[...] = run

    slab = jnp.zeros(lg.shape, F32)
    for ln, val in ((RT_W0, w0), (RT_W1, w1), (RT_E0, (i1 - N_GROUPS).astype(F32)),
                    (RT_E1, (i2 - N_GROUPS).astype(F32)), (RT_R0, rank0), (RT_R1, rank1)):
        slab = jnp.where(lane == ln, val, slab)
    rt_ref[...] = slab


assert EXPERTS_PER_GROUP == 8


def _block_layout(rt, cnt, n_tok):
    m = n_tok * TOP_K
    counts = cnt[0, N_GROUPS:N_GROUPS + N_EXPERTS].astype(jnp.int32)
    pcounts = (counts + MOE_TB - 1) // MOE_TB * MOE_TB
    pends = jnp.cumsum(pcounts)
    pstarts = pends - pcounts
    n_blocks = -(-(m + N_EXPERTS * (MOE_TB - 1)) // MOE_TB)
    e_idx = rt[:, RT_E0:RT_E1 + 1].astype(jnp.int32)
    rank = rt[:, RT_R0:RT_R1 + 1].astype(jnp.int32)
    onehot = e_idx[:, :, None] == jnp.arange(N_EXPERTS, dtype=jnp.int32)
    pos = jnp.sum(jnp.where(onehot, pstarts, 0), axis=-1) + rank
    blk_start = jnp.arange(n_blocks, dtype=jnp.int32) * MOE_TB
    blk_e = jnp.minimum(jnp.sum(blk_start[:, None] >= pends[None, :], axis=-1), N_EXPERTS - 1).astype(jnp.int32)
    nused = (pends[-1:] // MOE_TB).astype(jnp.int32)
    ids = jnp.arange(N_EXPERTS, dtype=jnp.int32)
    later_used = (ids[None, :] > ids[:, None]) & (counts[None, :] > 0)
    next_used = jnp.min(jnp.where(later_used, ids[None, :], N_EXPERTS), axis=-1)
    next_used = jnp.where(next_used == N_EXPERTS, -1, next_used).astype(jnp.int32)
    nxt_e = jnp.sum(jnp.where(blk_e[:, None] == ids[None, :], next_used[None, :], 0), axis=-1).astype(jnp.int32)
    ends_expert = jnp.any(((blk_start + MOE_TB)[:, None] == pends[None, :]) & (pcounts[None, :] > 0), axis=-1)
    zero_blk = (ends_expert | (blk_start >= pends[-1])).astype(jnp.int32)
    return pos, blk_e, nxt_e, nused, zero_blk


def _step_indices(pos, tm):
    nsteps = pos.shape[0] // tm
    return pos.reshape(nsteps, tm, TOP_K).transpose(0, 2, 1).reshape(nsteps, 1, TOP_K * tm)


def _tile_rows(i):
    return pl.ds(pl.multiple_of(i * TOK_TILE, TOK_TILE), TOK_TILE)


def _dispatch_kernel(zero_blk_ref, pos_ref, xp_ref, xs_hbm, xbuf, zbuf, sem, zsem, *, tm, n_blocks):
    s = pl.program_id(0)
    nsteps = pl.num_programs(0)
    slot = s % 2
    blk_rows = MOE_TB * TOK_TILE

    def copy(j, dst, sl):
        return pltpu.make_async_copy(xbuf.at[sl, _tile_rows(j)], xs_hbm.at[_tile_rows(dst)], sem.at[sl])

    def drain(sl):
        def body(j, c):
            copy(0, 0, sl).wait()
            return c
        lax.fori_loop(0, TOP_K * tm, body, 0, unroll=8)

    @pl.when(s == 0)
    def _():
        zbuf[...] = jnp.zeros_like(zbuf)

        def zero_copy(blk):
            rows = pl.ds(pl.multiple_of(blk * blk_rows, blk_rows), blk_rows)
            return pltpu.make_async_copy(zbuf, xs_hbm.at[rows], zsem)

        def each_flagged(fn):
            def body(blk, c):
                @pl.when(zero_blk_ref[blk] > 0)
                def _():
                    fn(zero_copy(blk))
                return c
            lax.fori_loop(0, n_blocks, body, 0)

        each_flagged(lambda cp: cp.start())
        each_flagged(lambda cp: cp.wait())

    @pl.when(s >= 2)
    def _():
        drain(slot)

    xbuf[slot] = xp_ref[...]

    for j in range(tm):
        for k in range(TOP_K):
            copy(j, pos_ref[0, 0, k * tm + j], slot).start(priority=k)

    @pl.when(s == nsteps - 1)
    def _():
        drain(1 - slot)
        drain(slot)


def _moe_dispatch(xp, pos, zero_blk, tm=512):
    n = pos.shape[0]
    n_blocks = zero_blk.shape[0]
    assert n // tm >= 2
    pos3 = _step_indices(pos, tm)
    grid_spec = pltpu.PrefetchScalarGridSpec(
        num_scalar_prefetch=1,
        grid=(n // tm,),
        in_specs=[pl.BlockSpec((1, 1, TOP_K * tm), lambda i, zb: (i, 0, 0), memory_space=pltpu.SMEM),
                  pl.BlockSpec((tm * TOK_TILE, LANES_V7X), lambda i, zb: (i, 0))],
        out_specs=pl.BlockSpec(memory_space=pl.ANY),
        scratch_shapes=[pltpu.VMEM((2, tm * TOK_TILE, LANES_V7X), U32),
                        pltpu.VMEM((MOE_TB * TOK_TILE, LANES_V7X), U32),
                        pltpu.SemaphoreType.DMA((2,)), pltpu.SemaphoreType.DMA(())],
    )
    return pl.pallas_call(
        functools.partial(_dispatch_kernel, tm=tm, n_blocks=n_blocks),
        grid_spec=grid_spec,
        out_shape=jax.ShapeDtypeStruct((n_blocks * MOE_TB * TOK_TILE, LANES_V7X), U32),
        compiler_params=_cparams(1),
        name="moe_dispatch",
    )(zero_blk, pos3, xp)


def _moe_kernel(blk_e_ref, nxt_e_ref, nused_ref, xs_ref, w1_hbm, w3_hbm, w2_hbm, ys_ref,
                wf1, wf3, wf2, w1b, w3b, w2b, slot_ref, sem, *, layer):
    s = pl.program_id(0)
    nused = nused_ref[0]

    def fetch(e, sl):
        return [pltpu.make_async_copy(w_hbm.at[layer, e], wf.at[sl], sem.at[sl, k])
                for k, (w_hbm, wf) in enumerate(((w1_hbm, wf1), (w3_hbm, wf3), (w2_hbm, wf2)))]

    @pl.when(s >= nused)
    def _():
        ys_ref[...] = jnp.zeros_like(ys_ref)

    @pl.when(s < nused)
    def _():
        e = blk_e_ref[s]
        prev = blk_e_ref[jnp.maximum(s - 1, 0)]

        @pl.when(s == 0)
        def _():
            slot_ref[0] = 0
            for cp in fetch(e, 0):
                cp.start()

        @pl.when((s > 0) & (e != prev))
        def _():
            slot_ref[0] = 1 - slot_ref[0]

        @pl.when((s == 0) | (e != prev))
        def _():
            sl = slot_ref[0]
            for cp in fetch(e, sl):
                cp.wait()
            w1b[...] = wf1[sl].astype(BF16)
            w3b[...] = wf3[sl].astype(BF16)
            w2b[...] = wf2[sl].astype(BF16)
            nxt = nxt_e_ref[s]

            @pl.when(nxt >= 0)
            def _():
                for cp in fetch(nxt, 1 - sl):
                    cp.start()

        xb = _unpack_rows(_load_token_tiles(xs_ref, 0, MOE_TB)).astype(BF16)
        h1 = jnp.dot(xb, w1b[...], preferred_element_type=F32)
        h3 = jnp.dot(xb, w3b[...], preferred_element_type=F32)
        h = (h1 * jax.nn.sigmoid(h1) * h3).astype(BF16)
        y = jnp.dot(h, w2b[...], preferred_element_type=F32)
        _store_token_tiles(ys_ref, 0, _pack_rows(y))


def _moe_experts(xs, blk_e, nxt_e, nused, w1, w3, w2, layer):
    n_blocks = blk_e.shape[0]
    tiles = pl.BlockSpec((MOE_TB * TOK_TILE, LANES_V7X), lambda s, be, ne, nu: (s, 0))
    hbm = pl.BlockSpec(memory_space=pl.ANY)
    up, down = (D_MODEL, D_EXPERT), (D_EXPERT, D_MODEL)
    grid_spec = pltpu.PrefetchScalarGridSpec(
        num_scalar_prefetch=3,
        grid=(n_blocks,),
        in_specs=[tiles, hbm, hbm, hbm],
        out_specs=tiles,
        scratch_shapes=[pltpu.VMEM((2,) + up, F32), pltpu.VMEM((2,) + up, F32), pltpu.VMEM((2,) + down, F32),
                        pltpu.VMEM(up, BF16), pltpu.VMEM(up, BF16), pltpu.VMEM(down, BF16),
                        pltpu.SMEM((1,), jnp.int32), pltpu.SemaphoreType.DMA((2, 3))],
    )
    return pl.pallas_call(
        functools.partial(_moe_kernel, layer=layer),
        grid_spec=grid_spec,
        out_shape=jax.ShapeDtypeStruct(xs.shape, U32),
        compiler_params=_cparams(1, VMEM_LIMIT_V7X),
        name="moe_experts",
    )(blk_e, nxt_e, nused, xs, w1, w3, w2)


def _combine_kernel(pos_ref, nxt_ref, ys_hbm, x_ref, rt_ref, g_ref, b_ref, o_ref, ybuf, sem, *, tm):
    s = pl.program_id(0)
    nsteps = pl.num_programs(0)
    slot = s % 2

    def copy(src, j, sl):
        return pltpu.make_async_copy(ys_hbm.at[_tile_rows(src)], ybuf.at[sl, _tile_rows(j)], sem.at[sl])

    def start_gather(idx_ref, sl):
        def body(j, c):
            copy(idx_ref[0, 0, j], j, sl).start()
            return c
        lax.fori_loop(0, TOP_K * tm, body, 0, unroll=8)

    @pl.when(s == 0)
    def _():
        start_gather(pos_ref, 0)

    @pl.when(s + 1 < nsteps)
    def _():
        for j in range(TOP_K * tm):
            copy(nxt_ref[0, 0, j], j, 1 - slot).start()

    def wait_body(j, c):
        copy(0, j, slot).wait()
        return c
    lax.fori_loop(0, TOP_K * tm, wait_body, 0, unroll=8)

    rt = rt_ref[...]
    ffn = _unpack_rows(_load_token_tiles(ybuf, 0, tm, lead=slot)) * rt[:, RT_W0:RT_W0 + 1]
    ffn = ffn + _unpack_rows(_load_token_tiles(ybuf, tm * TOK_TILE, tm, lead=slot)) * rt[:, RT_W1:RT_W1 + 1]
    o_ref[...] = _layer_norm_rows(DN_ALPHA * x_ref[...] + ffn, g_ref[...], b_ref[...])


def _moe_combine(ys, pos, rt, x, g, b, tm=256):
    n = x.shape[0]
    nsteps = n // tm
    pos3 = _step_indices(pos, tm)
    smem_blk = lambda f: pl.BlockSpec((1, 1, TOP_K * tm), f, memory_space=pltpu.SMEM)
    row = lambda width: pl.BlockSpec((tm, width), lambda i: (i, 0))
    const = lambda shape: pl.BlockSpec(shape, lambda i: (0, 0))
    return pl.pallas_call(
        functools.partial(_combine_kernel, tm=tm),
        grid=(nsteps,),
        in_specs=[smem_blk(lambda i: (i, 0, 0)), smem_blk(lambda i: (jnp.minimum(i + 1, nsteps - 1), 0, 0)),
                  pl.BlockSpec(memory_space=pl.ANY), row(D_MODEL), row(ROUTE_W),
                  const((1, D_MODEL)), const((1, D_MODEL))],
        out_specs=row(D_MODEL),
        out_shape=jax.ShapeDtypeStruct((n, D_MODEL), F32),
        scratch_shapes=[pltpu.VMEM((2, TOP_K * tm * TOK_TILE, LANES_V7X), U32), pltpu.SemaphoreType.DMA((2,))],
        compiler_params=_cparams(1, 40 * MIB),
        name="moe_combine",
    )(pos3, pos3, ys, x, rt, g.reshape(1, -1), b.reshape(1, -1))


def _moe_layer(x1, xp, rt, cnt, w1, w3, w2, layer, g, b):
    n_tok = x1.shape[0]
    pos, blk_e, nxt_e, nused, zero_blk = _block_layout(rt, cnt, n_tok)
    xs = _moe_dispatch(xp, pos, zero_blk)
    ys = _moe_experts(xs, blk_e, nxt_e, nused, w1, w3, w2, layer)
    return _moe_combine(ys, pos, rt, x1, g, b)


def _gmlp_kernel(u_ref, v_ref, g_ref, b_ref, w_ref, bs_ref, o_ref, *, chunks):
    for ci in range(chunks):
        rows = slice(ci * C_CHUNK, (ci + 1) * C_CHUNK)
        u = jax.nn.gelu(u_ref[rows, :])
        v = _layer_norm_rows(jax.nn.gelu(v_ref[rows, :]), g_ref[...], b_ref[...]).astype(BF16)
        for gi in range(C_GROUPS):
            cols = slice(gi * C_GROUP_DIM, (gi + 1) * C_GROUP_DIM)
            mixed = jnp.dot(w_ref[gi], v[:, cols], preferred_element_type=F32) + bs_ref[:, cols]
            o_ref[rows, cols] = (u[:, cols] * mixed).astype(o_ref.dtype)


def _gmlp(proj, ln_g, ln_b, w_s, b_s, chunks=4):
    n = proj.shape[0]
    tm = chunks * C_CHUNK
    w = (w_s * jnp.tril(jnp.ones((C_CHUNK, C_CHUNK), w_s.dtype))).astype(BF16)
    bs_full = jnp.repeat(b_s.T, C_GROUP_DIM, axis=1)
    const2 = lambda shape: pl.BlockSpec(shape, lambda i: (0, 0))
    return pl.pallas_call(
        functools.partial(_gmlp_kernel, chunks=chunks),
        grid=(n // tm,),
        in_specs=[pl.BlockSpec((tm, C_W), lambda i: (i, 0)), pl.BlockSpec((tm, C_W), lambda i: (i, 1)),
                  const2((1, C_W)), const2((1, C_W)),
                  pl.BlockSpec((C_GROUPS, C_CHUNK, C_CHUNK), lambda i: (0, 0, 0)), const2((C_CHUNK, C_W))],
        out_specs=pl.BlockSpec((tm, C_W), lambda i: (i, 0)),
        out_shape=jax.ShapeDtypeStruct((n, C_W), BF16),
        compiler_params=_cparams(1, 32 * MIB),
        name="gmlp_gating",
    )(proj, proj, ln_g.reshape(1, -1), ln_b.reshape(1, -1), w, bs_full)


CONV_HIST = 32


def _conv_kernel(a_ref, gt_ref, ap_ref, gp_ref, w_ref, cb_ref, g_ref, b_ref, o_ref, hbuf, hshift, *, ts):
    i = pl.program_id(1)
    hist = ap_ref[...] * jax.nn.sigmoid(gp_ref[...])
    hbuf[0:CONV_HIST, :] = jnp.where(i > 0, hist, jnp.zeros_like(hist))
    hbuf[CONV_HIST:CONV_HIST + ts, :] = a_ref[...] * jax.nn.sigmoid(gt_ref[...])
    off = CONV_HIST - (D_CONV - 1)
    acc = jnp.zeros((ts, D_CHANNELS), F32) + cb_ref[...]
    for r in range(SUBLANES_V7X):
        taps = [j for j in range(D_CONV) if (off + j) % SUBLANES_V7X == r]
        if not taps:
            continue
        src = hbuf
        if r:
            span = max(taps) + off - r + ts
            hshift[0:span, :] = hbuf[r:r + span, :]
            src = hshift
        for j in taps:
            base = off + j - r
            acc = acc + w_ref[j:j + 1, :] * src[base:base + ts, :]
    y = _layer_norm_rows(acc, g_ref[...], b_ref[...])
    o_ref[...] = (y * jax.nn.sigmoid(y)).astype(o_ref.dtype)


def _conformer_conv(proj, conv_w, conv_b, ln_g, ln_b, bsz, seq, ts=512):
    nt = seq // ts
    acol = 2 * C_W // D_CHANNELS
    gcol = acol + 1
    hb = ts // CONV_HIST
    cur = lambda col: pl.BlockSpec((ts, D_CHANNELS), lambda b, i: (b * nt + i, col))
    prev = lambda col: pl.BlockSpec((CONV_HIST, D_CHANNELS),
                                    lambda b, i: (jnp.maximum((b * nt + i) * hb - 1, 0), col))
    const2 = lambda shape: pl.BlockSpec(shape, lambda b, i: (0, 0))
    return pl.pallas_call(
        functools.partial(_conv_kernel, ts=ts),
        grid=(bsz, nt),
        in_specs=[cur(acol), cur(gcol), prev(acol), prev(gcol),
                  const2((D_CONV, D_CHANNELS)), const2((1, D_CHANNELS)), const2((1, D_CHANNELS)),
                  const2((1, D_CHANNELS))],
        out_specs=pl.BlockSpec((ts, D_CHANNELS), lambda b, i: (b * nt + i, 0)),
        out_shape=jax.ShapeDtypeStruct((bsz * seq, D_CHANNELS), BF16),
        scratch_shapes=[pltpu.VMEM((CONV_HIST + ts, D_CHANNELS), F32), pltpu.VMEM((CONV_HIST + ts, D_CHANNELS), F32)],
        compiler_params=_cparams(2, 32 * MIB),
        name="conformer_conv",
    )(proj, proj, proj, proj, conv_w, conv_b.reshape(1, -1), ln_g.reshape(1, -1), ln_b.reshape(1, -1))


def kernel(x, w_in_ab, attn_sinks, rel_bias, hgrn_lb_logits, hgrn_norm_g, w_out_ab, w_in_cd, gmlp_ln_g, gmlp_ln_b, gmlp_w_s, gmlp_b_s, conv_w, conv_b, conv_ln_g, conv_ln_b, w_out_cd, ln_mix_g, ln_mix_b, ln_ffn_g, ln_ffn_b, moe_w_group, moe_b_group, moe_w_router, moe_b_router, moe_w1, moe_w3, moe_w2):
    bsz, seq = x.shape[0], x.shape[1]
    n_tok = bsz * seq
    xf = x.reshape(n_tok, D_MODEL)
    lb_table = jnp.cumsum(jax.nn.softmax(hgrn_lb_logits.astype(F32), axis=0), axis=0)
    bias = _bias_table(rel_bias.astype(F32), _t5_bucket_table())

    for layer in range(DEPTH):
        j = layer // 2
        if layer % 2 == 0:
            proj = _matmul(xf, w_in_ab[j].astype(BF16), 512, EVEN_IN // 3, F32)
            ya = _attention(proj, bias, attn_sinks[j].astype(F32), bsz, seq)
            yb = _hgrn(proj, lb_table[layer], hgrn_norm_g[j].astype(F32), bsz, seq)
            w_out = w_out_ab[j]
        else:
            proj = _matmul(xf, w_in_cd[j].astype(BF16), 512, ODD_IN // 2, F32)
            ya = _gmlp(proj, gmlp_ln_g[j], gmlp_ln_b[j], gmlp_w_s[j], gmlp_b_s[j])
            yb = _conformer_conv(proj, conv_w[j], conv_b[j], conv_ln_g[j], conv_ln_b[j], bsz, seq)
            w_out = w_out_cd[j]
        wr, rbias = _router_weights(moe_w_group[layer], moe_b_group[layer],
                                    moe_w_router[layer], moe_b_router[layer])
        x1, xp, rt, cnt = _outproj_ln_route(ya, yb, w_out.astype(BF16), xf, ln_mix_g[layer], ln_mix_b[layer],
                                            wr, rbias)
        xf = _moe_layer(x1, xp, rt, cnt, moe_w1, moe_w3, moe_w2, layer, ln_ffn_g[layer], ln_ffn_b[layer])
    return xf.reshape(bsz, seq, D_MODEL)
```

```python
import functools
import math

import jax
import jax.numpy as jnp
from jax import lax
from jax.experimental import pallas as pl
from jax.experimental.pallas import tpu as pltpu

D_MODEL = 2048
DEPTH = 2
A_HEADS = 16
A_KV_HEADS = 2
A_HEAD_DIM = 64
A_WINDOW = 128
A_BLOCK = 128
REL_BUCKETS = 32
REL_MAX_DIST = 128
B_HEADS = 8
B_DK = 128
B_DV = 128
C_GROUPS = 8
C_GROUP_DIM = 128
C_CHUNK = 128
D_CHANNELS = 1024
D_CONV = 31
A_QW = A_HEADS * A_HEAD_DIM
A_KVW = A_KV_HEADS * A_HEAD_DIM
B_KW = B_HEADS * B_DK
B_VW = B_HEADS * B_DV
C_W = C_GROUPS * C_GROUP_DIM
EVEN_IN = A_QW + 2 * A_KVW + 2 * B_KW + 2 * B_VW
ODD_IN = 2 * C_W + 2 * D_CHANNELS
N_GROUPS = 4
EXPERTS_PER_GROUP = 8
N_EXPERTS = N_GROUPS * EXPERTS_PER_GROUP
TOP_K = 2
D_EXPERT = 512
DN_ALPHA = (2 * DEPTH) ** 0.25
LN_EPS = 1e-5
RMS_EPS = 1e-6

LANES_V7X = 128
SUBLANES_V7X = 8
MIB = 1024 * 1024
VMEM_LIMIT_V7X = 56 * MIB

HGRN_CHUNK = 128
HGRN_SUB = 8
HGRN_GROUP = 32
HGRN_ROWS = 1024
MOE_TB = 256
ROUTE_W = LANES_V7X
HALF = D_MODEL // 2
TOK_TILE = HALF // LANES_V7X
assert TOK_TILE == SUBLANES_V7X

BF16 = jnp.bfloat16
F32 = jnp.float32
U32 = jnp.uint32
NEG_INF = float("-inf")
LOG2E = math.log2(math.e)
RT_W0, RT_W1, RT_E0, RT_E1, RT_R0, RT_R1 = range(6)


def _cparams(n_axes, vmem_bytes=None):
    return pltpu.CompilerParams(dimension_semantics=("arbitrary",) * n_axes, vmem_limit_bytes=vmem_bytes)


def _layer_norm_rows(z, g, b):
    mu = jnp.mean(z, axis=-1, keepdims=True)
    zc = z - mu
    var = jnp.mean(zc * zc, axis=-1, keepdims=True)
    return zc * lax.rsqrt(var + LN_EPS) * g + b


def _dot_nt(a, b):
    return lax.dot_general(a, b, (((1,), (1,)), ((), ())), preferred_element_type=F32)


def _dot_tn(a, b):
    return lax.dot_general(a, b, (((0,), (0,)), ((), ())), preferred_element_type=F32)


def _pack_rows(y):
    lo = lax.bitcast_convert_type(y[:, :HALF].astype(BF16).astype(F32), U32) >> 16
    hi = lax.bitcast_convert_type(y[:, HALF:].astype(BF16).astype(F32), U32) & jnp.uint32(0xFFFF0000)
    return lo | hi


def _unpack_rows(p):
    lo = lax.bitcast_convert_type(p << 16, F32)
    hi = lax.bitcast_convert_type(p & jnp.uint32(0xFFFF0000), F32)
    return jnp.concatenate([lo, hi], axis=1)


def _store_token_tiles(ref, base, packed):
    rows = packed.shape[0]
    for c in range(TOK_TILE):
        ref[pl.ds(base + c, rows, stride=TOK_TILE), :] = packed[:, c * LANES_V7X:(c + 1) * LANES_V7X]


def _load_token_tiles(ref, base, rows, lead=None):
    parts = []
    for c in range(TOK_TILE):
        idx = (pl.ds(base + c, rows, stride=TOK_TILE), slice(None))
        parts.append(ref[idx] if lead is None else ref[(lead,) + idx])
    return jnp.concatenate(parts, axis=1)


def _mm_kernel(a_ref, w_ref, o_ref):
    a = a_ref[...].astype(BF16)
    o_ref[...] = jnp.dot(a, w_ref[...], preferred_element_type=F32).astype(o_ref.dtype)


def _matmul(a, w, tm, tn, out_dtype):
    m, k = a.shape
    n = w.shape[1]
    return pl.pallas_call(
        _mm_kernel,
        grid=(n // tn, m // tm),
        in_specs=[pl.BlockSpec((tm, k), lambda j, i: (i, 0)), pl.BlockSpec((k, tn), lambda j, i: (0, j))],
        out_specs=pl.BlockSpec((tm, tn), lambda j, i: (i, j)),
        out_shape=jax.ShapeDtypeStruct((m, n), out_dtype),
        compiler_params=_cparams(2, VMEM_LIMIT_V7X),
        name="proj_matmul",
    )(a, w)


def _bias_table_kernel(rb_ref, bucket_ref, o_ref):
    bucket = bucket_ref[...]
    for h in range(A_HEADS):
        acc = jnp.zeros(bucket.shape, F32)
        for bk in range(REL_BUCKETS):
            acc = jnp.where(bucket == bk, rb_ref[bk, h], acc)
        o_ref[h] = acc


def _bias_table(rel_bias, bucket):
    return pl.pallas_call(
        _bias_table_kernel,
        in_specs=[pl.BlockSpec(memory_space=pltpu.SMEM), pl.BlockSpec(memory_space=pltpu.VMEM)],
        out_specs=pl.BlockSpec(memory_space=pltpu.VMEM),
        out_shape=jax.ShapeDtypeStruct((A_HEADS,) + bucket.shape, F32),
        name="rel_bias_table",
    )(rel_bias, bucket)


def _t5_bucket_table():
    t_loc = jnp.arange(A_BLOCK, dtype=jnp.int32)[:, None]
    j_loc = jnp.arange(A_BLOCK, dtype=jnp.int32)[None, :]
    dist = jnp.where(j_loc <= t_loc, t_loc - j_loc, t_loc + A_BLOCK - j_loc)
    max_exact = REL_BUCKETS // 2
    d = jnp.maximum(dist, 1).astype(F32)
    large = max_exact + (jnp.log(d / max_exact) / math.log(REL_MAX_DIST / max_exact)
                         * (REL_BUCKETS - max_exact)).astype(jnp.int32)
    large = jnp.minimum(large, REL_BUCKETS - 1)
    return jnp.where(dist < max_exact, dist, large)


assert A_WINDOW == A_BLOCK


def _attn_kernel(q_ref, kc_ref, vc_ref, kp_ref, vp_ref, bias_ref, sink_ref, o_ref):
    n = pl.program_id(1)
    blk = A_BLOCK
    k2 = jnp.concatenate([kp_ref[...], kc_ref[...]], axis=0)
    v2 = jnp.concatenate([vp_ref[...], vc_ref[...]], axis=0)
    k2r = pltpu.roll(k2, A_HEAD_DIM, 1)
    v2r = pltpu.roll(v2, A_HEAD_DIM, 1)
    lo = lax.broadcasted_iota(jnp.int32, k2.shape, 1) < A_HEAD_DIM
    zero = jnp.zeros_like(k2)

    def placed(x, xr, g, par):
        src = x if g == par else xr
        return (jnp.where(lo, src, zero) if par == 0 else jnp.where(lo, zero, src)).astype(BF16)

    kk = [[placed(k2, k2r, g, par) for par in range(2)] for g in range(A_KV_HEADS)]
    vv = [[placed(v2, v2r, g, par) for par in range(2)] for g in range(A_KV_HEADS)]

    own = (lax.broadcasted_iota(jnp.int32, (blk, blk), 1) <= lax.broadcasted_iota(jnp.int32, (blk, blk), 0))
    no_prev = jnp.where(n > 0, 0.0, NEG_INF)
    zero_p = jnp.zeros((blk, blk), F32)

    heads_per_kv = A_HEADS // A_KV_HEADS
    for p in range(A_HEADS // 2):
        g = (2 * p) // heads_per_kv
        qp = (q_ref[:, p * 128:(p + 1) * 128] * (A_HEAD_DIM ** -0.5)).astype(BF16)
        acc = jnp.zeros((blk, 128), F32)
        for par in range(2):
            h = 2 * p + par
            sink = sink_ref[h]
            s2 = _dot_nt(qp, kk[g][par])
            logits = jnp.where(own, s2[:, blk:], s2[:, :blk] + no_prev) + bias_ref[h]
            m = jnp.maximum(jnp.max(logits, axis=-1, keepdims=True), sink)
            e = jnp.exp(logits - m)
            den = jnp.sum(e, axis=-1, keepdims=True) + jnp.exp(sink - m)
            e2 = jnp.concatenate([jnp.where(own, zero_p, e), jnp.where(own, e, zero_p)], axis=1).astype(BF16)
            acc = acc + jnp.dot(e2, vv[g][par], preferred_element_type=F32) * (1.0 / den)
        o_ref[:, p * 128:(p + 1) * 128] = acc.astype(o_ref.dtype)


def _attention(proj, bias, sinks, bsz, seq):
    nb = seq // A_BLOCK
    kcol = A_QW // 128
    vcol = kcol + 1
    row = lambda b, n: b * nb + n
    prow = lambda b, n: b * nb + jnp.maximum(n - 1, 0)
    return pl.pallas_call(
        _attn_kernel,
        grid=(bsz, nb),
        in_specs=[
            pl.BlockSpec((A_BLOCK, A_QW), lambda b, n: (row(b, n), 0)),
            pl.BlockSpec((A_BLOCK, 128), lambda b, n: (row(b, n), kcol)),
            pl.BlockSpec((A_BLOCK, 128), lambda b, n: (row(b, n), vcol)),
            pl.BlockSpec((A_BLOCK, 128), lambda b, n: (prow(b, n), kcol)),
            pl.BlockSpec((A_BLOCK, 128), lambda b, n: (prow(b, n), vcol)),
            pl.BlockSpec((A_HEADS, A_BLOCK, A_BLOCK), lambda b, n: (0, 0, 0)),
            pl.BlockSpec(memory_space=pltpu.SMEM),
        ],
        out_specs=pl.BlockSpec((A_BLOCK, A_QW), lambda b, n: (row(b, n), 0)),
        out_shape=jax.ShapeDtypeStruct((bsz * seq, A_QW), BF16),
        compiler_params=_cparams(2, 32 * MIB),
        name="swa_attention",
    )(proj, proj, proj, proj, proj, bias, sinks)


HGRN_HEADS_PER_STEP = 2


def _hgrn_kernel(q_ref, f_ref, i_ref, g_ref, lb_ref, ng_ref, o_ref, st_ref):
    c, sub, grp = HGRN_CHUNK, HGRN_SUB, HGRN_GROUP
    ngrp = c // grp

    @pl.when(pl.program_id(2) == 0)
    def _():
        st_ref[...] = jnp.zeros_like(st_ref)

    rid = lax.broadcasted_iota(jnp.int32, (c, c), 0)
    cid = lax.broadcasted_iota(jnp.int32, (c, c), 1)
    grp_start = (rid // grp) * grp
    sub_start = (rid // sub) * sub
    m_cum = jnp.concatenate([(cid < grp_start).astype(F32),
                             ((cid >= grp_start) & (cid < sub_start)).astype(F32),
                             ((cid >= sub_start) & (cid <= rid)).astype(F32)], axis=0).astype(BF16)
    band = jnp.where(cid >= sub_start, rid - cid, -1)
    same_grp = (cid // grp) == (rid // grp)
    heads = range(HGRN_HEADS_PER_STEP)
    zeros_bf = lambda rows: jnp.zeros((rows, B_DK), BF16)

    def scores(hh, r0):
        cols = slice(hh * 128, (hh + 1) * 128)
        lb = lb_ref[0, :, cols]
        q = q_ref[pl.ds(r0, c), cols]
        fl = f_ref[pl.ds(r0, c), cols]
        qf = q * jax.nn.sigmoid(q)
        f = lb + (1.0 - lb) * jax.nn.sigmoid(fl)
        kin = 1.0 - f
        logf = jnp.log(f)
        l1 = logf.astype(BF16)
        res = logf - l1.astype(F32)
        l2 = res.astype(BF16)
        l3 = (res - l2.astype(F32)).astype(BF16)
        cum = jnp.dot(m_cum, jnp.concatenate([l1, l2, l3], axis=1), preferred_element_type=F32)
        cum = cum[:, :B_DK] + cum[:, B_DK:2 * B_DK] + cum[:, 2 * B_DK:]
        rg, rs, bq = cum[:c], cum[c:2 * c], cum[2 * c:]
        lk = jnp.log(kin)
        wq = lk - bq
        wg = wq - rs
        wb = wg - rg
        qt = qf * jnp.exp(bq)
        qg = qt * jnp.exp(rs)
        o = _dot_nt((qg * jnp.exp(rg)).astype(BF16), st_ref[hh].astype(BF16))

        kts = []
        for i in range(1, ngrp):
            n = i * grp
            kts += [jnp.exp(rg[n:n + 1, :] + wb[:n, :]).astype(BF16), zeros_bf(c - n)]
        g_far = _dot_nt(qg.astype(BF16), jnp.concatenate(kts, axis=0))
        s_far = jnp.concatenate([jnp.zeros((grp, c), F32)] +
                                [g_far[i * grp:(i + 1) * grp, (i - 1) * c:i * c] for i in range(1, ngrp)], axis=0)

        kts = []
        for j in range(1, grp // sub):
            for gi in range(ngrp):
                a, n = gi * grp, j * sub
                kts += [jnp.exp(rs[a + n:a + n + 1, :] + wg[a:a + n, :]).astype(BF16), zeros_bf(grp - n)]
        g_near = _dot_nt(qt.astype(BF16), jnp.concatenate(kts, axis=0))
        pieces = []
        for i in range(c // sub):
            j = i % (grp // sub)
            pieces.append(jnp.zeros((sub, c), F32) if j == 0 else g_near[i * sub:(i + 1) * sub, (j - 1) * c:j * c])
        s = jnp.where(same_grp, jnp.concatenate(pieces, axis=0), s_far)

        bq2, wq2 = bq * LOG2E, wq * LOG2E
        kds = [kin.astype(BF16)] + [jnp.exp2(pltpu.roll(bq2, c - d, 0) + wq2).astype(BF16) for d in range(1, sub)]
        g_diag = _dot_nt(qf.astype(BF16), jnp.concatenate(kds, axis=0))
        last = slice(c - 1, c)
        return dict(o=o, s=s, g_diag=g_diag, wb=wb, b_last=rg[last, :] + rs[last, :] + bq[last, :])

    def finish(hh, r0, h):
        cols = slice(hh * 128, (hh + 1) * 128)
        vb = i_ref[pl.ds(r0, c), cols].astype(BF16)
        gt = g_ref[pl.ds(r0, c), cols]
        o = h["o"] + jnp.dot(h["s"].astype(BF16), vb, preferred_element_type=F32)
        st_ref[hh] = (st_ref[hh] * jnp.exp(h["b_last"])
                      + _dot_tn(vb, jnp.exp(h["b_last"] + h["wb"]).astype(BF16)))
        o = o * lax.rsqrt(jnp.mean(o * o, axis=-1, keepdims=True) + RMS_EPS)
        o_ref[pl.ds(r0, c), cols] = (o * ng_ref[0, :, cols] * (gt * jax.nn.sigmoid(gt))).astype(o_ref.dtype)

    def chunk(ci, carry):
        r0 = pl.multiple_of(ci * c, c)
        hs = [scores(hh, r0) for hh in heads]
        for d in range(sub):
            on_diag = band == d
            for h in hs:
                h["s"] = jnp.where(on_diag, h["g_diag"][:, d * c:(d + 1) * c], h["s"])
        for hh in heads:
            finish(hh, r0, hs[hh])
        return carry

    lax.fori_loop(0, HGRN_ROWS // c, chunk, 0, unroll=True)


def _hgrn(proj, lb, norm_g, bsz, seq):
    nr = seq // HGRN_ROWS
    hp = HGRN_HEADS_PER_STEP
    width = hp * 128
    c0 = (A_QW + 2 * A_KVW) // width
    nhp = B_HEADS // hp
    spec = lambda off: pl.BlockSpec((HGRN_ROWS, width), lambda b, h, r: (b * nr + r, off + h))
    vec = pl.BlockSpec((1, 1, width), lambda b, h, r: (h, 0, 0))
    return pl.pallas_call(
        _hgrn_kernel,
        grid=(bsz, nhp, nr),
        in_specs=[spec(c0), spec(c0 + nhp), spec(c0 + 2 * nhp), spec(c0 + 3 * nhp), vec, vec],
        out_specs=pl.BlockSpec((HGRN_ROWS, width), lambda b, h, r: (b * nr + r, h)),
        out_shape=jax.ShapeDtypeStruct((bsz * seq, B_VW), BF16),
        scratch_shapes=[pltpu.VMEM((hp, B_DV, B_DK), F32)],
        compiler_params=_cparams(3, 32 * MIB),
        name="hgrn2",
    )(proj, proj, proj, proj, lb.reshape(nhp, 1, width), norm_g.reshape(nhp, 1, width))


OUTPROJ_LAG = 2


def _outproj_kernel(ya_ref, yb_ref, wa_ref, wb_ref, x_ref, g_ref, b_ref, wr_ref, rb_ref,
                    xo_ref, xp_ref, rt_ref, cnt_ref, run_ref, lg_ref, mix_ref):
    step = pl.program_id(0)

    @pl.when(step == 0)
    def _():
        run_ref[...] = jnp.zeros_like(run_ref)
        lg_ref[...] = jnp.zeros_like(lg_ref)
        mix_ref[...] = jnp.zeros_like(mix_ref)

    lg_prev = lg_ref[...]
    mix_prev = mix_ref[...]
    mix = jnp.dot(ya_ref[...], wa_ref[...], preferred_element_type=F32)
    mix_ref[...] = mix + jnp.dot(yb_ref[...], wb_ref[...], preferred_element_type=F32)

    y = _layer_norm_rows(DN_ALPHA * x_ref[...] + mix_prev, g_ref[...], b_ref[...])
    xo_ref[...] = y
    _store_token_tiles(xp_ref, 0, _pack_rows(y))
    tm = y.shape[0]
    y_hi = y.astype(BF16)
    y_lo = (y - y_hi.astype(F32)).astype(BF16)
    prod = jnp.dot(jnp.concatenate([y_hi, y_lo], axis=0), wr_ref[...], preferred_element_type=F32)
    lg = (prod[:tm, :ROUTE_W] + prod[tm:, :ROUTE_W]) + (prod[:tm, ROUTE_W:] + prod[tm:, ROUTE_W:])
    lg_ref[...] = lg + rb_ref[...]
    _route_rows(lg_prev, (step >= OUTPROJ_LAG).astype(F32), rt_ref, cnt_ref, run_ref)


def _outproj_ln_route(ya, yb, w_out, x, g, b, wr, rbias, tm=512):
    n = x.shape[0]
    ka = ya.shape[1]
    nsteps = n // tm
    lagged = lambda lag: (lambda i: (jnp.clip(i - lag, 0, nsteps - 1), 0))
    row = lambda width, lag: pl.BlockSpec((tm, width), lagged(lag))
    once = pl.Buffered(1)
    const = lambda shape: pl.BlockSpec(shape, lambda i: (0, 0), pipeline_mode=once)
    return pl.pallas_call(
        _outproj_kernel,
        grid=(nsteps + OUTPROJ_LAG,),
        in_specs=[row(ka, 0), row(ka, 0),
                  pl.BlockSpec((ka, D_MODEL), lambda i: (0, 0), pipeline_mode=once),
                  pl.BlockSpec((ka, D_MODEL), lambda i: (1, 0), pipeline_mode=once),
                  row(D_MODEL, 1), const((1, D_MODEL)), const((1, D_MODEL)),
                  const((D_MODEL, 2 * ROUTE_W)), const((1, ROUTE_W))],
        out_specs=[row(D_MODEL, 1), pl.BlockSpec((tm * TOK_TILE, LANES_V7X), lagged(1)),
                   row(ROUTE_W, OUTPROJ_LAG), pl.BlockSpec((1, ROUTE_W), lambda i: (0, 0))],
        out_shape=[jax.ShapeDtypeStruct((n, D_MODEL), F32),
                   jax.ShapeDtypeStruct((n * TOK_TILE, LANES_V7X), U32),
                   jax.ShapeDtypeStruct((n, ROUTE_W), F32), jax.ShapeDtypeStruct((1, ROUTE_W), F32)],
        scratch_shapes=[pltpu.VMEM((1, ROUTE_W), F32), pltpu.VMEM((tm, ROUTE_W), F32),
                        pltpu.VMEM((tm, D_MODEL), F32)],
        compiler_params=_cparams(1, VMEM_LIMIT_V7X),
        name="outproj_ln_route",
    )(ya, yb, w_out, w_out, x, g.reshape(1, -1), b.reshape(1, -1), wr, rbias)


def _router_weights(w_group, b_group, w_router, b_router):
    w = jnp.zeros((D_MODEL, ROUTE_W), F32)
    w = w.at[:, :N_GROUPS].set(w_group).at[:, N_GROUPS:N_GROUPS + N_EXPERTS].set(w_router)
    w_hi = w.astype(BF16)
    w_lo = (w - w_hi.astype(F32)).astype(BF16)
    rb = jnp.zeros((1, ROUTE_W), F32)
    rb = rb.at[0, :N_GROUPS].set(b_group).at[0, N_GROUPS:N_GROUPS + N_EXPERTS].set(b_router)
    return jnp.concatenate([w_hi, w_lo], axis=1), rb


def _route_rows(lg, live, rt_ref, cnt_ref, run_ref):
    tm = lg.shape[0]
    lane = lax.broadcasted_iota(jnp.int32, lg.shape, 1)
    sentinel = jnp.int32(ROUTE_W)
    rowmax = lambda mask: jnp.max(jnp.where(mask, lg, NEG_INF), axis=-1, keepdims=True)
    first = lambda mask: jnp.min(jnp.where(mask, lane, sentinel), axis=-1, keepdims=True)

    is_g = lane < N_GROUPS
    gmax = rowmax(is_g)
    g_idx = first(is_g & (lg == gmax))
    g_w = 1.0 / jnp.sum(jnp.where(is_g, jnp.exp(lg - gmax), 0.0), axis=-1, keepdims=True)

    e_lane = lane - N_GROUPS
    sel = (e_lane >= 0) & (e_lane < N_EXPERTS) & ((e_lane >> 3) == g_idx)
    m1 = rowmax(sel)
    i1 = first(sel & (lg == m1))
    sel2 = sel & (lane != i1)
    m2 = rowmax(sel2)
    i2 = first(sel2 & (lg == m2))
    ex = jnp.exp(m2 - m1)
    w0 = g_w / (1.0 + ex)
    w1 = g_w * ex / (1.0 + ex)

    oh0 = lane == i1
    oh1 = lane == i2
    both = (oh0 | oh1).astype(F32) * live
    rid = lax.broadcasted_iota(jnp.int32, (tm, tm), 0)
    cid = lax.broadcasted_iota(jnp.int32, (tm, tm), 1)
    before = jnp.dot((cid < rid).astype(BF16), both.astype(BF16), preferred_element_type=F32) + run_ref[...]
    rank0 = jnp.sum(jnp.where(oh0, before, 0.0), axis=-1, keepdims=True)
    rank1 = jnp.sum(jnp.where(oh1, before, 0.0), axis=-1, keepdims=True)
    run = run_ref[...] + jnp.sum(both, axis=0, keepdims=True)
    run_ref[...] = run
    cnt_ref[...] = run

    slab = jnp.zeros(lg.shape, F32)
    for ln, val in ((RT_W0, w0), (RT_W1, w1), (RT_E0, (i1 - N_GROUPS).astype(F32)),
                    (RT_E1, (i2 - N_GROUPS).astype(F32)), (RT_R0, rank0), (RT_R1, rank1)):
        slab = jnp.where(lane == ln, val, slab)
    rt_ref[...] = slab


assert EXPERTS_PER_GROUP == 8


def _block_layout(rt, cnt, n_tok):
    m = n_tok * TOP_K
    counts = cnt[0, N_GROUPS:N_GROUPS + N_EXPERTS].astype(jnp.int32)
    pcounts = (counts + MOE_TB - 1) // MOE_TB * MOE_TB
    pends = jnp.cumsum(pcounts)
    pstarts = pends - pcounts
    n_blocks = -(-(m + N_EXPERTS * (MOE_TB - 1)) // MOE_TB)
    e_idx = rt[:, RT_E0:RT_E1 + 1].astype(jnp.int32)
    rank = rt[:, RT_R0:RT_R1 + 1].astype(jnp.int32)
    onehot = e_idx[:, :, None] == jnp.arange(N_EXPERTS, dtype=jnp.int32)
    pos = jnp.sum(jnp.where(onehot, pstarts, 0), axis=-1) + rank
    blk_start = jnp.arange(n_blocks, dtype=jnp.int32) * MOE_TB
    blk_e = jnp.minimum(jnp.sum(blk_start[:, None] >= pends[None, :], axis=-1), N_EXPERTS - 1).astype(jnp.int32)
    nused = (pends[-1:] // MOE_TB).astype(jnp.int32)
    ids = jnp.arange(N_EXPERTS, dtype=jnp.int32)
    later_used = (ids[None, :] > ids[:, None]) & (counts[None, :] > 0)
    next_used = jnp.min(jnp.where(later_used, ids[None, :], N_EXPERTS), axis=-1)
    next_used = jnp.where(next_used == N_EXPERTS, -1, next_used).astype(jnp.int32)
    nxt_e = jnp.sum(jnp.where(blk_e[:, None] == ids[None, :], next_used[None, :], 0), axis=-1).astype(jnp.int32)
    ends_expert = jnp.any(((blk_start + MOE_TB)[:, None] == pends[None, :]) & (pcounts[None, :] > 0), axis=-1)
    zero_blk = (ends_expert | (blk_start >= pends[-1])).astype(jnp.int32)
    return pos, blk_e, nxt_e, nused, zero_blk


def _step_indices(pos, tm):
    nsteps = pos.shape[0] // tm
    return pos.reshape(nsteps, tm, TOP_K).transpose(0, 2, 1).reshape(nsteps, 1, TOP_K * tm)


def _tile_rows(i):
    return pl.ds(pl.multiple_of(i * TOK_TILE, TOK_TILE), TOK_TILE)


def _dispatch_kernel(zero_blk_ref, pos_ref, xp_ref, xs_hbm, xbuf, zbuf, sem, zsem, *, tm, n_blocks):
    s = pl.program_id(0)
    nsteps = pl.num_programs(0)
    slot = s % 2
    blk_rows = MOE_TB * TOK_TILE

    def copy(j, dst, sl):
        return pltpu.make_async_copy(xbuf.at[sl, _tile_rows(j)], xs_hbm.at[_tile_rows(dst)], sem.at[sl])

    def drain(sl):
        def body(j, c):
            copy(0, 0, sl).wait()
            return c
        lax.fori_loop(0, TOP_K * tm, body, 0, unroll=8)

    @pl.when(s == 0)
    def _():
        zbuf[...] = jnp.zeros_like(zbuf)

        def zero_copy(blk):
            rows = pl.ds(pl.multiple_of(blk * blk_rows, blk_rows), blk_rows)
            return pltpu.make_async_copy(zbuf, xs_hbm.at[rows], zsem)

        def each_flagged(fn):
            def body(blk, c):
                @pl.when(zero_blk_ref[blk] > 0)
                def _():
                    fn(zero_copy(blk))
                return c
            lax.fori_loop(0, n_blocks, body, 0)

        each_flagged(lambda cp: cp.start())
        each_flagged(lambda cp: cp.wait())

    @pl.when(s >= 2)
    def _():
        drain(slot)

    xbuf[slot] = xp_ref[...]

    for j in range(tm):
        for k in range(TOP_K):
            copy(j, pos_ref[0, 0, k * tm + j], slot).start(priority=k)

    @pl.when(s == nsteps - 1)
    def _():
        drain(1 - slot)
        drain(slot)


def _moe_dispatch(xp, pos, zero_blk, tm=512):
    n = pos.shape[0]
    n_blocks = zero_blk.shape[0]
    assert n // tm >= 2
    pos3 = _step_indices(pos, tm)
    grid_spec = pltpu.PrefetchScalarGridSpec(
        num_scalar_prefetch=1,
        grid=(n // tm,),
        in_specs=[pl.BlockSpec((1, 1, TOP_K * tm), lambda i, zb: (i, 0, 0), memory_space=pltpu.SMEM),
                  pl.BlockSpec((tm * TOK_TILE, LANES_V7X), lambda i, zb: (i, 0))],
        out_specs=pl.BlockSpec(memory_space=pl.ANY),
        scratch_shapes=[pltpu.VMEM((2, tm * TOK_TILE, LANES_V7X), U32),
                        pltpu.VMEM((MOE_TB * TOK_TILE, LANES_V7X), U32),
                        pltpu.SemaphoreType.DMA((2,)), pltpu.SemaphoreType.DMA(())],
    )
    return pl.pallas_call(
        functools.partial(_dispatch_kernel, tm=tm, n_blocks=n_blocks),
        grid_spec=grid_spec,
        out_shape=jax.ShapeDtypeStruct((n_blocks * MOE_TB * TOK_TILE, LANES_V7X), U32),
        compiler_params=_cparams(1),
        name="moe_dispatch",
    )(zero_blk, pos3, xp)


def _moe_kernel(blk_e_ref, nxt_e_ref, nused_ref, xs_ref, w1_hbm, w3_hbm, w2_hbm, ys_ref,
                wf1, wf3, wf2, w1b, w3b, w2b, slot_ref, sem, *, layer):
    s = pl.program_id(0)
    nused = nused_ref[0]

    def fetch(e, sl):
        return [pltpu.make_async_copy(w_hbm.at[layer, e], wf.at[sl], sem.at[sl, k])
                for k, (w_hbm, wf) in enumerate(((w1_hbm, wf1), (w3_hbm, wf3), (w2_hbm, wf2)))]

    @pl.when(s >= nused)
    def _():
        ys_ref[...] = jnp.zeros_like(ys_ref)

    @pl.when(s < nused)
    def _():
        e = blk_e_ref[s]
        prev = blk_e_ref[jnp.maximum(s - 1, 0)]

        @pl.when(s == 0)
        def _():
            slot_ref[0] = 0
            for cp in fetch(e, 0):
                cp.start()

        @pl.when((s > 0) & (e != prev))
        def _():
            slot_ref[0] = 1 - slot_ref[0]

        @pl.when((s == 0) | (e != prev))
        def _():
            sl = slot_ref[0]
            for cp in fetch(e, sl):
                cp.wait()
            w1b[...] = wf1[sl].astype(BF16)
            w3b[...] = wf3[sl].astype(BF16)
            w2b[...] = wf2[sl].astype(BF16)
            nxt = nxt_e_ref[s]

            @pl.when(nxt >= 0)
            def _():
                for cp in fetch(nxt, 1 - sl):
                    cp.start()

        xb = _unpack_rows(_load_token_tiles(xs_ref, 0, MOE_TB)).astype(BF16)
        h1 = jnp.dot(xb, w1b[...], preferred_element_type=F32)
        h3 = jnp.dot(xb, w3b[...], preferred_element_type=F32)
        h = (h1 * jax.nn.sigmoid(h1) * h3).astype(BF16)
        y = jnp.dot(h, w2b[...], preferred_element_type=F32)
        _store_token_tiles(ys_ref, 0, _pack_rows(y))


def _moe_experts(xs, blk_e, nxt_e, nused, w1, w3, w2, layer):
    n_blocks = blk_e.shape[0]
    tiles = pl.BlockSpec((MOE_TB * TOK_TILE, LANES_V7X), lambda s, be, ne, nu: (s, 0))
    hbm = pl.BlockSpec(memory_space=pl.ANY)
    up, down = (D_MODEL, D_EXPERT), (D_EXPERT, D_MODEL)
    grid_spec = pltpu.PrefetchScalarGridSpec(
        num_scalar_prefetch=3,
        grid=(n_blocks,),
        in_specs=[tiles, hbm, hbm, hbm],
        out_specs=tiles,
        scratch_shapes=[pltpu.VMEM((2,) + up, F32), pltpu.VMEM((2,) + up, F32), pltpu.VMEM((2,) + down, F32),
                        pltpu.VMEM(up, BF16), pltpu.VMEM(up, BF16), pltpu.VMEM(down, BF16),
                        pltpu.SMEM((1,), jnp.int32), pltpu.SemaphoreType.DMA((2, 3))],
    )
    return pl.pallas_call(
        functools.partial(_moe_kernel, layer=layer),
        grid_spec=grid_spec,
        out_shape=jax.ShapeDtypeStruct(xs.shape, U32),
        compiler_params=_cparams(1, VMEM_LIMIT_V7X),
        name="moe_experts",
    )(blk_e, nxt_e, nused, xs, w1, w3, w2)


def _combine_kernel(pos_ref, nxt_ref, ys_hbm, x_ref, rt_ref, g_ref, b_ref, o_ref, ybuf, sem, *, tm):
    s = pl.program_id(0)
    nsteps = pl.num_programs(0)
    slot = s % 2

    def copy(src, j, sl):
        return pltpu.make_async_copy(ys_hbm.at[_tile_rows(src)], ybuf.at[sl, _tile_rows(j)], sem.at[sl])

    def start_gather(idx_ref, sl):
        def body(j, c):
            copy(idx_ref[0, 0, j], j, sl).start()
            return c
        lax.fori_loop(0, TOP_K * tm, body, 0, unroll=8)

    @pl.when(s == 0)
    def _():
        start_gather(pos_ref, 0)

    @pl.when(s + 1 < nsteps)
    def _():
        for j in range(TOP_K * tm):
            copy(nxt_ref[0, 0, j], j, 1 - slot).start(priority=1)

    def wait_body(j, c):
        copy(0, j, slot).wait()
        return c
    lax.fori_loop(0, TOP_K * tm, wait_body, 0, unroll=8)

    rt = rt_ref[...]
    ffn = _unpack_rows(_load_token_tiles(ybuf, 0, tm, lead=slot)) * rt[:, RT_W0:RT_W0 + 1]
    ffn = ffn + _unpack_rows(_load_token_tiles(ybuf, tm * TOK_TILE, tm, lead=slot)) * rt[:, RT_W1:RT_W1 + 1]
    o_ref[...] = _layer_norm_rows(DN_ALPHA * x_ref[...] + ffn, g_ref[...], b_ref[...])


def _moe_combine(ys, pos, rt, x, g, b, tm=256):
    n = x.shape[0]
    nsteps = n // tm
    pos3 = _step_indices(pos, tm)
    smem_blk = lambda f: pl.BlockSpec((1, 1, TOP_K * tm), f, memory_space=pltpu.SMEM)
    row = lambda width: pl.BlockSpec((tm, width), lambda i: (i, 0))
    const = lambda shape: pl.BlockSpec(shape, lambda i: (0, 0))
    return pl.pallas_call(
        functools.partial(_combine_kernel, tm=tm),
        grid=(nsteps,),
        in_specs=[smem_blk(lambda i: (i, 0, 0)), smem_blk(lambda i: (jnp.minimum(i + 1, nsteps - 1), 0, 0)),
                  pl.BlockSpec(memory_space=pl.ANY), row(D_MODEL), row(ROUTE_W),
                  const((1, D_MODEL)), const((1, D_MODEL))],
        out_specs=row(D_MODEL),
        out_shape=jax.ShapeDtypeStruct((n, D_MODEL), F32),
        scratch_shapes=[pltpu.VMEM((2, TOP_K * tm * TOK_TILE, LANES_V7X), U32), pltpu.SemaphoreType.DMA((2,))],
        compiler_params=_cparams(1, 40 * MIB),
        name="moe_combine",
    )(pos3, pos3, ys, x, rt, g.reshape(1, -1), b.reshape(1, -1))


def _moe_layer(x1, xp, rt, cnt, w1, w3, w2, layer, g, b):
    n_tok = x1.shape[0]
    pos, blk_e, nxt_e, nused, zero_blk = _block_layout(rt, cnt, n_tok)
    xs = _moe_dispatch(xp, pos, zero_blk)
    ys = _moe_experts(xs, blk_e, nxt_e, nused, w1, w3, w2, layer)
    return _moe_combine(ys, pos, rt, x1, g, b)


def _gmlp_kernel(u_ref, v_ref, g_ref, b_ref, w_ref, bs_ref, o_ref, *, chunks):
    for ci in range(chunks):
        rows = slice(ci * C_CHUNK, (ci + 1) * C_CHUNK)
        u = jax.nn.gelu(u_ref[rows, :])
        v = _layer_norm_rows(jax.nn.gelu(v_ref[rows, :]), g_ref[...], b_ref[...]).astype(BF16)
        for gi in range(C_GROUPS):
            cols = slice(gi * C_GROUP_DIM, (gi + 1) * C_GROUP_DIM)
            mixed = jnp.dot(w_ref[gi], v[:, cols], preferred_element_type=F32) + bs_ref[:, cols]
            o_ref[rows, cols] = (u[:, cols] * mixed).astype(o_ref.dtype)


def _gmlp(proj, ln_g, ln_b, w_s, b_s, chunks=4):
    n = proj.shape[0]
    tm = chunks * C_CHUNK
    w = (w_s * jnp.tril(jnp.ones((C_CHUNK, C_CHUNK), w_s.dtype))).astype(BF16)
    bs_full = jnp.repeat(b_s.T, C_GROUP_DIM, axis=1)
    const2 = lambda shape: pl.BlockSpec(shape, lambda i: (0, 0))
    return pl.pallas_call(
        functools.partial(_gmlp_kernel, chunks=chunks),
        grid=(n // tm,),
        in_specs=[pl.BlockSpec((tm, C_W), lambda i: (i, 0)), pl.BlockSpec((tm, C_W), lambda i: (i, 1)),
                  const2((1, C_W)), const2((1, C_W)),
                  pl.BlockSpec((C_GROUPS, C_CHUNK, C_CHUNK), lambda i: (0, 0, 0)), const2((C_CHUNK, C_W))],
        out_specs=pl.BlockSpec((tm, C_W), lambda i: (i, 0)),
        out_shape=jax.ShapeDtypeStruct((n, C_W), BF16),
        compiler_params=_cparams(1, 32 * MIB),
        name="gmlp_gating",
    )(proj, proj, ln_g.reshape(1, -1), ln_b.reshape(1, -1), w, bs_full)


CONV_HIST = 32


def _conv_kernel(a_ref, gt_ref, ap_ref, gp_ref, w_ref, cb_ref, g_ref, b_ref, o_ref, hbuf, hshift, *, ts):
    i = pl.program_id(1)
    hist = ap_ref[...] * jax.nn.sigmoid(gp_ref[...])
    hbuf[0:CONV_HIST, :] = jnp.where(i > 0, hist, jnp.zeros_like(hist))
    hbuf[CONV_HIST:CONV_HIST + ts, :] = a_ref[...] * jax.nn.sigmoid(gt_ref[...])
    off = CONV_HIST - (D_CONV - 1)
    acc = jnp.zeros((ts, D_CHANNELS), F32) + cb_ref[...]
    for r in range(SUBLANES_V7X):
        taps = [j for j in range(D_CONV) if (off + j) % SUBLANES_V7X == r]
        if not taps:
            continue
        src = hbuf
        if r:
            span = max(taps) + off - r + ts
            hshift[0:span, :] = hbuf[r:r + span, :]
            src = hshift
        for j in taps:
            base = off + j - r
            acc = acc + w_ref[j:j + 1, :] * src[base:base + ts, :]
    y = _layer_norm_rows(acc, g_ref[...], b_ref[...])
    o_ref[...] = (y * jax.nn.sigmoid(y)).astype(o_ref.dtype)


def _conformer_conv(proj, conv_w, conv_b, ln_g, ln_b, bsz, seq, ts=512):
    nt = seq // ts
    acol = 2 * C_W // D_CHANNELS
    gcol = acol + 1
    hb = ts // CONV_HIST
    cur = lambda col: pl.BlockSpec((ts, D_CHANNELS), lambda b, i: (b * nt + i, col))
    prev = lambda col: pl.BlockSpec((CONV_HIST, D_CHANNELS),
                                    lambda b, i: (jnp.maximum((b * nt + i) * hb - 1, 0), col))
    const2 = lambda shape: pl.BlockSpec(shape, lambda b, i: (0, 0))
    return pl.pallas_call(
        functools.partial(_conv_kernel, ts=ts),
        grid=(bsz, nt),
        in_specs=[cur(acol), cur(gcol), prev(acol), prev(gcol),
                  const2((D_CONV, D_CHANNELS)), const2((1, D_CHANNELS)), const2((1, D_CHANNELS)),
                  const2((1, D_CHANNELS))],
        out_specs=pl.BlockSpec((ts, D_CHANNELS), lambda b, i: (b * nt + i, 0)),
        out_shape=jax.ShapeDtypeStruct((bsz * seq, D_CHANNELS), BF16),
        scratch_shapes=[pltpu.VMEM((CONV_HIST + ts, D_CHANNELS), F32), pltpu.VMEM((CONV_HIST + ts, D_CHANNELS), F32)],
        compiler_params=_cparams(2, 32 * MIB),
        name="conformer_conv",
    )(proj, proj, proj, proj, conv_w, conv_b.reshape(1, -1), ln_g.reshape(1, -1), ln_b.reshape(1, -1))


def kernel(x, w_in_ab, attn_sinks, rel_bias, hgrn_lb_logits, hgrn_norm_g, w_out_ab, w_in_cd, gmlp_ln_g, gmlp_ln_b, gmlp_w_s, gmlp_b_s, conv_w, conv_b, conv_ln_g, conv_ln_b, w_out_cd, ln_mix_g, ln_mix_b, ln_ffn_g, ln_ffn_b, moe_w_group, moe_b_group, moe_w_router, moe_b_router, moe_w1, moe_w3, moe_w2):
    bsz, seq = x.shape[0], x.shape[1]
    n_tok = bsz * seq
    xf = x.reshape(n_tok, D_MODEL)
    lb_table = jnp.cumsum(jax.nn.softmax(hgrn_lb_logits.astype(F32), axis=0), axis=0)
    bias = _bias_table(rel_bias.astype(F32), _t5_bucket_table())

    for layer in range(DEPTH):
        j = layer // 2
        if layer % 2 == 0:
            proj = _matmul(xf, w_in_ab[j].astype(BF16), 1024, EVEN_IN // 3, F32)
            ya = _attention(proj, bias, attn_sinks[j].astype(F32), bsz, seq)
            yb = _hgrn(proj, lb_table[layer], hgrn_norm_g[j].astype(F32), bsz, seq)
            w_out = w_out_ab[j]
        else:
            proj = _matmul(xf, w_in_cd[j].astype(BF16), 1024, ODD_IN // 2, F32)
            ya = _gmlp(proj, gmlp_ln_g[j], gmlp_ln_b[j], gmlp_w_s[j], gmlp_b_s[j])
            yb = _conformer_conv(proj, conv_w[j], conv_b[j], conv_ln_g[j], conv_ln_b[j], bsz, seq)
            w_out = w_out_cd[j]
        wr, rbias = _router_weights(moe_w_group[layer], moe_b_group[layer],
                                    moe_w_router[layer], moe_b_router[layer])
        x1, xp, rt, cnt = _outproj_ln_route(ya, yb, w_out.astype(BF16), xf, ln_mix_g[layer], ln_mix_b[layer],
                                            wr, rbias)
        xf = _moe_layer(x1, xp, rt, cnt, moe_w1, moe_w3, moe_w2, layer, ln_ffn_g[layer], ln_ffn_b[layer])
    return xf.reshape(bsz, seq, D_MODEL)
```

```python
import functools
import math

import jax
import jax.numpy as jnp
from jax import lax
from jax.experimental import pallas as pl
from jax.experimental.pallas import tpu as pltpu

D_MODEL = 2048
DEPTH = 2
A_HEADS = 16
A_KV_HEADS = 2
A_HEAD_DIM = 64
A_WINDOW = 128
A_BLOCK = 128
REL_BUCKETS = 32
REL_MAX_DIST = 128
B_HEADS = 8
B_DK = 128
B_DV = 128
C_GROUPS = 8
C_GROUP_DIM = 128
C_CHUNK = 128
D_CHANNELS = 1024
D_CONV = 31
A_QW = A_HEADS * A_HEAD_DIM
A_KVW = A_KV_HEADS * A_HEAD_DIM
B_KW = B_HEADS * B_DK
B_VW = B_HEADS * B_DV
C_W = C_GROUPS * C_GROUP_DIM
EVEN_IN = A_QW + 2 * A_KVW + 2 * B_KW + 2 * B_VW
ODD_IN = 2 * C_W + 2 * D_CHANNELS
N_GROUPS = 4
EXPERTS_PER_GROUP = 8
N_EXPERTS = N_GROUPS * EXPERTS_PER_GROUP
TOP_K = 2
D_EXPERT = 512
DN_ALPHA = (2 * DEPTH) ** 0.25
LN_EPS = 1e-5
RMS_EPS = 1e-6

LANES_V7X = 128
SUBLANES_V7X = 8
MIB = 1024 * 1024
VMEM_LIMIT_V7X = 56 * MIB

HGRN_CHUNK = 128
HGRN_SUB = 8
HGRN_GROUP = 32
HGRN_ROWS = 512
MOE_TB = 256
ROUTE_W = LANES_V7X
HALF = D_MODEL // 2
TOK_TILE = HALF // LANES_V7X
assert TOK_TILE == SUBLANES_V7X

BF16 = jnp.bfloat16
F32 = jnp.float32
U32 = jnp.uint32
NEG_INF = float("-inf")
LOG2E = math.log2(math.e)
RT_W0, RT_W1, RT_E0, RT_E1, RT_R0, RT_R1 = range(6)


def _cparams(n_axes, vmem_bytes=None):
    return pltpu.CompilerParams(dimension_semantics=("arbitrary",) * n_axes, vmem_limit_bytes=vmem_bytes)


def _layer_norm_rows(z, g, b):
    mu = jnp.mean(z, axis=-1, keepdims=True)
    zc = z - mu
    var = jnp.mean(zc * zc, axis=-1, keepdims=True)
    return zc * lax.rsqrt(var + LN_EPS) * g + b


def _dot_nt(a, b):
    return lax.dot_general(a, b, (((1,), (1,)), ((), ())), preferred_element_type=F32)


def _dot_tn(a, b):
    return lax.dot_general(a, b, (((0,), (0,)), ((), ())), preferred_element_type=F32)


def _pack_rows(y):
    lo = lax.bitcast_convert_type(y[:, :HALF].astype(BF16).astype(F32), U32) >> 16
    hi = lax.bitcast_convert_type(y[:, HALF:].astype(BF16).astype(F32), U32) & jnp.uint32(0xFFFF0000)
    return lo | hi


def _unpack_rows(p):
    lo = lax.bitcast_convert_type(p << 16, F32)
    hi = lax.bitcast_convert_type(p & jnp.uint32(0xFFFF0000), F32)
    return jnp.concatenate([lo, hi], axis=1)


def _store_token_tiles(ref, base, packed):
    rows = packed.shape[0]
    for c in range(TOK_TILE):
        ref[pl.ds(base + c, rows, stride=TOK_TILE), :] = packed[:, c * LANES_V7X:(c + 1) * LANES_V7X]


def _load_token_tiles(ref, base, rows, lead=None):
    parts = []
    for c in range(TOK_TILE):
        idx = (pl.ds(base + c, rows, stride=TOK_TILE), slice(None))
        parts.append(ref[idx] if lead is None else ref[(lead,) + idx])
    return jnp.concatenate(parts, axis=1)


def _mm_kernel(a_ref, w_ref, o_ref):
    a = a_ref[...].astype(BF16)
    o_ref[...] = jnp.dot(a, w_ref[...], preferred_element_type=F32).astype(o_ref.dtype)


def _matmul(a, w, tm, tn, out_dtype):
    m, k = a.shape
    n = w.shape[1]
    return pl.pallas_call(
        _mm_kernel,
        grid=(n // tn, m // tm),
        in_specs=[pl.BlockSpec((tm, k), lambda j, i: (i, 0)), pl.BlockSpec((k, tn), lambda j, i: (0, j))],
        out_specs=pl.BlockSpec((tm, tn), lambda j, i: (i, j)),
        out_shape=jax.ShapeDtypeStruct((m, n), out_dtype),
        compiler_params=_cparams(2, VMEM_LIMIT_V7X),
        name="proj_matmul",
    )(a, w)


def _bias_table_kernel(rb_ref, bucket_ref, o_ref):
    bucket = bucket_ref[...]
    for h in range(A_HEADS):
        acc = jnp.zeros(bucket.shape, F32)
        for bk in range(REL_BUCKETS):
            acc = jnp.where(bucket == bk, rb_ref[bk, h], acc)
        o_ref[h] = acc


def _bias_table(rel_bias, bucket):
    return pl.pallas_call(
        _bias_table_kernel,
        in_specs=[pl.BlockSpec(memory_space=pltpu.SMEM), pl.BlockSpec(memory_space=pltpu.VMEM)],
        out_specs=pl.BlockSpec(memory_space=pltpu.VMEM),
        out_shape=jax.ShapeDtypeStruct((A_HEADS,) + bucket.shape, F32),
        name="rel_bias_table",
    )(rel_bias, bucket)


def _t5_bucket_table():
    t_loc = jnp.arange(A_BLOCK, dtype=jnp.int32)[:, None]
    j_loc = jnp.arange(A_BLOCK, dtype=jnp.int32)[None, :]
    dist = jnp.where(j_loc <= t_loc, t_loc - j_loc, t_loc + A_BLOCK - j_loc)
    max_exact = REL_BUCKETS // 2
    d = jnp.maximum(dist, 1).astype(F32)
    large = max_exact + (jnp.log(d / max_exact) / math.log(REL_MAX_DIST / max_exact)
                         * (REL_BUCKETS - max_exact)).astype(jnp.int32)
    large = jnp.minimum(large, REL_BUCKETS - 1)
    return jnp.where(dist < max_exact, dist, large)


assert A_WINDOW == A_BLOCK


ATTN_BLOCKS_PER_STEP = 4


def _attn_kernel(q_ref, kc_ref, vc_ref, kp_ref, vp_ref, bias_ref, sink_ref, o_ref):
    blk = A_BLOCK
    first_pair = pl.program_id(1) == 0
    for sb in range(ATTN_BLOCKS_PER_STEP):
        rows = slice(sb * blk, (sb + 1) * blk)
        before = slice((sb - 1) * blk, sb * blk)
        k_prev, v_prev = (kp_ref[...], vp_ref[...]) if sb == 0 else (kc_ref[before, :], vc_ref[before, :])
        no_prev = jnp.where(first_pair, NEG_INF, 0.0) if sb == 0 else 0.0
        _attn_block(q_ref, rows, k_prev, v_prev, kc_ref[rows, :], vc_ref[rows, :], no_prev, bias_ref, sink_ref, o_ref)


def _attn_block(q_ref, rows, k_prev, v_prev, k_own, v_own, no_prev, bias_ref, sink_ref, o_ref):
    blk = A_BLOCK
    k2 = jnp.concatenate([k_prev, k_own], axis=0)
    v2 = jnp.concatenate([v_prev, v_own], axis=0)
    k2r = pltpu.roll(k2, A_HEAD_DIM, 1)
    v2r = pltpu.roll(v2, A_HEAD_DIM, 1)
    lo = lax.broadcasted_iota(jnp.int32, k2.shape, 1) < A_HEAD_DIM
    zero = jnp.zeros_like(k2)

    def placed(x, xr, g, par):
        src = x if g == par else xr
        return (jnp.where(lo, src, zero) if par == 0 else jnp.where(lo, zero, src)).astype(BF16)

    kk = [[placed(k2, k2r, g, par) for par in range(2)] for g in range(A_KV_HEADS)]
    vv = [[placed(v2, v2r, g, par) for par in range(2)] for g in range(A_KV_HEADS)]

    own = (lax.broadcasted_iota(jnp.int32, (blk, blk), 1) <= lax.broadcasted_iota(jnp.int32, (blk, blk), 0))
    zero_p = jnp.zeros((blk, blk), F32)

    heads_per_kv = A_HEADS // A_KV_HEADS
    for p in range(A_HEADS // 2):
        g = (2 * p) // heads_per_kv
        qp = (q_ref[rows, p * 128:(p + 1) * 128] * (A_HEAD_DIM ** -0.5)).astype(BF16)
        acc = jnp.zeros((blk, 128), F32)
        for par in range(2):
            h = 2 * p + par
            sink = sink_ref[h]
            s2 = _dot_nt(qp, kk[g][par])
            logits = jnp.where(own, s2[:, blk:], s2[:, :blk] + no_prev) + bias_ref[h]
            m = jnp.maximum(jnp.max(logits, axis=-1, keepdims=True), sink)
            e = jnp.exp(logits - m)
            den = jnp.sum(e, axis=-1, keepdims=True) + jnp.exp(sink - m)
            e2 = jnp.concatenate([jnp.where(own, zero_p, e), jnp.where(own, e, zero_p)], axis=1).astype(BF16)
            acc = acc + jnp.dot(e2, vv[g][par], preferred_element_type=F32) * (1.0 / den)
        o_ref[rows, p * 128:(p + 1) * 128] = acc.astype(o_ref.dtype)


def _attention(proj, bias, sinks, bsz, seq):
    per = ATTN_BLOCKS_PER_STEP
    npair = seq // (per * A_BLOCK)
    kcol = A_QW // 128
    vcol = kcol + 1
    row = lambda b, m: b * npair + m
    prow = lambda b, m: (b * npair + m) * per - jnp.where(m > 0, 1, 0)
    return pl.pallas_call(
        _attn_kernel,
        grid=(bsz, npair),
        in_specs=[
            pl.BlockSpec((per * A_BLOCK, A_QW), lambda b, m: (row(b, m), 0)),
            pl.BlockSpec((per * A_BLOCK, 128), lambda b, m: (row(b, m), kcol)),
            pl.BlockSpec((per * A_BLOCK, 128), lambda b, m: (row(b, m), vcol)),
            pl.BlockSpec((A_BLOCK, 128), lambda b, m: (prow(b, m), kcol)),
            pl.BlockSpec((A_BLOCK, 128), lambda b, m: (prow(b, m), vcol)),
            pl.BlockSpec((A_HEADS, A_BLOCK, A_BLOCK), lambda b, m: (0, 0, 0)),
            pl.BlockSpec(memory_space=pltpu.SMEM),
        ],
        out_specs=pl.BlockSpec((per * A_BLOCK, A_QW), lambda b, m: (row(b, m), 0)),
        out_shape=jax.ShapeDtypeStruct((bsz * seq, A_QW), BF16),
        compiler_params=_cparams(2, 32 * MIB),
        name="swa_attention",
    )(proj, proj, proj, proj, proj, bias, sinks)


HGRN_HEADS_PER_STEP = 4
HGRN_IN_BLOCKS = HGRN_HEADS_PER_STEP // 2


def _hgrn_kernel(*refs):
    nin = HGRN_IN_BLOCKS
    q_refs, f_refs, i_refs, g_refs = (refs[k * nin:(k + 1) * nin] for k in range(4))
    lb_ref, ng_ref, o_ref, st_ref = refs[4 * nin:]
    in_cols = lambda hh: slice((hh % 2) * 128, (hh % 2 + 1) * 128)
    c, sub, grp = HGRN_CHUNK, HGRN_SUB, HGRN_GROUP
    ngrp = c // grp

    @pl.when(pl.program_id(2) == 0)
    def _():
        st_ref[...] = jnp.zeros_like(st_ref)

    rid = lax.broadcasted_iota(jnp.int32, (c, c), 0)
    cid = lax.broadcasted_iota(jnp.int32, (c, c), 1)
    grp_start = (rid // grp) * grp
    sub_start = (rid // sub) * sub
    m_cum = jnp.concatenate([(cid < grp_start).astype(F32),
                             ((cid >= grp_start) & (cid < sub_start)).astype(F32),
                             ((cid >= sub_start) & (cid <= rid)).astype(F32)], axis=0).astype(BF16)
    band = jnp.where(cid >= sub_start, rid - cid, -1)
    same_grp = (cid // grp) == (rid // grp)
    heads = range(HGRN_HEADS_PER_STEP)
    zeros_bf = lambda rows: jnp.zeros((rows, B_DK), BF16)

    def scores(hh, r0):
        cols = slice(hh * 128, (hh + 1) * 128)
        lb = lb_ref[0, :, cols]
        q = q_refs[hh // 2][pl.ds(r0, c), in_cols(hh)]
        fl = f_refs[hh // 2][pl.ds(r0, c), in_cols(hh)]
        qf = q * jax.nn.sigmoid(q)
        f = lb + (1.0 - lb) * jax.nn.sigmoid(fl)
        kin = 1.0 - f
        logf = jnp.log(f)
        l1 = logf.astype(BF16)
        res = logf - l1.astype(F32)
        l2 = res.astype(BF16)
        l3 = (res - l2.astype(F32)).astype(BF16)
        cum = jnp.dot(m_cum, jnp.concatenate([l1, l2, l3], axis=1), preferred_element_type=F32)
        cum = cum[:, :B_DK] + cum[:, B_DK:2 * B_DK] + cum[:, 2 * B_DK:]
        rg, rs, bq = cum[:c], cum[c:2 * c], cum[2 * c:]
        lk = jnp.log(kin)
        wq = lk - bq
        wg = wq - rs
        wb = wg - rg
        qt = qf * jnp.exp(bq)
        qg = qt * jnp.exp(rs)
        o = _dot_nt((qg * jnp.exp(rg)).astype(BF16), st_ref[hh].astype(BF16))

        kts = []
        for i in range(1, ngrp):
            n = i * grp
            kts += [jnp.exp(rg[n:n + 1, :] + wb[:n, :]).astype(BF16), zeros_bf(c - n)]
        g_far = _dot_nt(qg.astype(BF16), jnp.concatenate(kts, axis=0))
        s_far = jnp.concatenate([jnp.zeros((grp, c), F32)] +
                                [g_far[i * grp:(i + 1) * grp, (i - 1) * c:i * c] for i in range(1, ngrp)], axis=0)

        kts = []
        for j in range(1, grp // sub):
            for gi in range(ngrp):
                a, n = gi * grp, j * sub
                kts += [jnp.exp(rs[a + n:a + n + 1, :] + wg[a:a + n, :]).astype(BF16), zeros_bf(grp - n)]
        g_near = _dot_nt(qt.astype(BF16), jnp.concatenate(kts, axis=0))
        pieces = []
        for i in range(c // sub):
            j = i % (grp // sub)
            pieces.append(jnp.zeros((sub, c), F32) if j == 0 else g_near[i * sub:(i + 1) * sub, (j - 1) * c:j * c])
        s = jnp.where(same_grp, jnp.concatenate(pieces, axis=0), s_far)

        bq2, wq2 = bq * LOG2E, wq * LOG2E
        kds = [kin.astype(BF16)] + [jnp.exp2(pltpu.roll(bq2, c - d, 0) + wq2).astype(BF16) for d in range(1, sub)]
        g_diag = _dot_nt(qf.astype(BF16), jnp.concatenate(kds, axis=0))
        last = slice(c - 1, c)
        return dict(o=o, s=s, g_diag=g_diag, wb=wb, b_last=rg[last, :] + rs[last, :] + bq[last, :])

    def finish(hh, r0, h):
        cols = slice(hh * 128, (hh + 1) * 128)
        vb = i_refs[hh // 2][pl.ds(r0, c), in_cols(hh)].astype(BF16)
        gt = g_refs[hh // 2][pl.ds(r0, c), in_cols(hh)]
        o = h["o"] + jnp.dot(h["s"].astype(BF16), vb, preferred_element_type=F32)
        st_ref[hh] = (st_ref[hh] * jnp.exp(h["b_last"])
                      + _dot_tn(vb, jnp.exp(h["b_last"] + h["wb"]).astype(BF16)))
        o = o * lax.rsqrt(jnp.mean(o * o, axis=-1, keepdims=True) + RMS_EPS)
        o_ref[pl.ds(r0, c), cols] = (o * ng_ref[0, :, cols] * (gt * jax.nn.sigmoid(gt))).astype(o_ref.dtype)

    def chunk(ci, carry):
        r0 = pl.multiple_of(ci * c, c)
        hs = [scores(hh, r0) for hh in heads]
        for d in range(sub):
            on_diag = band == d
            for h in hs:
                h["s"] = jnp.where(on_diag, h["g_diag"][:, d * c:(d + 1) * c], h["s"])
        for hh in heads:
            finish(hh, r0, hs[hh])
        return carry

    lax.fori_loop(0, HGRN_ROWS // c, chunk, 0, unroll=True)


def _hgrn(proj, lb, norm_g, bsz, seq):
    nr = seq // HGRN_ROWS
    hp, nin = HGRN_HEADS_PER_STEP, HGRN_IN_BLOCKS
    width = hp * 128
    c0 = (A_QW + 2 * A_KVW) // 256
    per_tensor = B_HEADS // 2
    nhp = B_HEADS // hp
    in_specs = [pl.BlockSpec((HGRN_ROWS, 256),
                             lambda b, h, r, t=t, p=p: (b * nr + r, c0 + t * per_tensor + h * nin + p))
                for t in range(4) for p in range(nin)]
    vec = pl.BlockSpec((1, 1, width), lambda b, h, r: (h, 0, 0))
    return pl.pallas_call(
        _hgrn_kernel,
        grid=(bsz, nhp, nr),
        in_specs=in_specs + [vec, vec],
        out_specs=pl.BlockSpec((HGRN_ROWS, width), lambda b, h, r: (b * nr + r, h)),
        out_shape=jax.ShapeDtypeStruct((bsz * seq, B_VW), BF16),
        scratch_shapes=[pltpu.VMEM((hp, B_DV, B_DK), F32)],
        compiler_params=_cparams(3, 32 * MIB),
        name="hgrn2",
    )(*([proj] * (4 * nin)), lb.reshape(nhp, 1, width), norm_g.reshape(nhp, 1, width))


OUTPROJ_LAG = 2


def _outproj_kernel(ya_ref, yb_ref, wa_ref, wb_ref, x_ref, g_ref, b_ref, wr_ref, rb_ref,
                    xo_ref, xp_ref, rt_ref, cnt_ref, run_ref, lg_ref, mix_ref):
    step = pl.program_id(0)

    @pl.when(step == 0)
    def _():
        run_ref[...] = jnp.zeros_like(run_ref)
        lg_ref[...] = jnp.zeros_like(lg_ref)
        mix_ref[...] = jnp.zeros_like(mix_ref)

    lg_prev = lg_ref[...]
    mix_prev = mix_ref[...]
    mix = jnp.dot(ya_ref[...], wa_ref[...], preferred_element_type=F32)
    mix_ref[...] = mix + jnp.dot(yb_ref[...], wb_ref[...], preferred_element_type=F32)

    y = _layer_norm_rows(DN_ALPHA * x_ref[...] + mix_prev, g_ref[...], b_ref[...])
    xo_ref[...] = y
    _store_token_tiles(xp_ref, 0, _pack_rows(y))
    tm = y.shape[0]
    y_hi = y.astype(BF16)
    y_lo = (y - y_hi.astype(F32)).astype(BF16)
    prod = jnp.dot(jnp.concatenate([y_hi, y_lo], axis=0), wr_ref[...], preferred_element_type=F32)
    lg = (prod[:tm, :ROUTE_W] + prod[tm:, :ROUTE_W]) + (prod[:tm, ROUTE_W:] + prod[tm:, ROUTE_W:])
    lg_ref[...] = lg + rb_ref[...]
    _route_rows(lg_prev, (step >= OUTPROJ_LAG).astype(F32), rt_ref, cnt_ref, run_ref)


def _outproj_ln_route(ya, yb, w_out, x, g, b, wr, rbias, tm=512):
    n = x.shape[0]
    ka = ya.shape[1]
    nsteps = n // tm
    lagged = lambda lag: (lambda i: (jnp.clip(i - lag, 0, nsteps - 1), 0))
    row = lambda width, lag: pl.BlockSpec((tm, width), lagged(lag))
    once = pl.Buffered(1)
    const = lambda shape: pl.BlockSpec(shape, lambda i: (0, 0), pipeline_mode=once)
    return pl.pallas_call(
        _outproj_kernel,
        grid=(nsteps + OUTPROJ_LAG,),
        in_specs=[row(ka, 0), row(ka, 0),
                  pl.BlockSpec((ka, D_MODEL), lambda i: (0, 0), pipeline_mode=once),
                  pl.BlockSpec((ka, D_MODEL), lambda i: (1, 0), pipeline_mode=once),
                  row(D_MODEL, 1), const((1, D_MODEL)), const((1, D_MODEL)),
                  const((D_MODEL, 2 * ROUTE_W)), const((1, ROUTE_W))],
        out_specs=[row(D_MODEL, 1), pl.BlockSpec((tm * TOK_TILE, LANES_V7X), lagged(1)),
                   row(ROUTE_W, OUTPROJ_LAG), pl.BlockSpec((1, ROUTE_W), lambda i: (0, 0))],
        out_shape=[jax.ShapeDtypeStruct((n, D_MODEL), F32),
                   jax.ShapeDtypeStruct((n * TOK_TILE, LANES_V7X), U32),
                   jax.ShapeDtypeStruct((n, ROUTE_W), F32), jax.ShapeDtypeStruct((1, ROUTE_W), F32)],
        scratch_shapes=[pltpu.VMEM((1, ROUTE_W), F32), pltpu.VMEM((tm, ROUTE_W), F32),
                        pltpu.VMEM((tm, D_MODEL), F32)],
        compiler_params=_cparams(1, VMEM_LIMIT_V7X),
        name="outproj_ln_route",
    )(ya, yb, w_out, w_out, x, g.reshape(1, -1), b.reshape(1, -1), wr, rbias)


def _router_weights(w_group, b_group, w_router, b_router):
    w = jnp.zeros((D_MODEL, ROUTE_W), F32)
    w = w.at[:, :N_GROUPS].set(w_group).at[:, N_GROUPS:N_GROUPS + N_EXPERTS].set(w_router)
    w_hi = w.astype(BF16)
    w_lo = (w - w_hi.astype(F32)).astype(BF16)
    rb = jnp.zeros((1, ROUTE_W), F32)
    rb = rb.at[0, :N_GROUPS].set(b_group).at[0, N_GROUPS:N_GROUPS + N_EXPERTS].set(b_router)
    return jnp.concatenate([w_hi, w_lo], axis=1), rb


def _route_rows(lg, live, rt_ref, cnt_ref, run_ref):
    tm = lg.shape[0]
    lane = lax.broadcasted_iota(jnp.int32, lg.shape, 1)
    sentinel = jnp.int32(ROUTE_W)
    rowmax = lambda mask: jnp.max(jnp.where(mask, lg, NEG_INF), axis=-1, keepdims=True)
    first = lambda mask: jnp.min(jnp.where(mask, lane, sentinel), axis=-1, keepdims=True)

    is_g = lane < N_GROUPS
    gmax = rowmax(is_g)
    g_idx = first(is_g & (lg == gmax))
    g_w = 1.0 / jnp.sum(jnp.where(is_g, jnp.exp(lg - gmax), 0.0), axis=-1, keepdims=True)

    e_lane = lane - N_GROUPS
    sel = (e_lane >= 0) & (e_lane < N_EXPERTS) & ((e_lane >> 3) == g_idx)
    m1 = rowmax(sel)
    i1 = first(sel & (lg == m1))
    sel2 = sel & (lane != i1)
    m2 = rowmax(sel2)
    i2 = first(sel2 & (lg == m2))
    ex = jnp.exp(m2 - m1)
    w0 = g_w / (1.0 + ex)
    w1 = g_w * ex / (1.0 + ex)

    oh0 = lane == i1
    oh1 = lane == i2
    both = (oh0 | oh1).astype(F32) * live
    rid = lax.broadcasted_iota(jnp.int32, (tm, tm), 0)
    cid = lax.broadcasted_iota(jnp.int32, (tm, tm), 1)
    before = jnp.dot((cid < rid).astype(BF16), both.astype(BF16), preferred_element_type=F32) + run_ref[...]
    rank0 = jnp.sum(jnp.where(oh0, before, 0.0), axis=-1, keepdims=True)
    rank1 = jnp.sum(jnp.where(oh1, before, 0.0), axis=-1, keepdims=True)
    run = run_ref[...] + jnp.sum(both, axis=0, keepdims=True)
    run_ref[...] = run
    cnt_ref[...] = run

    slab = jnp.zeros(lg.shape, F32)
    for ln, val in ((RT_W0, w0), (RT_W1, w1), (RT_E0, (i1 - N_GROUPS).astype(F32)),
                    (RT_E1, (i2 - N_GROUPS).astype(F32)), (RT_R0, rank0), (RT_R1, rank1)):
        slab = jnp.where(lane == ln, val, slab)
    rt_ref[...] = slab


assert EXPERTS_PER_GROUP == 8


def _block_layout(rt, cnt, n_tok):
    m = n_tok * TOP_K
    counts = cnt[0, N_GROUPS:N_GROUPS + N_EXPERTS].astype(jnp.int32)
    pcounts = (counts + MOE_TB - 1) // MOE_TB * MOE_TB
    pends = jnp.cumsum(pcounts)
    pstarts = pends - pcounts
    n_blocks = -(-(m + N_EXPERTS * (MOE_TB - 1)) // MOE_TB)
    e_idx = rt[:, RT_E0:RT_E1 + 1].astype(jnp.int32)
    rank = rt[:, RT_R0:RT_R1 + 1].astype(jnp.int32)
    onehot = e_idx[:, :, None] == jnp.arange(N_EXPERTS, dtype=jnp.int32)
    pos = jnp.sum(jnp.where(onehot, pstarts, 0), axis=-1) + rank
    blk_start = jnp.arange(n_blocks, dtype=jnp.int32) * MOE_TB
    blk_e = jnp.minimum(jnp.sum(blk_start[:, None] >= pends[None, :], axis=-1), N_EXPERTS - 1).astype(jnp.int32)
    nused = (pends[-1:] // MOE_TB).astype(jnp.int32)
    ids = jnp.arange(N_EXPERTS, dtype=jnp.int32)
    later_used = (ids[None, :] > ids[:, None]) & (counts[None, :] > 0)
    next_used = jnp.min(jnp.where(later_used, ids[None, :], N_EXPERTS), axis=-1)
    next_used = jnp.where(next_used == N_EXPERTS, -1, next_used).astype(jnp.int32)
    nxt_e = jnp.sum(jnp.where(blk_e[:, None] == ids[None, :], next_used[None, :], 0), axis=-1).astype(jnp.int32)
    ends_expert = jnp.any(((blk_start + MOE_TB)[:, None] == pends[None, :]) & (pcounts[None, :] > 0), axis=-1)
    zero_blk = (ends_expert | (blk_start >= pends[-1])).astype(jnp.int32)
    return pos, blk_e, nxt_e, nused, zero_blk


def _step_indices(pos, tm):
    nsteps = pos.shape[0] // tm
    return pos.reshape(nsteps, tm, TOP_K).transpose(0, 2, 1).reshape(nsteps, 1, TOP_K * tm)


def _tile_rows(i):
    return pl.ds(pl.multiple_of(i * TOK_TILE, TOK_TILE), TOK_TILE)


def _dispatch_kernel(zero_blk_ref, pos_ref, xp_ref, xs_hbm, xbuf, zbuf, sem, zsem, *, tm, n_blocks):
    s = pl.program_id(0)
    nsteps = pl.num_programs(0)
    slot = s % 2
    blk_rows = MOE_TB * TOK_TILE

    def copy(j, dst, sl):
        return pltpu.make_async_copy(xbuf.at[sl, _tile_rows(j)], xs_hbm.at[_tile_rows(dst)], sem.at[sl])

    def drain(sl):
        def body(j, c):
            copy(0, 0, sl).wait()
            return c
        lax.fori_loop(0, TOP_K * tm, body, 0, unroll=8)

    @pl.when(s == 0)
    def _():
        zbuf[...] = jnp.zeros_like(zbuf)

        def zero_copy(blk):
            rows = pl.ds(pl.multiple_of(blk * blk_rows, blk_rows), blk_rows)
            return pltpu.make_async_copy(zbuf, xs_hbm.at[rows], zsem)

        def each_flagged(fn):
            def body(blk, c):
                @pl.when(zero_blk_ref[blk] > 0)
                def _():
                    fn(zero_copy(blk))
                return c
            lax.fori_loop(0, n_blocks, body, 0)

        each_flagged(lambda cp: cp.start())
        each_flagged(lambda cp: cp.wait())

    @pl.when(s >= 2)
    def _():
        drain(slot)

    xbuf[slot] = xp_ref[...]

    for j in range(tm):
        for k in range(TOP_K):
            copy(j, pos_ref[0, 0, k * tm + j], slot).start(priority=k)

    @pl.when(s == nsteps - 1)
    def _():
        drain(1 - slot)
        drain(slot)


def _moe_dispatch(xp, pos, zero_blk, tm=512):
    n = pos.shape[0]
    n_blocks = zero_blk.shape[0]
    assert n // tm >= 2
    pos3 = _step_indices(pos, tm)
    grid_spec = pltpu.PrefetchScalarGridSpec(
        num_scalar_prefetch=1,
        grid=(n // tm,),
        in_specs=[pl.BlockSpec((1, 1, TOP_K * tm), lambda i, zb: (i, 0, 0), memory_space=pltpu.SMEM),
                  pl.BlockSpec((tm * TOK_TILE, LANES_V7X), lambda i, zb: (i, 0))],
        out_specs=pl.BlockSpec(memory_space=pl.ANY),
        scratch_shapes=[pltpu.VMEM((2, tm * TOK_TILE, LANES_V7X), U32),
                        pltpu.VMEM((MOE_TB * TOK_TILE, LANES_V7X), U32),
                        pltpu.SemaphoreType.DMA((2,)), pltpu.SemaphoreType.DMA(())],
    )
    return pl.pallas_call(
        functools.partial(_dispatch_kernel, tm=tm, n_blocks=n_blocks),
        grid_spec=grid_spec,
        out_shape=jax.ShapeDtypeStruct((n_blocks * MOE_TB * TOK_TILE, LANES_V7X), U32),
        compiler_params=_cparams(1),
        name="moe_dispatch",
    )(zero_blk, pos3, xp)


def _moe_kernel(blk_e_ref, nxt_e_ref, nused_ref, xs_ref, w1_hbm, w3_hbm, w2_hbm, ys_ref,
                wf1, wf3, wf2, w1b, w3b, w2b, slot_ref, sem, *, layer):
    s = pl.program_id(0)
    nused = nused_ref[0]

    def fetch(e, sl):
        return [pltpu.make_async_copy(w_hbm.at[layer, e], wf.at[sl], sem.at[sl, k])
                for k, (w_hbm, wf) in enumerate(((w1_hbm, wf1), (w3_hbm, wf3), (w2_hbm, wf2)))]

    @pl.when(s >= nused)
    def _():
        ys_ref[...] = jnp.zeros_like(ys_ref)

    @pl.when(s < nused)
    def _():
        e = blk_e_ref[s]
        prev = blk_e_ref[jnp.maximum(s - 1, 0)]

        @pl.when(s == 0)
        def _():
            slot_ref[0] = 0
            for cp in fetch(e, 0):
                cp.start()

        @pl.when((s > 0) & (e != prev))
        def _():
            slot_ref[0] = 1 - slot_ref[0]

        @pl.when((s == 0) | (e != prev))
        def _():
            sl = slot_ref[0]
            for cp in fetch(e, sl):
                cp.wait()
            w1b[...] = wf1[sl].astype(BF16)
            w3b[...] = wf3[sl].astype(BF16)
            w2b[...] = wf2[sl].astype(BF16)
            nxt = nxt_e_ref[s]

            @pl.when(nxt >= 0)
            def _():
                for cp in fetch(nxt, 1 - sl):
                    cp.start()

        xb = _unpack_rows(_load_token_tiles(xs_ref, 0, MOE_TB)).astype(BF16)
        h1 = jnp.dot(xb, w1b[...], preferred_element_type=F32)
        h3 = jnp.dot(xb, w3b[...], preferred_element_type=F32)
        h = (h1 * jax.nn.sigmoid(h1) * h3).astype(BF16)
        y = jnp.dot(h, w2b[...], preferred_element_type=F32)
        _store_token_tiles(ys_ref, 0, _pack_rows(y))


def _moe_experts(xs, blk_e, nxt_e, nused, w1, w3, w2, layer):
    n_blocks = blk_e.shape[0]
    tiles = pl.BlockSpec((MOE_TB * TOK_TILE, LANES_V7X), lambda s, be, ne, nu: (s, 0))
    hbm = pl.BlockSpec(memory_space=pl.ANY)
    up, down = (D_MODEL, D_EXPERT), (D_EXPERT, D_MODEL)
    grid_spec = pltpu.PrefetchScalarGridSpec(
        num_scalar_prefetch=3,
        grid=(n_blocks,),
        in_specs=[tiles, hbm, hbm, hbm],
        out_specs=tiles,
        scratch_shapes=[pltpu.VMEM((2,) + up, F32), pltpu.VMEM((2,) + up, F32), pltpu.VMEM((2,) + down, F32),
                        pltpu.VMEM(up, BF16), pltpu.VMEM(up, BF16), pltpu.VMEM(down, BF16),
                        pltpu.SMEM((1,), jnp.int32), pltpu.SemaphoreType.DMA((2, 3))],
    )
    return pl.pallas_call(
        functools.partial(_moe_kernel, layer=layer),
        grid_spec=grid_spec,
        out_shape=jax.ShapeDtypeStruct(xs.shape, U32),
        compiler_params=_cparams(1, VMEM_LIMIT_V7X),
        name="moe_experts",
    )(blk_e, nxt_e, nused, xs, w1, w3, w2)


def _combine_kernel(pos_ref, nxt_ref, ys_hbm, x_ref, rt_ref, g_ref, b_ref, o_ref, ybuf, sem, *, tm):
    s = pl.program_id(0)
    nsteps = pl.num_programs(0)
    slot = s % 2

    def copy(src, j, sl):
        return pltpu.make_async_copy(ys_hbm.at[_tile_rows(src)], ybuf.at[sl, _tile_rows(j)], sem.at[sl])

    def start_gather(idx_ref, sl):
        def body(j, c):
            copy(idx_ref[0, 0, j], j, sl).start()
            return c
        lax.fori_loop(0, TOP_K * tm, body, 0, unroll=8)

    @pl.when(s == 0)
    def _():
        start_gather(pos_ref, 0)

    @pl.when(s + 1 < nsteps)
    def _():
        for j in range(TOP_K * tm):
            copy(nxt_ref[0, 0, j], j, 1 - slot).start()

    def wait_body(j, c):
        copy(0, j, slot).wait()
        return c
    lax.fori_loop(0, TOP_K * tm, wait_body, 0, unroll=8)

    rt = rt_ref[...]
    ffn = _unpack_rows(_load_token_tiles(ybuf, 0, tm, lead=slot)) * rt[:, RT_W0:RT_W0 + 1]
    ffn = ffn + _unpack_rows(_load_token_tiles(ybuf, tm * TOK_TILE, tm, lead=slot)) * rt[:, RT_W1:RT_W1 + 1]
    o_ref[...] = _layer_norm_rows(DN_ALPHA * x_ref[...] + ffn, g_ref[...], b_ref[...])


def _moe_combine(ys, pos, rt, x, g, b, tm=256):
    n = x.shape[0]
    nsteps = n // tm
    pos3 = _step_indices(pos, tm)
    smem_blk = lambda f: pl.BlockSpec((1, 1, TOP_K * tm), f, memory_space=pltpu.SMEM)
    row = lambda width: pl.BlockSpec((tm, width), lambda i: (i, 0))
    const = lambda shape: pl.BlockSpec(shape, lambda i: (0, 0))
    return pl.pallas_call(
        functools.partial(_combine_kernel, tm=tm),
        grid=(nsteps,),
        in_specs=[smem_blk(lambda i: (i, 0, 0)), smem_blk(lambda i: (jnp.minimum(i + 1, nsteps - 1), 0, 0)),
                  pl.BlockSpec(memory_space=pl.ANY), row(D_MODEL), row(ROUTE_W),
                  const((1, D_MODEL)), const((1, D_MODEL))],
        out_specs=row(D_MODEL),
        out_shape=jax.ShapeDtypeStruct((n, D_MODEL), F32),
        scratch_shapes=[pltpu.VMEM((2, TOP_K * tm * TOK_TILE, LANES_V7X), U32), pltpu.SemaphoreType.DMA((2,))],
        compiler_params=_cparams(1, 40 * MIB),
        name="moe_combine",
    )(pos3, pos3, ys, x, rt, g.reshape(1, -1), b.reshape(1, -1))


def _moe_layer(x1, xp, rt, cnt, w1, w3, w2, layer, g, b):
    n_tok = x1.shape[0]
    pos, blk_e, nxt_e, nused, zero_blk = _block_layout(rt, cnt, n_tok)
    xs = _moe_dispatch(xp, pos, zero_blk)
    ys = _moe_experts(xs, blk_e, nxt_e, nused, w1, w3, w2, layer)
    return _moe_combine(ys, pos, rt, x1, g, b)


def _gmlp_kernel(u_ref, v_ref, g_ref, b_ref, w_ref, bs_ref, o_ref, *, chunks):
    for ci in range(chunks):
        rows = slice(ci * C_CHUNK, (ci + 1) * C_CHUNK)
        u = jax.nn.gelu(u_ref[rows, :])
        v = _layer_norm_rows(jax.nn.gelu(v_ref[rows, :]), g_ref[...], b_ref[...]).astype(BF16)
        for gi in range(C_GROUPS):
            cols = slice(gi * C_GROUP_DIM, (gi + 1) * C_GROUP_DIM)
            mixed = jnp.dot(w_ref[gi], v[:, cols], preferred_element_type=F32) + bs_ref[:, cols]
            o_ref[rows, cols] = (u[:, cols] * mixed).astype(o_ref.dtype)


def _gmlp(proj, ln_g, ln_b, w_s, b_s, chunks=4):
    n = proj.shape[0]
    tm = chunks * C_CHUNK
    w = (w_s * jnp.tril(jnp.ones((C_CHUNK, C_CHUNK), w_s.dtype))).astype(BF16)
    bs_full = jnp.repeat(b_s.T, C_GROUP_DIM, axis=1)
    const2 = lambda shape: pl.BlockSpec(shape, lambda i: (0, 0))
    return pl.pallas_call(
        functools.partial(_gmlp_kernel, chunks=chunks),
        grid=(n // tm,),
        in_specs=[pl.BlockSpec((tm, C_W), lambda i: (i, 0)), pl.BlockSpec((tm, C_W), lambda i: (i, 1)),
                  const2((1, C_W)), const2((1, C_W)),
                  pl.BlockSpec((C_GROUPS, C_CHUNK, C_CHUNK), lambda i: (0, 0, 0)), const2((C_CHUNK, C_W))],
        out_specs=pl.BlockSpec((tm, C_W), lambda i: (i, 0)),
        out_shape=jax.ShapeDtypeStruct((n, C_W), BF16),
        compiler_params=_cparams(1, 32 * MIB),
        name="gmlp_gating",
    )(proj, proj, ln_g.reshape(1, -1), ln_b.reshape(1, -1), w, bs_full)


CONV_HIST = 32


def _conv_kernel(a_ref, gt_ref, ap_ref, gp_ref, w_ref, cb_ref, g_ref, b_ref, o_ref, hbuf, hshift, *, ts):
    i = pl.program_id(1)
    hist = ap_ref[...] * jax.nn.sigmoid(gp_ref[...])
    hbuf[0:CONV_HIST, :] = jnp.where(i > 0, hist, jnp.zeros_like(hist))
    hbuf[CONV_HIST:CONV_HIST + ts, :] = a_ref[...] * jax.nn.sigmoid(gt_ref[...])
    off = CONV_HIST - (D_CONV - 1)
    acc = jnp.zeros((ts, D_CHANNELS), F32) + cb_ref[...]
    for r in range(SUBLANES_V7X):
        taps = [j for j in range(D_CONV) if (off + j) % SUBLANES_V7X == r]
        if not taps:
            continue
        src = hbuf
        if r:
            span = max(taps) + off - r + ts
            hshift[0:span, :] = hbuf[r:r + span, :]
            src = hshift
        for j in taps:
            base = off + j - r
            acc = acc + w_ref[j:j + 1, :] * src[base:base + ts, :]
    y = _layer_norm_rows(acc, g_ref[...], b_ref[...])
    o_ref[...] = (y * jax.nn.sigmoid(y)).astype(o_ref.dtype)


def _conformer_conv(proj, conv_w, conv_b, ln_g, ln_b, bsz, seq, ts=512):
    nt = seq // ts
    acol = 2 * C_W // D_CHANNELS
    gcol = acol + 1
    hb = ts // CONV_HIST
    cur = lambda col: pl.BlockSpec((ts, D_CHANNELS), lambda b, i: (b * nt + i, col))
    prev = lambda col: pl.BlockSpec((CONV_HIST, D_CHANNELS),
                                    lambda b, i: (jnp.maximum((b * nt + i) * hb - 1, 0), col))
    const2 = lambda shape: pl.BlockSpec(shape, lambda b, i: (0, 0))
    return pl.pallas_call(
        functools.partial(_conv_kernel, ts=ts),
        grid=(bsz, nt),
        in_specs=[cur(acol), cur(gcol), prev(acol), prev(gcol),
                  const2((D_CONV, D_CHANNELS)), const2((1, D_CHANNELS)), const2((1, D_CHANNELS)),
                  const2((1, D_CHANNELS))],
        out_specs=pl.BlockSpec((ts, D_CHANNELS), lambda b, i: (b * nt + i, 0)),
        out_shape=jax.ShapeDtypeStruct((bsz * seq, D_CHANNELS), BF16),
        scratch_shapes=[pltpu.VMEM((CONV_HIST + ts, D_CHANNELS), F32), pltpu.VMEM((CONV_HIST + ts, D_CHANNELS), F32)],
        compiler_params=_cparams(2, 32 * MIB),
        name="conformer_conv",
    )(proj, proj, proj, proj, conv_w, conv_b.reshape(1, -1), ln_g.reshape(1, -1), ln_b.reshape(1, -1))


def kernel(x, w_in_ab, attn_sinks, rel_bias, hgrn_lb_logits, hgrn_norm_g, w_out_ab, w_in_cd, gmlp_ln_g, gmlp_ln_b, gmlp_w_s, gmlp_b_s, conv_w, conv_b, conv_ln_g, conv_ln_b, w_out_cd, ln_mix_g, ln_mix_b, ln_ffn_g, ln_ffn_b, moe_w_group, moe_b_group, moe_w_router, moe_b_router, moe_w1, moe_w3, moe_w2):
    bsz, seq = x.shape[0], x.shape[1]
    n_tok = bsz * seq
    xf = x.reshape(n_tok, D_MODEL)
    lb_table = jnp.cumsum(jax.nn.softmax(hgrn_lb_logits.astype(F32), axis=0), axis=0)
    bias = _bias_table(rel_bias.astype(F32), _t5_bucket_table())

    for layer in range(DEPTH):
        j = layer // 2
        if layer % 2 == 0:
            proj = _matmul(xf, w_in_ab[j].astype(BF16), 1024, EVEN_IN // 3, F32)
            ya = _attention(proj, bias, attn_sinks[j].astype(F32), bsz, seq)
            yb = _hgrn(proj, lb_table[layer], hgrn_norm_g[j].astype(F32), bsz, seq)
            w_out = w_out_ab[j]
        else:
            proj = _matmul(xf, w_in_cd[j].astype(BF16), 1024, ODD_IN // 2, F32)
            ya = _gmlp(proj, gmlp_ln_g[j], gmlp_ln_b[j], gmlp_w_s[j], gmlp_b_s[j])
            yb = _conformer_conv(proj, conv_w[j], conv_b[j], conv_ln_g[j], conv_ln_b[j], bsz, seq)
            w_out = w_out_cd[j]
        wr, rbias = _router_weights(moe_w_group[layer], moe_b_group[layer],
                                    moe_w_router[layer], moe_b_router[layer])
        x1, xp, rt, cnt = _outproj_ln_route(ya, yb, w_out.astype(BF16), xf, ln_mix_g[layer], ln_mix_b[layer],
                                            wr, rbias)
        xf = _moe_layer(x1, xp, rt, cnt, moe_w1, moe_w3, moe_w2, layer, ln_ffn_g[layer], ln_ffn_b[layer])
    return xf.reshape(bsz, seq, D_MODEL)
```

```python
import functools
import math

import jax
import jax.numpy as jnp
from jax import lax
from jax.experimental import pallas as pl
from jax.experimental.pallas import tpu as pltpu

D_MODEL = 2048
DEPTH = 2
A_HEADS = 16
A_KV_HEADS = 2
A_HEAD_DIM = 64
A_WINDOW = 128
A_BLOCK = 128
REL_BUCKETS = 32
REL_MAX_DIST = 128
B_HEADS = 8
B_DK = 128
B_DV = 128
C_GROUPS = 8
C_GROUP_DIM = 128
C_CHUNK = 128
D_CHANNELS = 1024
D_CONV = 31
A_QW = A_HEADS * A_HEAD_DIM
A_KVW = A_KV_HEADS * A_HEAD_DIM
B_KW = B_HEADS * B_DK
B_VW = B_HEADS * B_DV
C_W = C_GROUPS * C_GROUP_DIM
EVEN_IN = A_QW + 2 * A_KVW + 2 * B_KW + 2 * B_VW
ODD_IN = 2 * C_W + 2 * D_CHANNELS
N_GROUPS = 4
EXPERTS_PER_GROUP = 8
N_EXPERTS = N_GROUPS * EXPERTS_PER_GROUP
TOP_K = 2
D_EXPERT = 512
DN_ALPHA = (2 * DEPTH) ** 0.25
LN_EPS = 1e-5
RMS_EPS = 1e-6

LANES_V7X = 128
SUBLANES_V7X = 8
MIB = 1024 * 1024
VMEM_LIMIT_V7X = 56 * MIB

HGRN_CHUNK = 128
HGRN_SUB = 8
HGRN_GROUP = 32
HGRN_ROWS = 512
MOE_TB = 256
ROUTE_W = LANES_V7X
HALF = D_MODEL // 2
TOK_TILE = HALF // LANES_V7X
assert TOK_TILE == SUBLANES_V7X

BF16 = jnp.bfloat16
F32 = jnp.float32
U32 = jnp.uint32
NEG_INF = float("-inf")
LOG2E = math.log2(math.e)
RT_W0, RT_W1, RT_E0, RT_E1, RT_R0, RT_R1 = range(6)
ROUTE_FIELDS = SUBLANES_V7X


def _cparams(n_axes, vmem_bytes=None):
    return pltpu.CompilerParams(dimension_semantics=("arbitrary",) * n_axes, vmem_limit_bytes=vmem_bytes)


def _layer_norm_rows(z, g, b):
    mu = jnp.mean(z, axis=-1, keepdims=True)
    zc = z - mu
    var = jnp.mean(zc * zc, axis=-1, keepdims=True)
    return zc * lax.rsqrt(var + LN_EPS) * g + b


def _dot_nt(a, b):
    return lax.dot_general(a, b, (((1,), (1,)), ((), ())), preferred_element_type=F32)


def _dot_tn(a, b):
    return lax.dot_general(a, b, (((0,), (0,)), ((), ())), preferred_element_type=F32)


def _pack_rows(y):
    lo = lax.bitcast_convert_type(y[:, :HALF].astype(BF16).astype(F32), U32) >> 16
    hi = lax.bitcast_convert_type(y[:, HALF:].astype(BF16).astype(F32), U32) & jnp.uint32(0xFFFF0000)
    return lo | hi


def _unpack_rows(p):
    lo = lax.bitcast_convert_type(p << 16, F32)
    hi = lax.bitcast_convert_type(p & jnp.uint32(0xFFFF0000), F32)
    return jnp.concatenate([lo, hi], axis=1)


def _store_token_tiles(ref, base, packed):
    rows = packed.shape[0]
    for c in range(TOK_TILE):
        ref[pl.ds(base + c, rows, stride=TOK_TILE), :] = packed[:, c * LANES_V7X:(c + 1) * LANES_V7X]


def _load_token_tiles(ref, base, rows, lead=None):
    parts = []
    for c in range(TOK_TILE):
        idx = (pl.ds(base + c, rows, stride=TOK_TILE), slice(None))
        parts.append(ref[idx] if lead is None else ref[(lead,) + idx])
    return jnp.concatenate(parts, axis=1)


def _mm_kernel(a_ref, w_ref, o_ref):
    a = a_ref[...].astype(BF16)
    o_ref[...] = jnp.dot(a, w_ref[...], preferred_element_type=F32).astype(o_ref.dtype)


def _matmul(a, w, tm, tn, out_dtype):
    m, k = a.shape
    n = w.shape[1]
    return pl.pallas_call(
        _mm_kernel,
        grid=(n // tn, m // tm),
        in_specs=[pl.BlockSpec((tm, k), lambda j, i: (i, 0)), pl.BlockSpec((k, tn), lambda j, i: (0, j))],
        out_specs=pl.BlockSpec((tm, tn), lambda j, i: (i, j)),
        out_shape=jax.ShapeDtypeStruct((m, n), out_dtype),
        compiler_params=_cparams(2, VMEM_LIMIT_V7X),
        name="proj_matmul",
    )(a, w)


def _bias_table_kernel(rb_ref, bucket_ref, o_ref):
    bucket = bucket_ref[...]
    for h in range(A_HEADS):
        acc = jnp.zeros(bucket.shape, F32)
        for bk in range(REL_BUCKETS):
            acc = jnp.where(bucket == bk, rb_ref[bk, h], acc)
        o_ref[h] = acc


def _bias_table(rel_bias, bucket):
    return pl.pallas_call(
        _bias_table_kernel,
        in_specs=[pl.BlockSpec(memory_space=pltpu.SMEM), pl.BlockSpec(memory_space=pltpu.VMEM)],
        out_specs=pl.BlockSpec(memory_space=pltpu.VMEM),
        out_shape=jax.ShapeDtypeStruct((A_HEADS,) + bucket.shape, F32),
        name="rel_bias_table",
    )(rel_bias, bucket)


def _t5_bucket_table():
    t_loc = jnp.arange(A_BLOCK, dtype=jnp.int32)[:, None]
    j_loc = jnp.arange(A_BLOCK, dtype=jnp.int32)[None, :]
    dist = jnp.where(j_loc <= t_loc, t_loc - j_loc, t_loc + A_BLOCK - j_loc)
    max_exact = REL_BUCKETS // 2
    d = jnp.maximum(dist, 1).astype(F32)
    large = max_exact + (jnp.log(d / max_exact) / math.log(REL_MAX_DIST / max_exact)
                         * (REL_BUCKETS - max_exact)).astype(jnp.int32)
    large = jnp.minimum(large, REL_BUCKETS - 1)
    return jnp.where(dist < max_exact, dist, large)


assert A_WINDOW == A_BLOCK


ATTN_BLOCKS_PER_STEP = 4


def _attn_kernel(q_ref, kc_ref, vc_ref, kp_ref, vp_ref, bias_ref, sink_ref, o_ref):
    blk = A_BLOCK
    first_pair = pl.program_id(1) == 0
    for sb in range(ATTN_BLOCKS_PER_STEP):
        rows = slice(sb * blk, (sb + 1) * blk)
        before = slice((sb - 1) * blk, sb * blk)
        k_prev, v_prev = (kp_ref[...], vp_ref[...]) if sb == 0 else (kc_ref[before, :], vc_ref[before, :])
        no_prev = jnp.where(first_pair, NEG_INF, 0.0) if sb == 0 else 0.0
        _attn_block(q_ref, rows, k_prev, v_prev, kc_ref[rows, :], vc_ref[rows, :], no_prev, bias_ref, sink_ref, o_ref)


def _attn_block(q_ref, rows, k_prev, v_prev, k_own, v_own, no_prev, bias_ref, sink_ref, o_ref):
    blk = A_BLOCK
    k2 = jnp.concatenate([k_prev, k_own], axis=0)
    v2 = jnp.concatenate([v_prev, v_own], axis=0)
    k2r = pltpu.roll(k2, A_HEAD_DIM, 1)
    v2r = pltpu.roll(v2, A_HEAD_DIM, 1)
    lo = lax.broadcasted_iota(jnp.int32, k2.shape, 1) < A_HEAD_DIM
    zero = jnp.zeros_like(k2)

    def placed(x, xr, g, par):
        src = x if g == par else xr
        return (jnp.where(lo, src, zero) if par == 0 else jnp.where(lo, zero, src)).astype(BF16)

    kk = [[placed(k2, k2r, g, par) for par in range(2)] for g in range(A_KV_HEADS)]
    vv = [[placed(v2, v2r, g, par) for par in range(2)] for g in range(A_KV_HEADS)]

    own = (lax.broadcasted_iota(jnp.int32, (blk, blk), 1) <= lax.broadcasted_iota(jnp.int32, (blk, blk), 0))
    zero_p = jnp.zeros((blk, blk), F32)

    heads_per_kv = A_HEADS // A_KV_HEADS
    for p in range(A_HEADS // 2):
        g = (2 * p) // heads_per_kv
        qp = (q_ref[rows, p * 128:(p + 1) * 128] * (A_HEAD_DIM ** -0.5)).astype(BF16)
        acc = jnp.zeros((blk, 128), F32)
        for par in range(2):
            h = 2 * p + par
            sink = sink_ref[h]
            s2 = _dot_nt(qp, kk[g][par])
            logits = jnp.where(own, s2[:, blk:], s2[:, :blk] + no_prev) + bias_ref[h]
            m = jnp.maximum(jnp.max(logits, axis=-1, keepdims=True), sink)
            e = jnp.exp(logits - m)
            den = jnp.sum(e, axis=-1, keepdims=True) + jnp.exp(sink - m)
            e2 = jnp.concatenate([jnp.where(own, zero_p, e), jnp.where(own, e, zero_p)], axis=1).astype(BF16)
            acc = acc + jnp.dot(e2, vv[g][par], preferred_element_type=F32) * (1.0 / den)
        o_ref[rows, p * 128:(p + 1) * 128] = acc.astype(o_ref.dtype)


def _attention(proj, bias, sinks, bsz, seq):
    per = ATTN_BLOCKS_PER_STEP
    npair = seq // (per * A_BLOCK)
    kcol = A_QW // 128
    vcol = kcol + 1
    row = lambda b, m: b * npair + m
    prow = lambda b, m: (b * npair + m) * per - jnp.where(m > 0, 1, 0)
    return pl.pallas_call(
        _attn_kernel,
        grid=(bsz, npair),
        in_specs=[
            pl.BlockSpec((per * A_BLOCK, A_QW), lambda b, m: (row(b, m), 0)),
            pl.BlockSpec((per * A_BLOCK, 128), lambda b, m: (row(b, m), kcol)),
            pl.BlockSpec((per * A_BLOCK, 128), lambda b, m: (row(b, m), vcol)),
            pl.BlockSpec((A_BLOCK, 128), lambda b, m: (prow(b, m), kcol)),
            pl.BlockSpec((A_BLOCK, 128), lambda b, m: (prow(b, m), vcol)),
            pl.BlockSpec((A_HEADS, A_BLOCK, A_BLOCK), lambda b, m: (0, 0, 0)),
            pl.BlockSpec(memory_space=pltpu.SMEM),
        ],
        out_specs=pl.BlockSpec((per * A_BLOCK, A_QW), lambda b, m: (row(b, m), 0)),
        out_shape=jax.ShapeDtypeStruct((bsz * seq, A_QW), BF16),
        compiler_params=_cparams(2, 32 * MIB),
        name="swa_attention",
    )(proj, proj, proj, proj, proj, bias, sinks)


HGRN_HEADS_PER_STEP = 4
HGRN_IN_BLOCKS = HGRN_HEADS_PER_STEP // 2


def _hgrn_kernel(*refs):
    nin = HGRN_IN_BLOCKS
    q_refs, f_refs, i_refs, g_refs = (refs[k * nin:(k + 1) * nin] for k in range(4))
    lb_ref, ng_ref, o_ref, st_ref = refs[4 * nin:]
    in_cols = lambda hh: slice((hh % 2) * 128, (hh % 2 + 1) * 128)
    c, sub, grp = HGRN_CHUNK, HGRN_SUB, HGRN_GROUP
    ngrp = c // grp

    @pl.when(pl.program_id(2) == 0)
    def _():
        st_ref[...] = jnp.zeros_like(st_ref)

    rid = lax.broadcasted_iota(jnp.int32, (c, c), 0)
    cid = lax.broadcasted_iota(jnp.int32, (c, c), 1)
    grp_start = (rid // grp) * grp
    sub_start = (rid // sub) * sub
    m_cum = jnp.concatenate([(cid < grp_start).astype(F32),
                             ((cid >= grp_start) & (cid < sub_start)).astype(F32),
                             ((cid >= sub_start) & (cid <= rid)).astype(F32)], axis=0).astype(BF16)
    band = jnp.where(cid >= sub_start, rid - cid, -1)
    same_grp = (cid // grp) == (rid // grp)
    heads = range(HGRN_HEADS_PER_STEP)
    zeros_bf = lambda rows: jnp.zeros((rows, B_DK), BF16)

    def scores(hh, r0):
        cols = slice(hh * 128, (hh + 1) * 128)
        lb = lb_ref[0, :, cols]
        q = q_refs[hh // 2][pl.ds(r0, c), in_cols(hh)]
        fl = f_refs[hh // 2][pl.ds(r0, c), in_cols(hh)]
        qf = q * jax.nn.sigmoid(q)
        f = lb + (1.0 - lb) * jax.nn.sigmoid(fl)
        kin = 1.0 - f
        logf = jnp.log(f)
        l1 = logf.astype(BF16)
        res = logf - l1.astype(F32)
        l2 = res.astype(BF16)
        l3 = (res - l2.astype(F32)).astype(BF16)
        cum = jnp.dot(m_cum, jnp.concatenate([l1, l2, l3], axis=1), preferred_element_type=F32)
        cum = cum[:, :B_DK] + cum[:, B_DK:2 * B_DK] + cum[:, 2 * B_DK:]
        rg, rs, bq = cum[:c], cum[c:2 * c], cum[2 * c:]
        lk = jnp.log(kin)
        wq = lk - bq
        wg = wq - rs
        wb = wg - rg
        qt = qf * jnp.exp(bq)
        qg = qt * jnp.exp(rs)
        o = _dot_nt((qg * jnp.exp(rg)).astype(BF16), st_ref[hh].astype(BF16))

        kts = []
        for i in range(1, ngrp):
            n = i * grp
            kts += [jnp.exp(rg[n:n + 1, :] + wb[:n, :]).astype(BF16), zeros_bf(c - n)]
        g_far = _dot_nt(qg.astype(BF16), jnp.concatenate(kts, axis=0))
        s_far = jnp.concatenate([jnp.zeros((grp, c), F32)] +
                                [g_far[i * grp:(i + 1) * grp, (i - 1) * c:i * c] for i in range(1, ngrp)], axis=0)

        kts = []
        for j in range(1, grp // sub):
            for gi in range(ngrp):
                a, n = gi * grp, j * sub
                kts += [jnp.exp(rs[a + n:a + n + 1, :] + wg[a:a + n, :]).astype(BF16), zeros_bf(grp - n)]
        g_near = _dot_nt(qt.astype(BF16), jnp.concatenate(kts, axis=0))
        pieces = []
        for i in range(c // sub):
            j = i % (grp // sub)
            pieces.append(jnp.zeros((sub, c), F32) if j == 0 else g_near[i * sub:(i + 1) * sub, (j - 1) * c:j * c])
        s = jnp.where(same_grp, jnp.concatenate(pieces, axis=0), s_far)

        bq2, wq2 = bq * LOG2E, wq * LOG2E
        kds = [kin.astype(BF16)] + [jnp.exp2(pltpu.roll(bq2, c - d, 0) + wq2).astype(BF16) for d in range(1, sub)]
        g_diag = _dot_nt(qf.astype(BF16), jnp.concatenate(kds, axis=0))
        last = slice(c - 1, c)
        return dict(o=o, s=s, g_diag=g_diag, wb=wb, b_last=rg[last, :] + rs[last, :] + bq[last, :])

    def finish(hh, r0, h):
        cols = slice(hh * 128, (hh + 1) * 128)
        vb = i_refs[hh // 2][pl.ds(r0, c), in_cols(hh)].astype(BF16)
        gt = g_refs[hh // 2][pl.ds(r0, c), in_cols(hh)]
        o = h["o"] + jnp.dot(h["s"].astype(BF16), vb, preferred_element_type=F32)
        st_ref[hh] = (st_ref[hh] * jnp.exp(h["b_last"])
                      + _dot_tn(vb, jnp.exp(h["b_last"] + h["wb"]).astype(BF16)))
        o = o * lax.rsqrt(jnp.mean(o * o, axis=-1, keepdims=True) + RMS_EPS)
        o_ref[pl.ds(r0, c), cols] = (o * ng_ref[0, :, cols] * (gt * jax.nn.sigmoid(gt))).astype(o_ref.dtype)

    def chunk(ci, carry):
        r0 = pl.multiple_of(ci * c, c)
        hs = [scores(hh, r0) for hh in heads]
        for d in range(sub):
            on_diag = band == d
            for h in hs:
                h["s"] = jnp.where(on_diag, h["g_diag"][:, d * c:(d + 1) * c], h["s"])
        for hh in heads:
            finish(hh, r0, hs[hh])
        return carry

    lax.fori_loop(0, HGRN_ROWS // c, chunk, 0, unroll=True)


def _hgrn(proj, lb, norm_g, bsz, seq):
    nr = seq // HGRN_ROWS
    hp, nin = HGRN_HEADS_PER_STEP, HGRN_IN_BLOCKS
    width = hp * 128
    c0 = (A_QW + 2 * A_KVW) // 256
    per_tensor = B_HEADS // 2
    nhp = B_HEADS // hp
    in_specs = [pl.BlockSpec((HGRN_ROWS, 256),
                             lambda b, h, r, t=t, p=p: (b * nr + r, c0 + t * per_tensor + h * nin + p))
                for t in range(4) for p in range(nin)]
    vec = pl.BlockSpec((1, 1, width), lambda b, h, r: (h, 0, 0))
    return pl.pallas_call(
        _hgrn_kernel,
        grid=(bsz, nhp, nr),
        in_specs=in_specs + [vec, vec],
        out_specs=pl.BlockSpec((HGRN_ROWS, width), lambda b, h, r: (b * nr + r, h)),
        out_shape=jax.ShapeDtypeStruct((bsz * seq, B_VW), BF16),
        scratch_shapes=[pltpu.VMEM((hp, B_DV, B_DK), F32)],
        compiler_params=_cparams(3, 32 * MIB),
        name="hgrn2",
    )(*([proj] * (4 * nin)), lb.reshape(nhp, 1, width), norm_g.reshape(nhp, 1, width))


OUTPROJ_LAG = 2


def _outproj_kernel(ya_ref, yb_ref, wa_ref, wb_ref, x_ref, g_ref, b_ref, wr_ref, rb_ref,
                    xo_ref, xp_ref, rt_ref, rtt_ref, cnt_ref, run_ref, lg_ref, mix_ref):
    step = pl.program_id(0)

    @pl.when(step == 0)
    def _():
        run_ref[...] = jnp.zeros_like(run_ref)
        lg_ref[...] = jnp.zeros_like(lg_ref)
        mix_ref[...] = jnp.zeros_like(mix_ref)

    lg_prev = lg_ref[...]
    mix_prev = mix_ref[...]
    mix = jnp.dot(ya_ref[...], wa_ref[...], preferred_element_type=F32)
    mix_ref[...] = mix + jnp.dot(yb_ref[...], wb_ref[...], preferred_element_type=F32)

    y = _layer_norm_rows(DN_ALPHA * x_ref[...] + mix_prev, g_ref[...], b_ref[...])
    xo_ref[...] = y
    _store_token_tiles(xp_ref, 0, _pack_rows(y))
    tm = y.shape[0]
    y_hi = y.astype(BF16)
    y_lo = (y - y_hi.astype(F32)).astype(BF16)
    prod = jnp.dot(jnp.concatenate([y_hi, y_lo], axis=0), wr_ref[...], preferred_element_type=F32)
    lg = (prod[:tm, :ROUTE_W] + prod[tm:, :ROUTE_W]) + (prod[:tm, ROUTE_W:] + prod[tm:, ROUTE_W:])
    lg_ref[...] = lg + rb_ref[...]
    _route_rows(lg_prev, (step >= OUTPROJ_LAG).astype(F32), rt_ref, rtt_ref, cnt_ref, run_ref)


def _outproj_ln_route(ya, yb, w_out, x, g, b, wr, rbias, tm=512):
    n = x.shape[0]
    ka = ya.shape[1]
    nsteps = n // tm
    lagged = lambda lag: (lambda i: (jnp.clip(i - lag, 0, nsteps - 1), 0))
    row = lambda width, lag: pl.BlockSpec((tm, width), lagged(lag))
    once = pl.Buffered(1)
    const = lambda shape: pl.BlockSpec(shape, lambda i: (0, 0), pipeline_mode=once)
    return pl.pallas_call(
        _outproj_kernel,
        grid=(nsteps + OUTPROJ_LAG,),
        in_specs=[row(ka, 0), row(ka, 0),
                  pl.BlockSpec((ka, D_MODEL), lambda i: (0, 0), pipeline_mode=once),
                  pl.BlockSpec((ka, D_MODEL), lambda i: (1, 0), pipeline_mode=once),
                  row(D_MODEL, 1), const((1, D_MODEL)), const((1, D_MODEL)),
                  const((D_MODEL, 2 * ROUTE_W)), const((1, ROUTE_W))],
        out_specs=[row(D_MODEL, 1), pl.BlockSpec((tm * TOK_TILE, LANES_V7X), lagged(1)),
                   row(ROUTE_W, OUTPROJ_LAG),
                   pl.BlockSpec((ROUTE_FIELDS, tm), lambda i: (0, jnp.clip(i - OUTPROJ_LAG, 0, nsteps - 1))),
                   pl.BlockSpec((1, ROUTE_W), lambda i: (0, 0))],
        out_shape=[jax.ShapeDtypeStruct((n, D_MODEL), F32),
                   jax.ShapeDtypeStruct((n * TOK_TILE, LANES_V7X), U32),
                   jax.ShapeDtypeStruct((n, ROUTE_W), F32), jax.ShapeDtypeStruct((ROUTE_FIELDS, n), F32),
                   jax.ShapeDtypeStruct((1, ROUTE_W), F32)],
        scratch_shapes=[pltpu.VMEM((1, ROUTE_W), F32), pltpu.VMEM((tm, ROUTE_W), F32),
                        pltpu.VMEM((tm, D_MODEL), F32)],
        compiler_params=_cparams(1, VMEM_LIMIT_V7X),
        name="outproj_ln_route",
    )(ya, yb, w_out, w_out, x, g.reshape(1, -1), b.reshape(1, -1), wr, rbias)


def _router_weights(w_group, b_group, w_router, b_router):
    w = jnp.zeros((D_MODEL, ROUTE_W), F32)
    w = w.at[:, :N_GROUPS].set(w_group).at[:, N_GROUPS:N_GROUPS + N_EXPERTS].set(w_router)
    w_hi = w.astype(BF16)
    w_lo = (w - w_hi.astype(F32)).astype(BF16)
    rb = jnp.zeros((1, ROUTE_W), F32)
    rb = rb.at[0, :N_GROUPS].set(b_group).at[0, N_GROUPS:N_GROUPS + N_EXPERTS].set(b_router)
    return jnp.concatenate([w_hi, w_lo], axis=1), rb


def _route_rows(lg, live, rt_ref, rtt_ref, cnt_ref, run_ref):
    tm = lg.shape[0]
    lane = lax.broadcasted_iota(jnp.int32, lg.shape, 1)
    sentinel = jnp.int32(ROUTE_W)
    rowmax = lambda mask: jnp.max(jnp.where(mask, lg, NEG_INF), axis=-1, keepdims=True)
    first = lambda mask: jnp.min(jnp.where(mask, lane, sentinel), axis=-1, keepdims=True)

    is_g = lane < N_GROUPS
    gmax = rowmax(is_g)
    g_idx = first(is_g & (lg == gmax))
    g_w = 1.0 / jnp.sum(jnp.where(is_g, jnp.exp(lg - gmax), 0.0), axis=-1, keepdims=True)

    e_lane = lane - N_GROUPS
    sel = (e_lane >= 0) & (e_lane < N_EXPERTS) & ((e_lane >> 3) == g_idx)
    m1 = rowmax(sel)
    i1 = first(sel & (lg == m1))
    sel2 = sel & (lane != i1)
    m2 = rowmax(sel2)
    i2 = first(sel2 & (lg == m2))
    ex = jnp.exp(m2 - m1)
    w0 = g_w / (1.0 + ex)
    w1 = g_w * ex / (1.0 + ex)

    oh0 = lane == i1
    oh1 = lane == i2
    both = (oh0 | oh1).astype(F32) * live
    rid = lax.broadcasted_iota(jnp.int32, (tm, tm), 0)
    cid = lax.broadcasted_iota(jnp.int32, (tm, tm), 1)
    before = jnp.dot((cid < rid).astype(BF16), both.astype(BF16), preferred_element_type=F32) + run_ref[...]
    rank0 = jnp.sum(jnp.where(oh0, before, 0.0), axis=-1, keepdims=True)
    rank1 = jnp.sum(jnp.where(oh1, before, 0.0), axis=-1, keepdims=True)
    run = run_ref[...] + jnp.sum(both, axis=0, keepdims=True)
    run_ref[...] = run
    cnt_ref[...] = run

    slab = jnp.zeros(lg.shape, F32)
    for ln, val in ((RT_W0, w0), (RT_W1, w1), (RT_E0, (i1 - N_GROUPS).astype(F32)),
                    (RT_E1, (i2 - N_GROUPS).astype(F32)), (RT_R0, rank0), (RT_R1, rank1)):
        slab = jnp.where(lane == ln, val, slab)
    rt_ref[...] = slab
    rtt_ref[...] = jnp.transpose(slab)[:ROUTE_FIELDS, :]


assert EXPERTS_PER_GROUP == 8


def _block_layout(rtt, cnt, n_tok):
    m = n_tok * TOP_K
    counts = cnt[0, N_GROUPS:N_GROUPS + N_EXPERTS].astype(jnp.int32)
    pcounts = (counts + MOE_TB - 1) // MOE_TB * MOE_TB
    pends = jnp.cumsum(pcounts)
    pstarts = pends - pcounts
    n_blocks = -(-(m + N_EXPERTS * (MOE_TB - 1)) // MOE_TB)
    e_idx = rtt[RT_E0:RT_E1 + 1, :].astype(jnp.int32)
    rank = rtt[RT_R0:RT_R1 + 1, :].astype(jnp.int32)
    onehot = e_idx[:, None, :] == jnp.arange(N_EXPERTS, dtype=jnp.int32)[None, :, None]
    pos = jnp.sum(jnp.where(onehot, pstarts[None, :, None], 0), axis=1) + rank
    blk_start = jnp.arange(n_blocks, dtype=jnp.int32) * MOE_TB
    blk_e = jnp.minimum(jnp.sum(blk_start[:, None] >= pends[None, :], axis=-1), N_EXPERTS - 1).astype(jnp.int32)
    nused = (pends[-1:] // MOE_TB).astype(jnp.int32)
    ids = jnp.arange(N_EXPERTS, dtype=jnp.int32)
    later_used = (ids[None, :] > ids[:, None]) & (counts[None, :] > 0)
    next_used = jnp.min(jnp.where(later_used, ids[None, :], N_EXPERTS), axis=-1)
    next_used = jnp.where(next_used == N_EXPERTS, -1, next_used).astype(jnp.int32)
    nxt_e = jnp.sum(jnp.where(blk_e[:, None] == ids[None, :], next_used[None, :], 0), axis=-1).astype(jnp.int32)
    ends_expert = jnp.any(((blk_start + MOE_TB)[:, None] == pends[None, :]) & (pcounts[None, :] > 0), axis=-1)
    zero_blk = (ends_expert | (blk_start >= pends[-1])).astype(jnp.int32)
    return pos, blk_e, nxt_e, nused, zero_blk


def _step_indices(pos, tm):
    nsteps = pos.shape[1] // tm
    return pos.reshape(TOP_K, nsteps, tm).transpose(1, 0, 2).reshape(nsteps, 1, TOP_K * tm)


def _tile_rows(i):
    return pl.ds(pl.multiple_of(i * TOK_TILE, TOK_TILE), TOK_TILE)


def _dispatch_kernel(zero_blk_ref, pos_ref, xp_ref, xs_hbm, xbuf, zbuf, sem, zsem, *, tm, n_blocks):
    s = pl.program_id(0)
    nsteps = pl.num_programs(0)
    slot = s % 2
    blk_rows = MOE_TB * TOK_TILE

    def copy(j, dst, sl):
        return pltpu.make_async_copy(xbuf.at[sl, _tile_rows(j)], xs_hbm.at[_tile_rows(dst)], sem.at[sl])

    def drain(sl):
        def body(j, c):
            copy(0, 0, sl).wait()
            return c
        lax.fori_loop(0, TOP_K * tm, body, 0, unroll=8)

    @pl.when(s == 0)
    def _():
        zbuf[...] = jnp.zeros_like(zbuf)

        def zero_copy(blk):
            rows = pl.ds(pl.multiple_of(blk * blk_rows, blk_rows), blk_rows)
            return pltpu.make_async_copy(zbuf, xs_hbm.at[rows], zsem)

        def each_flagged(fn):
            def body(blk, c):
                @pl.when(zero_blk_ref[blk] > 0)
                def _():
                    fn(zero_copy(blk))
                return c
            lax.fori_loop(0, n_blocks, body, 0)

        each_flagged(lambda cp: cp.start())
        each_flagged(lambda cp: cp.wait())

    @pl.when(s >= 2)
    def _():
        drain(slot)

    xbuf[slot] = xp_ref[...]

    for j in range(tm):
        for k in range(TOP_K):
            copy(j, pos_ref[0, 0, k * tm + j], slot).start(priority=k)

    @pl.when(s == nsteps - 1)
    def _():
        drain(1 - slot)
        drain(slot)


def _moe_dispatch(xp, pos, zero_blk, tm=512):
    n = pos.shape[1]
    n_blocks = zero_blk.shape[0]
    assert n // tm >= 2
    pos3 = _step_indices(pos, tm)
    grid_spec = pltpu.PrefetchScalarGridSpec(
        num_scalar_prefetch=1,
        grid=(n // tm,),
        in_specs=[pl.BlockSpec((1, 1, TOP_K * tm), lambda i, zb: (i, 0, 0), memory_space=pltpu.SMEM),
                  pl.BlockSpec((tm * TOK_TILE, LANES_V7X), lambda i, zb: (i, 0))],
        out_specs=pl.BlockSpec(memory_space=pl.ANY),
        scratch_shapes=[pltpu.VMEM((2, tm * TOK_TILE, LANES_V7X), U32),
                        pltpu.VMEM((MOE_TB * TOK_TILE, LANES_V7X), U32),
                        pltpu.SemaphoreType.DMA((2,)), pltpu.SemaphoreType.DMA(())],
    )
    return pl.pallas_call(
        functools.partial(_dispatch_kernel, tm=tm, n_blocks=n_blocks),
        grid_spec=grid_spec,
        out_shape=jax.ShapeDtypeStruct((n_blocks * MOE_TB * TOK_TILE, LANES_V7X), U32),
        compiler_params=_cparams(1),
        name="moe_dispatch",
    )(zero_blk, pos3, xp)


def _moe_kernel(blk_e_ref, nxt_e_ref, nused_ref, xs_ref, w1_hbm, w3_hbm, w2_hbm, ys_ref,
                wf1, wf3, wf2, w1b, w3b, w2b, slot_ref, sem, *, layer):
    s = pl.program_id(0)
    nused = nused_ref[0]

    def fetch(e, sl):
        return [pltpu.make_async_copy(w_hbm.at[layer, e], wf.at[sl], sem.at[sl, k])
                for k, (w_hbm, wf) in enumerate(((w1_hbm, wf1), (w3_hbm, wf3), (w2_hbm, wf2)))]

    @pl.when(s >= nused)
    def _():
        ys_ref[...] = jnp.zeros_like(ys_ref)

    @pl.when(s < nused)
    def _():
        e = blk_e_ref[s]
        prev = blk_e_ref[jnp.maximum(s - 1, 0)]

        @pl.when(s == 0)
        def _():
            slot_ref[0] = 0
            for cp in fetch(e, 0):
                cp.start()

        @pl.when((s > 0) & (e != prev))
        def _():
            slot_ref[0] = 1 - slot_ref[0]

        @pl.when((s == 0) | (e != prev))
        def _():
            sl = slot_ref[0]
            for cp in fetch(e, sl):
                cp.wait()
            w1b[...] = wf1[sl].astype(BF16)
            w3b[...] = wf3[sl].astype(BF16)
            w2b[...] = wf2[sl].astype(BF16)
            nxt = nxt_e_ref[s]

            @pl.when(nxt >= 0)
            def _():
                for cp in fetch(nxt, 1 - sl):
                    cp.start()

        xb = _unpack_rows(_load_token_tiles(xs_ref, 0, MOE_TB)).astype(BF16)
        h1 = jnp.dot(xb, w1b[...], preferred_element_type=F32)
        h3 = jnp.dot(xb, w3b[...], preferred_element_type=F32)
        h = (h1 * jax.nn.sigmoid(h1) * h3).astype(BF16)
        y = jnp.dot(h, w2b[...], preferred_element_type=F32)
        _store_token_tiles(ys_ref, 0, _pack_rows(y))


def _moe_experts(xs, blk_e, nxt_e, nused, w1, w3, w2, layer):
    n_blocks = blk_e.shape[0]
    tiles = pl.BlockSpec((MOE_TB * TOK_TILE, LANES_V7X), lambda s, be, ne, nu: (s, 0))
    hbm = pl.BlockSpec(memory_space=pl.ANY)
    up, down = (D_MODEL, D_EXPERT), (D_EXPERT, D_MODEL)
    grid_spec = pltpu.PrefetchScalarGridSpec(
        num_scalar_prefetch=3,
        grid=(n_blocks,),
        in_specs=[tiles, hbm, hbm, hbm],
        out_specs=tiles,
        scratch_shapes=[pltpu.VMEM((2,) + up, F32), pltpu.VMEM((2,) + up, F32), pltpu.VMEM((2,) + down, F32),
                        pltpu.VMEM(up, BF16), pltpu.VMEM(up, BF16), pltpu.VMEM(down, BF16),
                        pltpu.SMEM((1,), jnp.int32), pltpu.SemaphoreType.DMA((2, 3))],
    )
    return pl.pallas_call(
        functools.partial(_moe_kernel, layer=layer),
        grid_spec=grid_spec,
        out_shape=jax.ShapeDtypeStruct(xs.shape, U32),
        compiler_params=_cparams(1, VMEM_LIMIT_V7X),
        name="moe_experts",
    )(blk_e, nxt_e, nused, xs, w1, w3, w2)


def _combine_kernel(pos_ref, nxt_ref, ys_hbm, x_ref, rt_ref, g_ref, b_ref, o_ref, ybuf, sem, *, tm):
    s = pl.program_id(0)
    nsteps = pl.num_programs(0)
    slot = s % 2

    def copy(src, j, sl):
        return pltpu.make_async_copy(ys_hbm.at[_tile_rows(src)], ybuf.at[sl, _tile_rows(j)], sem.at[sl])

    def start_gather(idx_ref, sl):
        def body(j, c):
            copy(idx_ref[0, 0, j], j, sl).start()
            return c
        lax.fori_loop(0, TOP_K * tm, body, 0, unroll=8)

    @pl.when(s == 0)
    def _():
        start_gather(pos_ref, 0)

    @pl.when(s + 1 < nsteps)
    def _():
        for j in range(TOP_K * tm):
            copy(nxt_ref[0, 0, j], j, 1 - slot).start()

    def wait_body(j, c):
        copy(0, j, slot).wait()
        return c
    lax.fori_loop(0, TOP_K * tm, wait_body, 0, unroll=8)

    rt = rt_ref[...]
    ffn = _unpack_rows(_load_token_tiles(ybuf, 0, tm, lead=slot)) * rt[:, RT_W0:RT_W0 + 1]
    ffn = ffn + _unpack_rows(_load_token_tiles(ybuf, tm * TOK_TILE, tm, lead=slot)) * rt[:, RT_W1:RT_W1 + 1]
    o_ref[...] = _layer_norm_rows(DN_ALPHA * x_ref[...] + ffn, g_ref[...], b_ref[...])


def _moe_combine(ys, pos, rt, x, g, b, tm=256):
    n = x.shape[0]
    nsteps = n // tm
    pos3 = _step_indices(pos, tm)
    smem_blk = lambda f: pl.BlockSpec((1, 1, TOP_K * tm), f, memory_space=pltpu.SMEM)
    row = lambda width: pl.BlockSpec((tm, width), lambda i: (i, 0))
    const = lambda shape: pl.BlockSpec(shape, lambda i: (0, 0))
    return pl.pallas_call(
        functools.partial(_combine_kernel, tm=tm),
        grid=(nsteps,),
        in_specs=[smem_blk(lambda i: (i, 0, 0)), smem_blk(lambda i: (jnp.minimum(i + 1, nsteps - 1), 0, 0)),
                  pl.BlockSpec(memory_space=pl.ANY), row(D_MODEL), row(ROUTE_W),
                  const((1, D_MODEL)), const((1, D_MODEL))],
        out_specs=row(D_MODEL),
        out_shape=jax.ShapeDtypeStruct((n, D_MODEL), F32),
        scratch_shapes=[pltpu.VMEM((2, TOP_K * tm * TOK_TILE, LANES_V7X), U32), pltpu.SemaphoreType.DMA((2,))],
        compiler_params=_cparams(1, 40 * MIB),
        name="moe_combine",
    )(pos3, pos3, ys, x, rt, g.reshape(1, -1), b.reshape(1, -1))


def _moe_layer(x1, xp, rt, rtt, cnt, w1, w3, w2, layer, g, b):
    n_tok = x1.shape[0]
    pos, blk_e, nxt_e, nused, zero_blk = _block_layout(rtt, cnt, n_tok)
    xs = _moe_dispatch(xp, pos, zero_blk)
    ys = _moe_experts(xs, blk_e, nxt_e, nused, w1, w3, w2, layer)
    return _moe_combine(ys, pos, rt, x1, g, b)


def _gmlp_kernel(u_ref, v_ref, g_ref, b_ref, w_ref, bs_ref, o_ref, *, chunks):
    for ci in range(chunks):
        rows = slice(ci * C_CHUNK, (ci + 1) * C_CHUNK)
        u = jax.nn.gelu(u_ref[rows, :])
        v = _layer_norm_rows(jax.nn.gelu(v_ref[rows, :]), g_ref[...], b_ref[...]).astype(BF16)
        for gi in range(C_GROUPS):
            cols = slice(gi * C_GROUP_DIM, (gi + 1) * C_GROUP_DIM)
            mixed = jnp.dot(w_ref[gi], v[:, cols], preferred_element_type=F32) + bs_ref[:, cols]
            o_ref[rows, cols] = (u[:, cols] * mixed).astype(o_ref.dtype)


def _gmlp(proj, ln_g, ln_b, w_s, b_s, chunks=4):
    n = proj.shape[0]
    tm = chunks * C_CHUNK
    w = (w_s * jnp.tril(jnp.ones((C_CHUNK, C_CHUNK), w_s.dtype))).astype(BF16)
    bs_full = jnp.repeat(b_s.T, C_GROUP_DIM, axis=1)
    const2 = lambda shape: pl.BlockSpec(shape, lambda i: (0, 0))
    return pl.pallas_call(
        functools.partial(_gmlp_kernel, chunks=chunks),
        grid=(n // tm,),
        in_specs=[pl.BlockSpec((tm, C_W), lambda i: (i, 0)), pl.BlockSpec((tm, C_W), lambda i: (i, 1)),
                  const2((1, C_W)), const2((1, C_W)),
                  pl.BlockSpec((C_GROUPS, C_CHUNK, C_CHUNK), lambda i: (0, 0, 0)), const2((C_CHUNK, C_W))],
        out_specs=pl.BlockSpec((tm, C_W), lambda i: (i, 0)),
        out_shape=jax.ShapeDtypeStruct((n, C_W), BF16),
        compiler_params=_cparams(1, 32 * MIB),
        name="gmlp_gating",
    )(proj, proj, ln_g.reshape(1, -1), ln_b.reshape(1, -1), w, bs_full)


CONV_HIST = 32


def _conv_kernel(a_ref, gt_ref, ap_ref, gp_ref, w_ref, cb_ref, g_ref, b_ref, o_ref, hbuf, hshift, *, ts):
    i = pl.program_id(1)
    hist = ap_ref[...] * jax.nn.sigmoid(gp_ref[...])
    hbuf[0:CONV_HIST, :] = jnp.where(i > 0, hist, jnp.zeros_like(hist))
    hbuf[CONV_HIST:CONV_HIST + ts, :] = a_ref[...] * jax.nn.sigmoid(gt_ref[...])
    off = CONV_HIST - (D_CONV - 1)
    acc = jnp.zeros((ts, D_CHANNELS), F32) + cb_ref[...]
    for r in range(SUBLANES_V7X):
        taps = [j for j in range(D_CONV) if (off + j) % SUBLANES_V7X == r]
        if not taps:
            continue
        src = hbuf
        if r:
            span = max(taps) + off - r + ts
            hshift[0:span, :] = hbuf[r:r + span, :]
            src = hshift
        for j in taps:
            base = off + j - r
            acc = acc + w_ref[j:j + 1, :] * src[base:base + ts, :]
    y = _layer_norm_rows(acc, g_ref[...], b_ref[...])
    o_ref[...] = (y * jax.nn.sigmoid(y)).astype(o_ref.dtype)


def _conformer_conv(proj, conv_w, conv_b, ln_g, ln_b, bsz, seq, ts=512):
    nt = seq // ts
    acol = 2 * C_W // D_CHANNELS
    gcol = acol + 1
    hb = ts // CONV_HIST
    cur = lambda col: pl.BlockSpec((ts, D_CHANNELS), lambda b, i: (b * nt + i, col))
    prev = lambda col: pl.BlockSpec((CONV_HIST, D_CHANNELS),
                                    lambda b, i: (jnp.maximum((b * nt + i) * hb - 1, 0), col))
    const2 = lambda shape: pl.BlockSpec(shape, lambda b, i: (0, 0))
    return pl.pallas_call(
        functools.partial(_conv_kernel, ts=ts),
        grid=(bsz, nt),
        in_specs=[cur(acol), cur(gcol), prev(acol), prev(gcol),
                  const2((D_CONV, D_CHANNELS)), const2((1, D_CHANNELS)), const2((1, D_CHANNELS)),
                  const2((1, D_CHANNELS))],
        out_specs=pl.BlockSpec((ts, D_CHANNELS), lambda b, i: (b * nt + i, 0)),
        out_shape=jax.ShapeDtypeStruct((bsz * seq, D_CHANNELS), BF16),
        scratch_shapes=[pltpu.VMEM((CONV_HIST + ts, D_CHANNELS), F32), pltpu.VMEM((CONV_HIST + ts, D_CHANNELS), F32)],
        compiler_params=_cparams(2, 32 * MIB),
        name="conformer_conv",
    )(proj, proj, proj, proj, conv_w, conv_b.reshape(1, -1), ln_g.reshape(1, -1), ln_b.reshape(1, -1))


def kernel(x, w_in_ab, attn_sinks, rel_bias, hgrn_lb_logits, hgrn_norm_g, w_out_ab, w_in_cd, gmlp_ln_g, gmlp_ln_b, gmlp_w_s, gmlp_b_s, conv_w, conv_b, conv_ln_g, conv_ln_b, w_out_cd, ln_mix_g, ln_mix_b, ln_ffn_g, ln_ffn_b, moe_w_group, moe_b_group, moe_w_router, moe_b_router, moe_w1, moe_w3, moe_w2):
    bsz, seq = x.shape[0], x.shape[1]
    n_tok = bsz * seq
    xf = x.reshape(n_tok, D_MODEL)
    lb_table = jnp.cumsum(jax.nn.softmax(hgrn_lb_logits.astype(F32), axis=0), axis=0)
    bias = _bias_table(rel_bias.astype(F32), _t5_bucket_table())

    for layer in range(DEPTH):
        j = layer // 2
        if layer % 2 == 0:
            proj = _matmul(xf, w_in_ab[j].astype(BF16), 1024, EVEN_IN // 3, F32)
            ya = _attention(proj, bias, attn_sinks[j].astype(F32), bsz, seq)
            yb = _hgrn(proj, lb_table[layer], hgrn_norm_g[j].astype(F32), bsz, seq)
            w_out = w_out_ab[j]
        else:
            proj = _matmul(xf, w_in_cd[j].astype(BF16), 1024, ODD_IN // 2, F32)
            ya = _gmlp(proj, gmlp_ln_g[j], gmlp_ln_b[j], gmlp_w_s[j], gmlp_b_s[j])
            yb = _conformer_conv(proj, conv_w[j], conv_b[j], conv_ln_g[j], conv_ln_b[j], bsz, seq)
            w_out = w_out_cd[j]
        wr, rbias = _router_weights(moe_w_group[layer], moe_b_group[layer],
                                    moe_w_router[layer], moe_b_router[layer])
        x1, xp, rt, rtt, cnt = _outproj_ln_route(ya, yb, w_out.astype(BF16), xf, ln_mix_g[layer],
                                                 ln_mix_b[layer], wr, rbias)
        xf = _moe_layer(x1, xp, rt, rtt, cnt, moe_w1, moe_w3, moe_w2, layer, ln_ffn_g[layer], ln_ffn_b[layer])
    return xf.reshape(bsz, seq, D_MODEL)
```

```python
import functools
import math

import jax
import jax.numpy as jnp
from jax import lax
from jax.experimental import pallas as pl
from jax.experimental.pallas import tpu as pltpu

D_MODEL = 2048
DEPTH = 2
A_HEADS = 16
A_KV_HEADS = 2
A_HEAD_DIM = 64
A_WINDOW = 128
A_BLOCK = 128
REL_BUCKETS = 32
REL_MAX_DIST = 128
B_HEADS = 8
B_DK = 128
B_DV = 128
C_GROUPS = 8
C_GROUP_DIM = 128
C_CHUNK = 128
D_CHANNELS = 1024
D_CONV = 31
A_QW = A_HEADS * A_HEAD_DIM
A_KVW = A_KV_HEADS * A_HEAD_DIM
B_KW = B_HEADS * B_DK
B_VW = B_HEADS * B_DV
C_W = C_GROUPS * C_GROUP_DIM
EVEN_IN = A_QW + 2 * A_KVW + 2 * B_KW + 2 * B_VW
ODD_IN = 2 * C_W + 2 * D_CHANNELS
N_GROUPS = 4
EXPERTS_PER_GROUP = 8
N_EXPERTS = N_GROUPS * EXPERTS_PER_GROUP
TOP_K = 2
D_EXPERT = 512
DN_ALPHA = (2 * DEPTH) ** 0.25
LN_EPS = 1e-5
RMS_EPS = 1e-6

LANES_V7X = 128
SUBLANES_V7X = 8
MIB = 1024 * 1024
VMEM_LIMIT_V7X = 56 * MIB

HGRN_CHUNK = 128
HGRN_SUB = 8
HGRN_GROUP = 32
HGRN_ROWS = 512
MOE_TB = 256
ROUTE_W = LANES_V7X
HALF = D_MODEL // 2
TOK_TILE = HALF // LANES_V7X
assert TOK_TILE == SUBLANES_V7X

BF16 = jnp.bfloat16
F32 = jnp.float32
U32 = jnp.uint32
NEG_INF = float("-inf")
LOG2E = math.log2(math.e)
RT_W0, RT_W1, RT_E0, RT_E1, RT_R0, RT_R1 = range(6)
ROUTE_FIELDS = SUBLANES_V7X


def _cparams(n_axes, vmem_bytes=None):
    return pltpu.CompilerParams(dimension_semantics=("arbitrary",) * n_axes, vmem_limit_bytes=vmem_bytes)


def _layer_norm_rows(z, g, b):
    mu = jnp.mean(z, axis=-1, keepdims=True)
    zc = z - mu
    var = jnp.mean(zc * zc, axis=-1, keepdims=True)
    return zc * lax.rsqrt(var + LN_EPS) * g + b


def _dot_nt(a, b):
    return lax.dot_general(a, b, (((1,), (1,)), ((), ())), preferred_element_type=F32)


def _dot_tn(a, b):
    return lax.dot_general(a, b, (((0,), (0,)), ((), ())), preferred_element_type=F32)


def _pack_rows(y):
    lo = lax.bitcast_convert_type(y[:, :HALF].astype(BF16).astype(F32), U32) >> 16
    hi = lax.bitcast_convert_type(y[:, HALF:].astype(BF16).astype(F32), U32) & jnp.uint32(0xFFFF0000)
    return lo | hi


def _unpack_rows(p):
    lo = lax.bitcast_convert_type(p << 16, F32)
    hi = lax.bitcast_convert_type(p & jnp.uint32(0xFFFF0000), F32)
    return jnp.concatenate([lo, hi], axis=1)


def _store_token_tiles(ref, base, packed):
    rows = packed.shape[0]
    for c in range(TOK_TILE):
        ref[pl.ds(base + c, rows, stride=TOK_TILE), :] = packed[:, c * LANES_V7X:(c + 1) * LANES_V7X]


def _load_token_tiles(ref, base, rows, lead=None):
    parts = []
    for c in range(TOK_TILE):
        idx = (pl.ds(base + c, rows, stride=TOK_TILE), slice(None))
        parts.append(ref[idx] if lead is None else ref[(lead,) + idx])
    return jnp.concatenate(parts, axis=1)


def _mm_kernel(a_ref, w_ref, o_ref):
    a = a_ref[...].astype(BF16)
    o_ref[...] = jnp.dot(a, w_ref[...], preferred_element_type=F32).astype(o_ref.dtype)


def _matmul(a, w, tm, tn, out_dtype):
    m, k = a.shape
    n = w.shape[1]
    return pl.pallas_call(
        _mm_kernel,
        grid=(n // tn, m // tm),
        in_specs=[pl.BlockSpec((tm, k), lambda j, i: (i, 0)), pl.BlockSpec((k, tn), lambda j, i: (0, j))],
        out_specs=pl.BlockSpec((tm, tn), lambda j, i: (i, j)),
        out_shape=jax.ShapeDtypeStruct((m, n), out_dtype),
        compiler_params=_cparams(2, VMEM_LIMIT_V7X),
        name="proj_matmul",
    )(a, w)


def _bias_table_kernel(rb_ref, bucket_ref, o_ref):
    bucket = bucket_ref[...]
    for h in range(A_HEADS):
        acc = jnp.zeros(bucket.shape, F32)
        for bk in range(REL_BUCKETS):
            acc = jnp.where(bucket == bk, rb_ref[bk, h], acc)
        o_ref[h] = acc


def _bias_table(rel_bias, bucket):
    return pl.pallas_call(
        _bias_table_kernel,
        in_specs=[pl.BlockSpec(memory_space=pltpu.SMEM), pl.BlockSpec(memory_space=pltpu.VMEM)],
        out_specs=pl.BlockSpec(memory_space=pltpu.VMEM),
        out_shape=jax.ShapeDtypeStruct((A_HEADS,) + bucket.shape, F32),
        name="rel_bias_table",
    )(rel_bias, bucket)


def _t5_bucket_table():
    t_loc = jnp.arange(A_BLOCK, dtype=jnp.int32)[:, None]
    j_loc = jnp.arange(A_BLOCK, dtype=jnp.int32)[None, :]
    dist = jnp.where(j_loc <= t_loc, t_loc - j_loc, t_loc + A_BLOCK - j_loc)
    max_exact = REL_BUCKETS // 2
    d = jnp.maximum(dist, 1).astype(F32)
    large = max_exact + (jnp.log(d / max_exact) / math.log(REL_MAX_DIST / max_exact)
                         * (REL_BUCKETS - max_exact)).astype(jnp.int32)
    large = jnp.minimum(large, REL_BUCKETS - 1)
    return jnp.where(dist < max_exact, dist, large)


assert A_WINDOW == A_BLOCK


ATTN_BLOCKS_PER_STEP = 4


def _attn_kernel(q_ref, kc_ref, vc_ref, kp_ref, vp_ref, bias_ref, sink_ref, o_ref):
    blk = A_BLOCK
    first_pair = pl.program_id(1) == 0
    for sb in range(ATTN_BLOCKS_PER_STEP):
        rows = slice(sb * blk, (sb + 1) * blk)
        before = slice((sb - 1) * blk, sb * blk)
        k_prev, v_prev = (kp_ref[...], vp_ref[...]) if sb == 0 else (kc_ref[before, :], vc_ref[before, :])
        no_prev = jnp.where(first_pair, NEG_INF, 0.0) if sb == 0 else 0.0
        _attn_block(q_ref, rows, k_prev, v_prev, kc_ref[rows, :], vc_ref[rows, :], no_prev, bias_ref, sink_ref, o_ref)


def _attn_block(q_ref, rows, k_prev, v_prev, k_own, v_own, no_prev, bias_ref, sink_ref, o_ref):
    blk = A_BLOCK
    k2 = jnp.concatenate([k_prev, k_own], axis=0)
    v2 = jnp.concatenate([v_prev, v_own], axis=0)
    k2r = pltpu.roll(k2, A_HEAD_DIM, 1)
    v2r = pltpu.roll(v2, A_HEAD_DIM, 1)
    lo = lax.broadcasted_iota(jnp.int32, k2.shape, 1) < A_HEAD_DIM
    zero = jnp.zeros_like(k2)

    def placed(x, xr, g, par):
        src = x if g == par else xr
        return (jnp.where(lo, src, zero) if par == 0 else jnp.where(lo, zero, src)).astype(BF16)

    kk = [[placed(k2, k2r, g, par) for par in range(2)] for g in range(A_KV_HEADS)]
    vv = [[placed(v2, v2r, g, par) for par in range(2)] for g in range(A_KV_HEADS)]

    own = (lax.broadcasted_iota(jnp.int32, (blk, blk), 1) <= lax.broadcasted_iota(jnp.int32, (blk, blk), 0))
    zero_p = jnp.zeros((blk, blk), F32)

    heads_per_kv = A_HEADS // A_KV_HEADS
    for p in range(A_HEADS // 2):
        g = (2 * p) // heads_per_kv
        qp = (q_ref[rows, p * 128:(p + 1) * 128] * (A_HEAD_DIM ** -0.5)).astype(BF16)
        acc = jnp.zeros((blk, 128), F32)
        for par in range(2):
            h = 2 * p + par
            sink = sink_ref[h]
            s2 = _dot_nt(qp, kk[g][par])
            logits = jnp.where(own, s2[:, blk:], s2[:, :blk] + no_prev) + bias_ref[h]
            m = jnp.maximum(jnp.max(logits, axis=-1, keepdims=True), sink)
            e = jnp.exp(logits - m)
            den = jnp.sum(e, axis=-1, keepdims=True) + jnp.exp(sink - m)
            e2 = jnp.concatenate([jnp.where(own, zero_p, e), jnp.where(own, e, zero_p)], axis=1).astype(BF16)
            acc = acc + jnp.dot(e2, vv[g][par], preferred_element_type=F32) * (1.0 / den)
        o_ref[rows, p * 128:(p + 1) * 128] = acc.astype(o_ref.dtype)


def _attention(proj, bias, sinks, bsz, seq):
    per = ATTN_BLOCKS_PER_STEP
    npair = seq // (per * A_BLOCK)
    kcol = A_QW // 128
    vcol = kcol + 1
    row = lambda b, m: b * npair + m
    prow = lambda b, m: (b * npair + m) * per - jnp.where(m > 0, 1, 0)
    return pl.pallas_call(
        _attn_kernel,
        grid=(bsz, npair),
        in_specs=[
            pl.BlockSpec((per * A_BLOCK, A_QW), lambda b, m: (row(b, m), 0)),
            pl.BlockSpec((per * A_BLOCK, 128), lambda b, m: (row(b, m), kcol)),
            pl.BlockSpec((per * A_BLOCK, 128), lambda b, m: (row(b, m), vcol)),
            pl.BlockSpec((A_BLOCK, 128), lambda b, m: (prow(b, m), kcol)),
            pl.BlockSpec((A_BLOCK, 128), lambda b, m: (prow(b, m), vcol)),
            pl.BlockSpec((A_HEADS, A_BLOCK, A_BLOCK), lambda b, m: (0, 0, 0)),
            pl.BlockSpec(memory_space=pltpu.SMEM),
        ],
        out_specs=pl.BlockSpec((per * A_BLOCK, A_QW), lambda b, m: (row(b, m), 0)),
        out_shape=jax.ShapeDtypeStruct((bsz * seq, A_QW), BF16),
        compiler_params=_cparams(2, 32 * MIB),
        name="swa_attention",
    )(proj, proj, proj, proj, proj, bias, sinks)


HGRN_HEADS_PER_STEP = 4
HGRN_IN_BLOCKS = HGRN_HEADS_PER_STEP // 2


def _hgrn_kernel(*refs):
    nin = HGRN_IN_BLOCKS
    q_refs, f_refs, i_refs, g_refs = (refs[k * nin:(k + 1) * nin] for k in range(4))
    lb_ref, ng_ref, o_ref, st_ref = refs[4 * nin:]
    in_cols = lambda hh: slice((hh % 2) * 128, (hh % 2 + 1) * 128)
    c, sub, grp = HGRN_CHUNK, HGRN_SUB, HGRN_GROUP
    ngrp = c // grp

    @pl.when(pl.program_id(2) == 0)
    def _():
        st_ref[...] = jnp.zeros_like(st_ref)

    rid = lax.broadcasted_iota(jnp.int32, (c, c), 0)
    cid = lax.broadcasted_iota(jnp.int32, (c, c), 1)
    grp_start = (rid // grp) * grp
    sub_start = (rid // sub) * sub
    m_cum = jnp.concatenate([(cid < grp_start).astype(F32),
                             ((cid >= grp_start) & (cid < sub_start)).astype(F32),
                             ((cid >= sub_start) & (cid <= rid)).astype(F32)], axis=0).astype(BF16)
    band = jnp.where(cid >= sub_start, rid - cid, -1)
    same_grp = (cid // grp) == (rid // grp)
    heads = range(HGRN_HEADS_PER_STEP)
    zeros_bf = lambda rows: jnp.zeros((rows, B_DK), BF16)

    def scores(hh, r0):
        cols = slice(hh * 128, (hh + 1) * 128)
        lb = lb_ref[0, :, cols]
        q = q_refs[hh // 2][pl.ds(r0, c), in_cols(hh)]
        fl = f_refs[hh // 2][pl.ds(r0, c), in_cols(hh)]
        qf = q * jax.nn.sigmoid(q)
        f = lb + (1.0 - lb) * jax.nn.sigmoid(fl)
        kin = 1.0 - f
        logf = jnp.log(f)
        l1 = logf.astype(BF16)
        res = logf - l1.astype(F32)
        l2 = res.astype(BF16)
        l3 = (res - l2.astype(F32)).astype(BF16)
        cum = jnp.dot(m_cum, jnp.concatenate([l1, l2, l3], axis=1), preferred_element_type=F32)
        cum = cum[:, :B_DK] + cum[:, B_DK:2 * B_DK] + cum[:, 2 * B_DK:]
        rg, rs, bq = cum[:c], cum[c:2 * c], cum[2 * c:]
        lk = jnp.log(kin)
        wq = lk - bq
        wg = wq - rs
        wb = wg - rg
        qt = qf * jnp.exp(bq)
        qg = qt * jnp.exp(rs)
        o = _dot_nt((qg * jnp.exp(rg)).astype(BF16), st_ref[hh].astype(BF16))

        kts = []
        for i in range(1, ngrp):
            n = i * grp
            kts += [jnp.exp(rg[n:n + 1, :] + wb[:n, :]).astype(BF16), zeros_bf(c - n)]
        g_far = _dot_nt(qg.astype(BF16), jnp.concatenate(kts, axis=0))
        s_far = jnp.concatenate([jnp.zeros((grp, c), F32)] +
                                [g_far[i * grp:(i + 1) * grp, (i - 1) * c:i * c] for i in range(1, ngrp)], axis=0)

        kts = []
        for j in range(1, grp // sub):
            for gi in range(ngrp):
                a, n = gi * grp, j * sub
                kts += [jnp.exp(rs[a + n:a + n + 1, :] + wg[a:a + n, :]).astype(BF16), zeros_bf(grp - n)]
        g_near = _dot_nt(qt.astype(BF16), jnp.concatenate(kts, axis=0))
        pieces = []
        for i in range(c // sub):
            j = i % (grp // sub)
            pieces.append(jnp.zeros((sub, c), F32) if j == 0 else g_near[i * sub:(i + 1) * sub, (j - 1) * c:j * c])
        s = jnp.where(same_grp, jnp.concatenate(pieces, axis=0), s_far)

        bq2, wq2 = bq * LOG2E, wq * LOG2E
        kds = [kin.astype(BF16)] + [jnp.exp2(pltpu.roll(bq2, c - d, 0) + wq2).astype(BF16) for d in range(1, sub)]
        g_diag = _dot_nt(qf.astype(BF16), jnp.concatenate(kds, axis=0))
        last = slice(c - 1, c)
        return dict(o=o, s=s, g_diag=g_diag, wb=wb, b_last=rg[last, :] + rs[last, :] + bq[last, :])

    def finish(hh, r0, h):
        cols = slice(hh * 128, (hh + 1) * 128)
        vb = i_refs[hh // 2][pl.ds(r0, c), in_cols(hh)].astype(BF16)
        gt = g_refs[hh // 2][pl.ds(r0, c), in_cols(hh)]
        o = h["o"] + jnp.dot(h["s"].astype(BF16), vb, preferred_element_type=F32)
        st_ref[hh] = (st_ref[hh] * jnp.exp(h["b_last"])
                      + _dot_tn(vb, jnp.exp(h["b_last"] + h["wb"]).astype(BF16)))
        o = o * lax.rsqrt(jnp.mean(o * o, axis=-1, keepdims=True) + RMS_EPS)
        o_ref[pl.ds(r0, c), cols] = (o * ng_ref[0, :, cols] * (gt * jax.nn.sigmoid(gt))).astype(o_ref.dtype)

    def chunk(ci, carry):
        r0 = pl.multiple_of(ci * c, c)
        hs = [scores(hh, r0) for hh in heads]
        for d in range(sub):
            on_diag = band == d
            for h in hs:
                h["s"] = jnp.where(on_diag, h["g_diag"][:, d * c:(d + 1) * c], h["s"])
        for hh in heads:
            finish(hh, r0, hs[hh])
        return carry

    lax.fori_loop(0, HGRN_ROWS // c, chunk, 0, unroll=True)


def _hgrn(proj, lb, norm_g, bsz, seq):
    nr = seq // HGRN_ROWS
    hp, nin = HGRN_HEADS_PER_STEP, HGRN_IN_BLOCKS
    width = hp * 128
    c0 = (A_QW + 2 * A_KVW) // 256
    per_tensor = B_HEADS // 2
    nhp = B_HEADS // hp
    in_specs = [pl.BlockSpec((HGRN_ROWS, 256),
                             lambda b, h, r, t=t, p=p: (b * nr + r, c0 + t * per_tensor + h * nin + p))
                for t in range(4) for p in range(nin)]
    vec = pl.BlockSpec((1, 1, width), lambda b, h, r: (h, 0, 0))
    return pl.pallas_call(
        _hgrn_kernel,
        grid=(bsz, nhp, nr),
        in_specs=in_specs + [vec, vec],
        out_specs=pl.BlockSpec((HGRN_ROWS, width), lambda b, h, r: (b * nr + r, h)),
        out_shape=jax.ShapeDtypeStruct((bsz * seq, B_VW), BF16),
        scratch_shapes=[pltpu.VMEM((hp, B_DV, B_DK), F32)],
        compiler_params=_cparams(3, 32 * MIB),
        name="hgrn2",
    )(*([proj] * (4 * nin)), lb.reshape(nhp, 1, width), norm_g.reshape(nhp, 1, width))


OUTPROJ_LAG = 2


def _outproj_kernel(ya_ref, yb_ref, wa_ref, wb_ref, x_ref, g_ref, b_ref, wr_ref, rb_ref,
                    xo_ref, xp_ref, rt_ref, rtt_ref, cnt_ref, run_ref, lg_ref, mix_ref):
    step = pl.program_id(0)

    @pl.when(step == 0)
    def _():
        run_ref[...] = jnp.zeros_like(run_ref)
        lg_ref[...] = jnp.zeros_like(lg_ref)
        mix_ref[...] = jnp.zeros_like(mix_ref)

    lg_prev = lg_ref[...]
    mix_prev = mix_ref[...]
    mix = jnp.dot(ya_ref[...], wa_ref[...], preferred_element_type=F32)
    mix_ref[...] = mix + jnp.dot(yb_ref[...], wb_ref[...], preferred_element_type=F32)

    y = _layer_norm_rows(DN_ALPHA * x_ref[...] + mix_prev, g_ref[...], b_ref[...])
    xo_ref[...] = y
    _store_token_tiles(xp_ref, 0, _pack_rows(y))
    tm = y.shape[0]
    y_hi = y.astype(BF16)
    y_lo = (y - y_hi.astype(F32)).astype(BF16)
    prod = jnp.dot(jnp.concatenate([y_hi, y_lo], axis=0), wr_ref[...], preferred_element_type=F32)
    lg = (prod[:tm, :ROUTE_W] + prod[tm:, :ROUTE_W]) + (prod[:tm, ROUTE_W:] + prod[tm:, ROUTE_W:])
    lg_ref[...] = lg + rb_ref[...]
    _route_rows(lg_prev, (step >= OUTPROJ_LAG).astype(F32), rt_ref, rtt_ref, cnt_ref, run_ref)


def _outproj_ln_route(ya, yb, w_out, x, g, b, wr, rbias, tm=512):
    n = x.shape[0]
    ka = ya.shape[1]
    nsteps = n // tm
    lagged = lambda lag: (lambda i: (jnp.clip(i - lag, 0, nsteps - 1), 0))
    row = lambda width, lag: pl.BlockSpec((tm, width), lagged(lag))
    once = pl.Buffered(1)
    const = lambda shape: pl.BlockSpec(shape, lambda i: (0, 0), pipeline_mode=once)
    return pl.pallas_call(
        _outproj_kernel,
        grid=(nsteps + OUTPROJ_LAG,),
        in_specs=[row(ka, 0), row(ka, 0),
                  pl.BlockSpec((ka, D_MODEL), lambda i: (0, 0), pipeline_mode=once),
                  pl.BlockSpec((ka, D_MODEL), lambda i: (1, 0), pipeline_mode=once),
                  row(D_MODEL, 1), const((1, D_MODEL)), const((1, D_MODEL)),
                  const((D_MODEL, 2 * ROUTE_W)), const((1, ROUTE_W))],
        out_specs=[row(D_MODEL, 1), pl.BlockSpec((tm * TOK_TILE, LANES_V7X), lagged(1)),
                   row(ROUTE_W, OUTPROJ_LAG),
                   pl.BlockSpec((ROUTE_FIELDS, tm), lambda i: (0, jnp.clip(i - OUTPROJ_LAG, 0, nsteps - 1))),
                   pl.BlockSpec((1, ROUTE_W), lambda i: (0, 0))],
        out_shape=[jax.ShapeDtypeStruct((n, D_MODEL), F32),
                   jax.ShapeDtypeStruct((n * TOK_TILE, LANES_V7X), U32),
                   jax.ShapeDtypeStruct((n, ROUTE_W), F32), jax.ShapeDtypeStruct((ROUTE_FIELDS, n), F32),
                   jax.ShapeDtypeStruct((1, ROUTE_W), F32)],
        scratch_shapes=[pltpu.VMEM((1, ROUTE_W), F32), pltpu.VMEM((tm, ROUTE_W), F32),
                        pltpu.VMEM((tm, D_MODEL), F32)],
        compiler_params=_cparams(1, VMEM_LIMIT_V7X),
        name="outproj_ln_route",
    )(ya, yb, w_out, w_out, x, g.reshape(1, -1), b.reshape(1, -1), wr, rbias)


def _router_weights(w_group, b_group, w_router, b_router):
    w = jnp.zeros((D_MODEL, ROUTE_W), F32)
    w = w.at[:, :N_GROUPS].set(w_group).at[:, N_GROUPS:N_GROUPS + N_EXPERTS].set(w_router)
    w_hi = w.astype(BF16)
    w_lo = (w - w_hi.astype(F32)).astype(BF16)
    rb = jnp.zeros((1, ROUTE_W), F32)
    rb = rb.at[0, :N_GROUPS].set(b_group).at[0, N_GROUPS:N_GROUPS + N_EXPERTS].set(b_router)
    return jnp.concatenate([w_hi, w_lo], axis=1), rb


def _route_rows(lg, live, rt_ref, rtt_ref, cnt_ref, run_ref):
    tm = lg.shape[0]
    lane = lax.broadcasted_iota(jnp.int32, lg.shape, 1)
    sentinel = jnp.int32(ROUTE_W)
    rowmax = lambda mask: jnp.max(jnp.where(mask, lg, NEG_INF), axis=-1, keepdims=True)
    first = lambda mask: jnp.min(jnp.where(mask, lane, sentinel), axis=-1, keepdims=True)

    is_g = lane < N_GROUPS
    gmax = rowmax(is_g)
    g_idx = first(is_g & (lg == gmax))
    g_w = 1.0 / jnp.sum(jnp.where(is_g, jnp.exp(lg - gmax), 0.0), axis=-1, keepdims=True)

    e_lane = lane - N_GROUPS
    sel = (e_lane >= 0) & (e_lane < N_EXPERTS) & ((e_lane >> 3) == g_idx)
    m1 = rowmax(sel)
    i1 = first(sel & (lg == m1))
    sel2 = sel & (lane != i1)
    m2 = rowmax(sel2)
    i2 = first(sel2 & (lg == m2))
    ex = jnp.exp(m2 - m1)
    w0 = g_w / (1.0 + ex)
    w1 = g_w * ex / (1.0 + ex)

    oh0 = lane == i1
    oh1 = lane == i2
    both = (oh0 | oh1).astype(F32) * live
    rid = lax.broadcasted_iota(jnp.int32, (tm, tm), 0)
    cid = lax.broadcasted_iota(jnp.int32, (tm, tm), 1)
    before = jnp.dot((cid < rid).astype(BF16), both.astype(BF16), preferred_element_type=F32) + run_ref[...]
    rank0 = jnp.sum(jnp.where(oh0, before, 0.0), axis=-1, keepdims=True)
    rank1 = jnp.sum(jnp.where(oh1, before, 0.0), axis=-1, keepdims=True)
    run = run_ref[...] + jnp.sum(both, axis=0, keepdims=True)
    run_ref[...] = run
    cnt_ref[...] = run

    slab = jnp.zeros(lg.shape, F32)
    for ln, val in ((RT_W0, w0), (RT_W1, w1), (RT_E0, (i1 - N_GROUPS).astype(F32)),
                    (RT_E1, (i2 - N_GROUPS).astype(F32)), (RT_R0, rank0), (RT_R1, rank1)):
        slab = jnp.where(lane == ln, val, slab)
    rt_ref[...] = slab
    rtt_ref[...] = jnp.transpose(slab)[:ROUTE_FIELDS, :]


assert EXPERTS_PER_GROUP == 8


def _block_layout(rtt, cnt, n_tok):
    m = n_tok * TOP_K
    counts = cnt[0, N_GROUPS:N_GROUPS + N_EXPERTS].astype(jnp.int32)
    pcounts = (counts + MOE_TB - 1) // MOE_TB * MOE_TB
    pends = jnp.cumsum(pcounts)
    pstarts = pends - pcounts
    n_blocks = -(-(m + N_EXPERTS * (MOE_TB - 1)) // MOE_TB)
    e_idx = rtt[RT_E0:RT_E1 + 1, :].astype(jnp.int32)
    rank = rtt[RT_R0:RT_R1 + 1, :].astype(jnp.int32)
    onehot = e_idx[:, None, :] == jnp.arange(N_EXPERTS, dtype=jnp.int32)[None, :, None]
    pos = jnp.sum(jnp.where(onehot, pstarts[None, :, None], 0), axis=1) + rank
    blk_start = jnp.arange(n_blocks, dtype=jnp.int32) * MOE_TB
    blk_e = jnp.minimum(jnp.sum(blk_start[:, None] >= pends[None, :], axis=-1), N_EXPERTS - 1).astype(jnp.int32)
    nused = (pends[-1:] // MOE_TB).astype(jnp.int32)
    ids = jnp.arange(N_EXPERTS, dtype=jnp.int32)
    later_used = (ids[None, :] > ids[:, None]) & (counts[None, :] > 0)
    next_used = jnp.min(jnp.where(later_used, ids[None, :], N_EXPERTS), axis=-1)
    next_used = jnp.where(next_used == N_EXPERTS, -1, next_used).astype(jnp.int32)
    nxt_e = jnp.sum(jnp.where(blk_e[:, None] == ids[None, :], next_used[None, :], 0), axis=-1).astype(jnp.int32)
    ends_expert = jnp.any(((blk_start + MOE_TB)[:, None] == pends[None, :]) & (pcounts[None, :] > 0), axis=-1)
    zero_blk = (ends_expert | (blk_start >= pends[-1])).astype(jnp.int32)
    return pos, blk_e, nxt_e, nused, zero_blk


def _step_indices(pos, tm):
    nsteps = pos.shape[1] // tm
    return pos.reshape(TOP_K, nsteps, tm).transpose(1, 0, 2).reshape(nsteps, 1, TOP_K * tm)


def _tile_rows(i):
    return pl.ds(pl.multiple_of(i * TOK_TILE, TOK_TILE), TOK_TILE)


def _dispatch_kernel(zero_blk_ref, pos_ref, xp_ref, xs_hbm, xbuf, zbuf, sem, zsem, *, tm, n_blocks):
    s = pl.program_id(0)
    nsteps = pl.num_programs(0)
    slot = s % 2
    blk_rows = MOE_TB * TOK_TILE

    def copy(j, dst, sl):
        return pltpu.make_async_copy(xbuf.at[sl, _tile_rows(j)], xs_hbm.at[_tile_rows(dst)], sem.at[sl])

    def drain(sl):
        def body(j, c):
            copy(0, 0, sl).wait()
            return c
        lax.fori_loop(0, TOP_K * tm, body, 0, unroll=8)

    @pl.when(s == 0)
    def _():
        zbuf[...] = jnp.zeros_like(zbuf)

        def zero_copy(blk):
            rows = pl.ds(pl.multiple_of(blk * blk_rows, blk_rows), blk_rows)
            return pltpu.make_async_copy(zbuf, xs_hbm.at[rows], zsem)

        def each_flagged(fn):
            def body(blk, c):
                @pl.when(zero_blk_ref[blk] > 0)
                def _():
                    fn(zero_copy(blk))
                return c
            lax.fori_loop(0, n_blocks, body, 0)

        each_flagged(lambda cp: cp.start())
        each_flagged(lambda cp: cp.wait())

    @pl.when(s >= 2)
    def _():
        drain(slot)

    xbuf[slot] = xp_ref[...]

    for j in range(tm):
        for k in range(TOP_K):
            copy(j, pos_ref[0, 0, k * tm + j], slot).start(priority=k)

    @pl.when(s == nsteps - 1)
    def _():
        drain(1 - slot)
        drain(slot)


def _moe_dispatch(xp, pos, zero_blk, tm=512):
    n = pos.shape[1]
    n_blocks = zero_blk.shape[0]
    assert n // tm >= 2
    pos3 = _step_indices(pos, tm)
    grid_spec = pltpu.PrefetchScalarGridSpec(
        num_scalar_prefetch=1,
        grid=(n // tm,),
        in_specs=[pl.BlockSpec((1, 1, TOP_K * tm), lambda i, zb: (i, 0, 0), memory_space=pltpu.SMEM),
                  pl.BlockSpec((tm * TOK_TILE, LANES_V7X), lambda i, zb: (i, 0))],
        out_specs=pl.BlockSpec(memory_space=pl.ANY),
        scratch_shapes=[pltpu.VMEM((2, tm * TOK_TILE, LANES_V7X), U32),
                        pltpu.VMEM((MOE_TB * TOK_TILE, LANES_V7X), U32),
                        pltpu.SemaphoreType.DMA((2,)), pltpu.SemaphoreType.DMA(())],
    )
    return pl.pallas_call(
        functools.partial(_dispatch_kernel, tm=tm, n_blocks=n_blocks),
        grid_spec=grid_spec,
        out_shape=jax.ShapeDtypeStruct((n_blocks * MOE_TB * TOK_TILE, LANES_V7X), U32),
        compiler_params=_cparams(1),
        name="moe_dispatch",
    )(zero_blk, pos3, xp)


def _moe_kernel(blk_e_ref, nxt_e_ref, nused_ref, xs_ref, w1_hbm, w3_hbm, w2_hbm, ys_ref,
                wf1, wf3, wf2, w1b, w3b, w2b, slot_ref, sem, *, layer):
    s = pl.program_id(0)
    nused = nused_ref[0]

    def fetch(e, sl):
        return [pltpu.make_async_copy(w_hbm.at[layer, e], wf.at[sl], sem.at[sl, k])
                for k, (w_hbm, wf) in enumerate(((w1_hbm, wf1), (w3_hbm, wf3), (w2_hbm, wf2)))]

    @pl.when(s >= nused)
    def _():
        ys_ref[...] = jnp.zeros_like(ys_ref)

    @pl.when(s < nused)
    def _():
        e = blk_e_ref[s]
        prev = blk_e_ref[jnp.maximum(s - 1, 0)]

        @pl.when(s == 0)
        def _():
            slot_ref[0] = 0
            for cp in fetch(e, 0):
                cp.start()

        @pl.when((s > 0) & (e != prev))
        def _():
            slot_ref[0] = 1 - slot_ref[0]

        first = (s == 0) | (e != prev)

        def ffn(weights):
            xb = _unpack_rows(_load_token_tiles(xs_ref, 0, MOE_TB)).astype(BF16)
            h1 = jnp.dot(xb, weights(0), preferred_element_type=F32)
            h3 = jnp.dot(xb, weights(1), preferred_element_type=F32)
            h = (h1 * jax.nn.sigmoid(h1) * h3).astype(BF16)
            y = jnp.dot(h, weights(2), preferred_element_type=F32)
            _store_token_tiles(ys_ref, 0, _pack_rows(y))

        @pl.when(first)
        def _():
            sl = slot_ref[0]
            for cp in fetch(e, sl):
                cp.wait()
            nxt = nxt_e_ref[s]

            @pl.when(nxt >= 0)
            def _():
                for cp in fetch(nxt, 1 - sl):
                    cp.start()

            def cast(k):
                wb = (w1b, w3b, w2b)[k]
                w = (wf1, wf3, wf2)[k][sl].astype(BF16)
                wb[...] = w
                return w
            ffn(cast)

        @pl.when(jnp.logical_not(first))
        def _():
            ffn(lambda k: (w1b, w3b, w2b)[k][...])


def _moe_experts(xs, blk_e, nxt_e, nused, w1, w3, w2, layer):
    n_blocks = blk_e.shape[0]
    tiles = pl.BlockSpec((MOE_TB * TOK_TILE, LANES_V7X), lambda s, be, ne, nu: (s, 0))
    hbm = pl.BlockSpec(memory_space=pl.ANY)
    up, down = (D_MODEL, D_EXPERT), (D_EXPERT, D_MODEL)
    grid_spec = pltpu.PrefetchScalarGridSpec(
        num_scalar_prefetch=3,
        grid=(n_blocks,),
        in_specs=[tiles, hbm, hbm, hbm],
        out_specs=tiles,
        scratch_shapes=[pltpu.VMEM((2,) + up, F32), pltpu.VMEM((2,) + up, F32), pltpu.VMEM((2,) + down, F32),
                        pltpu.VMEM(up, BF16), pltpu.VMEM(up, BF16), pltpu.VMEM(down, BF16),
                        pltpu.SMEM((1,), jnp.int32), pltpu.SemaphoreType.DMA((2, 3))],
    )
    return pl.pallas_call(
        functools.partial(_moe_kernel, layer=layer),
        grid_spec=grid_spec,
        out_shape=jax.ShapeDtypeStruct(xs.shape, U32),
        compiler_params=_cparams(1, VMEM_LIMIT_V7X),
        name="moe_experts",
    )(blk_e, nxt_e, nused, xs, w1, w3, w2)


def _combine_kernel(pos_ref, nxt_ref, ys_hbm, x_ref, rt_ref, g_ref, b_ref, o_ref, ybuf, sem, *, tm):
    s = pl.program_id(0)
    nsteps = pl.num_programs(0)
    slot = s % 2

    def copy(src, j, sl):
        return pltpu.make_async_copy(ys_hbm.at[_tile_rows(src)], ybuf.at[sl, _tile_rows(j)], sem.at[sl])

    def start_gather(idx_ref, sl):
        def body(j, c):
            copy(idx_ref[0, 0, j], j, sl).start()
            return c
        lax.fori_loop(0, TOP_K * tm, body, 0, unroll=8)

    @pl.when(s == 0)
    def _():
        start_gather(pos_ref, 0)

    @pl.when(s + 1 < nsteps)
    def _():
        for j in range(TOP_K * tm):
            copy(nxt_ref[0, 0, j], j, 1 - slot).start()

    def wait_body(j, c):
        copy(0, j, slot).wait()
        return c
    lax.fori_loop(0, TOP_K * tm, wait_body, 0, unroll=8)

    rt = rt_ref[...]
    ffn = _unpack_rows(_load_token_tiles(ybuf, 0, tm, lead=slot)) * rt[:, RT_W0:RT_W0 + 1]
    ffn = ffn + _unpack_rows(_load_token_tiles(ybuf, tm * TOK_TILE, tm, lead=slot)) * rt[:, RT_W1:RT_W1 + 1]
    o_ref[...] = _layer_norm_rows(DN_ALPHA * x_ref[...] + ffn, g_ref[...], b_ref[...])


def _moe_combine(ys, pos, rt, x, g, b, tm=256):
    n = x.shape[0]
    nsteps = n // tm
    pos3 = _step_indices(pos, tm)
    smem_blk = lambda f: pl.BlockSpec((1, 1, TOP_K * tm), f, memory_space=pltpu.SMEM)
    row = lambda width: pl.BlockSpec((tm, width), lambda i: (i, 0))
    const = lambda shape: pl.BlockSpec(shape, lambda i: (0, 0))
    return pl.pallas_call(
        functools.partial(_combine_kernel, tm=tm),
        grid=(nsteps,),
        in_specs=[smem_blk(lambda i: (i, 0, 0)), smem_blk(lambda i: (jnp.minimum(i + 1, nsteps - 1), 0, 0)),
                  pl.BlockSpec(memory_space=pl.ANY), row(D_MODEL), row(ROUTE_W),
                  const((1, D_MODEL)), const((1, D_MODEL))],
        out_specs=row(D_MODEL),
        out_shape=jax.ShapeDtypeStruct((n, D_MODEL), F32),
        scratch_shapes=[pltpu.VMEM((2, TOP_K * tm * TOK_TILE, LANES_V7X), U32), pltpu.SemaphoreType.DMA((2,))],
        compiler_params=_cparams(1, 40 * MIB),
        name="moe_combine",
    )(pos3, pos3, ys, x, rt, g.reshape(1, -1), b.reshape(1, -1))


def _moe_layer(x1, xp, rt, rtt, cnt, w1, w3, w2, layer, g, b):
    n_tok = x1.shape[0]
    pos, blk_e, nxt_e, nused, zero_blk = _block_layout(rtt, cnt, n_tok)
    xs = _moe_dispatch(xp, pos, zero_blk)
    ys = _moe_experts(xs, blk_e, nxt_e, nused, w1, w3, w2, layer)
    return _moe_combine(ys, pos, rt, x1, g, b)


def _gmlp_kernel(u_ref, v_ref, g_ref, b_ref, w_ref, bs_ref, o_ref, *, chunks):
    for ci in range(chunks):
        rows = slice(ci * C_CHUNK, (ci + 1) * C_CHUNK)
        u = jax.nn.gelu(u_ref[rows, :])
        v = _layer_norm_rows(jax.nn.gelu(v_ref[rows, :]), g_ref[...], b_ref[...]).astype(BF16)
        for gi in range(C_GROUPS):
            cols = slice(gi * C_GROUP_DIM, (gi + 1) * C_GROUP_DIM)
            mixed = jnp.dot(w_ref[gi], v[:, cols], preferred_element_type=F32) + bs_ref[:, cols]
            o_ref[rows, cols] = (u[:, cols] * mixed).astype(o_ref.dtype)


def _gmlp(proj, ln_g, ln_b, w_s, b_s, chunks=4):
    n = proj.shape[0]
    tm = chunks * C_CHUNK
    w = (w_s * jnp.tril(jnp.ones((C_CHUNK, C_CHUNK), w_s.dtype))).astype(BF16)
    bs_full = jnp.repeat(b_s.T, C_GROUP_DIM, axis=1)
    const2 = lambda shape: pl.BlockSpec(shape, lambda i: (0, 0))
    return pl.pallas_call(
        functools.partial(_gmlp_kernel, chunks=chunks),
        grid=(n // tm,),
        in_specs=[pl.BlockSpec((tm, C_W), lambda i: (i, 0)), pl.BlockSpec((tm, C_W), lambda i: (i, 1)),
                  const2((1, C_W)), const2((1, C_W)),
                  pl.BlockSpec((C_GROUPS, C_CHUNK, C_CHUNK), lambda i: (0, 0, 0)), const2((C_CHUNK, C_W))],
        out_specs=pl.BlockSpec((tm, C_W), lambda i: (i, 0)),
        out_shape=jax.ShapeDtypeStruct((n, C_W), BF16),
        compiler_params=_cparams(1, 32 * MIB),
        name="gmlp_gating",
    )(proj, proj, ln_g.reshape(1, -1), ln_b.reshape(1, -1), w, bs_full)


CONV_HIST = 32


def _conv_kernel(a_ref, gt_ref, ap_ref, gp_ref, w_ref, cb_ref, g_ref, b_ref, o_ref, hbuf, hshift, *, ts):
    i = pl.program_id(1)
    hist = ap_ref[...] * jax.nn.sigmoid(gp_ref[...])
    hbuf[0:CONV_HIST, :] = jnp.where(i > 0, hist, jnp.zeros_like(hist))
    hbuf[CONV_HIST:CONV_HIST + ts, :] = a_ref[...] * jax.nn.sigmoid(gt_ref[...])
    off = CONV_HIST - (D_CONV - 1)
    acc = jnp.zeros((ts, D_CHANNELS), F32) + cb_ref[...]
    for r in range(SUBLANES_V7X):
        taps = [j for j in range(D_CONV) if (off + j) % SUBLANES_V7X == r]
        if not taps:
            continue
        src = hbuf
        if r:
            span = max(taps) + off - r + ts
            hshift[0:span, :] = hbuf[r:r + span, :]
            src = hshift
        for j in taps:
            base = off + j - r
            acc = acc + w_ref[j:j + 1, :] * src[base:base + ts, :]
    y = _layer_norm_rows(acc, g_ref[...], b_ref[...])
    o_ref[...] = (y * jax.nn.sigmoid(y)).astype(o_ref.dtype)


def _conformer_conv(proj, conv_w, conv_b, ln_g, ln_b, bsz, seq, ts=512):
    nt = seq // ts
    acol = 2 * C_W // D_CHANNELS
    gcol = acol + 1
    hb = ts // CONV_HIST
    cur = lambda col: pl.BlockSpec((ts, D_CHANNELS), lambda b, i: (b * nt + i, col))
    prev = lambda col: pl.BlockSpec((CONV_HIST, D_CHANNELS),
                                    lambda b, i: (jnp.maximum((b * nt + i) * hb - 1, 0), col))
    const2 = lambda shape: pl.BlockSpec(shape, lambda b, i: (0, 0))
    return pl.pallas_call(
        functools.partial(_conv_kernel, ts=ts),
        grid=(bsz, nt),
        in_specs=[cur(acol), cur(gcol), prev(acol), prev(gcol),
                  const2((D_CONV, D_CHANNELS)), const2((1, D_CHANNELS)), const2((1, D_CHANNELS)),
                  const2((1, D_CHANNELS))],
        out_specs=pl.BlockSpec((ts, D_CHANNELS), lambda b, i: (b * nt + i, 0)),
        out_shape=jax.ShapeDtypeStruct((bsz * seq, D_CHANNELS), BF16),
        scratch_shapes=[pltpu.VMEM((CONV_HIST + ts, D_CHANNELS), F32), pltpu.VMEM((CONV_HIST + ts, D_CHANNELS), F32)],
        compiler_params=_cparams(2, 32 * MIB),
        name="conformer_conv",
    )(proj, proj, proj, proj, conv_w, conv_b.reshape(1, -1), ln_g.reshape(1, -1), ln_b.reshape(1, -1))


def kernel(x, w_in_ab, attn_sinks, rel_bias, hgrn_lb_logits, hgrn_norm_g, w_out_ab, w_in_cd, gmlp_ln_g, gmlp_ln_b, gmlp_w_s, gmlp_b_s, conv_w, conv_b, conv_ln_g, conv_ln_b, w_out_cd, ln_mix_g, ln_mix_b, ln_ffn_g, ln_ffn_b, moe_w_group, moe_b_group, moe_w_router, moe_b_router, moe_w1, moe_w3, moe_w2):
    bsz, seq = x.shape[0], x.shape[1]
    n_tok = bsz * seq
    xf = x.reshape(n_tok, D_MODEL)
    lb_table = jnp.cumsum(jax.nn.softmax(hgrn_lb_logits.astype(F32), axis=0), axis=0)
    bias = _bias_table(rel_bias.astype(F32), _t5_bucket_table())

    for layer in range(DEPTH):
        j = layer // 2
        if layer % 2 == 0:
            proj = _matmul(xf, w_in_ab[j].astype(BF16), 1024, EVEN_IN // 3, F32)
            ya = _attention(proj, bias, attn_sinks[j].astype(F32), bsz, seq)
            yb = _hgrn(proj, lb_table[layer], hgrn_norm_g[j].astype(F32), bsz, seq)
            w_out = w_out_ab[j]
        else:
            proj = _matmul(xf, w_in_cd[j].astype(BF16), 1024, ODD_IN // 2, F32)
            ya = _gmlp(proj, gmlp_ln_g[j], gmlp_ln_b[j], gmlp_w_s[j], gmlp_b_s[j])
            yb = _conformer_conv(proj, conv_w[j], conv_b[j], conv_ln_g[j], conv_ln_b[j], bsz, seq)
            w_out = w_out_cd[j]
        wr, rbias = _router_weights(moe_w_group[layer], moe_b_group[layer],
                                    moe_w_router[layer], moe_b_router[layer])
        x1, xp, rt, rtt, cnt = _outproj_ln_route(ya, yb, w_out.astype(BF16), xf, ln_mix_g[layer],
                                                 ln_mix_b[layer], wr, rbias)
        xf = _moe_layer(x1, xp, rt, rtt, cnt, moe_w1, moe_w3, moe_w2, layer, ln_ffn_g[layer], ln_ffn_b[layer])
    return xf.reshape(bsz, seq, D_MODEL)
```

```python
import functools
import math

import jax
import jax.numpy as jnp
from jax import lax
from jax.experimental import pallas as pl
from jax.experimental.pallas import tpu as pltpu

D_MODEL = 2048
DEPTH = 2
A_HEADS = 16
A_KV_HEADS = 2
A_HEAD_DIM = 64
A_WINDOW = 128
A_BLOCK = 128
REL_BUCKETS = 32
REL_MAX_DIST = 128
B_HEADS = 8
B_DK = 128
B_DV = 128
C_GROUPS = 8
C_GROUP_DIM = 128
C_CHUNK = 128
D_CHANNELS = 1024
D_CONV = 31
A_QW = A_HEADS * A_HEAD_DIM
A_KVW = A_KV_HEADS * A_HEAD_DIM
B_KW = B_HEADS * B_DK
B_VW = B_HEADS * B_DV
C_W = C_GROUPS * C_GROUP_DIM
EVEN_IN = A_QW + 2 * A_KVW + 2 * B_KW + 2 * B_VW
ODD_IN = 2 * C_W + 2 * D_CHANNELS
N_GROUPS = 4
EXPERTS_PER_GROUP = 8
N_EXPERTS = N_GROUPS * EXPERTS_PER_GROUP
TOP_K = 2
D_EXPERT = 512
DN_ALPHA = (2 * DEPTH) ** 0.25
LN_EPS = 1e-5
RMS_EPS = 1e-6

LANES_V7X = 128
SUBLANES_V7X = 8
MIB = 1024 * 1024
VMEM_LIMIT_V7X = 56 * MIB

HGRN_CHUNK = 128
HGRN_SUB = 8
HGRN_GROUP = 32
HGRN_ROWS = 512
MOE_TB = 256
ROUTE_W = LANES_V7X
HALF = D_MODEL // 2
TOK_TILE = HALF // LANES_V7X
assert TOK_TILE == SUBLANES_V7X

BF16 = jnp.bfloat16
F32 = jnp.float32
U32 = jnp.uint32
NEG_INF = float("-inf")
LOG2E = math.log2(math.e)
RT_W0, RT_W1, RT_E0, RT_E1, RT_R0, RT_R1 = range(6)
ROUTE_FIELDS = SUBLANES_V7X


def _cparams(n_axes, vmem_bytes=None):
    return pltpu.CompilerParams(dimension_semantics=("arbitrary",) * n_axes, vmem_limit_bytes=vmem_bytes)


def _layer_norm_rows(z, g, b):
    mu = jnp.mean(z, axis=-1, keepdims=True)
    zc = z - mu
    var = jnp.mean(zc * zc, axis=-1, keepdims=True)
    return zc * lax.rsqrt(var + LN_EPS) * g + b


def _dot_nt(a, b):
    return lax.dot_general(a, b, (((1,), (1,)), ((), ())), preferred_element_type=F32)


def _dot_tn(a, b):
    return lax.dot_general(a, b, (((0,), (0,)), ((), ())), preferred_element_type=F32)


def _pack_rows(y):
    lo = lax.bitcast_convert_type(y[:, :HALF].astype(BF16).astype(F32), U32) >> 16
    hi = lax.bitcast_convert_type(y[:, HALF:].astype(BF16).astype(F32), U32) & jnp.uint32(0xFFFF0000)
    return lo | hi


def _unpack_rows(p):
    lo = lax.bitcast_convert_type(p << 16, F32)
    hi = lax.bitcast_convert_type(p & jnp.uint32(0xFFFF0000), F32)
    return jnp.concatenate([lo, hi], axis=1)


def _store_token_tiles(ref, base, packed):
    rows = packed.shape[0]
    for c in range(TOK_TILE):
        ref[pl.ds(base + c, rows, stride=TOK_TILE), :] = packed[:, c * LANES_V7X:(c + 1) * LANES_V7X]


def _load_token_tiles(ref, base, rows, lead=None):
    parts = []
    for c in range(TOK_TILE):
        idx = (pl.ds(base + c, rows, stride=TOK_TILE), slice(None))
        parts.append(ref[idx] if lead is None else ref[(lead,) + idx])
    return jnp.concatenate(parts, axis=1)


def _mm_kernel(a_ref, w_ref, o_ref):
    a = a_ref[...].astype(BF16)
    o_ref[...] = jnp.dot(a, w_ref[...], preferred_element_type=F32).astype(o_ref.dtype)


def _matmul(a, w, tm, tn, out_dtype):
    m, k = a.shape
    n = w.shape[1]
    return pl.pallas_call(
        _mm_kernel,
        grid=(n // tn, m // tm),
        in_specs=[pl.BlockSpec((tm, k), lambda j, i: (i, 0)), pl.BlockSpec((k, tn), lambda j, i: (0, j))],
        out_specs=pl.BlockSpec((tm, tn), lambda j, i: (i, j)),
        out_shape=jax.ShapeDtypeStruct((m, n), out_dtype),
        compiler_params=_cparams(2, VMEM_LIMIT_V7X),
        name="proj_matmul",
    )(a, w)


def _bias_table_kernel(rb_ref, bucket_ref, o_ref):
    bucket = bucket_ref[...]
    for h in range(A_HEADS):
        acc = jnp.zeros(bucket.shape, F32)
        for bk in range(REL_BUCKETS):
            acc = jnp.where(bucket == bk, rb_ref[bk, h], acc)
        o_ref[h] = acc


def _bias_table(rel_bias, bucket):
    return pl.pallas_call(
        _bias_table_kernel,
        in_specs=[pl.BlockSpec(memory_space=pltpu.SMEM), pl.BlockSpec(memory_space=pltpu.VMEM)],
        out_specs=pl.BlockSpec(memory_space=pltpu.VMEM),
        out_shape=jax.ShapeDtypeStruct((A_HEADS,) + bucket.shape, F32),
        name="rel_bias_table",
    )(rel_bias, bucket)


def _t5_bucket_table():
    t_loc = jnp.arange(A_BLOCK, dtype=jnp.int32)[:, None]
    j_loc = jnp.arange(A_BLOCK, dtype=jnp.int32)[None, :]
    dist = jnp.where(j_loc <= t_loc, t_loc - j_loc, t_loc + A_BLOCK - j_loc)
    max_exact = REL_BUCKETS // 2
    d = jnp.maximum(dist, 1).astype(F32)
    large = max_exact + (jnp.log(d / max_exact) / math.log(REL_MAX_DIST / max_exact)
                         * (REL_BUCKETS - max_exact)).astype(jnp.int32)
    large = jnp.minimum(large, REL_BUCKETS - 1)
    return jnp.where(dist < max_exact, dist, large)


assert A_WINDOW == A_BLOCK


ATTN_BLOCKS_PER_STEP = 4


def _attn_kernel(q_ref, kc_ref, vc_ref, kp_ref, vp_ref, bias_ref, sink_ref, o_ref):
    blk = A_BLOCK
    first_pair = pl.program_id(1) == 0
    for sb in range(ATTN_BLOCKS_PER_STEP):
        rows = slice(sb * blk, (sb + 1) * blk)
        before = slice((sb - 1) * blk, sb * blk)
        k_prev, v_prev = (kp_ref[...], vp_ref[...]) if sb == 0 else (kc_ref[before, :], vc_ref[before, :])
        no_prev = jnp.where(first_pair, NEG_INF, 0.0) if sb == 0 else 0.0
        _attn_block(q_ref, rows, k_prev, v_prev, kc_ref[rows, :], vc_ref[rows, :], no_prev, bias_ref, sink_ref, o_ref)


def _attn_block(q_ref, rows, k_prev, v_prev, k_own, v_own, no_prev, bias_ref, sink_ref, o_ref):
    blk = A_BLOCK
    k2 = jnp.concatenate([k_prev, k_own], axis=0)
    v2 = jnp.concatenate([v_prev, v_own], axis=0)
    k2r = pltpu.roll(k2, A_HEAD_DIM, 1)
    v2r = pltpu.roll(v2, A_HEAD_DIM, 1)
    lo = lax.broadcasted_iota(jnp.int32, k2.shape, 1) < A_HEAD_DIM
    zero = jnp.zeros_like(k2)

    def placed(x, xr, g, par):
        src = x if g == par else xr
        return (jnp.where(lo, src, zero) if par == 0 else jnp.where(lo, zero, src)).astype(BF16)

    kk = [[placed(k2, k2r, g, par) for par in range(2)] for g in range(A_KV_HEADS)]
    vv = [[placed(v2, v2r, g, par) for par in range(2)] for g in range(A_KV_HEADS)]

    own = (lax.broadcasted_iota(jnp.int32, (blk, blk), 1) <= lax.broadcasted_iota(jnp.int32, (blk, blk), 0))
    zero_p = jnp.zeros((blk, blk), F32)

    heads_per_kv = A_HEADS // A_KV_HEADS
    for p in range(A_HEADS // 2):
        g = (2 * p) // heads_per_kv
        qp = (q_ref[rows, p * 128:(p + 1) * 128] * (A_HEAD_DIM ** -0.5)).astype(BF16)
        acc = jnp.zeros((blk, 128), F32)
        for par in range(2):
            h = 2 * p + par
            sink = sink_ref[h]
            s2 = _dot_nt(qp, kk[g][par])
            logits = jnp.where(own, s2[:, blk:], s2[:, :blk] + no_prev) + bias_ref[h]
            m = jnp.maximum(jnp.max(logits, axis=-1, keepdims=True), sink)
            e = jnp.exp(logits - m)
            den = jnp.sum(e, axis=-1, keepdims=True) + jnp.exp(sink - m)
            e2 = jnp.concatenate([jnp.where(own, zero_p, e), jnp.where(own, e, zero_p)], axis=1).astype(BF16)
            acc = acc + jnp.dot(e2, vv[g][par], preferred_element_type=F32) * (1.0 / den)
        o_ref[rows, p * 128:(p + 1) * 128] = acc.astype(o_ref.dtype)


def _attention(proj, bias, sinks, bsz, seq):
    per = ATTN_BLOCKS_PER_STEP
    npair = seq // (per * A_BLOCK)
    kcol = A_QW // 128
    vcol = kcol + 1
    row = lambda b, m: b * npair + m
    prow = lambda b, m: (b * npair + m) * per - jnp.where(m > 0, 1, 0)
    return pl.pallas_call(
        _attn_kernel,
        grid=(bsz, npair),
        in_specs=[
            pl.BlockSpec((per * A_BLOCK, A_QW), lambda b, m: (row(b, m), 0)),
            pl.BlockSpec((per * A_BLOCK, 128), lambda b, m: (row(b, m), kcol)),
            pl.BlockSpec((per * A_BLOCK, 128), lambda b, m: (row(b, m), vcol)),
            pl.BlockSpec((A_BLOCK, 128), lambda b, m: (prow(b, m), kcol)),
            pl.BlockSpec((A_BLOCK, 128), lambda b, m: (prow(b, m), vcol)),
            pl.BlockSpec((A_HEADS, A_BLOCK, A_BLOCK), lambda b, m: (0, 0, 0)),
            pl.BlockSpec(memory_space=pltpu.SMEM),
        ],
        out_specs=pl.BlockSpec((per * A_BLOCK, A_QW), lambda b, m: (row(b, m), 0)),
        out_shape=jax.ShapeDtypeStruct((bsz * seq, A_QW), BF16),
        compiler_params=_cparams(2, 32 * MIB),
        name="swa_attention",
    )(proj, proj, proj, proj, proj, bias, sinks)


HGRN_HEADS_PER_STEP = 4
HGRN_IN_BLOCKS = HGRN_HEADS_PER_STEP // 2


def _hgrn_kernel(*refs):
    nin = HGRN_IN_BLOCKS
    q_refs, f_refs, i_refs, g_refs = (refs[k * nin:(k + 1) * nin] for k in range(4))
    lb_ref, ng_ref, o_ref, st_ref = refs[4 * nin:]
    in_cols = lambda hh: slice((hh % 2) * 128, (hh % 2 + 1) * 128)
    c, sub, grp = HGRN_CHUNK, HGRN_SUB, HGRN_GROUP
    ngrp = c // grp

    @pl.when(pl.program_id(2) == 0)
    def _():
        st_ref[...] = jnp.zeros_like(st_ref)

    rid = lax.broadcasted_iota(jnp.int32, (c, c), 0)
    cid = lax.broadcasted_iota(jnp.int32, (c, c), 1)
    grp_start = (rid // grp) * grp
    sub_start = (rid // sub) * sub
    m_cum = jnp.concatenate([(cid < grp_start).astype(F32),
                             ((cid >= grp_start) & (cid < sub_start)).astype(F32),
                             ((cid >= sub_start) & (cid <= rid)).astype(F32)], axis=0).astype(BF16)
    band = jnp.where(cid >= sub_start, rid - cid, -1)
    same_grp = (cid // grp) == (rid // grp)
    heads = range(HGRN_HEADS_PER_STEP)
    zeros_bf = lambda rows: jnp.zeros((rows, B_DK), BF16)

    def scores(hh, r0):
        cols = slice(hh * 128, (hh + 1) * 128)
        lb = lb_ref[0, :, cols]
        q = q_refs[hh // 2][pl.ds(r0, c), in_cols(hh)]
        fl = f_refs[hh // 2][pl.ds(r0, c), in_cols(hh)]
        qf = q * jax.nn.sigmoid(q)
        f = lb + (1.0 - lb) * jax.nn.sigmoid(fl)
        kin = 1.0 - f
        logf = jnp.log(f)
        l1 = logf.astype(BF16)
        res = logf - l1.astype(F32)
        l2 = res.astype(BF16)
        l3 = (res - l2.astype(F32)).astype(BF16)
        cum = jnp.dot(m_cum, jnp.concatenate([l1, l2, l3], axis=1), preferred_element_type=F32)
        cum = cum[:, :B_DK] + cum[:, B_DK:2 * B_DK] + cum[:, 2 * B_DK:]
        rg, rs, bq = cum[:c], cum[c:2 * c], cum[2 * c:]
        lk = jnp.log(kin)
        wq = lk - bq
        wg = wq - rs
        wb = wg - rg
        qt = qf * jnp.exp(bq)
        qg = qt * jnp.exp(rs)
        o = _dot_nt((qg * jnp.exp(rg)).astype(BF16), st_ref[hh].astype(BF16))

        kts = []
        for i in range(1, ngrp):
            n = i * grp
            kts += [jnp.exp(rg[n:n + 1, :] + wb[:n, :]).astype(BF16), zeros_bf(c - n)]
        g_far = _dot_nt(qg.astype(BF16), jnp.concatenate(kts, axis=0))
        s_far = jnp.concatenate([jnp.zeros((grp, c), F32)] +
                                [g_far[i * grp:(i + 1) * grp, (i - 1) * c:i * c] for i in range(1, ngrp)], axis=0)

        kts = []
        for j in range(1, grp // sub):
            for gi in range(ngrp):
                a, n = gi * grp, j * sub
                kts += [jnp.exp(rs[a + n:a + n + 1, :] + wg[a:a + n, :]).astype(BF16), zeros_bf(grp - n)]
        g_near = _dot_nt(qt.astype(BF16), jnp.concatenate(kts, axis=0))
        pieces = []
        for i in range(c // sub):
            j = i % (grp // sub)
            pieces.append(jnp.zeros((sub, c), F32) if j == 0 else g_near[i * sub:(i + 1) * sub, (j - 1) * c:j * c])
        s = jnp.where(same_grp, jnp.concatenate(pieces, axis=0), s_far)

        bq2, wq2 = bq * LOG2E, wq * LOG2E
        kds = [kin.astype(BF16)] + [jnp.exp2(pltpu.roll(bq2, c - d, 0) + wq2).astype(BF16) for d in range(1, sub)]
        g_diag = _dot_nt(qf.astype(BF16), jnp.concatenate(kds, axis=0))
        last = slice(c - 1, c)
        return dict(o=o, s=s, g_diag=g_diag, wb=wb, b_last=rg[last, :] + rs[last, :] + bq[last, :])

    def finish(hh, r0, h):
        cols = slice(hh * 128, (hh + 1) * 128)
        vb = i_refs[hh // 2][pl.ds(r0, c), in_cols(hh)].astype(BF16)
        gt = g_refs[hh // 2][pl.ds(r0, c), in_cols(hh)]
        o = h["o"] + jnp.dot(h["s"].astype(BF16), vb, preferred_element_type=F32)
        st_ref[hh] = (st_ref[hh] * jnp.exp(h["b_last"])
                      + _dot_tn(vb, jnp.exp(h["b_last"] + h["wb"]).astype(BF16)))
        o = o * lax.rsqrt(jnp.mean(o * o, axis=-1, keepdims=True) + RMS_EPS)
        o_ref[pl.ds(r0, c), cols] = (o * ng_ref[0, :, cols] * (gt * jax.nn.sigmoid(gt))).astype(o_ref.dtype)

    def chunk(ci, carry):
        r0 = pl.multiple_of(ci * c, c)
        hs = [scores(hh, r0) for hh in heads]
        for d in range(sub):
            on_diag = band == d
            for h in hs:
                h["s"] = jnp.where(on_diag, h["g_diag"][:, d * c:(d + 1) * c], h["s"])
        for hh in heads:
            finish(hh, r0, hs[hh])
        return carry

    lax.fori_loop(0, HGRN_ROWS // c, chunk, 0, unroll=True)


def _hgrn(proj, lb, norm_g, bsz, seq):
    nr = seq // HGRN_ROWS
    hp, nin = HGRN_HEADS_PER_STEP, HGRN_IN_BLOCKS
    width = hp * 128
    c0 = (A_QW + 2 * A_KVW) // 256
    per_tensor = B_HEADS // 2
    nhp = B_HEADS // hp
    in_specs = [pl.BlockSpec((HGRN_ROWS, 256),
                             lambda b, h, r, t=t, p=p: (b * nr + r, c0 + t * per_tensor + h * nin + p))
                for t in range(4) for p in range(nin)]
    vec = pl.BlockSpec((1, 1, width), lambda b, h, r: (h, 0, 0))
    return pl.pallas_call(
        _hgrn_kernel,
        grid=(bsz, nhp, nr),
        in_specs=in_specs + [vec, vec],
        out_specs=pl.BlockSpec((HGRN_ROWS, width), lambda b, h, r: (b * nr + r, h)),
        out_shape=jax.ShapeDtypeStruct((bsz * seq, B_VW), BF16),
        scratch_shapes=[pltpu.VMEM((hp, B_DV, B_DK), F32)],
        compiler_params=_cparams(3, 32 * MIB),
        name="hgrn2",
    )(*([proj] * (4 * nin)), lb.reshape(nhp, 1, width), norm_g.reshape(nhp, 1, width))


OUTPROJ_LAG = 2


def _outproj_kernel(ya_ref, yb_ref, wa_ref, wb_ref, x_ref, g_ref, b_ref, wr_ref, rb_ref,
                    xo_ref, xp_ref, rt_ref, rtt_ref, cnt_ref, run_ref, lg_ref, mix_ref):
    step = pl.program_id(0)

    @pl.when(step == 0)
    def _():
        run_ref[...] = jnp.zeros_like(run_ref)
        lg_ref[...] = jnp.zeros_like(lg_ref)
        mix_ref[...] = jnp.zeros_like(mix_ref)

    lg_prev = lg_ref[...]
    mix_prev = mix_ref[...]
    mix = jnp.dot(ya_ref[...], wa_ref[...], preferred_element_type=F32)
    mix_ref[...] = mix + jnp.dot(yb_ref[...], wb_ref[...], preferred_element_type=F32)

    y = _layer_norm_rows(DN_ALPHA * x_ref[...] + mix_prev, g_ref[...], b_ref[...])
    xo_ref[...] = y
    _store_token_tiles(xp_ref, 0, _pack_rows(y))
    tm = y.shape[0]
    y_hi = y.astype(BF16)
    y_lo = (y - y_hi.astype(F32)).astype(BF16)
    prod = jnp.dot(jnp.concatenate([y_hi, y_lo], axis=0), wr_ref[...], preferred_element_type=F32)
    lg = (prod[:tm, :ROUTE_W] + prod[tm:, :ROUTE_W]) + (prod[:tm, ROUTE_W:] + prod[tm:, ROUTE_W:])
    lg_ref[...] = lg + rb_ref[...]
    _route_rows(lg_prev, (step >= OUTPROJ_LAG).astype(F32), rt_ref, rtt_ref, cnt_ref, run_ref)


def _outproj_ln_route(ya, yb, w_out, x, g, b, wr, rbias, tm=512):
    n = x.shape[0]
    ka = ya.shape[1]
    nsteps = n // tm
    lagged = lambda lag: (lambda i: (jnp.clip(i - lag, 0, nsteps - 1), 0))
    row = lambda width, lag: pl.BlockSpec((tm, width), lagged(lag))
    once = pl.Buffered(1)
    const = lambda shape: pl.BlockSpec(shape, lambda i: (0, 0), pipeline_mode=once)
    return pl.pallas_call(
        _outproj_kernel,
        grid=(nsteps + OUTPROJ_LAG,),
        in_specs=[row(ka, 0), row(ka, 0),
                  pl.BlockSpec((ka, D_MODEL), lambda i: (0, 0), pipeline_mode=once),
                  pl.BlockSpec((ka, D_MODEL), lambda i: (1, 0), pipeline_mode=once),
                  row(D_MODEL, 1), const((1, D_MODEL)), const((1, D_MODEL)),
                  const((D_MODEL, 2 * ROUTE_W)), const((1, ROUTE_W))],
        out_specs=[row(D_MODEL, 1), pl.BlockSpec((tm * TOK_TILE, LANES_V7X), lagged(1)),
                   row(ROUTE_W, OUTPROJ_LAG),
                   pl.BlockSpec((ROUTE_FIELDS, tm), lambda i: (0, jnp.clip(i - OUTPROJ_LAG, 0, nsteps - 1))),
                   pl.BlockSpec((1, ROUTE_W), lambda i: (0, 0))],
        out_shape=[jax.ShapeDtypeStruct((n, D_MODEL), F32),
                   jax.ShapeDtypeStruct((n * TOK_TILE, LANES_V7X), U32),
                   jax.ShapeDtypeStruct((n, ROUTE_W), F32), jax.ShapeDtypeStruct((ROUTE_FIELDS, n), F32),
                   jax.ShapeDtypeStruct((1, ROUTE_W), F32)],
        scratch_shapes=[pltpu.VMEM((1, ROUTE_W), F32), pltpu.VMEM((tm, ROUTE_W), F32),
                        pltpu.VMEM((tm, D_MODEL), F32)],
        compiler_params=_cparams(1, VMEM_LIMIT_V7X),
        name="outproj_ln_route",
    )(ya, yb, w_out, w_out, x, g.reshape(1, -1), b.reshape(1, -1), wr, rbias)


def _router_weights(w_group, b_group, w_router, b_router):
    w = jnp.zeros((D_MODEL, ROUTE_W), F32)
    w = w.at[:, :N_GROUPS].set(w_group).at[:, N_GROUPS:N_GROUPS + N_EXPERTS].set(w_router)
    w_hi = w.astype(BF16)
    w_lo = (w - w_hi.astype(F32)).astype(BF16)
    rb = jnp.zeros((1, ROUTE_W), F32)
    rb = rb.at[0, :N_GROUPS].set(b_group).at[0, N_GROUPS:N_GROUPS + N_EXPERTS].set(b_router)
    return jnp.concatenate([w_hi, w_lo], axis=1), rb


def _route_rows(lg, live, rt_ref, rtt_ref, cnt_ref, run_ref):
    tm = lg.shape[0]
    lane = lax.broadcasted_iota(jnp.int32, lg.shape, 1)
    sentinel = jnp.int32(ROUTE_W)
    rowmax = lambda mask: jnp.max(jnp.where(mask, lg, NEG_INF), axis=-1, keepdims=True)
    first = lambda mask: jnp.min(jnp.where(mask, lane, sentinel), axis=-1, keepdims=True)

    is_g = lane < N_GROUPS
    gmax = rowmax(is_g)
    g_idx = first(is_g & (lg == gmax))
    g_w = 1.0 / jnp.sum(jnp.where(is_g, jnp.exp(lg - gmax), 0.0), axis=-1, keepdims=True)

    e_lane = lane - N_GROUPS
    sel = (e_lane >= 0) & (e_lane < N_EXPERTS) & ((e_lane >> 3) == g_idx)
    m1 = rowmax(sel)
    i1 = first(sel & (lg == m1))
    sel2 = sel & (lane != i1)
    m2 = rowmax(sel2)
    i2 = first(sel2 & (lg == m2))
    ex = jnp.exp(m2 - m1)
    w0 = g_w / (1.0 + ex)
    w1 = g_w * ex / (1.0 + ex)

    oh0 = lane == i1
    oh1 = lane == i2
    both = (oh0 | oh1).astype(F32) * live
    rid = lax.broadcasted_iota(jnp.int32, (tm, tm), 0)
    cid = lax.broadcasted_iota(jnp.int32, (tm, tm), 1)
    before = jnp.dot((cid < rid).astype(BF16), both.astype(BF16), preferred_element_type=F32) + run_ref[...]
    rank0 = jnp.sum(jnp.where(oh0, before, 0.0), axis=-1, keepdims=True)
    rank1 = jnp.sum(jnp.where(oh1, before, 0.0), axis=-1, keepdims=True)
    run = run_ref[...] + jnp.sum(both, axis=0, keepdims=True)
    run_ref[...] = run
    cnt_ref[...] = run

    slab = jnp.zeros(lg.shape, F32)
    for ln, val in ((RT_W0, w0), (RT_W1, w1), (RT_E0, (i1 - N_GROUPS).astype(F32)),
                    (RT_E1, (i2 - N_GROUPS).astype(F32)), (RT_R0, rank0), (RT_R1, rank1)):
        slab = jnp.where(lane == ln, val, slab)
    rt_ref[...] = slab
    rtt_ref[...] = jnp.transpose(slab)[:ROUTE_FIELDS, :]


assert EXPERTS_PER_GROUP == 8


def _block_layout(rtt, cnt, n_tok):
    m = n_tok * TOP_K
    counts = cnt[0, N_GROUPS:N_GROUPS + N_EXPERTS].astype(jnp.int32)
    pcounts = (counts + MOE_TB - 1) // MOE_TB * MOE_TB
    pends = jnp.cumsum(pcounts)
    pstarts = pends - pcounts
    n_blocks = -(-(m + N_EXPERTS * (MOE_TB - 1)) // MOE_TB)
    e_idx = rtt[RT_E0:RT_E1 + 1, :].astype(jnp.int32)
    rank = rtt[RT_R0:RT_R1 + 1, :].astype(jnp.int32)
    onehot = e_idx[:, None, :] == jnp.arange(N_EXPERTS, dtype=jnp.int32)[None, :, None]
    pos = jnp.sum(jnp.where(onehot, pstarts[None, :, None], 0), axis=1) + rank
    blk_start = jnp.arange(n_blocks, dtype=jnp.int32) * MOE_TB
    blk_e = jnp.minimum(jnp.sum(blk_start[:, None] >= pends[None, :], axis=-1), N_EXPERTS - 1).astype(jnp.int32)
    nused = (pends[-1:] // MOE_TB).astype(jnp.int32)
    ids = jnp.arange(N_EXPERTS, dtype=jnp.int32)
    later_used = (ids[None, :] > ids[:, None]) & (counts[None, :] > 0)
    next_used = jnp.min(jnp.where(later_used, ids[None, :], N_EXPERTS), axis=-1)
    next_used = jnp.where(next_used == N_EXPERTS, -1, next_used).astype(jnp.int32)
    after_next = jnp.sum(jnp.where(next_used[:, None] == ids[None, :], next_used[None, :], 0), axis=-1)
    after_next = jnp.where(next_used >= 0, after_next, -1)
    of_block = lambda table: jnp.sum(jnp.where(blk_e[:, None] == ids[None, :], table[None, :], 0), axis=-1)
    nxt_e = jnp.stack([of_block(next_used), of_block(after_next)]).astype(jnp.int32)
    ends_expert = jnp.any(((blk_start + MOE_TB)[:, None] == pends[None, :]) & (pcounts[None, :] > 0), axis=-1)
    zero_blk = (ends_expert | (blk_start >= pends[-1])).astype(jnp.int32)
    return pos, blk_e, nxt_e, nused, zero_blk


def _step_indices(pos, tm):
    nsteps = pos.shape[1] // tm
    return pos.reshape(TOP_K, nsteps, tm).transpose(1, 0, 2).reshape(nsteps, 1, TOP_K * tm)


def _tile_rows(i):
    return pl.ds(pl.multiple_of(i * TOK_TILE, TOK_TILE), TOK_TILE)


def _dispatch_kernel(zero_blk_ref, pos_ref, xp_ref, xs_hbm, xbuf, zbuf, sem, zsem, *, tm, n_blocks):
    s = pl.program_id(0)
    nsteps = pl.num_programs(0)
    slot = s % 2
    blk_rows = MOE_TB * TOK_TILE

    def copy(j, dst, sl):
        return pltpu.make_async_copy(xbuf.at[sl, _tile_rows(j)], xs_hbm.at[_tile_rows(dst)], sem.at[sl])

    def drain(sl):
        def body(j, c):
            copy(0, 0, sl).wait()
            return c
        lax.fori_loop(0, TOP_K * tm, body, 0, unroll=8)

    @pl.when(s == 0)
    def _():
        zbuf[...] = jnp.zeros_like(zbuf)

        def zero_copy(blk):
            rows = pl.ds(pl.multiple_of(blk * blk_rows, blk_rows), blk_rows)
            return pltpu.make_async_copy(zbuf, xs_hbm.at[rows], zsem)

        def each_flagged(fn):
            def body(blk, c):
                @pl.when(zero_blk_ref[blk] > 0)
                def _():
                    fn(zero_copy(blk))
                return c
            lax.fori_loop(0, n_blocks, body, 0)

        each_flagged(lambda cp: cp.start())
        each_flagged(lambda cp: cp.wait())

    @pl.when(s >= 2)
    def _():
        drain(slot)

    xbuf[slot] = xp_ref[...]

    for j in range(tm):
        for k in range(TOP_K):
            copy(j, pos_ref[0, 0, k * tm + j], slot).start(priority=k)

    @pl.when(s == nsteps - 1)
    def _():
        drain(1 - slot)
        drain(slot)


def _moe_dispatch(xp, pos, zero_blk, tm=512):
    n = pos.shape[1]
    n_blocks = zero_blk.shape[0]
    assert n // tm >= 2
    pos3 = _step_indices(pos, tm)
    grid_spec = pltpu.PrefetchScalarGridSpec(
        num_scalar_prefetch=1,
        grid=(n // tm,),
        in_specs=[pl.BlockSpec((1, 1, TOP_K * tm), lambda i, zb: (i, 0, 0), memory_space=pltpu.SMEM),
                  pl.BlockSpec((tm * TOK_TILE, LANES_V7X), lambda i, zb: (i, 0))],
        out_specs=pl.BlockSpec(memory_space=pl.ANY),
        scratch_shapes=[pltpu.VMEM((2, tm * TOK_TILE, LANES_V7X), U32),
                        pltpu.VMEM((MOE_TB * TOK_TILE, LANES_V7X), U32),
                        pltpu.SemaphoreType.DMA((2,)), pltpu.SemaphoreType.DMA(())],
    )
    return pl.pallas_call(
        functools.partial(_dispatch_kernel, tm=tm, n_blocks=n_blocks),
        grid_spec=grid_spec,
        out_shape=jax.ShapeDtypeStruct((n_blocks * MOE_TB * TOK_TILE, LANES_V7X), U32),
        compiler_params=_cparams(1),
        name="moe_dispatch",
    )(zero_blk, pos3, xp)


MOE_STAGING = 3


def _moe_kernel(blk_e_ref, nxt_e_ref, nused_ref, xs_ref, w1_hbm, w3_hbm, w2_hbm, ys_ref,
                wf1, wf3, wf2, w1b, w3b, w2b, slot_ref, sem, *, layer):
    s = pl.program_id(0)
    nused = nused_ref[0]

    def fetch(e, sl):
        return [pltpu.make_async_copy(w_hbm.at[layer, e], wf.at[sl], sem.at[sl, k])
                for k, (w_hbm, wf) in enumerate(((w1_hbm, wf1), (w3_hbm, wf3), (w2_hbm, wf2)))]

    @pl.when(s >= nused)
    def _():
        ys_ref[...] = jnp.zeros_like(ys_ref)

    @pl.when(s < nused)
    def _():
        e = blk_e_ref[s]
        prev = blk_e_ref[jnp.maximum(s - 1, 0)]

        @pl.when(s == 0)
        def _():
            slot_ref[0] = 0
            for cp in fetch(e, 0):
                cp.start()

            @pl.when(nxt_e_ref[0, 0] >= 0)
            def _():
                for cp in fetch(nxt_e_ref[0, 0], 1):
                    cp.start()

        @pl.when((s > 0) & (e != prev))
        def _():
            slot_ref[0] = (slot_ref[0] + 1) % MOE_STAGING

        first = (s == 0) | (e != prev)

        def ffn(weights):
            xb = _unpack_rows(_load_token_tiles(xs_ref, 0, MOE_TB)).astype(BF16)
            h1 = jnp.dot(xb, weights(0), preferred_element_type=F32)
            h3 = jnp.dot(xb, weights(1), preferred_element_type=F32)
            h = (h1 * jax.nn.sigmoid(h1) * h3).astype(BF16)
            y = jnp.dot(h, weights(2), preferred_element_type=F32)
            _store_token_tiles(ys_ref, 0, _pack_rows(y))

        @pl.when(first)
        def _():
            sl = slot_ref[0]
            for cp in fetch(e, sl):
                cp.wait()
            ahead = nxt_e_ref[1, s]

            @pl.when(ahead >= 0)
            def _():
                for cp in fetch(ahead, (sl + 2) % MOE_STAGING):
                    cp.start()

            def cast(k):
                wb = (w1b, w3b, w2b)[k]
                w = (wf1, wf3, wf2)[k][sl].astype(BF16)
                wb[...] = w
                return w
            ffn(cast)

        @pl.when(jnp.logical_not(first))
        def _():
            ffn(lambda k: (w1b, w3b, w2b)[k][...])


def _moe_experts(xs, blk_e, nxt_e, nused, w1, w3, w2, layer):
    n_blocks = blk_e.shape[0]
    tiles = pl.BlockSpec((MOE_TB * TOK_TILE, LANES_V7X), lambda s, be, ne, nu: (s, 0))
    hbm = pl.BlockSpec(memory_space=pl.ANY)
    up, down = (D_MODEL, D_EXPERT), (D_EXPERT, D_MODEL)
    grid_spec = pltpu.PrefetchScalarGridSpec(
        num_scalar_prefetch=3,
        grid=(n_blocks,),
        in_specs=[tiles, hbm, hbm, hbm],
        out_specs=tiles,
        scratch_shapes=[pltpu.VMEM((MOE_STAGING,) + up, F32), pltpu.VMEM((MOE_STAGING,) + up, F32),
                        pltpu.VMEM((MOE_STAGING,) + down, F32),
                        pltpu.VMEM(up, BF16), pltpu.VMEM(up, BF16), pltpu.VMEM(down, BF16),
                        pltpu.SMEM((1,), jnp.int32), pltpu.SemaphoreType.DMA((MOE_STAGING, 3))],
    )
    return pl.pallas_call(
        functools.partial(_moe_kernel, layer=layer),
        grid_spec=grid_spec,
        out_shape=jax.ShapeDtypeStruct(xs.shape, U32),
        compiler_params=_cparams(1, VMEM_LIMIT_V7X),
        name="moe_experts",
    )(blk_e, nxt_e, nused, xs, w1, w3, w2)


def _combine_kernel(pos_ref, nxt_ref, ys_hbm, x_ref, rt_ref, g_ref, b_ref, o_ref, ybuf, sem, *, tm):
    s = pl.program_id(0)
    nsteps = pl.num_programs(0)
    slot = s % 2

    def copy(src, j, sl):
        return pltpu.make_async_copy(ys_hbm.at[_tile_rows(src)], ybuf.at[sl, _tile_rows(j)], sem.at[sl])

    def start_gather(idx_ref, sl):
        def body(j, c):
            copy(idx_ref[0, 0, j], j, sl).start()
            return c
        lax.fori_loop(0, TOP_K * tm, body, 0, unroll=8)

    @pl.when(s == 0)
    def _():
        start_gather(pos_ref, 0)

    @pl.when(s + 1 < nsteps)
    def _():
        for j in range(TOP_K * tm):
            copy(nxt_ref[0, 0, j], j, 1 - slot).start()

    def wait_body(j, c):
        copy(0, j, slot).wait()
        return c
    lax.fori_loop(0, TOP_K * tm, wait_body, 0, unroll=8)

    rt = rt_ref[...]
    ffn = _unpack_rows(_load_token_tiles(ybuf, 0, tm, lead=slot)) * rt[:, RT_W0:RT_W0 + 1]
    ffn = ffn + _unpack_rows(_load_token_tiles(ybuf, tm * TOK_TILE, tm, lead=slot)) * rt[:, RT_W1:RT_W1 + 1]
    o_ref[...] = _layer_norm_rows(DN_ALPHA * x_ref[...] + ffn, g_ref[...], b_ref[...])


def _moe_combine(ys, pos, rt, x, g, b, tm=256):
    n = x.shape[0]
    nsteps = n // tm
    pos3 = _step_indices(pos, tm)
    smem_blk = lambda f: pl.BlockSpec((1, 1, TOP_K * tm), f, memory_space=pltpu.SMEM)
    row = lambda width: pl.BlockSpec((tm, width), lambda i: (i, 0))
    const = lambda shape: pl.BlockSpec(shape, lambda i: (0, 0))
    return pl.pallas_call(
        functools.partial(_combine_kernel, tm=tm),
        grid=(nsteps,),
        in_specs=[smem_blk(lambda i: (i, 0, 0)), smem_blk(lambda i: (jnp.minimum(i + 1, nsteps - 1), 0, 0)),
                  pl.BlockSpec(memory_space=pl.ANY), row(D_MODEL), row(ROUTE_W),
                  const((1, D_MODEL)), const((1, D_MODEL))],
        out_specs=row(D_MODEL),
        out_shape=jax.ShapeDtypeStruct((n, D_MODEL), F32),
        scratch_shapes=[pltpu.VMEM((2, TOP_K * tm * TOK_TILE, LANES_V7X), U32), pltpu.SemaphoreType.DMA((2,))],
        compiler_params=_cparams(1, 40 * MIB),
        name="moe_combine",
    )(pos3, pos3, ys, x, rt, g.reshape(1, -1), b.reshape(1, -1))


def _moe_layer(x1, xp, rt, rtt, cnt, w1, w3, w2, layer, g, b):
    n_tok = x1.shape[0]
    pos, blk_e, nxt_e, nused, zero_blk = _block_layout(rtt, cnt, n_tok)
    xs = _moe_dispatch(xp, pos, zero_blk)
    ys = _moe_experts(xs, blk_e, nxt_e, nused, w1, w3, w2, layer)
    return _moe_combine(ys, pos, rt, x1, g, b)


def _gmlp_kernel(u_ref, v_ref, g_ref, b_ref, w_ref, bs_ref, o_ref, *, chunks):
    for ci in range(chunks):
        rows = slice(ci * C_CHUNK, (ci + 1) * C_CHUNK)
        u = jax.nn.gelu(u_ref[rows, :])
        v = _layer_norm_rows(jax.nn.gelu(v_ref[rows, :]), g_ref[...], b_ref[...]).astype(BF16)
        for gi in range(C_GROUPS):
            cols = slice(gi * C_GROUP_DIM, (gi + 1) * C_GROUP_DIM)
            mixed = jnp.dot(w_ref[gi], v[:, cols], preferred_element_type=F32) + bs_ref[:, cols]
            o_ref[rows, cols] = (u[:, cols] * mixed).astype(o_ref.dtype)


def _gmlp(proj, ln_g, ln_b, w_s, b_s, chunks=4):
    n = proj.shape[0]
    tm = chunks * C_CHUNK
    w = (w_s * jnp.tril(jnp.ones((C_CHUNK, C_CHUNK), w_s.dtype))).astype(BF16)
    bs_full = jnp.repeat(b_s.T, C_GROUP_DIM, axis=1)
    const2 = lambda shape: pl.BlockSpec(shape, lambda i: (0, 0))
    return pl.pallas_call(
        functools.partial(_gmlp_kernel, chunks=chunks),
        grid=(n // tm,),
        in_specs=[pl.BlockSpec((tm, C_W), lambda i: (i, 0)), pl.BlockSpec((tm, C_W), lambda i: (i, 1)),
                  const2((1, C_W)), const2((1, C_W)),
                  pl.BlockSpec((C_GROUPS, C_CHUNK, C_CHUNK), lambda i: (0, 0, 0)), const2((C_CHUNK, C_W))],
        out_specs=pl.BlockSpec((tm, C_W), lambda i: (i, 0)),
        out_shape=jax.ShapeDtypeStruct((n, C_W), BF16),
        compiler_params=_cparams(1, 32 * MIB),
        name="gmlp_gating",
    )(proj, proj, ln_g.reshape(1, -1), ln_b.reshape(1, -1), w, bs_full)


CONV_HIST = 32


def _conv_kernel(a_ref, gt_ref, ap_ref, gp_ref, w_ref, cb_ref, g_ref, b_ref, o_ref, hbuf, hshift, *, ts):
    i = pl.program_id(1)
    hist = ap_ref[...] * jax.nn.sigmoid(gp_ref[...])
    hbuf[0:CONV_HIST, :] = jnp.where(i > 0, hist, jnp.zeros_like(hist))
    hbuf[CONV_HIST:CONV_HIST + ts, :] = a_ref[...] * jax.nn.sigmoid(gt_ref[...])
    off = CONV_HIST - (D_CONV - 1)
    acc = jnp.zeros((ts, D_CHANNELS), F32) + cb_ref[...]
    for r in range(SUBLANES_V7X):
        taps = [j for j in range(D_CONV) if (off + j) % SUBLANES_V7X == r]
        if not taps:
            continue
        src = hbuf
        if r:
            span = max(taps) + off - r + ts
            hshift[0:span, :] = hbuf[r:r + span, :]
            src = hshift
        for j in taps:
            base = off + j - r
            acc = acc + w_ref[j:j + 1, :] * src[base:base + ts, :]
    y = _layer_norm_rows(acc, g_ref[...], b_ref[...])
    o_ref[...] = (y * jax.nn.sigmoid(y)).astype(o_ref.dtype)


def _conformer_conv(proj, conv_w, conv_b, ln_g, ln_b, bsz, seq, ts=512):
    nt = seq // ts
    acol = 2 * C_W // D_CHANNELS
    gcol = acol + 1
    hb = ts // CONV_HIST
    cur = lambda col: pl.BlockSpec((ts, D_CHANNELS), lambda b, i: (b * nt + i, col))
    prev = lambda col: pl.BlockSpec((CONV_HIST, D_CHANNELS),
                                    lambda b, i: (jnp.maximum((b * nt + i) * hb - 1, 0), col))
    const2 = lambda shape: pl.BlockSpec(shape, lambda b, i: (0, 0))
    return pl.pallas_call(
        functools.partial(_conv_kernel, ts=ts),
        grid=(bsz, nt),
        in_specs=[cur(acol), cur(gcol), prev(acol), prev(gcol),
                  const2((D_CONV, D_CHANNELS)), const2((1, D_CHANNELS)), const2((1, D_CHANNELS)),
                  const2((1, D_CHANNELS))],
        out_specs=pl.BlockSpec((ts, D_CHANNELS), lambda b, i: (b * nt + i, 0)),
        out_shape=jax.ShapeDtypeStruct((bsz * seq, D_CHANNELS), BF16),
        scratch_shapes=[pltpu.VMEM((CONV_HIST + ts, D_CHANNELS), F32), pltpu.VMEM((CONV_HIST + ts, D_CHANNELS), F32)],
        compiler_params=_cparams(2, 32 * MIB),
        name="conformer_conv",
    )(proj, proj, proj, proj, conv_w, conv_b.reshape(1, -1), ln_g.reshape(1, -1), ln_b.reshape(1, -1))


def kernel(x, w_in_ab, attn_sinks, rel_bias, hgrn_lb_logits, hgrn_norm_g, w_out_ab, w_in_cd, gmlp_ln_g, gmlp_ln_b, gmlp_w_s, gmlp_b_s, conv_w, conv_b, conv_ln_g, conv_ln_b, w_out_cd, ln_mix_g, ln_mix_b, ln_ffn_g, ln_ffn_b, moe_w_group, moe_b_group, moe_w_router, moe_b_router, moe_w1, moe_w3, moe_w2):
    bsz, seq = x.shape[0], x.shape[1]
    n_tok = bsz * seq
    xf = x.reshape(n_tok, D_MODEL)
    lb_table = jnp.cumsum(jax.nn.softmax(hgrn_lb_logits.astype(F32), axis=0), axis=0)
    bias = _bias_table(rel_bias.astype(F32), _t5_bucket_table())

    for layer in range(DEPTH):
        j = layer // 2
        if layer % 2 == 0:
            proj = _matmul(xf, w_in_ab[j].astype(BF16), 1024, EVEN_IN // 3, F32)
            ya = _attention(proj, bias, attn_sinks[j].astype(F32), bsz, seq)
            yb = _hgrn(proj, lb_table[layer], hgrn_norm_g[j].astype(F32), bsz, seq)
            w_out = w_out_ab[j]
        else:
            proj = _matmul(xf, w_in_cd[j].astype(BF16), 1024, ODD_IN // 2, F32)
            ya = _gmlp(proj, gmlp_ln_g[j], gmlp_ln_b[j], gmlp_w_s[j], gmlp_b_s[j])
            yb = _conformer_conv(proj, conv_w[j], conv_b[j], conv_ln_g[j], conv_ln_b[j], bsz, seq)
            w_out = w_out_cd[j]
        wr, rbias = _router_weights(moe_w_group[layer], moe_b_group[layer],
                                    moe_w_router[layer], moe_b_router[layer])
        x1, xp, rt, rtt, cnt = _outproj_ln_route(ya, yb, w_out.astype(BF16), xf, ln_mix_g[layer],
                                                 ln_mix_b[layer], wr, rbias)
        xf = _moe_layer(x1, xp, rt, rtt, cnt, moe_w1, moe_w3, moe_w2, layer, ln_ffn_g[layer], ln_ffn_b[layer])
    return xf.reshape(bsz, seq, D_MODEL)
```

```python
import functools
import math

import jax
import jax.numpy as jnp
from jax import lax
from jax.experimental import pallas as pl
from jax.experimental.pallas import tpu as pltpu

D_MODEL = 2048
DEPTH = 2
A_HEADS = 16
A_KV_HEADS = 2
A_HEAD_DIM = 64
A_WINDOW = 128
A_BLOCK = 128
REL_BUCKETS = 32
REL_MAX_DIST = 128
B_HEADS = 8
B_DK = 128
B_DV = 128
C_GROUPS = 8
C_GROUP_DIM = 128
C_CHUNK = 128
D_CHANNELS = 1024
D_CONV = 31
A_QW = A_HEADS * A_HEAD_DIM
A_KVW = A_KV_HEADS * A_HEAD_DIM
B_KW = B_HEADS * B_DK
B_VW = B_HEADS * B_DV
C_W = C_GROUPS * C_GROUP_DIM
EVEN_IN = A_QW + 2 * A_KVW + 2 * B_KW + 2 * B_VW
ODD_IN = 2 * C_W + 2 * D_CHANNELS
N_GROUPS = 4
EXPERTS_PER_GROUP = 8
N_EXPERTS = N_GROUPS * EXPERTS_PER_GROUP
TOP_K = 2
D_EXPERT = 512
DN_ALPHA = (2 * DEPTH) ** 0.25
LN_EPS = 1e-5
RMS_EPS = 1e-6

LANES_V7X = 128
SUBLANES_V7X = 8
MIB = 1024 * 1024
VMEM_LIMIT_V7X = 56 * MIB

HGRN_CHUNK = 128
HGRN_SUB = 8
HGRN_GROUP = 32
HGRN_ROWS = 512
MOE_TB = 256
ROUTE_W = LANES_V7X
HALF = D_MODEL // 2
TOK_TILE = HALF // LANES_V7X
assert TOK_TILE == SUBLANES_V7X

BF16 = jnp.bfloat16
F32 = jnp.float32
U32 = jnp.uint32
NEG_INF = float("-inf")
LOG2E = math.log2(math.e)
RT_W0, RT_W1, RT_E0, RT_E1, RT_R0, RT_R1 = range(6)
ROUTE_FIELDS = SUBLANES_V7X


def _cparams(n_axes, vmem_bytes=None):
    return pltpu.CompilerParams(dimension_semantics=("arbitrary",) * n_axes, vmem_limit_bytes=vmem_bytes)


def _layer_norm_rows(z, g, b):
    mu = jnp.mean(z, axis=-1, keepdims=True)
    zc = z - mu
    var = jnp.mean(zc * zc, axis=-1, keepdims=True)
    return zc * lax.rsqrt(var + LN_EPS) * g + b


def _dot_nt(a, b):
    return lax.dot_general(a, b, (((1,), (1,)), ((), ())), preferred_element_type=F32)


def _dot_tn(a, b):
    return lax.dot_general(a, b, (((0,), (0,)), ((), ())), preferred_element_type=F32)


def _pack_rows(y):
    lo = lax.bitcast_convert_type(y[:, :HALF].astype(BF16).astype(F32), U32) >> 16
    hi = lax.bitcast_convert_type(y[:, HALF:].astype(BF16).astype(F32), U32) & jnp.uint32(0xFFFF0000)
    return lo | hi


def _unpack_rows(p):
    lo = lax.bitcast_convert_type(p << 16, F32)
    hi = lax.bitcast_convert_type(p & jnp.uint32(0xFFFF0000), F32)
    return jnp.concatenate([lo, hi], axis=1)


def _store_token_tiles(ref, base, packed):
    rows = packed.shape[0]
    for c in range(TOK_TILE):
        ref[pl.ds(base + c, rows, stride=TOK_TILE), :] = packed[:, c * LANES_V7X:(c + 1) * LANES_V7X]


def _load_token_tiles(ref, base, rows, lead=None):
    parts = []
    for c in range(TOK_TILE):
        idx = (pl.ds(base + c, rows, stride=TOK_TILE), slice(None))
        parts.append(ref[idx] if lead is None else ref[(lead,) + idx])
    return jnp.concatenate(parts, axis=1)


def _mm_kernel(a_ref, w_ref, o_ref):
    a = a_ref[...].astype(BF16)
    o_ref[...] = jnp.dot(a, w_ref[...], preferred_element_type=F32).astype(o_ref.dtype)


def _matmul(a, w, tm, tn, out_dtype):
    m, k = a.shape
    n = w.shape[1]
    return pl.pallas_call(
        _mm_kernel,
        grid=(n // tn, m // tm),
        in_specs=[pl.BlockSpec((tm, k), lambda j, i: (i, 0)), pl.BlockSpec((k, tn), lambda j, i: (0, j))],
        out_specs=pl.BlockSpec((tm, tn), lambda j, i: (i, j)),
        out_shape=jax.ShapeDtypeStruct((m, n), out_dtype),
        compiler_params=_cparams(2, VMEM_LIMIT_V7X),
        name="proj_matmul",
    )(a, w)


def _bias_table_kernel(rb_ref, bucket_ref, o_ref):
    bucket = bucket_ref[...]
    for h in range(A_HEADS):
        acc = jnp.zeros(bucket.shape, F32)
        for bk in range(REL_BUCKETS):
            acc = jnp.where(bucket == bk, rb_ref[bk, h], acc)
        o_ref[h] = acc


def _bias_table(rel_bias, bucket):
    return pl.pallas_call(
        _bias_table_kernel,
        in_specs=[pl.BlockSpec(memory_space=pltpu.SMEM), pl.BlockSpec(memory_space=pltpu.VMEM)],
        out_specs=pl.BlockSpec(memory_space=pltpu.VMEM),
        out_shape=jax.ShapeDtypeStruct((A_HEADS,) + bucket.shape, F32),
        name="rel_bias_table",
    )(rel_bias, bucket)


def _t5_bucket_table():
    t_loc = jnp.arange(A_BLOCK, dtype=jnp.int32)[:, None]
    j_loc = jnp.arange(A_BLOCK, dtype=jnp.int32)[None, :]
    dist = jnp.where(j_loc <= t_loc, t_loc - j_loc, t_loc + A_BLOCK - j_loc)
    max_exact = REL_BUCKETS // 2
    d = jnp.maximum(dist, 1).astype(F32)
    large = max_exact + (jnp.log(d / max_exact) / math.log(REL_MAX_DIST / max_exact)
                         * (REL_BUCKETS - max_exact)).astype(jnp.int32)
    large = jnp.minimum(large, REL_BUCKETS - 1)
    return jnp.where(dist < max_exact, dist, large)


assert A_WINDOW == A_BLOCK


ATTN_BLOCKS_PER_STEP = 8


def _attn_kernel(q_ref, kc_ref, vc_ref, kp_ref, vp_ref, bias_ref, sink_ref, o_ref):
    blk = A_BLOCK
    first_pair = pl.program_id(1) == 0
    for sb in range(ATTN_BLOCKS_PER_STEP):
        rows = slice(sb * blk, (sb + 1) * blk)
        before = slice((sb - 1) * blk, sb * blk)
        k_prev, v_prev = (kp_ref[...], vp_ref[...]) if sb == 0 else (kc_ref[before, :], vc_ref[before, :])
        no_prev = jnp.where(first_pair, NEG_INF, 0.0) if sb == 0 else 0.0
        _attn_block(q_ref, rows, k_prev, v_prev, kc_ref[rows, :], vc_ref[rows, :], no_prev, bias_ref, sink_ref, o_ref)


def _attn_block(q_ref, rows, k_prev, v_prev, k_own, v_own, no_prev, bias_ref, sink_ref, o_ref):
    blk = A_BLOCK
    k2 = jnp.concatenate([k_prev, k_own], axis=0)
    v2 = jnp.concatenate([v_prev, v_own], axis=0)
    k2r = pltpu.roll(k2, A_HEAD_DIM, 1)
    v2r = pltpu.roll(v2, A_HEAD_DIM, 1)
    lo = lax.broadcasted_iota(jnp.int32, k2.shape, 1) < A_HEAD_DIM
    zero = jnp.zeros_like(k2)

    def placed(x, xr, g, par):
        src = x if g == par else xr
        return (jnp.where(lo, src, zero) if par == 0 else jnp.where(lo, zero, src)).astype(BF16)

    kk = [[placed(k2, k2r, g, par) for par in range(2)] for g in range(A_KV_HEADS)]
    vv = [[placed(v2, v2r, g, par) for par in range(2)] for g in range(A_KV_HEADS)]

    own = (lax.broadcasted_iota(jnp.int32, (blk, blk), 1) <= lax.broadcasted_iota(jnp.int32, (blk, blk), 0))
    zero_p = jnp.zeros((blk, blk), F32)

    heads_per_kv = A_HEADS // A_KV_HEADS
    for p in range(A_HEADS // 2):
        g = (2 * p) // heads_per_kv
        qp = (q_ref[rows, p * 128:(p + 1) * 128] * (A_HEAD_DIM ** -0.5)).astype(BF16)
        acc = jnp.zeros((blk, 128), F32)
        for par in range(2):
            h = 2 * p + par
            sink = sink_ref[h]
            s2 = _dot_nt(qp, kk[g][par])
            logits = jnp.where(own, s2[:, blk:], s2[:, :blk] + no_prev) + bias_ref[h]
            m = jnp.maximum(jnp.max(logits, axis=-1, keepdims=True), sink)
            e = jnp.exp(logits - m)
            den = jnp.sum(e, axis=-1, keepdims=True) + jnp.exp(sink - m)
            e2 = jnp.concatenate([jnp.where(own, zero_p, e), jnp.where(own, e, zero_p)], axis=1).astype(BF16)
            acc = acc + jnp.dot(e2, vv[g][par], preferred_element_type=F32) * (1.0 / den)
        o_ref[rows, p * 128:(p + 1) * 128] = acc.astype(o_ref.dtype)


def _attention(proj, bias, sinks, bsz, seq):
    per = ATTN_BLOCKS_PER_STEP
    npair = seq // (per * A_BLOCK)
    kcol = A_QW // 128
    vcol = kcol + 1
    row = lambda b, m: b * npair + m
    prow = lambda b, m: (b * npair + m) * per - jnp.where(m > 0, 1, 0)
    return pl.pallas_call(
        _attn_kernel,
        grid=(bsz, npair),
        in_specs=[
            pl.BlockSpec((per * A_BLOCK, A_QW), lambda b, m: (row(b, m), 0)),
            pl.BlockSpec((per * A_BLOCK, 128), lambda b, m: (row(b, m), kcol)),
            pl.BlockSpec((per * A_BLOCK, 128), lambda b, m: (row(b, m), vcol)),
            pl.BlockSpec((A_BLOCK, 128), lambda b, m: (prow(b, m), kcol)),
            pl.BlockSpec((A_BLOCK, 128), lambda b, m: (prow(b, m), vcol)),
            pl.BlockSpec((A_HEADS, A_BLOCK, A_BLOCK), lambda b, m: (0, 0, 0)),
            pl.BlockSpec(memory_space=pltpu.SMEM),
        ],
        out_specs=pl.BlockSpec((per * A_BLOCK, A_QW), lambda b, m: (row(b, m), 0)),
        out_shape=jax.ShapeDtypeStruct((bsz * seq, A_QW), BF16),
        compiler_params=_cparams(2, 32 * MIB),
        name="swa_attention",
    )(proj, proj, proj, proj, proj, bias, sinks)


HGRN_HEADS_PER_STEP = 4
HGRN_IN_BLOCKS = HGRN_HEADS_PER_STEP // 2


def _hgrn_kernel(*refs):
    nin = HGRN_IN_BLOCKS
    q_refs, f_refs, i_refs, g_refs = (refs[k * nin:(k + 1) * nin] for k in range(4))
    lb_ref, ng_ref, o_ref, st_ref = refs[4 * nin:]
    in_cols = lambda hh: slice((hh % 2) * 128, (hh % 2 + 1) * 128)
    c, sub, grp = HGRN_CHUNK, HGRN_SUB, HGRN_GROUP
    ngrp = c // grp

    @pl.when(pl.program_id(2) == 0)
    def _():
        st_ref[...] = jnp.zeros_like(st_ref)

    rid = lax.broadcasted_iota(jnp.int32, (c, c), 0)
    cid = lax.broadcasted_iota(jnp.int32, (c, c), 1)
    grp_start = (rid // grp) * grp
    sub_start = (rid // sub) * sub
    m_cum = jnp.concatenate([(cid < grp_start).astype(F32),
                             ((cid >= grp_start) & (cid < sub_start)).astype(F32),
                             ((cid >= sub_start) & (cid <= rid)).astype(F32)], axis=0).astype(BF16)
    band = jnp.where(cid >= sub_start, rid - cid, -1)
    same_grp = (cid // grp) == (rid // grp)
    heads = range(HGRN_HEADS_PER_STEP)
    zeros_bf = lambda rows: jnp.zeros((rows, B_DK), BF16)

    def scores(hh, r0):
        cols = slice(hh * 128, (hh + 1) * 128)
        lb = lb_ref[0, :, cols]
        q = q_refs[hh // 2][pl.ds(r0, c), in_cols(hh)]
        fl = f_refs[hh // 2][pl.ds(r0, c), in_cols(hh)]
        qf = q * jax.nn.sigmoid(q)
        f = lb + (1.0 - lb) * jax.nn.sigmoid(fl)
        kin = 1.0 - f
        logf = jnp.log(f)
        l1 = logf.astype(BF16)
        res = logf - l1.astype(F32)
        l2 = res.astype(BF16)
        l3 = (res - l2.astype(F32)).astype(BF16)
        cum = jnp.dot(m_cum, jnp.concatenate([l1, l2, l3], axis=1), preferred_element_type=F32)
        cum = cum[:, :B_DK] + cum[:, B_DK:2 * B_DK] + cum[:, 2 * B_DK:]
        rg, rs, bq = cum[:c], cum[c:2 * c], cum[2 * c:]
        lk = jnp.log(kin)
        wq = lk - bq
        wg = wq - rs
        wb = wg - rg
        qt = qf * jnp.exp(bq)
        qg = qt * jnp.exp(rs)
        o = _dot_nt((qg * jnp.exp(rg)).astype(BF16), st_ref[hh].astype(BF16))

        kts = []
        for i in range(1, ngrp):
            n = i * grp
            kts += [jnp.exp(rg[n:n + 1, :] + wb[:n, :]).astype(BF16), zeros_bf(c - n)]
        g_far = _dot_nt(qg.astype(BF16), jnp.concatenate(kts, axis=0))
        s_far = jnp.concatenate([jnp.zeros((grp, c), F32)] +
                                [g_far[i * grp:(i + 1) * grp, (i - 1) * c:i * c] for i in range(1, ngrp)], axis=0)

        kts = []
        for j in range(1, grp // sub):
            for gi in range(ngrp):
                a, n = gi * grp, j * sub
                kts += [jnp.exp(rs[a + n:a + n + 1, :] + wg[a:a + n, :]).astype(BF16), zeros_bf(grp - n)]
        g_near = _dot_nt(qt.astype(BF16), jnp.concatenate(kts, axis=0))
        pieces = []
        for i in range(c // sub):
            j = i % (grp // sub)
            pieces.append(jnp.zeros((sub, c), F32) if j == 0 else g_near[i * sub:(i + 1) * sub, (j - 1) * c:j * c])
        s = jnp.where(same_grp, jnp.concatenate(pieces, axis=0), s_far)

        bq2, wq2 = bq * LOG2E, wq * LOG2E
        kds = [kin.astype(BF16)] + [jnp.exp2(pltpu.roll(bq2, c - d, 0) + wq2).astype(BF16) for d in range(1, sub)]
        g_diag = _dot_nt(qf.astype(BF16), jnp.concatenate(kds, axis=0))
        last = slice(c - 1, c)
        return dict(o=o, s=s, g_diag=g_diag, wb=wb, b_last=rg[last, :] + rs[last, :] + bq[last, :])

    def finish(hh, r0, h):
        cols = slice(hh * 128, (hh + 1) * 128)
        vb = i_refs[hh // 2][pl.ds(r0, c), in_cols(hh)].astype(BF16)
        gt = g_refs[hh // 2][pl.ds(r0, c), in_cols(hh)]
        o = h["o"] + jnp.dot(h["s"].astype(BF16), vb, preferred_element_type=F32)
        st_ref[hh] = (st_ref[hh] * jnp.exp(h["b_last"])
                      + _dot_tn(vb, jnp.exp(h["b_last"] + h["wb"]).astype(BF16)))
        o = o * lax.rsqrt(jnp.mean(o * o, axis=-1, keepdims=True) + RMS_EPS)
        o_ref[pl.ds(r0, c), cols] = (o * ng_ref[0, :, cols] * (gt * jax.nn.sigmoid(gt))).astype(o_ref.dtype)

    def chunk(ci, carry):
        r0 = pl.multiple_of(ci * c, c)
        hs = [scores(hh, r0) for hh in heads]
        for d in range(sub):
            on_diag = band == d
            for h in hs:
                h["s"] = jnp.where(on_diag, h["g_diag"][:, d * c:(d + 1) * c], h["s"])
        for hh in heads:
            finish(hh, r0, hs[hh])
        return carry

    lax.fori_loop(0, HGRN_ROWS // c, chunk, 0, unroll=True)


def _hgrn(proj, lb, norm_g, bsz, seq):
    nr = seq // HGRN_ROWS
    hp, nin = HGRN_HEADS_PER_STEP, HGRN_IN_BLOCKS
    width = hp * 128
    c0 = (A_QW + 2 * A_KVW) // 256
    per_tensor = B_HEADS // 2
    nhp = B_HEADS // hp
    in_specs = [pl.BlockSpec((HGRN_ROWS, 256),
                             lambda b, h, r, t=t, p=p: (b * nr + r, c0 + t * per_tensor + h * nin + p))
                for t in range(4) for p in range(nin)]
    vec = pl.BlockSpec((1, 1, width), lambda b, h, r: (h, 0, 0))
    return pl.pallas_call(
        _hgrn_kernel,
        grid=(bsz, nhp, nr),
        in_specs=in_specs + [vec, vec],
        out_specs=pl.BlockSpec((HGRN_ROWS, width), lambda b, h, r: (b * nr + r, h)),
        out_shape=jax.ShapeDtypeStruct((bsz * seq, B_VW), BF16),
        scratch_shapes=[pltpu.VMEM((hp, B_DV, B_DK), F32)],
        compiler_params=_cparams(3, 32 * MIB),
        name="hgrn2",
    )(*([proj] * (4 * nin)), lb.reshape(nhp, 1, width), norm_g.reshape(nhp, 1, width))


OUTPROJ_LAG = 2


def _outproj_kernel(ya_ref, yb_ref, wa_ref, wb_ref, x_ref, g_ref, b_ref, wr_ref, rb_ref,
                    xo_ref, xp_ref, rt_ref, rtt_ref, cnt_ref, run_ref, lg_ref, mix_ref):
    step = pl.program_id(0)

    @pl.when(step == 0)
    def _():
        run_ref[...] = jnp.zeros_like(run_ref)
        lg_ref[...] = jnp.zeros_like(lg_ref)
        mix_ref[...] = jnp.zeros_like(mix_ref)

    lg_prev = lg_ref[...]
    mix_prev = mix_ref[...]
    mix = jnp.dot(ya_ref[...], wa_ref[...], preferred_element_type=F32)
    mix_ref[...] = mix + jnp.dot(yb_ref[...], wb_ref[...], preferred_element_type=F32)

    y = _layer_norm_rows(DN_ALPHA * x_ref[...] + mix_prev, g_ref[...], b_ref[...])
    xo_ref[...] = y
    _store_token_tiles(xp_ref, 0, _pack_rows(y))
    tm = y.shape[0]
    y_hi = y.astype(BF16)
    y_lo = (y - y_hi.astype(F32)).astype(BF16)
    prod = jnp.dot(jnp.concatenate([y_hi, y_lo], axis=0), wr_ref[...], preferred_element_type=F32)
    lg = (prod[:tm, :ROUTE_W] + prod[tm:, :ROUTE_W]) + (prod[:tm, ROUTE_W:] + prod[tm:, ROUTE_W:])
    lg_ref[...] = lg + rb_ref[...]
    _route_rows(lg_prev, (step >= OUTPROJ_LAG).astype(F32), rt_ref, rtt_ref, cnt_ref, run_ref)


def _outproj_ln_route(ya, yb, w_out, x, g, b, wr, rbias, tm=512):
    n = x.shape[0]
    ka = ya.shape[1]
    nsteps = n // tm
    lagged = lambda lag: (lambda i: (jnp.clip(i - lag, 0, nsteps - 1), 0))
    row = lambda width, lag: pl.BlockSpec((tm, width), lagged(lag))
    once = pl.Buffered(1)
    const = lambda shape: pl.BlockSpec(shape, lambda i: (0, 0), pipeline_mode=once)
    return pl.pallas_call(
        _outproj_kernel,
        grid=(nsteps + OUTPROJ_LAG,),
        in_specs=[row(ka, 0), row(ka, 0),
                  pl.BlockSpec((ka, D_MODEL), lambda i: (0, 0), pipeline_mode=once),
                  pl.BlockSpec((ka, D_MODEL), lambda i: (1, 0), pipeline_mode=once),
                  row(D_MODEL, 1), const((1, D_MODEL)), const((1, D_MODEL)),
                  const((D_MODEL, 2 * ROUTE_W)), const((1, ROUTE_W))],
        out_specs=[row(D_MODEL, 1), pl.BlockSpec((tm * TOK_TILE, LANES_V7X), lagged(1)),
                   row(ROUTE_W, OUTPROJ_LAG),
                   pl.BlockSpec((ROUTE_FIELDS, tm), lambda i: (0, jnp.clip(i - OUTPROJ_LAG, 0, nsteps - 1))),
                   pl.BlockSpec((1, ROUTE_W), lambda i: (0, 0))],
        out_shape=[jax.ShapeDtypeStruct((n, D_MODEL), F32),
                   jax.ShapeDtypeStruct((n * TOK_TILE, LANES_V7X), U32),
                   jax.ShapeDtypeStruct((n, ROUTE_W), F32), jax.ShapeDtypeStruct((ROUTE_FIELDS, n), F32),
                   jax.ShapeDtypeStruct((1, ROUTE_W), F32)],
        scratch_shapes=[pltpu.VMEM((1, ROUTE_W), F32), pltpu.VMEM((tm, ROUTE_W), F32),
                        pltpu.VMEM((tm, D_MODEL), F32)],
        compiler_params=_cparams(1, VMEM_LIMIT_V7X),
        name="outproj_ln_route",
    )(ya, yb, w_out, w_out, x, g.reshape(1, -1), b.reshape(1, -1), wr, rbias)


def _router_weights(w_group, b_group, w_router, b_router):
    w = jnp.zeros((D_MODEL, ROUTE_W), F32)
    w = w.at[:, :N_GROUPS].set(w_group).at[:, N_GROUPS:N_GROUPS + N_EXPERTS].set(w_router)
    w_hi = w.astype(BF16)
    w_lo = (w - w_hi.astype(F32)).astype(BF16)
    rb = jnp.zeros((1, ROUTE_W), F32)
    rb = rb.at[0, :N_GROUPS].set(b_group).at[0, N_GROUPS:N_GROUPS + N_EXPERTS].set(b_router)
    return jnp.concatenate([w_hi, w_lo], axis=1), rb


def _route_rows(lg, live, rt_ref, rtt_ref, cnt_ref, run_ref):
    tm = lg.shape[0]
    lane = lax.broadcasted_iota(jnp.int32, lg.shape, 1)
    sentinel = jnp.int32(ROUTE_W)
    rowmax = lambda mask: jnp.max(jnp.where(mask, lg, NEG_INF), axis=-1, keepdims=True)
    first = lambda mask: jnp.min(jnp.where(mask, lane, sentinel), axis=-1, keepdims=True)

    is_g = lane < N_GROUPS
    gmax = rowmax(is_g)
    g_idx = first(is_g & (lg == gmax))
    g_w = 1.0 / jnp.sum(jnp.where(is_g, jnp.exp(lg - gmax), 0.0), axis=-1, keepdims=True)

    e_lane = lane - N_GROUPS
    sel = (e_lane >= 0) & (e_lane < N_EXPERTS) & ((e_lane >> 3) == g_idx)
    m1 = rowmax(sel)
    i1 = first(sel & (lg == m1))
    sel2 = sel & (lane != i1)
    m2 = rowmax(sel2)
    i2 = first(sel2 & (lg == m2))
    ex = jnp.exp(m2 - m1)
    w0 = g_w / (1.0 + ex)
    w1 = g_w * ex / (1.0 + ex)

    oh0 = lane == i1
    oh1 = lane == i2
    both = (oh0 | oh1).astype(F32) * live
    rid = lax.broadcasted_iota(jnp.int32, (tm, tm), 0)
    cid = lax.broadcasted_iota(jnp.int32, (tm, tm), 1)
    before = jnp.dot((cid < rid).astype(BF16), both.astype(BF16), preferred_element_type=F32) + run_ref[...]
    rank0 = jnp.sum(jnp.where(oh0, before, 0.0), axis=-1, keepdims=True)
    rank1 = jnp.sum(jnp.where(oh1, before, 0.0), axis=-1, keepdims=True)
    run = run_ref[...] + jnp.sum(both, axis=0, keepdims=True)
    run_ref[...] = run
    cnt_ref[...] = run

    slab = jnp.zeros(lg.shape, F32)
    for ln, val in ((RT_W0, w0), (RT_W1, w1), (RT_E0, (i1 - N_GROUPS).astype(F32)),
                    (RT_E1, (i2 - N_GROUPS).astype(F32)), (RT_R0, rank0), (RT_R1, rank1)):
        slab = jnp.where(lane == ln, val, slab)
    rt_ref[...] = slab
    rtt_ref[...] = jnp.transpose(slab)[:ROUTE_FIELDS, :]


assert EXPERTS_PER_GROUP == 8


def _block_layout(rtt, cnt, n_tok):
    m = n_tok * TOP_K
    counts = cnt[0, N_GROUPS:N_GROUPS + N_EXPERTS].astype(jnp.int32)
    pcounts = (counts + MOE_TB - 1) // MOE_TB * MOE_TB
    pends = jnp.cumsum(pcounts)
    pstarts = pends - pcounts
    n_blocks = -(-(m + N_EXPERTS * (MOE_TB - 1)) // MOE_TB)
    e_idx = rtt[RT_E0:RT_E1 + 1, :].astype(jnp.int32)
    rank = rtt[RT_R0:RT_R1 + 1, :].astype(jnp.int32)
    onehot = e_idx[:, None, :] == jnp.arange(N_EXPERTS, dtype=jnp.int32)[None, :, None]
    pos = jnp.sum(jnp.where(onehot, pstarts[None, :, None], 0), axis=1) + rank
    blk_start = jnp.arange(n_blocks, dtype=jnp.int32) * MOE_TB
    blk_e = jnp.minimum(jnp.sum(blk_start[:, None] >= pends[None, :], axis=-1), N_EXPERTS - 1).astype(jnp.int32)
    nused = (pends[-1:] // MOE_TB).astype(jnp.int32)
    ids = jnp.arange(N_EXPERTS, dtype=jnp.int32)
    later_used = (ids[None, :] > ids[:, None]) & (counts[None, :] > 0)
    next_used = jnp.min(jnp.where(later_used, ids[None, :], N_EXPERTS), axis=-1)
    next_used = jnp.where(next_used == N_EXPERTS, -1, next_used).astype(jnp.int32)
    nxt_e = jnp.sum(jnp.where(blk_e[:, None] == ids[None, :], next_used[None, :], 0), axis=-1).astype(jnp.int32)
    ends_expert = jnp.any(((blk_start + MOE_TB)[:, None] == pends[None, :]) & (pcounts[None, :] > 0), axis=-1)
    zero_blk = (ends_expert | (blk_start >= pends[-1])).astype(jnp.int32)
    return pos, blk_e, nxt_e, nused, zero_blk


def _step_indices(pos, tm):
    nsteps = pos.shape[1] // tm
    return pos.reshape(TOP_K, nsteps, tm).transpose(1, 0, 2).reshape(nsteps, 1, TOP_K * tm)


def _tile_rows(i):
    return pl.ds(pl.multiple_of(i * TOK_TILE, TOK_TILE), TOK_TILE)


def _dispatch_kernel(zero_blk_ref, pos_ref, xp_ref, xs_hbm, xbuf, zbuf, sem, zsem, *, tm, n_blocks):
    s = pl.program_id(0)
    nsteps = pl.num_programs(0)
    slot = s % 2
    blk_rows = MOE_TB * TOK_TILE

    def copy(j, dst, sl):
        return pltpu.make_async_copy(xbuf.at[sl, _tile_rows(j)], xs_hbm.at[_tile_rows(dst)], sem.at[sl])

    def drain(sl):
        def body(j, c):
            copy(0, 0, sl).wait()
            return c
        lax.fori_loop(0, TOP_K * tm, body, 0, unroll=8)

    @pl.when(s == 0)
    def _():
        zbuf[...] = jnp.zeros_like(zbuf)

        def zero_copy(blk):
            rows = pl.ds(pl.multiple_of(blk * blk_rows, blk_rows), blk_rows)
            return pltpu.make_async_copy(zbuf, xs_hbm.at[rows], zsem)

        def each_flagged(fn):
            def body(blk, c):
                @pl.when(zero_blk_ref[blk] > 0)
                def _():
                    fn(zero_copy(blk))
                return c
            lax.fori_loop(0, n_blocks, body, 0)

        each_flagged(lambda cp: cp.start())
        each_flagged(lambda cp: cp.wait())

    @pl.when(s >= 2)
    def _():
        drain(slot)

    xbuf[slot] = xp_ref[...]

    for j in range(tm):
        for k in range(TOP_K):
            copy(j, pos_ref[0, 0, k * tm + j], slot).start(priority=k)

    @pl.when(s == nsteps - 1)
    def _():
        drain(1 - slot)
        drain(slot)


def _moe_dispatch(xp, pos, zero_blk, tm=512):
    n = pos.shape[1]
    n_blocks = zero_blk.shape[0]
    assert n // tm >= 2
    pos3 = _step_indices(pos, tm)
    grid_spec = pltpu.PrefetchScalarGridSpec(
        num_scalar_prefetch=1,
        grid=(n // tm,),
        in_specs=[pl.BlockSpec((1, 1, TOP_K * tm), lambda i, zb: (i, 0, 0), memory_space=pltpu.SMEM),
                  pl.BlockSpec((tm * TOK_TILE, LANES_V7X), lambda i, zb: (i, 0))],
        out_specs=pl.BlockSpec(memory_space=pl.ANY),
        scratch_shapes=[pltpu.VMEM((2, tm * TOK_TILE, LANES_V7X), U32),
                        pltpu.VMEM((MOE_TB * TOK_TILE, LANES_V7X), U32),
                        pltpu.SemaphoreType.DMA((2,)), pltpu.SemaphoreType.DMA(())],
    )
    return pl.pallas_call(
        functools.partial(_dispatch_kernel, tm=tm, n_blocks=n_blocks),
        grid_spec=grid_spec,
        out_shape=jax.ShapeDtypeStruct((n_blocks * MOE_TB * TOK_TILE, LANES_V7X), U32),
        compiler_params=_cparams(1),
        name="moe_dispatch",
    )(zero_blk, pos3, xp)


def _moe_kernel(blk_e_ref, nxt_e_ref, nused_ref, xs_ref, w1_hbm, w3_hbm, w2_hbm, ys_ref,
                wf1, wf3, wf2, w1b, w3b, w2b, slot_ref, sem, *, layer):
    s = pl.program_id(0)
    nused = nused_ref[0]

    def fetch(e, sl):
        return [pltpu.make_async_copy(w_hbm.at[layer, e], wf.at[sl], sem.at[sl, k])
                for k, (w_hbm, wf) in enumerate(((w1_hbm, wf1), (w3_hbm, wf3), (w2_hbm, wf2)))]

    @pl.when(s >= nused)
    def _():
        ys_ref[...] = jnp.zeros_like(ys_ref)

    @pl.when(s < nused)
    def _():
        e = blk_e_ref[s]
        prev = blk_e_ref[jnp.maximum(s - 1, 0)]

        @pl.when(s == 0)
        def _():
            slot_ref[0] = 0
            for cp in fetch(e, 0):
                cp.start()

        @pl.when((s > 0) & (e != prev))
        def _():
            slot_ref[0] = 1 - slot_ref[0]

        @pl.when((s == 0) | (e != prev))
        def _():
            sl = slot_ref[0]
            for cp in fetch(e, sl):
                cp.wait()
            w1b[...] = wf1[sl].astype(BF16)
            w3b[...] = wf3[sl].astype(BF16)
            w2b[...] = wf2[sl].astype(BF16)
            nxt = nxt_e_ref[s]

            @pl.when(nxt >= 0)
            def _():
                for cp in fetch(nxt, 1 - sl):
                    cp.start()

        xb = _unpack_rows(_load_token_tiles(xs_ref, 0, MOE_TB)).astype(BF16)
        h1 = jnp.dot(xb, w1b[...], preferred_element_type=F32)
        h3 = jnp.dot(xb, w3b[...], preferred_element_type=F32)
        h = (h1 * jax.nn.sigmoid(h1) * h3).astype(BF16)
        y = jnp.dot(h, w2b[...], preferred_element_type=F32)
        _store_token_tiles(ys_ref, 0, _pack_rows(y))


def _moe_experts(xs, blk_e, nxt_e, nused, w1, w3, w2, layer):
    n_blocks = blk_e.shape[0]
    tiles = pl.BlockSpec((MOE_TB * TOK_TILE, LANES_V7X), lambda s, be, ne, nu: (s, 0))
    hbm = pl.BlockSpec(memory_space=pl.ANY)
    up, down = (D_MODEL, D_EXPERT), (D_EXPERT, D_MODEL)
    grid_spec = pltpu.PrefetchScalarGridSpec(
        num_scalar_prefetch=3,
        grid=(n_blocks,),
        in_specs=[tiles, hbm, hbm, hbm],
        out_specs=tiles,
        scratch_shapes=[pltpu.VMEM((2,) + up, F32), pltpu.VMEM((2,) + up, F32), pltpu.VMEM((2,) + down, F32),
                        pltpu.VMEM(up, BF16), pltpu.VMEM(up, BF16), pltpu.VMEM(down, BF16),
                        pltpu.SMEM((1,), jnp.int32), pltpu.SemaphoreType.DMA((2, 3))],
    )
    return pl.pallas_call(
        functools.partial(_moe_kernel, layer=layer),
        grid_spec=grid_spec,
        out_shape=jax.ShapeDtypeStruct(xs.shape, U32),
        compiler_params=_cparams(1, VMEM_LIMIT_V7X),
        name="moe_experts",
    )(blk_e, nxt_e, nused, xs, w1, w3, w2)


def _combine_kernel(pos_ref, nxt_ref, ys_hbm, x_ref, rt_ref, g_ref, b_ref, o_ref, ybuf, sem, *, tm):
    s = pl.program_id(0)
    nsteps = pl.num_programs(0)
    slot = s % 2

    def copy(src, j, sl):
        return pltpu.make_async_copy(ys_hbm.at[_tile_rows(src)], ybuf.at[sl, _tile_rows(j)], sem.at[sl])

    def start_gather(idx_ref, sl):
        def body(j, c):
            copy(idx_ref[0, 0, j], j, sl).start()
            return c
        lax.fori_loop(0, TOP_K * tm, body, 0, unroll=8)

    @pl.when(s == 0)
    def _():
        start_gather(pos_ref, 0)

    @pl.when(s + 1 < nsteps)
    def _():
        for j in range(TOP_K * tm):
            copy(nxt_ref[0, 0, j], j, 1 - slot).start()

    def wait_body(j, c):
        copy(0, j, slot).wait()
        return c
    lax.fori_loop(0, TOP_K * tm, wait_body, 0, unroll=8)

    rt = rt_ref[...]
    ffn = _unpack_rows(_load_token_tiles(ybuf, 0, tm, lead=slot)) * rt[:, RT_W0:RT_W0 + 1]
    ffn = ffn + _unpack_rows(_load_token_tiles(ybuf, tm * TOK_TILE, tm, lead=slot)) * rt[:, RT_W1:RT_W1 + 1]
    o_ref[...] = _layer_norm_rows(DN_ALPHA * x_ref[...] + ffn, g_ref[...], b_ref[...])


def _moe_combine(ys, pos, rt, x, g, b, tm=512):
    n = x.shape[0]
    nsteps = n // tm
    pos3 = _step_indices(pos, tm)
    smem_blk = lambda f: pl.BlockSpec((1, 1, TOP_K * tm), f, memory_space=pltpu.SMEM)
    row = lambda width: pl.BlockSpec((tm, width), lambda i: (i, 0))
    const = lambda shape: pl.BlockSpec(shape, lambda i: (0, 0))
    return pl.pallas_call(
        functools.partial(_combine_kernel, tm=tm),
        grid=(nsteps,),
        in_specs=[smem_blk(lambda i: (i, 0, 0)), smem_blk(lambda i: (jnp.minimum(i + 1, nsteps - 1), 0, 0)),
                  pl.BlockSpec(memory_space=pl.ANY), row(D_MODEL), row(ROUTE_W),
                  const((1, D_MODEL)), const((1, D_MODEL))],
        out_specs=row(D_MODEL),
        out_shape=jax.ShapeDtypeStruct((n, D_MODEL), F32),
        scratch_shapes=[pltpu.VMEM((2, TOP_K * tm * TOK_TILE, LANES_V7X), U32), pltpu.SemaphoreType.DMA((2,))],
        compiler_params=_cparams(1, 40 * MIB),
        name="moe_combine",
    )(pos3, pos3, ys, x, rt, g.reshape(1, -1), b.reshape(1, -1))


def _moe_layer(x1, xp, rt, rtt, cnt, w1, w3, w2, layer, g, b):
    n_tok = x1.shape[0]
    pos, blk_e, nxt_e, nused, zero_blk = _block_layout(rtt, cnt, n_tok)
    xs = _moe_dispatch(xp, pos, zero_blk)
    ys = _moe_experts(xs, blk_e, nxt_e, nused, w1, w3, w2, layer)
    return _moe_combine(ys, pos, rt, x1, g, b)


def _gmlp_kernel(u_ref, v_ref, g_ref, b_ref, w_ref, bs_ref, o_ref, *, chunks):
    for ci in range(chunks):
        rows = slice(ci * C_CHUNK, (ci + 1) * C_CHUNK)
        u = jax.nn.gelu(u_ref[rows, :])
        v = _layer_norm_rows(jax.nn.gelu(v_ref[rows, :]), g_ref[...], b_ref[...]).astype(BF16)
        for gi in range(C_GROUPS):
            cols = slice(gi * C_GROUP_DIM, (gi + 1) * C_GROUP_DIM)
            mixed = jnp.dot(w_ref[gi], v[:, cols], preferred_element_type=F32) + bs_ref[:, cols]
            o_ref[rows, cols] = (u[:, cols] * mixed).astype(o_ref.dtype)


def _gmlp(proj, ln_g, ln_b, w_s, b_s, chunks=4):
    n = proj.shape[0]
    tm = chunks * C_CHUNK
    w = (w_s * jnp.tril(jnp.ones((C_CHUNK, C_CHUNK), w_s.dtype))).astype(BF16)
    bs_full = jnp.repeat(b_s.T, C_GROUP_DIM, axis=1)
    const2 = lambda shape: pl.BlockSpec(shape, lambda i: (0, 0))
    return pl.pallas_call(
        functools.partial(_gmlp_kernel, chunks=chunks),
        grid=(n // tm,),
        in_specs=[pl.BlockSpec((tm, C_W), lambda i: (i, 0)), pl.BlockSpec((tm, C_W), lambda i: (i, 1)),
                  const2((1, C_W)), const2((1, C_W)),
                  pl.BlockSpec((C_GROUPS, C_CHUNK, C_CHUNK), lambda i: (0, 0, 0)), const2((C_CHUNK, C_W))],
        out_specs=pl.BlockSpec((tm, C_W), lambda i: (i, 0)),
        out_shape=jax.ShapeDtypeStruct((n, C_W), BF16),
        compiler_params=_cparams(1, 32 * MIB),
        name="gmlp_gating",
    )(proj, proj, ln_g.reshape(1, -1), ln_b.reshape(1, -1), w, bs_full)


CONV_HIST = 32


def _conv_kernel(a_ref, gt_ref, ap_ref, gp_ref, w_ref, cb_ref, g_ref, b_ref, o_ref, hbuf, hshift, *, ts):
    i = pl.program_id(1)
    hist = ap_ref[...] * jax.nn.sigmoid(gp_ref[...])
    hbuf[0:CONV_HIST, :] = jnp.where(i > 0, hist, jnp.zeros_like(hist))
    hbuf[CONV_HIST:CONV_HIST + ts, :] = a_ref[...] * jax.nn.sigmoid(gt_ref[...])
    off = CONV_HIST - (D_CONV - 1)
    acc = jnp.zeros((ts, D_CHANNELS), F32) + cb_ref[...]
    for r in range(SUBLANES_V7X):
        taps = [j for j in range(D_CONV) if (off + j) % SUBLANES_V7X == r]
        if not taps:
            continue
        src = hbuf
        if r:
            span = max(taps) + off - r + ts
            hshift[0:span, :] = hbuf[r:r + span, :]
            src = hshift
        for j in taps:
            base = off + j - r
            acc = acc + w_ref[j:j + 1, :] * src[base:base + ts, :]
    y = _layer_norm_rows(acc, g_ref[...], b_ref[...])
    o_ref[...] = (y * jax.nn.sigmoid(y)).astype(o_ref.dtype)


def _conformer_conv(proj, conv_w, conv_b, ln_g, ln_b, bsz, seq, ts=512):
    nt = seq // ts
    acol = 2 * C_W // D_CHANNELS
    gcol = acol + 1
    hb = ts // CONV_HIST
    cur = lambda col: pl.BlockSpec((ts, D_CHANNELS), lambda b, i: (b * nt + i, col))
    prev = lambda col: pl.BlockSpec((CONV_HIST, D_CHANNELS),
                                    lambda b, i: (jnp.maximum((b * nt + i) * hb - 1, 0), col))
    const2 = lambda shape: pl.BlockSpec(shape, lambda b, i: (0, 0))
    return pl.pallas_call(
        functools.partial(_conv_kernel, ts=ts),
        grid=(bsz, nt),
        in_specs=[cur(acol), cur(gcol), prev(acol), prev(gcol),
                  const2((D_CONV, D_CHANNELS)), const2((1, D_CHANNELS)), const2((1, D_CHANNELS)),
                  const2((1, D_CHANNELS))],
        out_specs=pl.BlockSpec((ts, D_CHANNELS), lambda b, i: (b * nt + i, 0)),
        out_shape=jax.ShapeDtypeStruct((bsz * seq, D_CHANNELS), BF16),
        scratch_shapes=[pltpu.VMEM((CONV_HIST + ts, D_CHANNELS), F32), pltpu.VMEM((CONV_HIST + ts, D_CHANNELS), F32)],
        compiler_params=_cparams(2, 32 * MIB),
        name="conformer_conv",
    )(proj, proj, proj, proj, conv_w, conv_b.reshape(1, -1), ln_g.reshape(1, -1), ln_b.reshape(1, -1))


def kernel(x, w_in_ab, attn_sinks, rel_bias, hgrn_lb_logits, hgrn_norm_g, w_out_ab, w_in_cd, gmlp_ln_g, gmlp_ln_b, gmlp_w_s, gmlp_b_s, conv_w, conv_b, conv_ln_g, conv_ln_b, w_out_cd, ln_mix_g, ln_mix_b, ln_ffn_g, ln_ffn_b, moe_w_group, moe_b_group, moe_w_router, moe_b_router, moe_w1, moe_w3, moe_w2):
    bsz, seq = x.shape[0], x.shape[1]
    n_tok = bsz * seq
    xf = x.reshape(n_tok, D_MODEL)
    lb_table = jnp.cumsum(jax.nn.softmax(hgrn_lb_logits.astype(F32), axis=0), axis=0)
    bias = _bias_table(rel_bias.astype(F32), _t5_bucket_table())

    for layer in range(DEPTH):
        j = layer // 2
        if layer % 2 == 0:
            proj = _matmul(xf, w_in_ab[j].astype(BF16), 1024, EVEN_IN // 3, F32)
            ya = _attention(proj, bias, attn_sinks[j].astype(F32), bsz, seq)
            yb = _hgrn(proj, lb_table[layer], hgrn_norm_g[j].astype(F32), bsz, seq)
            w_out = w_out_ab[j]
        else:
            proj = _matmul(xf, w_in_cd[j].astype(BF16), 1024, ODD_IN // 2, F32)
            ya = _gmlp(proj, gmlp_ln_g[j], gmlp_ln_b[j], gmlp_w_s[j], gmlp_b_s[j])
            yb = _conformer_conv(proj, conv_w[j], conv_b[j], conv_ln_g[j], conv_ln_b[j], bsz, seq)
            w_out = w_out_cd[j]
        wr, rbias = _router_weights(moe_w_group[layer], moe_b_group[layer],
                                    moe_w_router[layer], moe_b_router[layer])
        x1, xp, rt, rtt, cnt = _outproj_ln_route(ya, yb, w_out.astype(BF16), xf, ln_mix_g[layer],
                                                 ln_mix_b[layer], wr, rbias)
        xf = _moe_layer(x1, xp, rt, rtt, cnt, moe_w1, moe_w3, moe_w2, layer, ln_ffn_g[layer], ln_ffn_b[layer])
    return xf.reshape(bsz, seq, D_MODEL)
```

```python
import functools
import math

import jax
import jax.numpy as jnp
from jax import lax
from jax.experimental import pallas as pl
from jax.experimental.pallas import tpu as pltpu

D_MODEL = 2048
DEPTH = 2
A_HEADS = 16
A_KV_HEADS = 2
A_HEAD_DIM = 64
A_WINDOW = 128
A_BLOCK = 128
REL_BUCKETS = 32
REL_MAX_DIST = 128
B_HEADS = 8
B_DK = 128
B_DV = 128
C_GROUPS = 8
C_GROUP_DIM = 128
C_CHUNK = 128
D_CHANNELS = 1024
D_CONV = 31
A_QW = A_HEADS * A_HEAD_DIM
A_KVW = A_KV_HEADS * A_HEAD_DIM
B_KW = B_HEADS * B_DK
B_VW = B_HEADS * B_DV
C_W = C_GROUPS * C_GROUP_DIM
EVEN_IN = A_QW + 2 * A_KVW + 2 * B_KW + 2 * B_VW
ODD_IN = 2 * C_W + 2 * D_CHANNELS
N_GROUPS = 4
EXPERTS_PER_GROUP = 8
N_EXPERTS = N_GROUPS * EXPERTS_PER_GROUP
TOP_K = 2
D_EXPERT = 512
DN_ALPHA = (2 * DEPTH) ** 0.25
LN_EPS = 1e-5
RMS_EPS = 1e-6

LANES_V7X = 128
SUBLANES_V7X = 8
MIB = 1024 * 1024
VMEM_LIMIT_V7X = 56 * MIB

HGRN_CHUNK = 128
HGRN_SUB = 8
HGRN_GROUP = 32
HGRN_ROWS = 512
MOE_TB = 256
ROUTE_W = LANES_V7X
HALF = D_MODEL // 2
TOK_TILE = HALF // LANES_V7X
assert TOK_TILE == SUBLANES_V7X

BF16 = jnp.bfloat16
F32 = jnp.float32
U32 = jnp.uint32
NEG_INF = float("-inf")
LOG2E = math.log2(math.e)
RT_W0, RT_W1, RT_E0, RT_E1, RT_R0, RT_R1 = range(6)
ROUTE_FIELDS = SUBLANES_V7X


def _cparams(n_axes, vmem_bytes=None):
    return pltpu.CompilerParams(dimension_semantics=("arbitrary",) * n_axes, vmem_limit_bytes=vmem_bytes)


def _layer_norm_rows(z, g, b):
    mu = jnp.mean(z, axis=-1, keepdims=True)
    zc = z - mu
    var = jnp.mean(zc * zc, axis=-1, keepdims=True)
    return zc * lax.rsqrt(var + LN_EPS) * g + b


def _dot_nt(a, b):
    return lax.dot_general(a, b, (((1,), (1,)), ((), ())), preferred_element_type=F32)


def _dot_tn(a, b):
    return lax.dot_general(a, b, (((0,), (0,)), ((), ())), preferred_element_type=F32)


def _pack_rows(y):
    lo = lax.bitcast_convert_type(y[:, :HALF].astype(BF16).astype(F32), U32) >> 16
    hi = lax.bitcast_convert_type(y[:, HALF:].astype(BF16).astype(F32), U32) & jnp.uint32(0xFFFF0000)
    return lo | hi


def _unpack_rows(p):
    lo = lax.bitcast_convert_type(p << 16, F32)
    hi = lax.bitcast_convert_type(p & jnp.uint32(0xFFFF0000), F32)
    return jnp.concatenate([lo, hi], axis=1)


def _store_token_tiles(ref, base, packed):
    rows = packed.shape[0]
    for c in range(TOK_TILE):
        ref[pl.ds(base + c, rows, stride=TOK_TILE), :] = packed[:, c * LANES_V7X:(c + 1) * LANES_V7X]


def _load_token_tiles(ref, base, rows, lead=None):
    parts = []
    for c in range(TOK_TILE):
        idx = (pl.ds(base + c, rows, stride=TOK_TILE), slice(None))
        parts.append(ref[idx] if lead is None else ref[(lead,) + idx])
    return jnp.concatenate(parts, axis=1)


def _mm_kernel(a_ref, w_ref, o_ref):
    a = a_ref[...].astype(BF16)
    o_ref[...] = jnp.dot(a, w_ref[...], preferred_element_type=F32).astype(o_ref.dtype)


def _matmul(a, w, tm, tn, out_dtype):
    m, k = a.shape
    n = w.shape[1]
    return pl.pallas_call(
        _mm_kernel,
        grid=(n // tn, m // tm),
        in_specs=[pl.BlockSpec((tm, k), lambda j, i: (i, 0)), pl.BlockSpec((k, tn), lambda j, i: (0, j))],
        out_specs=pl.BlockSpec((tm, tn), lambda j, i: (i, j)),
        out_shape=jax.ShapeDtypeStruct((m, n), out_dtype),
        compiler_params=_cparams(2, VMEM_LIMIT_V7X),
        name="proj_matmul",
    )(a, w)


def _bias_table_kernel(rb_ref, bucket_ref, o_ref):
    bucket = bucket_ref[...]
    for h in range(A_HEADS):
        acc = jnp.zeros(bucket.shape, F32)
        for bk in range(REL_BUCKETS):
            acc = jnp.where(bucket == bk, rb_ref[bk, h], acc)
        o_ref[h] = acc


def _bias_table(rel_bias, bucket):
    return pl.pallas_call(
        _bias_table_kernel,
        in_specs=[pl.BlockSpec(memory_space=pltpu.SMEM), pl.BlockSpec(memory_space=pltpu.VMEM)],
        out_specs=pl.BlockSpec(memory_space=pltpu.VMEM),
        out_shape=jax.ShapeDtypeStruct((A_HEADS,) + bucket.shape, F32),
        name="rel_bias_table",
    )(rel_bias, bucket)


def _t5_bucket_table():
    t_loc = jnp.arange(A_BLOCK, dtype=jnp.int32)[:, None]
    j_loc = jnp.arange(A_BLOCK, dtype=jnp.int32)[None, :]
    dist = jnp.where(j_loc <= t_loc, t_loc - j_loc, t_loc + A_BLOCK - j_loc)
    max_exact = REL_BUCKETS // 2
    d = jnp.maximum(dist, 1).astype(F32)
    large = max_exact + (jnp.log(d / max_exact) / math.log(REL_MAX_DIST / max_exact)
                         * (REL_BUCKETS - max_exact)).astype(jnp.int32)
    large = jnp.minimum(large, REL_BUCKETS - 1)
    return jnp.where(dist < max_exact, dist, large)


assert A_WINDOW == A_BLOCK


ATTN_BLOCKS_PER_STEP = 8


def _attn_kernel(q_ref, kc_ref, vc_ref, kp_ref, vp_ref, bias_ref, sink_ref, o_ref):
    blk = A_BLOCK
    first_pair = pl.program_id(1) == 0
    for sb in range(ATTN_BLOCKS_PER_STEP):
        rows = slice(sb * blk, (sb + 1) * blk)
        before = slice((sb - 1) * blk, sb * blk)
        k_prev, v_prev = (kp_ref[...], vp_ref[...]) if sb == 0 else (kc_ref[before, :], vc_ref[before, :])
        no_prev = jnp.where(first_pair, NEG_INF, 0.0) if sb == 0 else 0.0
        _attn_block(q_ref, rows, k_prev, v_prev, kc_ref[rows, :], vc_ref[rows, :], no_prev, bias_ref, sink_ref, o_ref)


def _attn_block(q_ref, rows, k_prev, v_prev, k_own, v_own, no_prev, bias_ref, sink_ref, o_ref):
    blk = A_BLOCK
    k2 = jnp.concatenate([k_prev, k_own], axis=0)
    v2 = jnp.concatenate([v_prev, v_own], axis=0)
    k2r = pltpu.roll(k2, A_HEAD_DIM, 1)
    v2r = pltpu.roll(v2, A_HEAD_DIM, 1)
    lo = lax.broadcasted_iota(jnp.int32, k2.shape, 1) < A_HEAD_DIM
    zero = jnp.zeros_like(k2)

    def placed(x, xr, g, par):
        src = x if g == par else xr
        return (jnp.where(lo, src, zero) if par == 0 else jnp.where(lo, zero, src)).astype(BF16)

    kk = [[placed(k2, k2r, g, par) for par in range(2)] for g in range(A_KV_HEADS)]
    vv = [[placed(v2, v2r, g, par) for par in range(2)] for g in range(A_KV_HEADS)]

    own = (lax.broadcasted_iota(jnp.int32, (blk, blk), 1) <= lax.broadcasted_iota(jnp.int32, (blk, blk), 0))
    zero_p = jnp.zeros((blk, blk), F32)

    heads_per_kv = A_HEADS // A_KV_HEADS
    for p in range(A_HEADS // 2):
        g = (2 * p) // heads_per_kv
        qp = (q_ref[rows, p * 128:(p + 1) * 128] * (A_HEAD_DIM ** -0.5)).astype(BF16)
        acc = jnp.zeros((blk, 128), F32)
        for par in range(2):
            h = 2 * p + par
            sink = sink_ref[h]
            s2 = _dot_nt(qp, kk[g][par])
            logits = jnp.where(own, s2[:, blk:], s2[:, :blk] + no_prev) + bias_ref[h]
            m = jnp.maximum(jnp.max(logits, axis=-1, keepdims=True), sink)
            e = jnp.exp(logits - m)
            den = jnp.sum(e, axis=-1, keepdims=True) + jnp.exp(sink - m)
            e2 = jnp.concatenate([jnp.where(own, zero_p, e), jnp.where(own, e, zero_p)], axis=1).astype(BF16)
            acc = acc + jnp.dot(e2, vv[g][par], preferred_element_type=F32) * (1.0 / den)
        o_ref[rows, p * 128:(p + 1) * 128] = acc.astype(o_ref.dtype)


def _attention(proj, bias, sinks, bsz, seq):
    per = ATTN_BLOCKS_PER_STEP
    npair = seq // (per * A_BLOCK)
    kcol = A_QW // 128
    vcol = kcol + 1
    row = lambda b, m: b * npair + m
    prow = lambda b, m: (b * npair + m) * per - jnp.where(m > 0, 1, 0)
    return pl.pallas_call(
        _attn_kernel,
        grid=(bsz, npair),
        in_specs=[
            pl.BlockSpec((per * A_BLOCK, A_QW), lambda b, m: (row(b, m), 0)),
            pl.BlockSpec((per * A_BLOCK, 128), lambda b, m: (row(b, m), kcol)),
            pl.BlockSpec((per * A_BLOCK, 128), lambda b, m: (row(b, m), vcol)),
            pl.BlockSpec((A_BLOCK, 128), lambda b, m: (prow(b, m), kcol)),
            pl.BlockSpec((A_BLOCK, 128), lambda b, m: (prow(b, m), vcol)),
            pl.BlockSpec((A_HEADS, A_BLOCK, A_BLOCK), lambda b, m: (0, 0, 0)),
            pl.BlockSpec(memory_space=pltpu.SMEM),
        ],
        out_specs=pl.BlockSpec((per * A_BLOCK, A_QW), lambda b, m: (row(b, m), 0)),
        out_shape=jax.ShapeDtypeStruct((bsz * seq, A_QW), BF16),
        compiler_params=_cparams(2, 32 * MIB),
        name="swa_attention",
    )(proj, proj, proj, proj, proj, bias, sinks)


HGRN_HEADS_PER_STEP = 4
HGRN_IN_BLOCKS = HGRN_HEADS_PER_STEP // 2


def _hgrn_kernel(*refs):
    nin = HGRN_IN_BLOCKS
    q_refs, f_refs, i_refs, g_refs = (refs[k * nin:(k + 1) * nin] for k in range(4))
    lb_ref, ng_ref, o_ref, st_ref = refs[4 * nin:]
    in_cols = lambda hh: slice((hh % 2) * 128, (hh % 2 + 1) * 128)
    c, sub, grp = HGRN_CHUNK, HGRN_SUB, HGRN_GROUP
    ngrp = c // grp

    @pl.when(pl.program_id(2) == 0)
    def _():
        st_ref[...] = jnp.zeros_like(st_ref)

    rid = lax.broadcasted_iota(jnp.int32, (c, c), 0)
    cid = lax.broadcasted_iota(jnp.int32, (c, c), 1)
    grp_start = (rid // grp) * grp
    sub_start = (rid // sub) * sub
    m_cum = jnp.concatenate([(cid < grp_start).astype(F32),
                             ((cid >= grp_start) & (cid < sub_start)).astype(F32),
                             ((cid >= sub_start) & (cid <= rid)).astype(F32)], axis=0).astype(BF16)
    band = jnp.where(cid >= sub_start, rid - cid, -1)
    same_grp = (cid // grp) == (rid // grp)
    heads = range(HGRN_HEADS_PER_STEP)
    zeros_bf = lambda rows: jnp.zeros((rows, B_DK), BF16)

    def scores(hh, r0):
        cols = slice(hh * 128, (hh + 1) * 128)
        lb = lb_ref[0, :, cols]
        q = q_refs[hh // 2][pl.ds(r0, c), in_cols(hh)]
        fl = f_refs[hh // 2][pl.ds(r0, c), in_cols(hh)]
        qf = q * jax.nn.sigmoid(q)
        f = lb + (1.0 - lb) * jax.nn.sigmoid(fl)
        kin = 1.0 - f
        logf = jnp.log(f)
        l1 = logf.astype(BF16)
        res = logf - l1.astype(F32)
        l2 = res.astype(BF16)
        l3 = (res - l2.astype(F32)).astype(BF16)
        cum = jnp.dot(m_cum, jnp.concatenate([l1, l2, l3], axis=1), preferred_element_type=F32)
        cum = cum[:, :B_DK] + cum[:, B_DK:2 * B_DK] + cum[:, 2 * B_DK:]
        rg, rs, bq = cum[:c], cum[c:2 * c], cum[2 * c:]
        lk = jnp.log(kin)
        wq = lk - bq
        wg = wq - rs
        wb = wg - rg
        qt = qf * jnp.exp(bq)
        qg = qt * jnp.exp(rs)
        o = _dot_nt((qg * jnp.exp(rg)).astype(BF16), st_ref[hh].astype(BF16))

        kts = []
        for i in range(1, ngrp):
            n = i * grp
            kts += [jnp.exp(rg[n:n + 1, :] + wb[:n, :]).astype(BF16), zeros_bf(c - n)]
        g_far = _dot_nt(qg.astype(BF16), jnp.concatenate(kts, axis=0))
        s_far = jnp.concatenate([jnp.zeros((grp, c), F32)] +
                                [g_far[i * grp:(i + 1) * grp, (i - 1) * c:i * c] for i in range(1, ngrp)], axis=0)

        kts = []
        for j in range(1, grp // sub):
            for gi in range(ngrp):
                a, n = gi * grp, j * sub
                kts += [jnp.exp(rs[a + n:a + n + 1, :] + wg[a:a + n, :]).astype(BF16), zeros_bf(grp - n)]
        g_near = _dot_nt(qt.astype(BF16), jnp.concatenate(kts, axis=0))
        pieces = []
        for i in range(c // sub):
            j = i % (grp // sub)
            pieces.append(jnp.zeros((sub, c), F32) if j == 0 else g_near[i * sub:(i + 1) * sub, (j - 1) * c:j * c])
        s = jnp.where(same_grp, jnp.concatenate(pieces, axis=0), s_far)

        bq2, wq2 = bq * LOG2E, wq * LOG2E
        kds = [kin.astype(BF16)] + [jnp.exp2(pltpu.roll(bq2, c - d, 0) + wq2).astype(BF16) for d in range(1, sub)]
        g_diag = _dot_nt(qf.astype(BF16), jnp.concatenate(kds, axis=0))
        last = slice(c - 1, c)
        return dict(o=o, s=s, g_diag=g_diag, wb=wb, b_last=rg[last, :] + rs[last, :] + bq[last, :])

    def finish(hh, r0, h):
        cols = slice(hh * 128, (hh + 1) * 128)
        vb = i_refs[hh // 2][pl.ds(r0, c), in_cols(hh)].astype(BF16)
        gt = g_refs[hh // 2][pl.ds(r0, c), in_cols(hh)]
        o = h["o"] + jnp.dot(h["s"].astype(BF16), vb, preferred_element_type=F32)
        st_ref[hh] = (st_ref[hh] * jnp.exp(h["b_last"])
                      + _dot_tn(vb, jnp.exp(h["b_last"] + h["wb"]).astype(BF16)))
        o = o * lax.rsqrt(jnp.mean(o * o, axis=-1, keepdims=True) + RMS_EPS)
        o_ref[pl.ds(r0, c), cols] = (o * ng_ref[0, :, cols] * (gt * jax.nn.sigmoid(gt))).astype(o_ref.dtype)

    def chunk(ci, carry):
        r0 = pl.multiple_of(ci * c, c)
        hs = [scores(hh, r0) for hh in heads]
        for d in range(sub):
            on_diag = band == d
            for h in hs:
                h["s"] = jnp.where(on_diag, h["g_diag"][:, d * c:(d + 1) * c], h["s"])
        for hh in heads:
            finish(hh, r0, hs[hh])
        return carry

    lax.fori_loop(0, HGRN_ROWS // c, chunk, 0, unroll=True)


def _hgrn(proj, lb, norm_g, bsz, seq):
    nr = seq // HGRN_ROWS
    hp, nin = HGRN_HEADS_PER_STEP, HGRN_IN_BLOCKS
    width = hp * 128
    c0 = (A_QW + 2 * A_KVW) // 256
    per_tensor = B_HEADS // 2
    nhp = B_HEADS // hp
    in_specs = [pl.BlockSpec((HGRN_ROWS, 256),
                             lambda b, h, r, t=t, p=p: (b * nr + r, c0 + t * per_tensor + h * nin + p))
                for t in range(4) for p in range(nin)]
    vec = pl.BlockSpec((1, 1, width), lambda b, h, r: (h, 0, 0))
    return pl.pallas_call(
        _hgrn_kernel,
        grid=(bsz, nhp, nr),
        in_specs=in_specs + [vec, vec],
        out_specs=pl.BlockSpec((HGRN_ROWS, width), lambda b, h, r: (b * nr + r, h)),
        out_shape=jax.ShapeDtypeStruct((bsz * seq, B_VW), BF16),
        scratch_shapes=[pltpu.VMEM((hp, B_DV, B_DK), F32)],
        compiler_params=_cparams(3, 32 * MIB),
        name="hgrn2",
    )(*([proj] * (4 * nin)), lb.reshape(nhp, 1, width), norm_g.reshape(nhp, 1, width))


OUTPROJ_LAG = 2


def _outproj_kernel(ya_ref, yb_ref, wa_ref, wb_ref, x_ref, g_ref, b_ref, wr_ref, rb_ref,
                    xo_ref, xp_ref, rt_ref, rtt_ref, cnt_ref, run_ref, lg_ref, mix_ref):
    step = pl.program_id(0)

    @pl.when(step == 0)
    def _():
        run_ref[...] = jnp.zeros_like(run_ref)
        lg_ref[...] = jnp.zeros_like(lg_ref)
        mix_ref[...] = jnp.zeros_like(mix_ref)

    lg_prev = lg_ref[...]
    mix_prev = mix_ref[...]
    mix = jnp.dot(ya_ref[...], wa_ref[...], preferred_element_type=F32)
    mix_ref[...] = mix + jnp.dot(yb_ref[...], wb_ref[...], preferred_element_type=F32)

    y = _layer_norm_rows(DN_ALPHA * x_ref[...] + mix_prev, g_ref[...], b_ref[...])
    xo_ref[...] = y
    _store_token_tiles(xp_ref, 0, _pack_rows(y))
    tm = y.shape[0]
    y_hi = y.astype(BF16)
    y_lo = (y - y_hi.astype(F32)).astype(BF16)
    prod = jnp.dot(jnp.concatenate([y_hi, y_lo], axis=0), wr_ref[...], preferred_element_type=F32)
    lg = (prod[:tm, :ROUTE_W] + prod[tm:, :ROUTE_W]) + (prod[:tm, ROUTE_W:] + prod[tm:, ROUTE_W:])
    lg_ref[...] = lg + rb_ref[...]
    _route_rows(lg_prev, (step >= OUTPROJ_LAG).astype(F32), rt_ref, rtt_ref, cnt_ref, run_ref)


def _outproj_ln_route(ya, yb, w_out, x, g, b, wr, rbias, tm=512):
    n = x.shape[0]
    ka = ya.shape[1]
    nsteps = n // tm
    lagged = lambda lag: (lambda i: (jnp.clip(i - lag, 0, nsteps - 1), 0))
    row = lambda width, lag: pl.BlockSpec((tm, width), lagged(lag))
    once = pl.Buffered(1)
    const = lambda shape: pl.BlockSpec(shape, lambda i: (0, 0), pipeline_mode=once)
    return pl.pallas_call(
        _outproj_kernel,
        grid=(nsteps + OUTPROJ_LAG,),
        in_specs=[row(ka, 0), row(ka, 0),
                  pl.BlockSpec((ka, D_MODEL), lambda i: (0, 0), pipeline_mode=once),
                  pl.BlockSpec((ka, D_MODEL), lambda i: (1, 0), pipeline_mode=once),
                  row(D_MODEL, 1), const((1, D_MODEL)), const((1, D_MODEL)),
                  const((D_MODEL, 2 * ROUTE_W)), const((1, ROUTE_W))],
        out_specs=[row(D_MODEL, 1), pl.BlockSpec((tm * TOK_TILE, LANES_V7X), lagged(1)),
                   row(ROUTE_W, OUTPROJ_LAG),
                   pl.BlockSpec((ROUTE_FIELDS, tm), lambda i: (0, jnp.clip(i - OUTPROJ_LAG, 0, nsteps - 1))),
                   pl.BlockSpec((1, ROUTE_W), lambda i: (0, 0))],
        out_shape=[jax.ShapeDtypeStruct((n, D_MODEL), F32),
                   jax.ShapeDtypeStruct((n * TOK_TILE, LANES_V7X), U32),
                   jax.ShapeDtypeStruct((n, ROUTE_W), F32), jax.ShapeDtypeStruct((ROUTE_FIELDS, n), F32),
                   jax.ShapeDtypeStruct((1, ROUTE_W), F32)],
        scratch_shapes=[pltpu.VMEM((1, ROUTE_W), F32), pltpu.VMEM((tm, ROUTE_W), F32),
                        pltpu.VMEM((tm, D_MODEL), F32)],
        compiler_params=_cparams(1, VMEM_LIMIT_V7X),
        name="outproj_ln_route",
    )(ya, yb, w_out, w_out, x, g.reshape(1, -1), b.reshape(1, -1), wr, rbias)


def _router_weights(w_group, b_group, w_router, b_router):
    w = jnp.zeros((D_MODEL, ROUTE_W), F32)
    w = w.at[:, :N_GROUPS].set(w_group).at[:, N_GROUPS:N_GROUPS + N_EXPERTS].set(w_router)
    w_hi = w.astype(BF16)
    w_lo = (w - w_hi.astype(F32)).astype(BF16)
    rb = jnp.zeros((1, ROUTE_W), F32)
    rb = rb.at[0, :N_GROUPS].set(b_group).at[0, N_GROUPS:N_GROUPS + N_EXPERTS].set(b_router)
    return jnp.concatenate([w_hi, w_lo], axis=1), rb


def _route_rows(lg, live, rt_ref, rtt_ref, cnt_ref, run_ref):
    tm = lg.shape[0]
    lane = lax.broadcasted_iota(jnp.int32, lg.shape, 1)
    sentinel = jnp.int32(ROUTE_W)
    rowmax = lambda mask: jnp.max(jnp.where(mask, lg, NEG_INF), axis=-1, keepdims=True)
    first = lambda mask: jnp.min(jnp.where(mask, lane, sentinel), axis=-1, keepdims=True)

    is_g = lane < N_GROUPS
    gmax = rowmax(is_g)
    g_idx = first(is_g & (lg == gmax))
    g_w = 1.0 / jnp.sum(jnp.where(is_g, jnp.exp(lg - gmax), 0.0), axis=-1, keepdims=True)

    e_lane = lane - N_GROUPS
    sel = (e_lane >= 0) & (e_lane < N_EXPERTS) & ((e_lane >> 3) == g_idx)
    m1 = rowmax(sel)
    i1 = first(sel & (lg == m1))
    sel2 = sel & (lane != i1)
    m2 = rowmax(sel2)
    i2 = first(sel2 & (lg == m2))
    ex = jnp.exp(m2 - m1)
    w0 = g_w / (1.0 + ex)
    w1 = g_w * ex / (1.0 + ex)

    oh0 = lane == i1
    oh1 = lane == i2
    both = (oh0 | oh1).astype(F32) * live
    rid = lax.broadcasted_iota(jnp.int32, (tm, tm), 0)
    cid = lax.broadcasted_iota(jnp.int32, (tm, tm), 1)
    before = jnp.dot((cid < rid).astype(BF16), both.astype(BF16), preferred_element_type=F32) + run_ref[...]
    rank0 = jnp.sum(jnp.where(oh0, before, 0.0), axis=-1, keepdims=True)
    rank1 = jnp.sum(jnp.where(oh1, before, 0.0), axis=-1, keepdims=True)
    run = run_ref[...] + jnp.sum(both, axis=0, keepdims=True)
    run_ref[...] = run
    cnt_ref[...] = run

    slab = jnp.zeros(lg.shape, F32)
    for ln, val in ((RT_W0, w0), (RT_W1, w1), (RT_E0, (i1 - N_GROUPS).astype(F32)),
                    (RT_E1, (i2 - N_GROUPS).astype(F32)), (RT_R0, rank0), (RT_R1, rank1)):
        slab = jnp.where(lane == ln, val, slab)
    rt_ref[...] = slab
    rtt_ref[...] = jnp.transpose(slab)[:ROUTE_FIELDS, :]


assert EXPERTS_PER_GROUP == 8


def _block_layout(rtt, cnt, n_tok):
    m = n_tok * TOP_K
    counts = cnt[0, N_GROUPS:N_GROUPS + N_EXPERTS].astype(jnp.int32)
    pcounts = (counts + MOE_TB - 1) // MOE_TB * MOE_TB
    pends = jnp.cumsum(pcounts)
    pstarts = pends - pcounts
    n_blocks = -(-(m + N_EXPERTS * (MOE_TB - 1)) // MOE_TB)
    e_idx = rtt[RT_E0:RT_E1 + 1, :].astype(jnp.int32)
    rank = rtt[RT_R0:RT_R1 + 1, :].astype(jnp.int32)
    onehot = e_idx[:, None, :] == jnp.arange(N_EXPERTS, dtype=jnp.int32)[None, :, None]
    pos = jnp.sum(jnp.where(onehot, pstarts[None, :, None], 0), axis=1) + rank
    blk_start = jnp.arange(n_blocks, dtype=jnp.int32) * MOE_TB
    blk_e = jnp.minimum(jnp.sum(blk_start[:, None] >= pends[None, :], axis=-1), N_EXPERTS - 1).astype(jnp.int32)
    nused = (pends[-1:] // MOE_TB).astype(jnp.int32)
    ids = jnp.arange(N_EXPERTS, dtype=jnp.int32)
    later_used = (ids[None, :] > ids[:, None]) & (counts[None, :] > 0)
    next_used = jnp.min(jnp.where(later_used, ids[None, :], N_EXPERTS), axis=-1)
    next_used = jnp.where(next_used == N_EXPERTS, -1, next_used).astype(jnp.int32)
    nxt_e = jnp.sum(jnp.where(blk_e[:, None] == ids[None, :], next_used[None, :], 0), axis=-1).astype(jnp.int32)
    ends_expert = jnp.any(((blk_start + MOE_TB)[:, None] == pends[None, :]) & (pcounts[None, :] > 0), axis=-1)
    zero_blk = (ends_expert | (blk_start >= pends[-1])).astype(jnp.int32)
    return pos, blk_e, nxt_e, nused, zero_blk


def _step_indices(pos, tm):
    nsteps = pos.shape[1] // tm
    return pos.reshape(TOP_K, nsteps, tm).transpose(1, 0, 2).reshape(nsteps, 1, TOP_K * tm)


def _tile_rows(i):
    return pl.ds(pl.multiple_of(i * TOK_TILE, TOK_TILE), TOK_TILE)


def _dispatch_kernel(zero_blk_ref, pos_ref, xp_ref, xs_hbm, xbuf, zbuf, sem, zsem, *, tm, n_blocks):
    s = pl.program_id(0)
    nsteps = pl.num_programs(0)
    slot = s % 2
    blk_rows = MOE_TB * TOK_TILE

    def copy(j, dst, sl):
        return pltpu.make_async_copy(xbuf.at[sl, _tile_rows(j)], xs_hbm.at[_tile_rows(dst)], sem.at[sl])

    def drain(sl):
        def body(j, c):
            copy(0, 0, sl).wait()
            return c
        lax.fori_loop(0, TOP_K * tm, body, 0, unroll=8)

    @pl.when(s == 0)
    def _():
        zbuf[...] = jnp.zeros_like(zbuf)

        def zero_copy(blk):
            rows = pl.ds(pl.multiple_of(blk * blk_rows, blk_rows), blk_rows)
            return pltpu.make_async_copy(zbuf, xs_hbm.at[rows], zsem)

        def each_flagged(fn):
            def body(blk, c):
                @pl.when(zero_blk_ref[blk] > 0)
                def _():
                    fn(zero_copy(blk))
                return c
            lax.fori_loop(0, n_blocks, body, 0)

        each_flagged(lambda cp: cp.start())
        each_flagged(lambda cp: cp.wait())

    @pl.when(s >= 2)
    def _():
        drain(slot)

    xbuf[slot] = xp_ref[...]

    for j in range(tm):
        for k in range(TOP_K):
            copy(j, pos_ref[0, 0, k * tm + j], slot).start(priority=k)

    @pl.when(s == nsteps - 1)
    def _():
        drain(1 - slot)
        drain(slot)


def _moe_dispatch(xp, pos, zero_blk, tm=512):
    n = pos.shape[1]
    n_blocks = zero_blk.shape[0]
    assert n // tm >= 2
    pos3 = _step_indices(pos, tm)
    grid_spec = pltpu.PrefetchScalarGridSpec(
        num_scalar_prefetch=1,
        grid=(n // tm,),
        in_specs=[pl.BlockSpec((1, 1, TOP_K * tm), lambda i, zb: (i, 0, 0), memory_space=pltpu.SMEM),
                  pl.BlockSpec((tm * TOK_TILE, LANES_V7X), lambda i, zb: (i, 0))],
        out_specs=pl.BlockSpec(memory_space=pl.ANY),
        scratch_shapes=[pltpu.VMEM((2, tm * TOK_TILE, LANES_V7X), U32),
                        pltpu.VMEM((MOE_TB * TOK_TILE, LANES_V7X), U32),
                        pltpu.SemaphoreType.DMA((2,)), pltpu.SemaphoreType.DMA(())],
    )
    return pl.pallas_call(
        functools.partial(_dispatch_kernel, tm=tm, n_blocks=n_blocks),
        grid_spec=grid_spec,
        out_shape=jax.ShapeDtypeStruct((n_blocks * MOE_TB * TOK_TILE, LANES_V7X), U32),
        compiler_params=_cparams(1),
        name="moe_dispatch",
    )(zero_blk, pos3, xp)


def _moe_kernel(blk_e_ref, nxt_e_ref, nused_ref, xs_ref, w1_hbm, w3_hbm, w2_hbm, ys_ref,
                wf1, wf3, wf2, w1b, w3b, w2b, slot_ref, sem, *, layer):
    s = pl.program_id(0)
    nused = nused_ref[0]

    def fetch(e, sl):
        return [pltpu.make_async_copy(w_hbm.at[layer, e], wf.at[sl], sem.at[sl, k])
                for k, (w_hbm, wf) in enumerate(((w1_hbm, wf1), (w3_hbm, wf3), (w2_hbm, wf2)))]

    @pl.when(s >= nused)
    def _():
        ys_ref[...] = jnp.zeros_like(ys_ref)

    @pl.when(s < nused)
    def _():
        e = blk_e_ref[s]
        prev = blk_e_ref[jnp.maximum(s - 1, 0)]

        @pl.when(s == 0)
        def _():
            slot_ref[0] = 0
            for cp in fetch(e, 0):
                cp.start()

        @pl.when((s > 0) & (e != prev))
        def _():
            slot_ref[0] = 1 - slot_ref[0]

        @pl.when((s == 0) | (e != prev))
        def _():
            sl = slot_ref[0]
            for cp in fetch(e, sl):
                cp.wait()
            w1b[...] = wf1[sl].astype(BF16)
            w3b[...] = wf3[sl].astype(BF16)
            w2b[...] = wf2[sl].astype(BF16)
            nxt = nxt_e_ref[s]

            @pl.when(nxt >= 0)
            def _():
                for cp in fetch(nxt, 1 - sl):
                    cp.start()

        xb = _unpack_rows(_load_token_tiles(xs_ref, 0, MOE_TB)).astype(BF16)
        h1 = jnp.dot(xb, w1b[...], preferred_element_type=F32)
        h3 = jnp.dot(xb, w3b[...], preferred_element_type=F32)
        h = (h1 * jax.nn.sigmoid(h1) * h3).astype(BF16)
        y = jnp.dot(h, w2b[...], preferred_element_type=F32)
        _store_token_tiles(ys_ref, 0, _pack_rows(y))


def _moe_experts(xs, blk_e, nxt_e, nused, w1, w3, w2, layer):
    n_blocks = blk_e.shape[0]
    tiles = pl.BlockSpec((MOE_TB * TOK_TILE, LANES_V7X), lambda s, be, ne, nu: (s, 0))
    hbm = pl.BlockSpec(memory_space=pl.ANY)
    up, down = (D_MODEL, D_EXPERT), (D_EXPERT, D_MODEL)
    grid_spec = pltpu.PrefetchScalarGridSpec(
        num_scalar_prefetch=3,
        grid=(n_blocks,),
        in_specs=[tiles, hbm, hbm, hbm],
        out_specs=tiles,
        scratch_shapes=[pltpu.VMEM((2,) + up, F32), pltpu.VMEM((2,) + up, F32), pltpu.VMEM((2,) + down, F32),
                        pltpu.VMEM(up, BF16), pltpu.VMEM(up, BF16), pltpu.VMEM(down, BF16),
                        pltpu.SMEM((1,), jnp.int32), pltpu.SemaphoreType.DMA((2, 3))],
    )
    return pl.pallas_call(
        functools.partial(_moe_kernel, layer=layer),
        grid_spec=grid_spec,
        out_shape=jax.ShapeDtypeStruct(xs.shape, U32),
        compiler_params=_cparams(1, VMEM_LIMIT_V7X),
        name="moe_experts",
    )(blk_e, nxt_e, nused, xs, w1, w3, w2)


def _combine_kernel(pos_ref, nxt_ref, ys_hbm, x_ref, rt_ref, g_ref, b_ref, o_ref, ybuf, sem, *, tm):
    s = pl.program_id(0)
    nsteps = pl.num_programs(0)
    slot = s % 2

    def copy(src, j, sl):
        return pltpu.make_async_copy(ys_hbm.at[_tile_rows(src)], ybuf.at[sl, _tile_rows(j)], sem.at[sl])

    def start_gather(idx_ref, sl):
        def body(j, c):
            copy(idx_ref[0, 0, j], j, sl).start()
            return c
        lax.fori_loop(0, TOP_K * tm, body, 0, unroll=8)

    @pl.when(s == 0)
    def _():
        start_gather(pos_ref, 0)

    @pl.when(s + 1 < nsteps)
    def _():
        for j in range(TOP_K * tm):
            copy(nxt_ref[0, 0, j], j, 1 - slot).start(priority=j % 2)

    def wait_body(j, c):
        copy(0, j, slot).wait()
        return c
    lax.fori_loop(0, TOP_K * tm, wait_body, 0, unroll=8)

    rt = rt_ref[...]
    ffn = _unpack_rows(_load_token_tiles(ybuf, 0, tm, lead=slot)) * rt[:, RT_W0:RT_W0 + 1]
    ffn = ffn + _unpack_rows(_load_token_tiles(ybuf, tm * TOK_TILE, tm, lead=slot)) * rt[:, RT_W1:RT_W1 + 1]
    o_ref[...] = _layer_norm_rows(DN_ALPHA * x_ref[...] + ffn, g_ref[...], b_ref[...])


def _moe_combine(ys, pos, rt, x, g, b, tm=256):
    n = x.shape[0]
    nsteps = n // tm
    pos3 = _step_indices(pos, tm)
    smem_blk = lambda f: pl.BlockSpec((1, 1, TOP_K * tm), f, memory_space=pltpu.SMEM)
    row = lambda width: pl.BlockSpec((tm, width), lambda i: (i, 0))
    const = lambda shape: pl.BlockSpec(shape, lambda i: (0, 0))
    return pl.pallas_call(
        functools.partial(_combine_kernel, tm=tm),
        grid=(nsteps,),
        in_specs=[smem_blk(lambda i: (i, 0, 0)), smem_blk(lambda i: (jnp.minimum(i + 1, nsteps - 1), 0, 0)),
                  pl.BlockSpec(memory_space=pl.ANY), row(D_MODEL), row(ROUTE_W),
                  const((1, D_MODEL)), const((1, D_MODEL))],
        out_specs=row(D_MODEL),
        out_shape=jax.ShapeDtypeStruct((n, D_MODEL), F32),
        scratch_shapes=[pltpu.VMEM((2, TOP_K * tm * TOK_TILE, LANES_V7X), U32), pltpu.SemaphoreType.DMA((2,))],
        compiler_params=_cparams(1, 40 * MIB),
        name="moe_combine",
    )(pos3, pos3, ys, x, rt, g.reshape(1, -1), b.reshape(1, -1))


def _moe_layer(x1, xp, rt, rtt, cnt, w1, w3, w2, layer, g, b):
    n_tok = x1.shape[0]
    pos, blk_e, nxt_e, nused, zero_blk = _block_layout(rtt, cnt, n_tok)
    xs = _moe_dispatch(xp, pos, zero_blk)
    ys = _moe_experts(xs, blk_e, nxt_e, nused, w1, w3, w2, layer)
    return _moe_combine(ys, pos, rt, x1, g, b)


def _gmlp_kernel(u_ref, v_ref, g_ref, b_ref, w_ref, bs_ref, o_ref, *, chunks):
    for ci in range(chunks):
        rows = slice(ci * C_CHUNK, (ci + 1) * C_CHUNK)
        u = jax.nn.gelu(u_ref[rows, :])
        v = _layer_norm_rows(jax.nn.gelu(v_ref[rows, :]), g_ref[...], b_ref[...]).astype(BF16)
        for gi in range(C_GROUPS):
            cols = slice(gi * C_GROUP_DIM, (gi + 1) * C_GROUP_DIM)
            mixed = jnp.dot(w_ref[gi], v[:, cols], preferred_element_type=F32) + bs_ref[:, cols]
            o_ref[rows, cols] = (u[:, cols] * mixed).astype(o_ref.dtype)


def _gmlp(proj, ln_g, ln_b, w_s, b_s, chunks=4):
    n = proj.shape[0]
    tm = chunks * C_CHUNK
    w = (w_s * jnp.tril(jnp.ones((C_CHUNK, C_CHUNK), w_s.dtype))).astype(BF16)
    bs_full = jnp.repeat(b_s.T, C_GROUP_DIM, axis=1)
    const2 = lambda shape: pl.BlockSpec(shape, lambda i: (0, 0))
    return pl.pallas_call(
        functools.partial(_gmlp_kernel, chunks=chunks),
        grid=(n // tm,),
        in_specs=[pl.BlockSpec((tm, C_W), lambda i: (i, 0)), pl.BlockSpec((tm, C_W), lambda i: (i, 1)),
                  const2((1, C_W)), const2((1, C_W)),
                  pl.BlockSpec((C_GROUPS, C_CHUNK, C_CHUNK), lambda i: (0, 0, 0)), const2((C_CHUNK, C_W))],
        out_specs=pl.BlockSpec((tm, C_W), lambda i: (i, 0)),
        out_shape=jax.ShapeDtypeStruct((n, C_W), BF16),
        compiler_params=_cparams(1, 32 * MIB),
        name="gmlp_gating",
    )(proj, proj, ln_g.reshape(1, -1), ln_b.reshape(1, -1), w, bs_full)


CONV_HIST = 32


def _conv_kernel(a_ref, gt_ref, ap_ref, gp_ref, w_ref, cb_ref, g_ref, b_ref, o_ref, hbuf, hshift, *, ts):
    i = pl.program_id(1)
    hist = ap_ref[...] * jax.nn.sigmoid(gp_ref[...])
    hbuf[0:CONV_HIST, :] = jnp.where(i > 0, hist, jnp.zeros_like(hist))
    hbuf[CONV_HIST:CONV_HIST + ts, :] = a_ref[...] * jax.nn.sigmoid(gt_ref[...])
    off = CONV_HIST - (D_CONV - 1)
    acc = jnp.zeros((ts, D_CHANNELS), F32) + cb_ref[...]
    for r in range(SUBLANES_V7X):
        taps = [j for j in range(D_CONV) if (off + j) % SUBLANES_V7X == r]
        if not taps:
            continue
        src = hbuf
        if r:
            span = max(taps) + off - r + ts
            hshift[0:span, :] = hbuf[r:r + span, :]
            src = hshift
        for j in taps:
            base = off + j - r
            acc = acc + w_ref[j:j + 1, :] * src[base:base + ts, :]
    y = _layer_norm_rows(acc, g_ref[...], b_ref[...])
    o_ref[...] = (y * jax.nn.sigmoid(y)).astype(o_ref.dtype)


def _conformer_conv(proj, conv_w, conv_b, ln_g, ln_b, bsz, seq, ts=512):
    nt = seq // ts
    acol = 2 * C_W // D_CHANNELS
    gcol = acol + 1
    hb = ts // CONV_HIST
    cur = lambda col: pl.BlockSpec((ts, D_CHANNELS), lambda b, i: (b * nt + i, col))
    prev = lambda col: pl.BlockSpec((CONV_HIST, D_CHANNELS),
                                    lambda b, i: (jnp.maximum((b * nt + i) * hb - 1, 0), col))
    const2 = lambda shape: pl.BlockSpec(shape, lambda b, i: (0, 0))
    return pl.pallas_call(
        functools.partial(_conv_kernel, ts=ts),
        grid=(bsz, nt),
        in_specs=[cur(acol), cur(gcol), prev(acol), prev(gcol),
                  const2((D_CONV, D_CHANNELS)), const2((1, D_CHANNELS)), const2((1, D_CHANNELS)),
                  const2((1, D_CHANNELS))],
        out_specs=pl.BlockSpec((ts, D_CHANNELS), lambda b, i: (b * nt + i, 0)),
        out_shape=jax.ShapeDtypeStruct((bsz * seq, D_CHANNELS), BF16),
        scratch_shapes=[pltpu.VMEM((CONV_HIST + ts, D_CHANNELS), F32), pltpu.VMEM((CONV_HIST + ts, D_CHANNELS), F32)],
        compiler_params=_cparams(2, 32 * MIB),
        name="conformer_conv",
    )(proj, proj, proj, proj, conv_w, conv_b.reshape(1, -1), ln_g.reshape(1, -1), ln_b.reshape(1, -1))


def kernel(x, w_in_ab, attn_sinks, rel_bias, hgrn_lb_logits, hgrn_norm_g, w_out_ab, w_in_cd, gmlp_ln_g, gmlp_ln_b, gmlp_w_s, gmlp_b_s, conv_w, conv_b, conv_ln_g, conv_ln_b, w_out_cd, ln_mix_g, ln_mix_b, ln_ffn_g, ln_ffn_b, moe_w_group, moe_b_group, moe_w_router, moe_b_router, moe_w1, moe_w3, moe_w2):
    bsz, seq = x.shape[0], x.shape[1]
    n_tok = bsz * seq
    xf = x.reshape(n_tok, D_MODEL)
    lb_table = jnp.cumsum(jax.nn.softmax(hgrn_lb_logits.astype(F32), axis=0), axis=0)
    bias = _bias_table(rel_bias.astype(F32), _t5_bucket_table())

    for layer in range(DEPTH):
        j = layer // 2
        if layer % 2 == 0:
            proj = _matmul(xf, w_in_ab[j].astype(BF16), 1024, EVEN_IN // 3, F32)
            ya = _attention(proj, bias, attn_sinks[j].astype(F32), bsz, seq)
            yb = _hgrn(proj, lb_table[layer], hgrn_norm_g[j].astype(F32), bsz, seq)
            w_out = w_out_ab[j]
        else:
            proj = _matmul(xf, w_in_cd[j].astype(BF16), 1024, ODD_IN // 2, F32)
            ya = _gmlp(proj, gmlp_ln_g[j], gmlp_ln_b[j], gmlp_w_s[j], gmlp_b_s[j])
            yb = _conformer_conv(proj, conv_w[j], conv_b[j], conv_ln_g[j], conv_ln_b[j], bsz, seq)
            w_out = w_out_cd[j]
        wr, rbias = _router_weights(moe_w_group[layer], moe_b_group[layer],
                                    moe_w_router[layer], moe_b_router[layer])
        x1, xp, rt, rtt, cnt = _outproj_ln_route(ya, yb, w_out.astype(BF16), xf, ln_mix_g[layer],
                                                 ln_mix_b[layer], wr, rbias)
        xf = _moe_layer(x1, xp, rt, rtt, cnt, moe_w1, moe_w3, moe_w2, layer, ln_ffn_g[layer], ln_ffn_b[layer])
    return xf.reshape(bsz, seq, D_MODEL)
```

```python
import functools
import math

import jax
import jax.numpy as jnp
from jax import lax
from jax.experimental import pallas as pl
from jax.experimental.pallas import tpu as pltpu

D_MODEL = 2048
DEPTH = 2
A_HEADS = 16
A_KV_HEADS = 2
A_HEAD_DIM = 64
A_WINDOW = 128
A_BLOCK = 128
REL_BUCKETS = 32
REL_MAX_DIST = 128
B_HEADS = 8
B_DK = 128
B_DV = 128
C_GROUPS = 8
C_GROUP_DIM = 128
C_CHUNK = 128
D_CHANNELS = 1024
D_CONV = 31
A_QW = A_HEADS * A_HEAD_DIM
A_KVW = A_KV_HEADS * A_HEAD_DIM
B_KW = B_HEADS * B_DK
B_VW = B_HEADS * B_DV
C_W = C_GROUPS * C_GROUP_DIM
EVEN_IN = A_QW + 2 * A_KVW + 2 * B_KW + 2 * B_VW
ODD_IN = 2 * C_W + 2 * D_CHANNELS
N_GROUPS = 4
EXPERTS_PER_GROUP = 8
N_EXPERTS = N_GROUPS * EXPERTS_PER_GROUP
TOP_K = 2
D_EXPERT = 512
DN_ALPHA = (2 * DEPTH) ** 0.25
LN_EPS = 1e-5
RMS_EPS = 1e-6

LANES_V7X = 128
SUBLANES_V7X = 8
MIB = 1024 * 1024
VMEM_LIMIT_V7X = 56 * MIB

HGRN_CHUNK = 128
HGRN_SUB = 8
HGRN_GROUP = 32
HGRN_ROWS = 512
MOE_TB = 256
MOE_HIDDEN_TILE = 256
ROUTE_W = LANES_V7X
HALF = D_MODEL // 2
TOK_TILE = HALF // LANES_V7X
assert TOK_TILE == SUBLANES_V7X

BF16 = jnp.bfloat16
F32 = jnp.float32
U32 = jnp.uint32
NEG_INF = float("-inf")
LOG2E = math.log2(math.e)
RT_W0, RT_W1, RT_E0, RT_E1, RT_R0, RT_R1 = range(6)
ROUTE_FIELDS = SUBLANES_V7X


def _cparams(n_axes, vmem_bytes=None):
    return pltpu.CompilerParams(dimension_semantics=("arbitrary",) * n_axes, vmem_limit_bytes=vmem_bytes)


def _layer_norm_rows(z, g, b):
    mu = jnp.mean(z, axis=-1, keepdims=True)
    zc = z - mu
    var = jnp.mean(zc * zc, axis=-1, keepdims=True)
    return zc * lax.rsqrt(var + LN_EPS) * g + b


def _dot_nt(a, b):
    return lax.dot_general(a, b, (((1,), (1,)), ((), ())), preferred_element_type=F32)


def _dot_tn(a, b):
    return lax.dot_general(a, b, (((0,), (0,)), ((), ())), preferred_element_type=F32)


def _pack_rows(y):
    lo = lax.bitcast_convert_type(y[:, :HALF].astype(BF16).astype(F32), U32) >> 16
    hi = lax.bitcast_convert_type(y[:, HALF:].astype(BF16).astype(F32), U32) & jnp.uint32(0xFFFF0000)
    return lo | hi


def _unpack_rows(p):
    lo = lax.bitcast_convert_type(p << 16, F32)
    hi = lax.bitcast_convert_type(p & jnp.uint32(0xFFFF0000), F32)
    return jnp.concatenate([lo, hi], axis=1)


def _store_token_tiles(ref, base, packed):
    rows = packed.shape[0]
    for c in range(TOK_TILE):
        ref[pl.ds(base + c, rows, stride=TOK_TILE), :] = packed[:, c * LANES_V7X:(c + 1) * LANES_V7X]


def _load_token_tiles(ref, base, rows, lead=None):
    parts = []
    for c in range(TOK_TILE):
        idx = (pl.ds(base + c, rows, stride=TOK_TILE), slice(None))
        parts.append(ref[idx] if lead is None else ref[(lead,) + idx])
    return jnp.concatenate(parts, axis=1)


def _mm_kernel(a_ref, w_ref, o_ref):
    a = a_ref[...].astype(BF16)
    o_ref[...] = jnp.dot(a, w_ref[...], preferred_element_type=F32).astype(o_ref.dtype)


def _matmul(a, w, tm, tn, out_dtype):
    m, k = a.shape
    n = w.shape[1]
    return pl.pallas_call(
        _mm_kernel,
        grid=(n // tn, m // tm),
        in_specs=[pl.BlockSpec((tm, k), lambda j, i: (i, 0)), pl.BlockSpec((k, tn), lambda j, i: (0, j))],
        out_specs=pl.BlockSpec((tm, tn), lambda j, i: (i, j)),
        out_shape=jax.ShapeDtypeStruct((m, n), out_dtype),
        compiler_params=_cparams(2, VMEM_LIMIT_V7X),
        name="proj_matmul",
    )(a, w)


def _bias_table_kernel(rb_ref, bucket_ref, o_ref):
    bucket = bucket_ref[...]
    for h in range(A_HEADS):
        acc = jnp.zeros(bucket.shape, F32)
        for bk in range(REL_BUCKETS):
            acc = jnp.where(bucket == bk, rb_ref[bk, h], acc)
        o_ref[h] = acc


def _bias_table(rel_bias, bucket):
    return pl.pallas_call(
        _bias_table_kernel,
        in_specs=[pl.BlockSpec(memory_space=pltpu.SMEM), pl.BlockSpec(memory_space=pltpu.VMEM)],
        out_specs=pl.BlockSpec(memory_space=pltpu.VMEM),
        out_shape=jax.ShapeDtypeStruct((A_HEADS,) + bucket.shape, F32),
        name="rel_bias_table",
    )(rel_bias, bucket)


def _t5_bucket_table():
    t_loc = jnp.arange(A_BLOCK, dtype=jnp.int32)[:, None]
    j_loc = jnp.arange(A_BLOCK, dtype=jnp.int32)[None, :]
    dist = jnp.where(j_loc <= t_loc, t_loc - j_loc, t_loc + A_BLOCK - j_loc)
    max_exact = REL_BUCKETS // 2
    d = jnp.maximum(dist, 1).astype(F32)
    large = max_exact + (jnp.log(d / max_exact) / math.log(REL_MAX_DIST / max_exact)
                         * (REL_BUCKETS - max_exact)).astype(jnp.int32)
    large = jnp.minimum(large, REL_BUCKETS - 1)
    return jnp.where(dist < max_exact, dist, large)


assert A_WINDOW == A_BLOCK


ATTN_BLOCKS_PER_STEP = 8


def _attn_kernel(q_ref, kc_ref, vc_ref, kp_ref, vp_ref, bias_ref, sink_ref, o_ref):
    blk = A_BLOCK
    first_pair = pl.program_id(1) == 0
    for sb in range(ATTN_BLOCKS_PER_STEP):
        rows = slice(sb * blk, (sb + 1) * blk)
        before = slice((sb - 1) * blk, sb * blk)
        k_prev, v_prev = (kp_ref[...], vp_ref[...]) if sb == 0 else (kc_ref[before, :], vc_ref[before, :])
        no_prev = jnp.where(first_pair, NEG_INF, 0.0) if sb == 0 else 0.0
        _attn_block(q_ref, rows, k_prev, v_prev, kc_ref[rows, :], vc_ref[rows, :], no_prev, bias_ref, sink_ref, o_ref)


def _attn_block(q_ref, rows, k_prev, v_prev, k_own, v_own, no_prev, bias_ref, sink_ref, o_ref):
    blk = A_BLOCK
    k2 = jnp.concatenate([k_prev, k_own], axis=0)
    v2 = jnp.concatenate([v_prev, v_own], axis=0)
    k2r = pltpu.roll(k2, A_HEAD_DIM, 1)
    v2r = pltpu.roll(v2, A_HEAD_DIM, 1)
    lo = lax.broadcasted_iota(jnp.int32, k2.shape, 1) < A_HEAD_DIM
    zero = jnp.zeros_like(k2)

    def placed(x, xr, g, par):
        src = x if g == par else xr
        return (jnp.where(lo, src, zero) if par == 0 else jnp.where(lo, zero, src)).astype(BF16)

    kk = [[placed(k2, k2r, g, par) for par in range(2)] for g in range(A_KV_HEADS)]
    vv = [[placed(v2, v2r, g, par) for par in range(2)] for g in range(A_KV_HEADS)]

    own = (lax.broadcasted_iota(jnp.int32, (blk, blk), 1) <= lax.broadcasted_iota(jnp.int32, (blk, blk), 0))
    zero_p = jnp.zeros((blk, blk), F32)

    heads_per_kv = A_HEADS // A_KV_HEADS
    for p in range(A_HEADS // 2):
        g = (2 * p) // heads_per_kv
        qp = (q_ref[rows, p * 128:(p + 1) * 128] * (A_HEAD_DIM ** -0.5)).astype(BF16)
        acc = jnp.zeros((blk, 128), F32)
        for par in range(2):
            h = 2 * p + par
            sink = sink_ref[h]
            s2 = _dot_nt(qp, kk[g][par])
            logits = jnp.where(own, s2[:, blk:], s2[:, :blk] + no_prev) + bias_ref[h]
            m = jnp.maximum(jnp.max(logits, axis=-1, keepdims=True), sink)
            e = jnp.exp(logits - m)
            den = jnp.sum(e, axis=-1, keepdims=True) + jnp.exp(sink - m)
            e2 = jnp.concatenate([jnp.where(own, zero_p, e), jnp.where(own, e, zero_p)], axis=1).astype(BF16)
            acc = acc + jnp.dot(e2, vv[g][par], preferred_element_type=F32) * (1.0 / den)
        o_ref[rows, p * 128:(p + 1) * 128] = acc.astype(o_ref.dtype)


def _attention(proj, bias, sinks, bsz, seq):
    per = ATTN_BLOCKS_PER_STEP
    npair = seq // (per * A_BLOCK)
    kcol = A_QW // 128
    vcol = kcol + 1
    row = lambda b, m: b * npair + m
    prow = lambda b, m: (b * npair + m) * per - jnp.where(m > 0, 1, 0)
    return pl.pallas_call(
        _attn_kernel,
        grid=(bsz, npair),
        in_specs=[
            pl.BlockSpec((per * A_BLOCK, A_QW), lambda b, m: (row(b, m), 0)),
            pl.BlockSpec((per * A_BLOCK, 128), lambda b, m: (row(b, m), kcol)),
            pl.BlockSpec((per * A_BLOCK, 128), lambda b, m: (row(b, m), vcol)),
            pl.BlockSpec((A_BLOCK, 128), lambda b, m: (prow(b, m), kcol)),
            pl.BlockSpec((A_BLOCK, 128), lambda b, m: (prow(b, m), vcol)),
            pl.BlockSpec((A_HEADS, A_BLOCK, A_BLOCK), lambda b, m: (0, 0, 0)),
            pl.BlockSpec(memory_space=pltpu.SMEM),
        ],
        out_specs=pl.BlockSpec((per * A_BLOCK, A_QW), lambda b, m: (row(b, m), 0)),
        out_shape=jax.ShapeDtypeStruct((bsz * seq, A_QW), BF16),
        compiler_params=_cparams(2, 32 * MIB),
        name="swa_attention",
    )(proj, proj, proj, proj, proj, bias, sinks)


HGRN_HEADS_PER_STEP = 4
HGRN_IN_BLOCKS = HGRN_HEADS_PER_STEP // 2


def _hgrn_kernel(*refs):
    nin = HGRN_IN_BLOCKS
    q_refs, f_refs, i_refs, g_refs = (refs[k * nin:(k + 1) * nin] for k in range(4))
    lb_ref, ng_ref, o_ref, st_ref = refs[4 * nin:]
    in_cols = lambda hh: slice((hh % 2) * 128, (hh % 2 + 1) * 128)
    c, sub, grp = HGRN_CHUNK, HGRN_SUB, HGRN_GROUP
    ngrp = c // grp

    @pl.when(pl.program_id(2) == 0)
    def _():
        st_ref[...] = jnp.zeros_like(st_ref)

    rid = lax.broadcasted_iota(jnp.int32, (c, c), 0)
    cid = lax.broadcasted_iota(jnp.int32, (c, c), 1)
    grp_start = (rid // grp) * grp
    sub_start = (rid // sub) * sub
    m_cum = jnp.concatenate([(cid < grp_start).astype(F32),
                             ((cid >= grp_start) & (cid < sub_start)).astype(F32),
                             ((cid >= sub_start) & (cid <= rid)).astype(F32)], axis=0).astype(BF16)
    band = jnp.where(cid >= sub_start, rid - cid, -1)
    same_grp = (cid // grp) == (rid // grp)
    heads = range(HGRN_HEADS_PER_STEP)
    zeros_bf = lambda rows: jnp.zeros((rows, B_DK), BF16)

    def scores(hh, r0):
        cols = slice(hh * 128, (hh + 1) * 128)
        lb = lb_ref[0, :, cols]
        q = q_refs[hh // 2][pl.ds(r0, c), in_cols(hh)]
        fl = f_refs[hh // 2][pl.ds(r0, c), in_cols(hh)]
        qf = q * jax.nn.sigmoid(q)
        f = lb + (1.0 - lb) * jax.nn.sigmoid(fl)
        kin = 1.0 - f
        logf = jnp.log(f)
        l1 = logf.astype(BF16)
        res = logf - l1.astype(F32)
        l2 = res.astype(BF16)
        l3 = (res - l2.astype(F32)).astype(BF16)
        cum = jnp.dot(m_cum, jnp.concatenate([l1, l2, l3], axis=1), preferred_element_type=F32)
        cum = cum[:, :B_DK] + cum[:, B_DK:2 * B_DK] + cum[:, 2 * B_DK:]
        rg, rs, bq = cum[:c], cum[c:2 * c], cum[2 * c:]
        lk = jnp.log(kin)
        wq = lk - bq
        wg = wq - rs
        wb = wg - rg
        qt = qf * jnp.exp(bq)
        qg = qt * jnp.exp(rs)
        o = _dot_nt((qg * jnp.exp(rg)).astype(BF16), st_ref[hh].astype(BF16))

        kts = []
        for i in range(1, ngrp):
            n = i * grp
            kts += [jnp.exp(rg[n:n + 1, :] + wb[:n, :]).astype(BF16), zeros_bf(c - n)]
        g_far = _dot_nt(qg.astype(BF16), jnp.concatenate(kts, axis=0))
        s_far = jnp.concatenate([jnp.zeros((grp, c), F32)] +
                                [g_far[i * grp:(i + 1) * grp, (i - 1) * c:i * c] for i in range(1, ngrp)], axis=0)

        kts = []
        for j in range(1, grp // sub):
            for gi in range(ngrp):
                a, n = gi * grp, j * sub
                kts += [jnp.exp(rs[a + n:a + n + 1, :] + wg[a:a + n, :]).astype(BF16), zeros_bf(grp - n)]
        g_near = _dot_nt(qt.astype(BF16), jnp.concatenate(kts, axis=0))
        pieces = []
        for i in range(c // sub):
            j = i % (grp // sub)
            pieces.append(jnp.zeros((sub, c), F32) if j == 0 else g_near[i * sub:(i + 1) * sub, (j - 1) * c:j * c])
        s = jnp.where(same_grp, jnp.concatenate(pieces, axis=0), s_far)

        bq2, wq2 = bq * LOG2E, wq * LOG2E
        kds = [kin.astype(BF16)] + [jnp.exp2(pltpu.roll(bq2, c - d, 0) + wq2).astype(BF16) for d in range(1, sub)]
        g_diag = _dot_nt(qf.astype(BF16), jnp.concatenate(kds, axis=0))
        last = slice(c - 1, c)
        return dict(o=o, s=s, g_diag=g_diag, wb=wb, b_last=rg[last, :] + rs[last, :] + bq[last, :])

    def finish(hh, r0, h):
        cols = slice(hh * 128, (hh + 1) * 128)
        vb = i_refs[hh // 2][pl.ds(r0, c), in_cols(hh)].astype(BF16)
        gt = g_refs[hh // 2][pl.ds(r0, c), in_cols(hh)]
        o = h["o"] + jnp.dot(h["s"].astype(BF16), vb, preferred_element_type=F32)
        st_ref[hh] = (st_ref[hh] * jnp.exp(h["b_last"])
                      + _dot_tn(vb, jnp.exp(h["b_last"] + h["wb"]).astype(BF16)))
        o = o * lax.rsqrt(jnp.mean(o * o, axis=-1, keepdims=True) + RMS_EPS)
        o_ref[pl.ds(r0, c), cols] = (o * ng_ref[0, :, cols] * (gt * jax.nn.sigmoid(gt))).astype(o_ref.dtype)

    def chunk(ci, carry):
        r0 = pl.multiple_of(ci * c, c)
        hs = [scores(hh, r0) for hh in heads]
        for d in range(sub):
            on_diag = band == d
            for h in hs:
                h["s"] = jnp.where(on_diag, h["g_diag"][:, d * c:(d + 1) * c], h["s"])
        for hh in heads:
            finish(hh, r0, hs[hh])
        return carry

    lax.fori_loop(0, HGRN_ROWS // c, chunk, 0, unroll=True)


def _hgrn(proj, lb, norm_g, bsz, seq):
    nr = seq // HGRN_ROWS
    hp, nin = HGRN_HEADS_PER_STEP, HGRN_IN_BLOCKS
    width = hp * 128
    c0 = (A_QW + 2 * A_KVW) // 256
    per_tensor = B_HEADS // 2
    nhp = B_HEADS // hp
    in_specs = [pl.BlockSpec((HGRN_ROWS, 256),
                             lambda b, h, r, t=t, p=p: (b * nr + r, c0 + t * per_tensor + h * nin + p))
                for t in range(4) for p in range(nin)]
    vec = pl.BlockSpec((1, 1, width), lambda b, h, r: (h, 0, 0))
    return pl.pallas_call(
        _hgrn_kernel,
        grid=(bsz, nhp, nr),
        in_specs=in_specs + [vec, vec],
        out_specs=pl.BlockSpec((HGRN_ROWS, width), lambda b, h, r: (b * nr + r, h)),
        out_shape=jax.ShapeDtypeStruct((bsz * seq, B_VW), BF16),
        scratch_shapes=[pltpu.VMEM((hp, B_DV, B_DK), F32)],
        compiler_params=_cparams(3, 32 * MIB),
        name="hgrn2",
    )(*([proj] * (4 * nin)), lb.reshape(nhp, 1, width), norm_g.reshape(nhp, 1, width))


OUTPROJ_LAG = 2


def _outproj_kernel(ya_ref, yb_ref, wa_ref, wb_ref, x_ref, g_ref, b_ref, wr_ref, rb_ref,
                    xo_ref, xp_ref, rt_ref, rtt_ref, cnt_ref, run_ref, lg_ref, mix_ref):
    step = pl.program_id(0)

    @pl.when(step == 0)
    def _():
        run_ref[...] = jnp.zeros_like(run_ref)
        lg_ref[...] = jnp.zeros_like(lg_ref)
        mix_ref[...] = jnp.zeros_like(mix_ref)

    lg_prev = lg_ref[...]
    mix_prev = mix_ref[...]
    mix = jnp.dot(ya_ref[...], wa_ref[...], preferred_element_type=F32)
    mix_ref[...] = mix + jnp.dot(yb_ref[...], wb_ref[...], preferred_element_type=F32)

    y = _layer_norm_rows(DN_ALPHA * x_ref[...] + mix_prev, g_ref[...], b_ref[...])
    xo_ref[...] = y
    _store_token_tiles(xp_ref, 0, _pack_rows(y))
    tm = y.shape[0]
    y_hi = y.astype(BF16)
    y_lo = (y - y_hi.astype(F32)).astype(BF16)
    prod = jnp.dot(jnp.concatenate([y_hi, y_lo], axis=0), wr_ref[...], preferred_element_type=F32)
    lg = (prod[:tm, :ROUTE_W] + prod[tm:, :ROUTE_W]) + (prod[:tm, ROUTE_W:] + prod[tm:, ROUTE_W:])
    lg_ref[...] = lg + rb_ref[...]
    _route_rows(lg_prev, (step >= OUTPROJ_LAG).astype(F32), rt_ref, rtt_ref, cnt_ref, run_ref)


def _outproj_ln_route(ya, yb, w_out, x, g, b, wr, rbias, tm=512):
    n = x.shape[0]
    ka = ya.shape[1]
    nsteps = n // tm
    lagged = lambda lag: (lambda i: (jnp.clip(i - lag, 0, nsteps - 1), 0))
    row = lambda width, lag: pl.BlockSpec((tm, width), lagged(lag))
    once = pl.Buffered(1)
    const = lambda shape: pl.BlockSpec(shape, lambda i: (0, 0), pipeline_mode=once)
    return pl.pallas_call(
        _outproj_kernel,
        grid=(nsteps + OUTPROJ_LAG,),
        in_specs=[row(ka, 0), row(ka, 0),
                  pl.BlockSpec((ka, D_MODEL), lambda i: (0, 0), pipeline_mode=once),
                  pl.BlockSpec((ka, D_MODEL), lambda i: (1, 0), pipeline_mode=once),
                  row(D_MODEL, 1), const((1, D_MODEL)), const((1, D_MODEL)),
                  const((D_MODEL, 2 * ROUTE_W)), const((1, ROUTE_W))],
        out_specs=[row(D_MODEL, 1), pl.BlockSpec((tm * TOK_TILE, LANES_V7X), lagged(1)),
                   row(ROUTE_W, OUTPROJ_LAG),
                   pl.BlockSpec((ROUTE_FIELDS, tm), lambda i: (0, jnp.clip(i - OUTPROJ_LAG, 0, nsteps - 1))),
                   pl.BlockSpec((1, ROUTE_W), lambda i: (0, 0))],
        out_shape=[jax.ShapeDtypeStruct((n, D_MODEL), F32),
                   jax.ShapeDtypeStruct((n * TOK_TILE, LANES_V7X), U32),
                   jax.ShapeDtypeStruct((n, ROUTE_W), F32), jax.ShapeDtypeStruct((ROUTE_FIELDS, n), F32),
                   jax.ShapeDtypeStruct((1, ROUTE_W), F32)],
        scratch_shapes=[pltpu.VMEM((1, ROUTE_W), F32), pltpu.VMEM((tm, ROUTE_W), F32),
                        pltpu.VMEM((tm, D_MODEL), F32)],
        compiler_params=_cparams(1, VMEM_LIMIT_V7X),
        name="outproj_ln_route",
    )(ya, yb, w_out, w_out, x, g.reshape(1, -1), b.reshape(1, -1), wr, rbias)


def _router_weights(w_group, b_group, w_router, b_router):
    w = jnp.zeros((D_MODEL, ROUTE_W), F32)
    w = w.at[:, :N_GROUPS].set(w_group).at[:, N_GROUPS:N_GROUPS + N_EXPERTS].set(w_router)
    w_hi = w.astype(BF16)
    w_lo = (w - w_hi.astype(F32)).astype(BF16)
    rb = jnp.zeros((1, ROUTE_W), F32)
    rb = rb.at[0, :N_GROUPS].set(b_group).at[0, N_GROUPS:N_GROUPS + N_EXPERTS].set(b_router)
    return jnp.concatenate([w_hi, w_lo], axis=1), rb


def _route_rows(lg, live, rt_ref, rtt_ref, cnt_ref, run_ref):
    tm = lg.shape[0]
    lane = lax.broadcasted_iota(jnp.int32, lg.shape, 1)
    sentinel = jnp.int32(ROUTE_W)
    rowmax = lambda mask: jnp.max(jnp.where(mask, lg, NEG_INF), axis=-1, keepdims=True)
    first = lambda mask: jnp.min(jnp.where(mask, lane, sentinel), axis=-1, keepdims=True)

    is_g = lane < N_GROUPS
    gmax = rowmax(is_g)
    g_idx = first(is_g & (lg == gmax))
    g_w = 1.0 / jnp.sum(jnp.where(is_g, jnp.exp(lg - gmax), 0.0), axis=-1, keepdims=True)

    e_lane = lane - N_GROUPS
    sel = (e_lane >= 0) & (e_lane < N_EXPERTS) & ((e_lane >> 3) == g_idx)
    m1 = rowmax(sel)
    i1 = first(sel & (lg == m1))
    sel2 = sel & (lane != i1)
    m2 = rowmax(sel2)
    i2 = first(sel2 & (lg == m2))
    ex = jnp.exp(m2 - m1)
    w0 = g_w / (1.0 + ex)
    w1 = g_w * ex / (1.0 + ex)

    oh0 = lane == i1
    oh1 = lane == i2
    both = (oh0 | oh1).astype(F32) * live
    rid = lax.broadcasted_iota(jnp.int32, (tm, tm), 0)
    cid = lax.broadcasted_iota(jnp.int32, (tm, tm), 1)
    before = jnp.dot((cid < rid).astype(BF16), both.astype(BF16), preferred_element_type=F32) + run_ref[...]
    rank0 = jnp.sum(jnp.where(oh0, before, 0.0), axis=-1, keepdims=True)
    rank1 = jnp.sum(jnp.where(oh1, before, 0.0), axis=-1, keepdims=True)
    run = run_ref[...] + jnp.sum(both, axis=0, keepdims=True)
    run_ref[...] = run
    cnt_ref[...] = run

    slab = jnp.zeros(lg.shape, F32)
    for ln, val in ((RT_W0, w0), (RT_W1, w1), (RT_E0, (i1 - N_GROUPS).astype(F32)),
                    (RT_E1, (i2 - N_GROUPS).astype(F32)), (RT_R0, rank0), (RT_R1, rank1)):
        slab = jnp.where(lane == ln, val, slab)
    rt_ref[...] = slab
    rtt_ref[...] = jnp.transpose(slab)[:ROUTE_FIELDS, :]


assert EXPERTS_PER_GROUP == 8


def _block_layout(rtt, cnt, n_tok):
    m = n_tok * TOP_K
    counts = cnt[0, N_GROUPS:N_GROUPS + N_EXPERTS].astype(jnp.int32)
    pcounts = (counts + MOE_TB - 1) // MOE_TB * MOE_TB
    pends = jnp.cumsum(pcounts)
    pstarts = pends - pcounts
    n_blocks = -(-(m + N_EXPERTS * (MOE_TB - 1)) // MOE_TB)
    e_idx = rtt[RT_E0:RT_E1 + 1, :].astype(jnp.int32)
    rank = rtt[RT_R0:RT_R1 + 1, :].astype(jnp.int32)
    onehot = e_idx[:, None, :] == jnp.arange(N_EXPERTS, dtype=jnp.int32)[None, :, None]
    pos = jnp.sum(jnp.where(onehot, pstarts[None, :, None], 0), axis=1) + rank
    blk_start = jnp.arange(n_blocks, dtype=jnp.int32) * MOE_TB
    blk_e = jnp.minimum(jnp.sum(blk_start[:, None] >= pends[None, :], axis=-1), N_EXPERTS - 1).astype(jnp.int32)
    nused = (pends[-1:] // MOE_TB).astype(jnp.int32)
    ids = jnp.arange(N_EXPERTS, dtype=jnp.int32)
    later_used = (ids[None, :] > ids[:, None]) & (counts[None, :] > 0)
    next_used = jnp.min(jnp.where(later_used, ids[None, :], N_EXPERTS), axis=-1)
    next_used = jnp.where(next_used == N_EXPERTS, -1, next_used).astype(jnp.int32)
    nxt_e = jnp.sum(jnp.where(blk_e[:, None] == ids[None, :], next_used[None, :], 0), axis=-1).astype(jnp.int32)
    ends_expert = jnp.any(((blk_start + MOE_TB)[:, None] == pends[None, :]) & (pcounts[None, :] > 0), axis=-1)
    zero_blk = (ends_expert | (blk_start >= pends[-1])).astype(jnp.int32)
    return pos, blk_e, nxt_e, nused, zero_blk


def _step_indices(pos, tm):
    nsteps = pos.shape[1] // tm
    return pos.reshape(TOP_K, nsteps, tm).transpose(1, 0, 2).reshape(nsteps, 1, TOP_K * tm)


def _tile_rows(i):
    return pl.ds(pl.multiple_of(i * TOK_TILE, TOK_TILE), TOK_TILE)


def _dispatch_kernel(zero_blk_ref, pos_ref, xp_ref, xs_hbm, xbuf, zbuf, sem, zsem, *, tm, n_blocks):
    s = pl.program_id(0)
    nsteps = pl.num_programs(0)
    slot = s % 2
    blk_rows = MOE_TB * TOK_TILE

    def copy(j, dst, sl):
        return pltpu.make_async_copy(xbuf.at[sl, _tile_rows(j)], xs_hbm.at[_tile_rows(dst)], sem.at[sl])

    def drain(sl):
        def body(j, c):
            copy(0, 0, sl).wait()
            return c
        lax.fori_loop(0, TOP_K * tm, body, 0, unroll=8)

    @pl.when(s == 0)
    def _():
        zbuf[...] = jnp.zeros_like(zbuf)

        def zero_copy(blk):
            rows = pl.ds(pl.multiple_of(blk * blk_rows, blk_rows), blk_rows)
            return pltpu.make_async_copy(zbuf, xs_hbm.at[rows], zsem)

        def each_flagged(fn):
            def body(blk, c):
                @pl.when(zero_blk_ref[blk] > 0)
                def _():
                    fn(zero_copy(blk))
                return c
            lax.fori_loop(0, n_blocks, body, 0)

        each_flagged(lambda cp: cp.start())
        each_flagged(lambda cp: cp.wait())

    @pl.when(s >= 2)
    def _():
        drain(slot)

    xbuf[slot] = xp_ref[...]

    for j in range(tm):
        for k in range(TOP_K):
            copy(j, pos_ref[0, 0, k * tm + j], slot).start(priority=k)

    @pl.when(s == nsteps - 1)
    def _():
        drain(1 - slot)
        drain(slot)


def _moe_dispatch(xp, pos, zero_blk, tm=512):
    n = pos.shape[1]
    n_blocks = zero_blk.shape[0]
    assert n // tm >= 2
    pos3 = _step_indices(pos, tm)
    grid_spec = pltpu.PrefetchScalarGridSpec(
        num_scalar_prefetch=1,
        grid=(n // tm,),
        in_specs=[pl.BlockSpec((1, 1, TOP_K * tm), lambda i, zb: (i, 0, 0), memory_space=pltpu.SMEM),
                  pl.BlockSpec((tm * TOK_TILE, LANES_V7X), lambda i, zb: (i, 0))],
        out_specs=pl.BlockSpec(memory_space=pl.ANY),
        scratch_shapes=[pltpu.VMEM((2, tm * TOK_TILE, LANES_V7X), U32),
                        pltpu.VMEM((MOE_TB * TOK_TILE, LANES_V7X), U32),
                        pltpu.SemaphoreType.DMA((2,)), pltpu.SemaphoreType.DMA(())],
    )
    return pl.pallas_call(
        functools.partial(_dispatch_kernel, tm=tm, n_blocks=n_blocks),
        grid_spec=grid_spec,
        out_shape=jax.ShapeDtypeStruct((n_blocks * MOE_TB * TOK_TILE, LANES_V7X), U32),
        compiler_params=_cparams(1),
        name="moe_dispatch",
    )(zero_blk, pos3, xp)


def _moe_kernel(blk_e_ref, nxt_e_ref, nused_ref, xs_ref, w1_hbm, w3_hbm, w2_hbm, ys_ref,
                wf1, wf3, wf2, w1b, w3b, w2b, slot_ref, sem, *, layer):
    s = pl.program_id(0)
    nused = nused_ref[0]

    def fetch(e, sl):
        return [pltpu.make_async_copy(w_hbm.at[layer, e], wf.at[sl], sem.at[sl, k])
                for k, (w_hbm, wf) in enumerate(((w1_hbm, wf1), (w3_hbm, wf3), (w2_hbm, wf2)))]

    @pl.when(s >= nused)
    def _():
        ys_ref[...] = jnp.zeros_like(ys_ref)

    @pl.when(s < nused)
    def _():
        e = blk_e_ref[s]
        prev = blk_e_ref[jnp.maximum(s - 1, 0)]

        @pl.when(s == 0)
        def _():
            slot_ref[0] = 0
            for cp in fetch(e, 0):
                cp.start()

        @pl.when((s > 0) & (e != prev))
        def _():
            slot_ref[0] = 1 - slot_ref[0]

        @pl.when((s == 0) | (e != prev))
        def _():
            sl = slot_ref[0]
            for cp in fetch(e, sl):
                cp.wait()
            w1b[...] = wf1[sl].astype(BF16)
            w3b[...] = wf3[sl].astype(BF16)
            w2b[...] = wf2[sl].astype(BF16)
            nxt = nxt_e_ref[s]

            @pl.when(nxt >= 0)
            def _():
                for cp in fetch(nxt, 1 - sl):
                    cp.start()

        xb = _unpack_rows(_load_token_tiles(xs_ref, 0, MOE_TB)).astype(BF16)
        y = jnp.zeros((MOE_TB, D_MODEL), F32)
        for c in range(D_EXPERT // MOE_HIDDEN_TILE):
            cols = slice(c * MOE_HIDDEN_TILE, (c + 1) * MOE_HIDDEN_TILE)
            h1 = jnp.dot(xb, w1b[:, cols], preferred_element_type=F32)
            h3 = jnp.dot(xb, w3b[:, cols], preferred_element_type=F32)
            h = (h1 * jax.nn.sigmoid(h1) * h3).astype(BF16)
            y = y + jnp.dot(h, w2b[cols, :], preferred_element_type=F32)
        _store_token_tiles(ys_ref, 0, _pack_rows(y))


def _moe_experts(xs, blk_e, nxt_e, nused, w1, w3, w2, layer):
    n_blocks = blk_e.shape[0]
    tiles = pl.BlockSpec((MOE_TB * TOK_TILE, LANES_V7X), lambda s, be, ne, nu: (s, 0))
    hbm = pl.BlockSpec(memory_space=pl.ANY)
    up, down = (D_MODEL, D_EXPERT), (D_EXPERT, D_MODEL)
    grid_spec = pltpu.PrefetchScalarGridSpec(
        num_scalar_prefetch=3,
        grid=(n_blocks,),
        in_specs=[tiles, hbm, hbm, hbm],
        out_specs=tiles,
        scratch_shapes=[pltpu.VMEM((2,) + up, F32), pltpu.VMEM((2,) + up, F32), pltpu.VMEM((2,) + down, F32),
                        pltpu.VMEM(up, BF16), pltpu.VMEM(up, BF16), pltpu.VMEM(down, BF16),
                        pltpu.SMEM((1,), jnp.int32), pltpu.SemaphoreType.DMA((2, 3))],
    )
    return pl.pallas_call(
        functools.partial(_moe_kernel, layer=layer),
        grid_spec=grid_spec,
        out_shape=jax.ShapeDtypeStruct(xs.shape, U32),
        compiler_params=_cparams(1, VMEM_LIMIT_V7X),
        name="moe_experts",
    )(blk_e, nxt_e, nused, xs, w1, w3, w2)


def _combine_kernel(pos_ref, nxt_ref, ys_hbm, x_ref, rt_ref, g_ref, b_ref, o_ref, ybuf, sem, *, tm):
    s = pl.program_id(0)
    nsteps = pl.num_programs(0)
    slot = s % 2

    def copy(src, j, sl):
        return pltpu.make_async_copy(ys_hbm.at[_tile_rows(src)], ybuf.at[sl, _tile_rows(j)], sem.at[sl])

    def start_gather(idx_ref, sl):
        def body(j, c):
            copy(idx_ref[0, 0, j], j, sl).start()
            return c
        lax.fori_loop(0, TOP_K * tm, body, 0, unroll=8)

    @pl.when(s == 0)
    def _():
        start_gather(pos_ref, 0)

    @pl.when(s + 1 < nsteps)
    def _():
        for j in range(TOP_K * tm):
            copy(nxt_ref[0, 0, j], j, 1 - slot).start(priority=j % 2)

    def wait_body(j, c):
        copy(0, j, slot).wait()
        return c
    lax.fori_loop(0, TOP_K * tm, wait_body, 0, unroll=8)

    rt = rt_ref[...]
    ffn = _unpack_rows(_load_token_tiles(ybuf, 0, tm, lead=slot)) * rt[:, RT_W0:RT_W0 + 1]
    ffn = ffn + _unpack_rows(_load_token_tiles(ybuf, tm * TOK_TILE, tm, lead=slot)) * rt[:, RT_W1:RT_W1 + 1]
    o_ref[...] = _layer_norm_rows(DN_ALPHA * x_ref[...] + ffn, g_ref[...], b_ref[...])


def _moe_combine(ys, pos, rt, x, g, b, tm=256):
    n = x.shape[0]
    nsteps = n // tm
    pos3 = _step_indices(pos, tm)
    smem_blk = lambda f: pl.BlockSpec((1, 1, TOP_K * tm), f, memory_space=pltpu.SMEM)
    row = lambda width: pl.BlockSpec((tm, width), lambda i: (i, 0))
    const = lambda shape: pl.BlockSpec(shape, lambda i: (0, 0))
    return pl.pallas_call(
        functools.partial(_combine_kernel, tm=tm),
        grid=(nsteps,),
        in_specs=[smem_blk(lambda i: (i, 0, 0)), smem_blk(lambda i: (jnp.minimum(i + 1, nsteps - 1), 0, 0)),
                  pl.BlockSpec(memory_space=pl.ANY), row(D_MODEL), row(ROUTE_W),
                  const((1, D_MODEL)), const((1, D_MODEL))],
        out_specs=row(D_MODEL),
        out_shape=jax.ShapeDtypeStruct((n, D_MODEL), F32),
        scratch_shapes=[pltpu.VMEM((2, TOP_K * tm * TOK_TILE, LANES_V7X), U32), pltpu.SemaphoreType.DMA((2,))],
        compiler_params=_cparams(1, 40 * MIB),
        name="moe_combine",
    )(pos3, pos3, ys, x, rt, g.reshape(1, -1), b.reshape(1, -1))


def _moe_layer(x1, xp, rt, rtt, cnt, w1, w3, w2, layer, g, b):
    n_tok = x1.shape[0]
    pos, blk_e, nxt_e, nused, zero_blk = _block_layout(rtt, cnt, n_tok)
    xs = _moe_dispatch(xp, pos, zero_blk)
    ys = _moe_experts(xs, blk_e, nxt_e, nused, w1, w3, w2, layer)
    return _moe_combine(ys, pos, rt, x1, g, b)


def _gmlp_kernel(u_ref, v_ref, g_ref, b_ref, w_ref, bs_ref, o_ref, *, chunks):
    for ci in range(chunks):
        rows = slice(ci * C_CHUNK, (ci + 1) * C_CHUNK)
        u = jax.nn.gelu(u_ref[rows, :])
        v = _layer_norm_rows(jax.nn.gelu(v_ref[rows, :]), g_ref[...], b_ref[...]).astype(BF16)
        for gi in range(C_GROUPS):
            cols = slice(gi * C_GROUP_DIM, (gi + 1) * C_GROUP_DIM)
            mixed = jnp.dot(w_ref[gi], v[:, cols], preferred_element_type=F32) + bs_ref[:, cols]
            o_ref[rows, cols] = (u[:, cols] * mixed).astype(o_ref.dtype)


def _gmlp(proj, ln_g, ln_b, w_s, b_s, chunks=4):
    n = proj.shape[0]
    tm = chunks * C_CHUNK
    w = (w_s * jnp.tril(jnp.ones((C_CHUNK, C_CHUNK), w_s.dtype))).astype(BF16)
    bs_full = jnp.repeat(b_s.T, C_GROUP_DIM, axis=1)
    const2 = lambda shape: pl.BlockSpec(shape, lambda i: (0, 0))
    return pl.pallas_call(
        functools.partial(_gmlp_kernel, chunks=chunks),
        grid=(n // tm,),
        in_specs=[pl.BlockSpec((tm, C_W), lambda i: (i, 0)), pl.BlockSpec((tm, C_W), lambda i: (i, 1)),
                  const2((1, C_W)), const2((1, C_W)),
                  pl.BlockSpec((C_GROUPS, C_CHUNK, C_CHUNK), lambda i: (0, 0, 0)), const2((C_CHUNK, C_W))],
        out_specs=pl.BlockSpec((tm, C_W), lambda i: (i, 0)),
        out_shape=jax.ShapeDtypeStruct((n, C_W), BF16),
        compiler_params=_cparams(1, 32 * MIB),
        name="gmlp_gating",
    )(proj, proj, ln_g.reshape(1, -1), ln_b.reshape(1, -1), w, bs_full)


CONV_HIST = 32


def _conv_kernel(a_ref, gt_ref, ap_ref, gp_ref, w_ref, cb_ref, g_ref, b_ref, o_ref, hbuf, hshift, *, ts):
    i = pl.program_id(1)
    hist = ap_ref[...] * jax.nn.sigmoid(gp_ref[...])
    hbuf[0:CONV_HIST, :] = jnp.where(i > 0, hist, jnp.zeros_like(hist))
    hbuf[CONV_HIST:CONV_HIST + ts, :] = a_ref[...] * jax.nn.sigmoid(gt_ref[...])
    off = CONV_HIST - (D_CONV - 1)
    acc = jnp.zeros((ts, D_CHANNELS), F32) + cb_ref[...]
    for r in range(SUBLANES_V7X):
        taps = [j for j in range(D_CONV) if (off + j) % SUBLANES_V7X == r]
        if not taps:
            continue
        src = hbuf
        if r:
            span = max(taps) + off - r + ts
            hshift[0:span, :] = hbuf[r:r + span, :]
            src = hshift
        for j in taps:
            base = off + j - r
            acc = acc + w_ref[j:j + 1, :] * src[base:base + ts, :]
    y = _layer_norm_rows(acc, g_ref[...], b_ref[...])
    o_ref[...] = (y * jax.nn.sigmoid(y)).astype(o_ref.dtype)


def _conformer_conv(proj, conv_w, conv_b, ln_g, ln_b, bsz, seq, ts=512):
    nt = seq // ts
    acol = 2 * C_W // D_CHANNELS
    gcol = acol + 1
    hb = ts // CONV_HIST
    cur = lambda col: pl.BlockSpec((ts, D_CHANNELS), lambda b, i: (b * nt + i, col))
    prev = lambda col: pl.BlockSpec((CONV_HIST, D_CHANNELS),
                                    lambda b, i: (jnp.maximum((b * nt + i) * hb - 1, 0), col))
    const2 = lambda shape: pl.BlockSpec(shape, lambda b, i: (0, 0))
    return pl.pallas_call(
        functools.partial(_conv_kernel, ts=ts),
        grid=(bsz, nt),
        in_specs=[cur(acol), cur(gcol), prev(acol), prev(gcol),
                  const2((D_CONV, D_CHANNELS)), const2((1, D_CHANNELS)), const2((1, D_CHANNELS)),
                  const2((1, D_CHANNELS))],
        out_specs=pl.BlockSpec((ts, D_CHANNELS), lambda b, i: (b * nt + i, 0)),
        out_shape=jax.ShapeDtypeStruct((bsz * seq, D_CHANNELS), BF16),
        scratch_shapes=[pltpu.VMEM((CONV_HIST + ts, D_CHANNELS), F32), pltpu.VMEM((CONV_HIST + ts, D_CHANNELS), F32)],
        compiler_params=_cparams(2, 32 * MIB),
        name="conformer_conv",
    )(proj, proj, proj, proj, conv_w, conv_b.reshape(1, -1), ln_g.reshape(1, -1), ln_b.reshape(1, -1))


def kernel(x, w_in_ab, attn_sinks, rel_bias, hgrn_lb_logits, hgrn_norm_g, w_out_ab, w_in_cd, gmlp_ln_g, gmlp_ln_b, gmlp_w_s, gmlp_b_s, conv_w, conv_b, conv_ln_g, conv_ln_b, w_out_cd, ln_mix_g, ln_mix_b, ln_ffn_g, ln_ffn_b, moe_w_group, moe_b_group, moe_w_router, moe_b_router, moe_w1, moe_w3, moe_w2):
    bsz, seq = x.shape[0], x.shape[1]
    n_tok = bsz * seq
    xf = x.reshape(n_tok, D_MODEL)
    lb_table = jnp.cumsum(jax.nn.softmax(hgrn_lb_logits.astype(F32), axis=0), axis=0)
    bias = _bias_table(rel_bias.astype(F32), _t5_bucket_table())

    for layer in range(DEPTH):
        j = layer // 2
        if layer % 2 == 0:
            proj = _matmul(xf, w_in_ab[j].astype(BF16), 1024, EVEN_IN // 3, F32)
            ya = _attention(proj, bias, attn_sinks[j].astype(F32), bsz, seq)
            yb = _hgrn(proj, lb_table[layer], hgrn_norm_g[j].astype(F32), bsz, seq)
            w_out = w_out_ab[j]
        else:
            proj = _matmul(xf, w_in_cd[j].astype(BF16), 1024, ODD_IN // 2, F32)
            ya = _gmlp(proj, gmlp_ln_g[j], gmlp_ln_b[j], gmlp_w_s[j], gmlp_b_s[j])
            yb = _conformer_conv(proj, conv_w[j], conv_b[j], conv_ln_g[j], conv_ln_b[j], bsz, seq)
            w_out = w_out_cd[j]
        wr, rbias = _router_weights(moe_w_group[layer], moe_b_group[layer],
                                    moe_w_router[layer], moe_b_router[layer])
        x1, xp, rt, rtt, cnt = _outproj_ln_route(ya, yb, w_out.astype(BF16), xf, ln_mix_g[layer],
                                                 ln_mix_b[layer], wr, rbias)
        xf = _moe_layer(x1, xp, rt, rtt, cnt, moe_w1, moe_w3, moe_w2, layer, ln_ffn_g[layer], ln_ffn_b[layer])
    return xf.reshape(bsz, seq, D_MODEL)
```
